```python
import math
import jax, jax.numpy as jnp
from jax import lax
import numpy as np


D_MODEL = 1024
BATCH = 16
SEQ = 2048
DEPTH = 1

CHUNK = 64
Q_BLOCK = 128
MLA_HEADS = 8
MLA_NOPE = 64
MLA_ROPE = 32
MLA_V = 64
MLA_Q_RANK = 384
MLA_KV_RANK = 256
ROPE_THETA = 10000.0
RWKV_HEADS = 8
RWKV_HEAD = 64
RWKV_WIDTH = RWKV_HEADS * RWKV_HEAD
DECAY_RANK = 64
AAA_RANK = 64
GATE_RANK = 128
GN_EPS = 64e-5
N_GROUPS = 4
EXPERTS_PER_GROUP = 8
N_EXPERTS = N_GROUPS * EXPERTS_PER_GROUP
TOP_K_IN_GROUP = 2
EXPERT_FF = 256
RMS_EPS = 1e-6
N_BRANCHES = 2
NEG_INF = -1e30

MLA_IN = MLA_Q_RANK + MLA_KV_RANK + MLA_ROPE
RWKV_IN = 3 * RWKV_WIDTH + DECAY_RANK + AAA_RANK + GATE_RANK
GATE_IN = N_BRANCHES * D_MODEL
IN_WIDTH = MLA_IN + RWKV_IN + GATE_IN
IN_SPLITS = (MLA_Q_RANK, MLA_Q_RANK + MLA_KV_RANK, MLA_IN, MLA_IN + RWKV_IN)
RWKV_SPLITS = (RWKV_WIDTH, 2 * RWKV_WIDTH, 3 * RWKV_WIDTH, 3 * RWKV_WIDTH + DECAY_RANK, 3 * RWKV_WIDTH + DECAY_RANK + AAA_RANK)

kernel_name = 'hybrid_mla_rwkv7_hmoe_block'


def _rmsnorm(x, g):
    xf = x.astype(jnp.float32)
    y = xf * lax.rsqrt(jnp.mean(xf * xf, axis=-1, keepdims=True) + RMS_EPS)
    return (y * g.astype(jnp.float32)).astype(x.dtype)


def _modulate(x, g, shift, scale):
    return _rmsnorm(x, g) * (1.0 + scale) + shift


def _rope_tables(positions):
    inv_freq = jnp.power(ROPE_THETA, -jnp.arange(0, MLA_ROPE, 2, dtype=jnp.float32) / MLA_ROPE)
    ang = positions.astype(jnp.float32)[..., None] * inv_freq
    return jnp.cos(ang), jnp.sin(ang)


def _apply_rope(x, cos, sin):
    xf = x.astype(jnp.float32)
    x1, x2 = jnp.split(xf, 2, axis=-1)
    return jnp.concatenate([x1 * cos - x2 * sin, x2 * cos + x1 * sin], axis=-1).astype(x.dtype)


def _mla_branch(cq, ckv, kr, cos, sin, g_cq, w_uq, g_ckv, w_ukv):
    b, s, _ = cq.shape
    q = (_rmsnorm(cq, g_cq) @ w_uq).reshape(b, s, MLA_HEADS, MLA_NOPE + MLA_ROPE)
    kv = (_rmsnorm(ckv, g_ckv) @ w_ukv).reshape(b, s, MLA_HEADS, MLA_NOPE + MLA_V)
    q_nope, q_rope = q[..., :MLA_NOPE], q[..., MLA_NOPE:]
    k_nope, v = kv[..., :MLA_NOPE], kv[..., MLA_NOPE:]
    q_rope = _apply_rope(q_rope, cos[:, :, None, :], sin[:, :, None, :])
    k_rope = _apply_rope(kr, cos, sin)
    scale = 1.0 / math.sqrt(MLA_NOPE + MLA_ROPE)
    outs = []
    for qb in range(s // Q_BLOCK):
        q0, k_end = qb * Q_BLOCK, (qb + 1) * Q_BLOCK
        scores = (jnp.einsum('bqhd,bkhd->bhqk', q_nope[:, q0:k_end], k_nope[:, :k_end])
                  + jnp.einsum('bqhr,bkr->bhqk', q_rope[:, q0:k_end], k_rope[:, :k_end]))
        scores = scores.astype(jnp.float32) * scale
        q_chunk = (q0 + jnp.arange(Q_BLOCK)) // CHUNK
        k_chunk = jnp.arange(k_end) // CHUNK
        mask = k_chunk[None, :] <= q_chunk[:, None]
        scores = jnp.where(mask, scores, NEG_INF)
        p = jax.nn.softmax(scores, axis=-1).astype(v.dtype)
        outs.append(jnp.einsum('bhqk,bkhd->bqhd', p, v[:, :k_end]))
    return jnp.concatenate(outs, axis=1).reshape(b, s, MLA_HEADS * MLA_V)


def _token_shift(z, mu):
    z_prev = jnp.pad(z, ((0, 0), (1, 0), (0, 0)))[:, :-1]
    return z + (z_prev - z) * mu


def _rwkv7_step(state, inp):
    r, w, k, v, kk, a = inp
    sa = jnp.einsum('bhvk,bhk->bhv', state, -kk)
    state = state * w[:, :, None, :] + sa[..., None] * (kk * a)[:, :, None, :] + v[..., None] * k[:, :, None, :]
    y = jnp.einsum('bhvk,bhk->bhv', state, r)
    return state, y


def _rwkv7_branch(zr, mu_shift, w_decay_up, decay_bias, w_a_up, a_bias, w_g_up, k_k, k_a, r_k, lnx_w, lnx_b):
    b, s, _ = zr.shape
    zs = _token_shift(zr, mu_shift).astype(jnp.float32)
    r, k, v, wd, ad, gd = jnp.split(zs, RWKV_SPLITS, axis=-1)
    w_log = -jax.nn.softplus(-(decay_bias + jnp.tanh(wd) @ w_decay_up)) - 0.5
    decay = jnp.exp(-jnp.exp(w_log))
    a = jax.nn.sigmoid(a_bias + ad @ w_a_up)
    g = jax.nn.sigmoid(gd) @ w_g_up

    def heads(t):
        return t.reshape(b, s, RWKV_HEADS, RWKV_HEAD)

    kk = heads(k * k_k)
    kk = kk / jnp.maximum(jnp.sqrt(jnp.sum(kk * kk, axis=-1, keepdims=True)), 1e-12)
    k = k * (1.0 + (a - 1.0) * k_a)
    r_h, k_h, v_h, w_h, a_h = heads(r), heads(k), heads(v), heads(decay), heads(a)
    xs = (jnp.swapaxes(r_h, 0, 1), jnp.swapaxes(w_h, 0, 1), jnp.swapaxes(k_h, 0, 1),
          jnp.swapaxes(v_h, 0, 1), jnp.swapaxes(kk, 0, 1), jnp.swapaxes(a_h, 0, 1))
    state0 = jnp.zeros((b, RWKV_HEADS, RWKV_HEAD, RWKV_HEAD), jnp.float32)
    _, y = lax.scan(_rwkv7_step, state0, xs)
    y = jnp.swapaxes(y, 0, 1)
    mean = jnp.mean(y, axis=-1, keepdims=True)
    var = jnp.mean(jnp.square(y - mean), axis=-1, keepdims=True)
    y = ((y - mean) * lax.rsqrt(var + GN_EPS)).reshape(b, s, RWKV_WIDTH) * lnx_w + lnx_b
    bonus = jnp.sum(r_h * k_h * r_k, axis=-1, keepdims=True) * v_h
    out = (y + bonus.reshape(b, s, RWKV_WIDTH)) * g
    return out.astype(zr.dtype)


def _hier_moe(h, w_rg, b_rg, w_re, b_re, w_gate, w_up, w_down):
    b, s, d = h.shape
    hf = h.reshape(b * s, d)
    g_logits = (hf @ w_rg).astype(jnp.float32) + b_rg.astype(jnp.float32)
    g_sel = jnp.argmax(g_logits, axis=-1)
    p_sel = jnp.max(jax.nn.softmax(g_logits, axis=-1), axis=-1, keepdims=True)
    e_logits = ((hf @ w_re).astype(jnp.float32) + b_re.astype(jnp.float32)).reshape(-1, N_GROUPS, EXPERTS_PER_GROUP)
    e_in_group = jnp.einsum('tge,tg->te', e_logits, jax.nn.one_hot(g_sel, N_GROUPS, dtype=jnp.float32))
    top_vals, top_idx = lax.top_k(e_in_group, TOP_K_IN_GROUP)
    weights = p_sel * jax.nn.softmax(top_vals, axis=-1)
    expert_id = g_sel[:, None] * EXPERTS_PER_GROUP + top_idx
    combine = jnp.einsum('tke,tk->te', jax.nn.one_hot(expert_id, N_EXPERTS, dtype=jnp.float32), weights).astype(h.dtype)
    y = jnp.zeros_like(hf)
    for e in range(N_EXPERTS):
        act = jax.nn.silu(hf @ w_gate[e]) * (hf @ w_up[e])
        y = y + combine[:, e:e + 1] * (act @ w_down[e])
    return y.reshape(b, s, d)


def _hybrid_layer(x, c, cos, sin, w_ada, b_ada, g_pre_mix, g_post_mix, g_pre_ffn, g_post_ffn,
                  w_in, g_cq, w_uq, g_ckv, w_ukv, w_o_mla, mu_shift, w_decay_up, decay_bias,
                  w_a_up, a_bias, w_g_up, k_k, k_a, r_k, lnx_w, lnx_b, w_o_rwkv, w_out,
                  w_router_group, b_router_group, w_router_expert, b_router_expert,
                  w_exp_gate, w_exp_up, w_exp_down):
    mod = jax.nn.silu(c) @ w_ada + b_ada
    shift1, scale1, gate1, shift2, scale2, gate2 = jnp.split(mod[:, None, :], 6, axis=-1)
    h = _modulate(x, g_pre_mix, shift1, scale1)
    z = h @ w_in
    cq, ckv, kr, zr, gate_logits = jnp.split(z, IN_SPLITS, axis=-1)
    o_mla = _mla_branch(cq, ckv, kr, cos, sin, g_cq, w_uq, g_ckv, w_ukv) @ w_o_mla
    o_rwkv = _rwkv7_branch(zr, mu_shift, w_decay_up, decay_bias, w_a_up, a_bias, w_g_up,
                           k_k, k_a, r_k, lnx_w, lnx_b) @ w_o_rwkv
    gate_a, gate_b = jnp.split(jax.nn.sigmoid(gate_logits), 2, axis=-1)
    y = (gate_a * o_mla + gate_b * o_rwkv) @ w_out
    x = x + gate1 * _rmsnorm(y, g_post_mix)
    h = _modulate(x, g_pre_ffn, shift2, scale2)
    y = _hier_moe(h, w_router_group, b_router_group, w_router_expert, b_router_expert,
                  w_exp_gate, w_exp_up, w_exp_down)
    return x + gate2 * _rmsnorm(y, g_post_ffn)


def setup_inputs(seed: int = 0) -> dict:
    key = jax.random.key(seed)
    ks = iter(jax.random.split(key, 40))
    L, D = DEPTH, D_MODEL

    def nrm(shape, scale):
        return jax.random.normal(next(ks), shape, jnp.float32) * scale

    def gain(shape):
        return 1.0 + nrm(shape, 0.02)

    x = nrm((BATCH, SEQ, D), 1.0)
    c = nrm((BATCH, D), 1.0)
    offsets = jax.random.randint(next(ks), (BATCH, 1), 0, 256, dtype=jnp.int32) * CHUNK
    positions = (offsets + jnp.arange(SEQ, dtype=jnp.int32)[None, :]).astype(jnp.int32)
    return {
        'x': x, 'c': c, 'positions': positions,
        'w_ada': nrm((L, D, 6 * D), 0.5 * D ** -0.5),
        'b_ada': nrm((L, 6 * D), 0.02),
        'g_pre_mix': gain((L, D)), 'g_post_mix': gain((L, D)),
        'g_pre_ffn': gain((L, D)), 'g_post_ffn': gain((L, D)),
        'w_in': nrm((L, D, IN_WIDTH), D ** -0.5),
        'g_cq': gain((L, MLA_Q_RANK)),
        'w_uq': nrm((L, MLA_Q_RANK, MLA_HEADS * (MLA_NOPE + MLA_ROPE)), MLA_Q_RANK ** -0.5),
        'g_ckv': gain((L, MLA_KV_RANK)),
        'w_ukv': nrm((L, MLA_KV_RANK, MLA_HEADS * (MLA_NOPE + MLA_V)), MLA_KV_RANK ** -0.5),
        'w_o_mla': nrm((L, MLA_HEADS * MLA_V, D), (MLA_HEADS * MLA_V) ** -0.5),
        'mu_shift': jax.random.uniform(next(ks), (L, RWKV_IN), jnp.float32),
        'w_decay_up': nrm((L, DECAY_RANK, RWKV_WIDTH), 0.1 * DECAY_RANK ** -0.5),
        'decay_bias': jax.random.uniform(next(ks), (L, RWKV_WIDTH), jnp.float32, -5.0, -1.0),
        'w_a_up': nrm((L, AAA_RANK, RWKV_WIDTH), AAA_RANK ** -0.5),
        'a_bias': nrm((L, RWKV_WIDTH), 0.1),
        'w_g_up': nrm((L, GATE_RANK, RWKV_WIDTH), GATE_RANK ** -0.5),
        'k_k': 0.85 + nrm((L, RWKV_WIDTH), 0.02),
        'k_a': gain((L, RWKV_WIDTH)),
        'r_k': nrm((L, RWKV_HEADS, RWKV_HEAD), 0.1),
        'lnx_w': gain((L, RWKV_WIDTH)),
        'lnx_b': nrm((L, RWKV_WIDTH), 0.02),
        'w_o_rwkv': nrm((L, RWKV_WIDTH, D), RWKV_WIDTH ** -0.5),
        'w_out': nrm((L, D, D), D ** -0.5),
        'w_router_group': nrm((L, D, N_GROUPS), D ** -0.5),
        'b_router_group': nrm((L, N_GROUPS), 0.01),
        'w_router_expert': nrm((L, D, N_EXPERTS), D ** -0.5),
        'b_router_expert': nrm((L, N_EXPERTS), 0.01),
        'w_exp_gate': nrm((L, N_EXPERTS, D, EXPERT_FF), D ** -0.5),
        'w_exp_up': nrm((L, N_EXPERTS, D, EXPERT_FF), D ** -0.5),
        'w_exp_down': nrm((L, N_EXPERTS, EXPERT_FF, D), EXPERT_FF ** -0.5),
    }


def reference(x, c, positions, w_ada, b_ada, g_pre_mix, g_post_mix, g_pre_ffn, g_post_ffn,
              w_in, g_cq, w_uq, g_ckv, w_ukv, w_o_mla, mu_shift, w_decay_up, decay_bias,
              w_a_up, a_bias, w_g_up, k_k, k_a, r_k, lnx_w, lnx_b, w_o_rwkv, w_out,
              w_router_group, b_router_group, w_router_expert, b_router_expert,
              w_exp_gate, w_exp_up, w_exp_down):
    cos, sin = _rope_tables(positions)
    for l in range(DEPTH):
        x = _hybrid_layer(x, c, cos, sin, w_ada[l], b_ada[l], g_pre_mix[l], g_post_mix[l],
                          g_pre_ffn[l], g_post_ffn[l], w_in[l], g_cq[l], w_uq[l], g_ckv[l],
                          w_ukv[l], w_o_mla[l], mu_shift[l], w_decay_up[l], decay_bias[l],
                          w_a_up[l], a_bias[l], w_g_up[l], k_k[l], k_a[l], r_k[l], lnx_w[l],
                          lnx_b[l], w_o_rwkv[l], w_out[l], w_router_group[l], b_router_group[l],
                          w_router_expert[l], b_router_expert[l], w_exp_gate[l], w_exp_up[l],
                          w_exp_down[l])
    return x
```

```python
import functools
import math

import jax
import jax.numpy as jnp
from jax import lax
from jax.experimental import pallas as pl
from jax.experimental.pallas import tpu as pltpu

F32 = jnp.float32
BF16 = jnp.bfloat16

D_MODEL = 1024
CHUNK = 64
HEADS = 8
MLA_NOPE = 64
MLA_ROPE = 32
MLA_V = 64
MLA_Q_RANK = 384
MLA_KV_RANK = 256
ROPE_THETA = 10000.0
RWKV_HEAD = 64
RWKV_WIDTH = HEADS * RWKV_HEAD
DECAY_RANK = 64
AAA_RANK = 64
GATE_RANK = 128
GN_EPS = 64e-5
N_GROUPS = 4
EXPERTS_PER_GROUP = 8
N_EXPERTS = N_GROUPS * EXPERTS_PER_GROUP
EXPERT_FF = 256
RMS_EPS = 1e-6
NEG_INF = -1e30
MLA_IN = MLA_Q_RANK + MLA_KV_RANK + MLA_ROPE
RWKV_IN = 3 * RWKV_WIDTH + DECAY_RANK + AAA_RANK + GATE_RANK

LANES = 128
HEAD_PAD = LANES
QK_WIDTH = HEADS * HEAD_PAD
MLA_C_WIDTH = 768
ROUTER_WIDTH = LANES
VMEM_LIMIT = 56 * 1024 * 1024


def _dot(a, b):
    return jnp.dot(a, b, preferred_element_type=F32)


def _dot_nt(a, b):
    return lax.dot_general(a, b, (((1,), (1,)), ((), ())), preferred_element_type=F32)


def _dot_tn(a, b):
    return lax.dot_general(a, b, (((0,), (0,)), ((), ())), preferred_element_type=F32)


def _split(x):
    hi = x.astype(BF16)
    lo = (x - hi.astype(F32)).astype(BF16)
    return hi, lo


def _dot_split(x, w_bf16):
    hi, lo = _split(x)
    return _dot(hi, w_bf16) + _dot(lo, w_bf16)


def _dot3(x, w_hi, w_lo):
    x_hi, x_lo = _split(x)
    return _dot(x_hi, w_hi) + (_dot(x_hi, w_lo) + _dot(x_lo, w_hi))


def _sigmoid(x):
    return 1.0 / (1.0 + jnp.exp(-x))


def _rms(x, g):
    return x * lax.rsqrt(jnp.mean(x * x, axis=-1, keepdims=True) + RMS_EPS) * g


def _ada_kernel(c_ref, w_ref, b_ref, o_ref):
    c = c_ref[...]
    s = c * _sigmoid(c)
    w_hi, w_lo = _split(w_ref[...])
    o_ref[...] = _dot3(s, w_hi, w_lo) + b_ref[...]


def _ada(c, w_ada, b_ada):
    b, d = c.shape
    n = w_ada.shape[1]
    tn = 512
    return pl.pallas_call(
        _ada_kernel,
        grid=(n // tn,),
        in_specs=[pl.BlockSpec((b, d), lambda j: (0, 0)),
                  pl.BlockSpec((d, tn), lambda j: (0, j)),
                  pl.BlockSpec((1, tn), lambda j: (0, j))],
        out_specs=pl.BlockSpec((b, tn), lambda j: (0, j)),
        out_shape=jax.ShapeDtypeStruct((b, n), F32),
        name="ada",
    )(c, w_ada, b_ada.reshape(1, n))


def _inproj_kernel(x_ref, mod_ref, pos_ref, gpre_ref, wc_ref, wr_ref, wg_ref, gcq_ref, gckv_ref,
                   wuq_ref, wuk_ref, wuv_ref, mu_ref, invf_ref,
                   q_ref, k_ref, v_ref, zs_ref, gate_ref, carry_ref):
    i = pl.program_id(1)
    tm = x_ref.shape[0]
    mod = mod_ref[0]
    shift, scale = mod[0:1], mod[1:2]
    h = (_rms(x_ref[...], gpre_ref[...]) * (1.0 + scale) + shift).astype(BF16)

    zc = _dot(h, wc_ref[...])
    nq = _rms(zc[:, :MLA_Q_RANK], gcq_ref[...]).astype(BF16)
    nkv = _rms(zc[:, MLA_Q_RANK:MLA_Q_RANK + MLA_KV_RANK], gckv_ref[...]).astype(BF16)
    kr = zc[:, MLA_Q_RANK + MLA_KV_RANK:]
    qf = _dot(nq, wuq_ref[...])
    kn = _dot(nkv, wuk_ref[...])
    v_ref[...] = _dot(nkv, wuv_ref[...]).astype(BF16)

    ang = pos_ref[...].astype(F32) * invf_ref[...]
    lane = lax.broadcasted_iota(jnp.int32, (1, LANES), 1)
    first_half = lane < MLA_NOPE + MLA_ROPE // 2
    in_rope = (lane >= MLA_NOPE) & (lane < MLA_NOPE + MLA_ROPE)
    cos_t = jnp.where(in_rope, jnp.cos(ang), 1.0)
    sin_a = jnp.sin(ang)
    sin_t = jnp.where(in_rope, jnp.where(first_half, -sin_a, sin_a), 0.0)

    def rope(t):
        rot = jnp.where(first_half, pltpu.roll(t, LANES - MLA_ROPE // 2, 1), pltpu.roll(t, MLA_ROPE // 2, 1))
        return t * cos_t + rot * sin_t

    kr_rot = rope(kr)
    q_scale = 1.0 / math.sqrt(MLA_NOPE + MLA_ROPE)
    for hh in range(HEADS):
        sl = slice(hh * HEAD_PAD, (hh + 1) * HEAD_PAD)
        q_ref[:, sl] = (rope(qf[:, sl]) * q_scale).astype(BF16)
        k_ref[:, sl] = (kn[:, sl] + kr_rot).astype(BF16)

    zr = _dot(h, wr_ref[...])

    @pl.when(i == 0)
    def _():
        carry_ref[...] = jnp.zeros_like(carry_ref)

    row = lax.broadcasted_iota(jnp.int32, (tm, 1), 0)
    prev = jnp.where(row == 0, carry_ref[0:1, :], pltpu.roll(zr, 1, 0))
    carry_ref[0:1, :] = zr[tm - 1:tm, :]
    zs_ref[...] = (zr + (prev - zr) * mu_ref[...]).astype(BF16)

    gate_ref[...] = _sigmoid(_dot(h, wg_ref[...])).astype(BF16)


def _const_spec(shape):
    nd = len(shape)
    return pl.BlockSpec(shape, lambda *_: (0,) * nd, pipeline_mode=pl.Buffered(1))


def _inproj(x2, mod3, pos2, g_pre, w_c, w_r, w_g, g_cq, g_ckv, w_uq, w_uk, w_uv, mu, invf, *, batch, seq, tm):
    t, d = x2.shape
    ns = seq // tm
    row = lambda b, i: (b * ns + i, 0)
    outs = [(QK_WIDTH, BF16), (QK_WIDTH, BF16), (HEADS * MLA_V, BF16), (RWKV_IN, BF16), (2 * d, BF16)]
    return pl.pallas_call(
        _inproj_kernel,
        grid=(batch, ns),
        in_specs=[pl.BlockSpec((tm, d), row),
                  pl.BlockSpec((1, 6, d), lambda b, i: (b, 0, 0)),
                  pl.BlockSpec((tm, 1), row),
                  _const_spec(g_pre.shape), _const_spec(w_c.shape), _const_spec(w_r.shape),
                  _const_spec(w_g.shape), _const_spec(g_cq.shape), _const_spec(g_ckv.shape),
                  _const_spec(w_uq.shape), _const_spec(w_uk.shape), _const_spec(w_uv.shape),
                  _const_spec(mu.shape), _const_spec(invf.shape)],
        out_specs=[pl.BlockSpec((tm, w), row) for w, _ in outs],
        out_shape=[jax.ShapeDtypeStruct((t, w), dt) for w, dt in outs],
        scratch_shapes=[pltpu.VMEM((8, RWKV_IN), F32)],
        compiler_params=pltpu.CompilerParams(dimension_semantics=("parallel", "arbitrary"),
                                             vmem_limit_bytes=VMEM_LIMIT),
        name="inproj",
    )(x2, mod3, pos2, g_pre, w_c, w_r, w_g, g_cq, g_ckv, w_uq, w_uk, w_uv, mu, invf)


def _attn_kernel(q_ref, k_ref, v_ref, o_ref):
    i = pl.program_id(2)
    tq = q_ref.shape[0]
    lane = lax.broadcasted_iota(jnp.int32, (1, LANES), 1)
    rq = lax.broadcasted_iota(jnp.int32, (tq, tq), 0) // CHUNK
    ck = lax.broadcasted_iota(jnp.int32, (tq, tq), 1) // CHUNK
    diag_mask = ck <= rq
    outs = []
    for hh in range(2):
        hs = slice(hh * HEAD_PAD, (hh + 1) * HEAD_PAD)
        q = q_ref[:, hs]

        def block(kb, carry, masked, hs=hs, q=q):
            m, l, acc = carry
            rows = pl.ds(pl.multiple_of(kb * tq, tq), tq)
            s = _dot_nt(q, k_ref[rows, hs])
            if masked:
                s = jnp.where(diag_mask, s, NEG_INF)
            m_new = jnp.maximum(m, jnp.max(s, axis=-1, keepdims=True))
            alpha = jnp.exp(m - m_new)
            p = jnp.exp(s - m_new)
            l = alpha * l + jnp.sum(p, axis=-1, keepdims=True)
            acc = alpha * acc + _dot(p.astype(BF16), v_ref[rows, :])
            return m_new, l, acc

        init = (jnp.full((tq, 1), NEG_INF, F32), jnp.zeros((tq, 1), F32), jnp.zeros((tq, LANES), F32))
        carry = lax.fori_loop(0, i, functools.partial(block, masked=False), init)
        _, l, acc = block(i, carry, True)
        outs.append(acc / l)
    o_ref[...] = jnp.where(lane < MLA_V, outs[0], outs[1]).astype(BF16)


def _attn(q_all, k_all, v_all, *, batch, seq, tq):
    t = q_all.shape[0]
    nq = seq // tq
    pair_w = 2 * HEAD_PAD
    return pl.pallas_call(
        _attn_kernel,
        grid=(batch, HEADS // 2, nq),
        in_specs=[pl.BlockSpec((tq, pair_w), lambda b, p, i: (b * nq + i, p)),
                  pl.BlockSpec((seq, pair_w), lambda b, p, i: (b, p)),
                  pl.BlockSpec((seq, 2 * MLA_V), lambda b, p, i: (b, p))],
        out_specs=pl.BlockSpec((tq, 2 * MLA_V), lambda b, p, i: (b * nq + i, p)),
        out_shape=jax.ShapeDtypeStruct((t, HEADS * MLA_V), BF16),
        compiler_params=pltpu.CompilerParams(dimension_semantics=("parallel", "parallel", "arbitrary"),
                                             vmem_limit_bytes=VMEM_LIMIT),
        name="attn",
    )(q_all, k_all, v_all)


def _rwkv_kernel(zs_ref, wda_ref, wgu_ref, dbias_ref, abias_ref, kk_ref, ka_ref, rk_ref, lnw_ref, lnb_ref,
                 ones_ref, tri_ref, blk_ref,
                 o_ref,
                 st_ref, a_s, r_s, b_s, k_s, bg_s, kg_s, v_s, gc_s, y_s, bonus_s, g_s):
    i = pl.program_id(1)
    tc = zs_ref.shape[0]
    w = RWKV_WIDTH

    @pl.when(i == 0)
    def _():
        st_ref[...] = jnp.zeros_like(st_ref)

    ones_bd = ones_ref[...]

    def head_sum(t):
        return _dot_split(t, ones_bd)

    zs = zs_ref[...].astype(F32)
    r, k, v = zs[:, 0:w], zs[:, w:2 * w], zs[:, 2 * w:3 * w]
    da = zs[:, 3 * w:3 * w + LANES]
    gd = zs[:, 3 * w + LANES:]
    lane = lax.broadcasted_iota(jnp.int32, (1, LANES), 1)
    lora_in = jnp.where(lane < DECAY_RANK, jnp.tanh(da), da).astype(BF16)
    pre = _dot(lora_in, wda_ref[...])
    u = -(dbias_ref[...] + pre[:, :w])
    softplus = jnp.maximum(u, 0.0) + jnp.log(1.0 + jnp.exp(-jnp.abs(u)))
    logw = -jnp.exp(-softplus - 0.5)
    eta = _sigmoid(abias_ref[...] + pre[:, w:])
    g_s[...] = _dot(_sigmoid(gd).astype(BF16), wgu_ref[...])
    kk = k * kk_ref[...]
    kk = kk / jnp.maximum(jnp.sqrt(head_sum(kk * kk)), 1e-12)
    kp = k * (1.0 + (eta - 1.0) * ka_ref[...])
    bonus_s[...] = head_sum(r * kp * rk_ref[...]) * v

    lw_hi, lw_lo = _split(logw)
    cum = _dot(tri_ref[...], lw_hi) + _dot(tri_ref[...], lw_lo)
    cum_end = _dot(blk_ref[...], lw_hi) + _dot(blk_ref[...], lw_lo)
    b_in = kk * eta
    a_s[...] = (-kk * jnp.exp(cum - logw)).astype(BF16)
    r_s[...] = (r * jnp.exp(cum)).astype(BF16)
    g_inv = jnp.exp(-cum)
    b_s[...] = (b_in * g_inv).astype(BF16)
    k_s[...] = (kp * g_inv).astype(BF16)
    g_rem = jnp.exp(cum_end - cum)
    bg_s[...] = (b_in * g_rem).astype(BF16)
    kg_s[...] = (kp * g_rem).astype(BF16)
    v_s[...] = v.astype(BF16)
    gc_s[...] = jnp.exp(cum_end)

    n2 = 2 * CHUNK
    ri = lax.broadcasted_iota(jnp.int32, (4 * CHUNK, n2), 0)
    ci = lax.broadcasted_iota(jnp.int32, (4 * CHUNK, n2), 1) % CHUNK
    tri_mask = (ri % CHUNK + ri // n2) > ci
    lane2 = lax.broadcasted_iota(jnp.int32, (CHUNK, n2), 1)
    h1 = lane2 < RWKV_HEAD
    bi = lax.broadcasted_iota(jnp.int32, (n2, n2), 0) // RWKV_HEAD
    bj = lax.broadcasted_iota(jnp.int32, (n2, n2), 1) // RWKV_HEAD
    bd_mask = bi == bj
    zero_bf = jnp.zeros((CHUNK, n2), BF16)
    zero_f = jnp.zeros((CHUNK, n2), F32)

    def chunk(c, carry):
        rows = pl.ds(pl.multiple_of(c * CHUNK, CHUNK), CHUNK)
        for p in range(HEADS // 2):
            ln = slice(p * n2, (p + 1) * n2)
            a_t, r_t, v_t = a_s[rows, ln], r_s[rows, ln], v_s[rows, ln]
            lhs = jnp.concatenate([jnp.where(h1, a_t, zero_bf), jnp.where(h1, zero_bf, a_t),
                                   jnp.where(h1, r_t, zero_bf), jnp.where(h1, zero_bf, r_t)], axis=0)
            rhs = jnp.concatenate([b_s[rows, ln], k_s[rows, ln]], axis=0)
            ml = jnp.where(tri_mask, _dot_nt(lhs, rhs), 0.0)
            ml_b = ml.astype(BF16)
            zv = jnp.concatenate([zero_bf, v_t], axis=0)
            a_f = a_t.astype(F32)
            xs = []
            for hh in range(2):
                hr = slice(hh * CHUNK, (hh + 1) * CHUNK)
                lp = ml[hr, 0:CHUNK]
                x = jnp.concatenate([a_f, _dot(ml_b[hr, :], zv)], axis=1)
                for j in range(6):
                    lb = lp.astype(BF16)
                    x = x + _dot(lb, x.astype(BF16))
                    if j < 5:
                        lp = _dot(lb, lb)
                xs.append(x)
            a_hat = jnp.where(h1, xs[0][:, :n2], xs[1][:, :n2])
            u0 = jnp.where(h1, xs[0][:, n2:], xs[1][:, n2:])
            z = jnp.concatenate([jnp.concatenate([a_hat, u0], axis=1),
                                 jnp.concatenate([zero_f, v_t.astype(F32)], axis=1)], axis=0).astype(BF16)
            o0 = _dot(ml_b[2 * CHUNK:3 * CHUNK, :], z)
            o1 = _dot(ml_b[3 * CHUNK:4 * CHUNK, :], z)
            r_hat = r_t.astype(F32) + jnp.where(h1, o0[:, :n2], o1[:, :n2])
            y0 = jnp.where(h1, o0[:, n2:], o1[:, n2:])
            bk = jnp.concatenate([bg_s[rows, ln], kg_s[rows, ln]], axis=0)
            pm = _dot_tn(z, bk)
            g_t = jnp.where(bd_mask, pm[:n2], 0.0)
            h_t = jnp.where(bd_mask, pm[n2:], 0.0)
            s = st_ref[p]
            s_b = s.astype(BF16)
            y_s[rows, ln] = _dot_nt(r_hat.astype(BF16), s_b) + y0
            gc = gc_s[pl.ds(pl.multiple_of(c * CHUNK, CHUNK), 8), ln][0:1]
            st_ref[p] = gc * s + _dot(s_b, g_t.astype(BF16)) + h_t
        return carry

    lax.fori_loop(0, tc // CHUNK, chunk, 0)

    y = y_s[...]
    inv_n = 1.0 / RWKV_HEAD
    dev = y - head_sum(y) * inv_n
    var = head_sum(dev * dev) * inv_n
    out = (dev * lax.rsqrt(var + GN_EPS) * lnw_ref[...] + lnb_ref[...] + bonus_s[...]) * g_s[...]
    o_ref[...] = out.astype(BF16)


def _rwkv(zs, w_da, w_gu, dbias, abias, k_k, k_a, r_k, lnw, lnb, ones_bd, tri, blk, *, batch, seq, tc):
    t = zs.shape[0]
    nc = seq // tc
    w = RWKV_WIDTH
    consts = [w_da, w_gu, dbias, abias, k_k, k_a, r_k, lnw, lnb, ones_bd, tri, blk]
    return pl.pallas_call(
        _rwkv_kernel,
        grid=(batch, nc),
        in_specs=[pl.BlockSpec((tc, RWKV_IN), lambda b, i: (b * nc + i, 0))] + [_const_spec(a.shape) for a in consts],
        out_specs=pl.BlockSpec((tc, w), lambda b, i: (b * nc + i, 0)),
        out_shape=jax.ShapeDtypeStruct((t, w), BF16),
        scratch_shapes=[pltpu.VMEM((HEADS // 2, 2 * RWKV_HEAD, 2 * RWKV_HEAD), F32)]
        + [pltpu.VMEM((tc, w), BF16)] * 7
        + [pltpu.VMEM((tc, w), F32)] * 4,
        compiler_params=pltpu.CompilerParams(dimension_semantics=("parallel", "arbitrary"),
                                             vmem_limit_bytes=VMEM_LIMIT),
        name="rwkv",
    )(zs, *consts)


def _merge_kernel(attn_ref, rw_ref, gate_ref, x_ref, mod_ref, womla_ref, worw_ref, wout_ref,
                  gpost_ref, gffn_ref, wrh_ref, wrl_ref, br_ref,
                  x1_ref, h2_ref, comb_ref):
    d = x_ref.shape[1]
    mod = mod_ref[0]
    gate1, shift2, scale2 = mod[2:3], mod[3:4], mod[4:5]
    gates = gate_ref[...].astype(F32)
    o = gates[:, :d] * _dot(attn_ref[...], womla_ref[...]) + gates[:, d:] * _dot(rw_ref[...], worw_ref[...])
    y = _dot(o.astype(BF16), wout_ref[...])
    x1 = x_ref[...] + gate1 * _rms(y, gpost_ref[...])
    x1_ref[...] = x1
    h2 = _rms(x1, gffn_ref[...]) * (1.0 + scale2) + shift2
    h2_ref[...] = h2.astype(BF16)

    logits = _dot3(h2, wrh_ref[...], wrl_ref[...]) + br_ref[...]
    tm = logits.shape[0]
    lane = lax.broadcasted_iota(jnp.int32, (tm, ROUTER_WIDTH), 1)
    lane_f = lane.astype(F32)
    big = float(ROUTER_WIDTH)
    is_group = (lane >= N_EXPERTS) & (lane < N_EXPERTS + N_GROUPS)
    gl = jnp.where(is_group, logits, NEG_INF)
    g_max = jnp.max(gl, axis=-1, keepdims=True)
    g_sel = jnp.min(jnp.where(gl == g_max, lane_f, big), axis=-1, keepdims=True) - float(N_EXPERTS)
    p_sel = 1.0 / jnp.sum(jnp.where(is_group, jnp.exp(gl - g_max), 0.0), axis=-1, keepdims=True)
    lane_group = (lane // EXPERTS_PER_GROUP).astype(F32)
    in_group = (lane < N_EXPERTS) & (lane_group == g_sel)
    el = jnp.where(in_group, logits, NEG_INF)
    t1 = jnp.max(el, axis=-1, keepdims=True)
    i1 = jnp.min(jnp.where(el == t1, lane_f, big), axis=-1, keepdims=True)
    el2 = jnp.where(lane_f == i1, NEG_INF, el)
    t2 = jnp.max(el2, axis=-1, keepdims=True)
    i2 = jnp.min(jnp.where(el2 == t2, lane_f, big), axis=-1, keepdims=True)
    e21 = jnp.exp(t2 - t1)
    w1 = p_sel / (1.0 + e21)
    w2 = w1 * e21
    comb_ref[...] = jnp.where(lane_f == i1, w1, 0.0) + jnp.where(lane_f == i2, w2, 0.0)


def _merge(attn, rw, gates, x2, mod3, w_o_mla, w_o_rwkv, w_out, g_post, g_ffn, wr_hi, wr_lo, b_r, *, seq, tm):
    t, d = x2.shape
    ns = seq // tm
    row = lambda i: (i, 0)
    consts = [w_o_mla, w_o_rwkv, w_out, g_post, g_ffn, wr_hi, wr_lo, b_r]
    return pl.pallas_call(
        _merge_kernel,
        grid=(t // tm,),
        in_specs=[pl.BlockSpec((tm, attn.shape[1]), row), pl.BlockSpec((tm, rw.shape[1]), row),
                  pl.BlockSpec((tm, 2 * d), row), pl.BlockSpec((tm, d), row),
                  pl.BlockSpec((1, 6, d), lambda i: (i // ns, 0, 0))] + [_const_spec(a.shape) for a in consts],
        out_specs=[pl.BlockSpec((tm, d), row), pl.BlockSpec((tm, d), row), pl.BlockSpec((tm, ROUTER_WIDTH), row)],
        out_shape=[jax.ShapeDtypeStruct((t, d), F32), jax.ShapeDtypeStruct((t, d), BF16),
                   jax.ShapeDtypeStruct((t, ROUTER_WIDTH), F32)],
        compiler_params=pltpu.CompilerParams(dimension_semantics=("parallel",), vmem_limit_bytes=VMEM_LIMIT),
        name="merge",
    )(attn, rw, gates, x2, mod3, *consts)


def _moe_kernel(h_ref, comb_ref, wg_ref, wu_ref, wd_ref, x1_ref, mod_ref, gpost_ref, o_ref, acc_ref):
    e = pl.program_id(1)

    @pl.when(e == 0)
    def _():
        acc_ref[...] = jnp.zeros_like(acc_ref)

    h = h_ref[...]
    a = _dot(h, wg_ref[0])
    act = (a * _sigmoid(a) * _dot(h, wu_ref[0])).astype(BF16)
    comb = comb_ref[...]
    lane = lax.broadcasted_iota(jnp.int32, comb.shape, 1)
    ce = jnp.sum(jnp.where(lane == e, comb, 0.0), axis=-1, keepdims=True)
    acc_ref[...] += ce * _dot(act, wd_ref[0])

    @pl.when(e == pl.num_programs(1) - 1)
    def _():
        gate2 = mod_ref[0][5:6]
        o_ref[...] = x1_ref[...] + gate2 * _rms(acc_ref[...], gpost_ref[...])


def _moe(h2, comb, w_gate, w_up, w_down, x1, mod3, g_post, *, seq, tm):
    t, d = x1.shape
    ns = seq // tm
    row = lambda i, e: (i, 0)
    return pl.pallas_call(
        _moe_kernel,
        grid=(t // tm, N_EXPERTS),
        in_specs=[pl.BlockSpec((tm, d), row), pl.BlockSpec((tm, ROUTER_WIDTH), row),
                  pl.BlockSpec((1, d, EXPERT_FF), lambda i, e: (e, 0, 0)),
                  pl.BlockSpec((1, d, EXPERT_FF), lambda i, e: (e, 0, 0)),
                  pl.BlockSpec((1, EXPERT_FF, d), lambda i, e: (e, 0, 0)),
                  pl.BlockSpec((tm, d), row),
                  pl.BlockSpec((1, 6, d), lambda i, e: (i // ns, 0, 0)),
                  pl.BlockSpec((1, d), lambda i, e: (0, 0))],
        out_specs=pl.BlockSpec((tm, d), row),
        out_shape=jax.ShapeDtypeStruct((t, d), F32),
        scratch_shapes=[pltpu.VMEM((tm, d), F32)],
        compiler_params=pltpu.CompilerParams(dimension_semantics=("parallel", "arbitrary"),
                                             vmem_limit_bytes=VMEM_LIMIT),
        name="moe",
    )(h2, comb, w_gate, w_up, w_down, x1, mod3, g_post)


def _pick_tile(seq, want):
    return want if seq % want == 0 else seq


def _layer(x, c, positions, w_ada, b_ada, g_pre_mix, g_post_mix, g_pre_ffn, g_post_ffn, w_in, g_cq, w_uq,
           g_ckv, w_ukv, w_o_mla, mu_shift, w_decay_up, decay_bias, w_a_up, a_bias, w_g_up, k_k, k_a, r_k,
           lnx_w, lnx_b, w_o_rwkv, w_out, w_router_group, b_router_group, w_router_expert, b_router_expert,
           w_exp_gate, w_exp_up, w_exp_down):
    batch, seq, d = x.shape
    t = batch * seq
    x2 = x.reshape(t, d)
    row1 = lambda a: a.reshape(1, -1)

    mod3 = _ada(c, w_ada, b_ada).reshape(batch, 6, d)

    zeros = lambda n: jnp.zeros((d, n), F32)
    w_c = jnp.concatenate([w_in[:, :MLA_Q_RANK + MLA_KV_RANK], zeros(MLA_NOPE),
                           w_in[:, MLA_Q_RANK + MLA_KV_RANK:MLA_IN], zeros(HEAD_PAD - MLA_NOPE - MLA_ROPE)],
                          axis=1).astype(BF16)
    w_r = w_in[:, MLA_IN:MLA_IN + RWKV_IN].astype(BF16)
    w_g = w_in[:, MLA_IN + RWKV_IN:].astype(BF16)
    w_uq_p = jnp.pad(w_uq.reshape(MLA_Q_RANK, HEADS, MLA_NOPE + MLA_ROPE),
                     ((0, 0), (0, 0), (0, HEAD_PAD - MLA_NOPE - MLA_ROPE))).reshape(MLA_Q_RANK, QK_WIDTH).astype(BF16)
    w_ukv3 = w_ukv.reshape(MLA_KV_RANK, HEADS, MLA_NOPE + MLA_V)
    w_uk_p = jnp.pad(w_ukv3[..., :MLA_NOPE], ((0, 0), (0, 0), (0, HEAD_PAD - MLA_NOPE))
                     ).reshape(MLA_KV_RANK, QK_WIDTH).astype(BF16)
    w_uv = w_ukv3[..., MLA_NOPE:].reshape(MLA_KV_RANK, HEADS * MLA_V).astype(BF16)
    inv_freq = jnp.power(ROPE_THETA, -jnp.arange(0, MLA_ROPE, 2, dtype=F32) / MLA_ROPE)
    invf = jnp.concatenate([jnp.zeros((MLA_NOPE,), F32), inv_freq, inv_freq,
                            jnp.zeros((HEAD_PAD - MLA_NOPE - MLA_ROPE,), F32)]).reshape(1, LANES)

    tm = _pick_tile(seq, 256)
    q_all, k_all, v_all, zs, gates = _inproj(
        x2, mod3, positions.reshape(t, 1), row1(g_pre_mix), w_c, w_r, w_g, row1(g_cq), row1(g_ckv),
        w_uq_p, w_uk_p, w_uv, row1(mu_shift), invf, batch=batch, seq=seq, tm=tm)

    attn = _attn(q_all, k_all, v_all, batch=batch, seq=seq, tq=_pick_tile(seq, 256))

    tc = _pick_tile(seq, 256)
    w_da = jnp.concatenate([
        jnp.concatenate([w_decay_up, jnp.zeros_like(w_decay_up)], axis=1),
        jnp.concatenate([jnp.zeros_like(w_a_up), w_a_up], axis=1)], axis=0).astype(BF16)
    hid = jnp.arange(RWKV_WIDTH) // RWKV_HEAD
    ones_bd = (hid[:, None] == hid[None, :]).astype(BF16)
    tid = jnp.arange(tc)
    same_chunk = (tid[:, None] // CHUNK) == (tid[None, :] // CHUNK)
    tri = (same_chunk & (tid[None, :] <= tid[:, None])).astype(BF16)
    blk = same_chunk.astype(BF16)
    rw = _rwkv(zs, w_da, w_g_up.astype(BF16), row1(decay_bias), row1(a_bias), row1(k_k), row1(k_a), row1(r_k),
               row1(lnx_w), row1(lnx_b), ones_bd, tri, blk, batch=batch, seq=seq, tc=tc)

    w_rt = jnp.concatenate([w_router_expert, w_router_group,
                            jnp.zeros((d, ROUTER_WIDTH - N_EXPERTS - N_GROUPS), F32)], axis=1)
    wr_hi = w_rt.astype(BF16)
    wr_lo = (w_rt - wr_hi.astype(F32)).astype(BF16)
    b_r = jnp.concatenate([b_router_expert, b_router_group,
                           jnp.zeros((ROUTER_WIDTH - N_EXPERTS - N_GROUPS,), F32)]).reshape(1, ROUTER_WIDTH)
    x1, h2, comb = _merge(attn, rw, gates, x2, mod3, w_o_mla.astype(BF16), w_o_rwkv.astype(BF16),
                          w_out.astype(BF16), row1(g_post_mix), row1(g_pre_ffn), wr_hi, wr_lo, b_r,
                          seq=seq, tm=_pick_tile(seq, 256))

    out = _moe(h2, comb, w_exp_gate.astype(BF16), w_exp_up.astype(BF16), w_exp_down.astype(BF16), x1, mod3,
               row1(g_post_ffn), seq=seq, tm=_pick_tile(seq, 512))
    return out.reshape(batch, seq, d)


def kernel(x, c, positions, w_ada, b_ada, g_pre_mix, g_post_mix, g_pre_ffn, g_post_ffn, w_in, g_cq, w_uq, g_ckv, w_ukv, w_o_mla, mu_shift, w_decay_up, decay_bias, w_a_up, a_bias, w_g_up, k_k, k_a, r_k, lnx_w, lnx_b, w_o_rwkv, w_out, w_router_group, b_router_group, w_router_expert, b_router_expert, w_exp_gate, w_exp_up, w_exp_down):
    depth = w_ada.shape[0]
    for l in range(depth):
        x = _layer(x, c, positions, w_ada[l], b_ada[l], g_pre_mix[l], g_post_mix[l], g_pre_ffn[l], g_post_ffn[l],
                   w_in[l], g_cq[l], w_uq[l], g_ckv[l], w_ukv[l], w_o_mla[l], mu_shift[l], w_decay_up[l],
                   decay_bias[l], w_a_up[l], a_bias[l], w_g_up[l], k_k[l], k_a[l], r_k[l], lnx_w[l], lnx_b[l],
                   w_o_rwkv[l], w_out[l], w_router_group[l], b_router_group[l], w_router_expert[l],
                   b_router_expert[l], w_exp_gate[l], w_exp_up[l], w_exp_down[l])
    return x
```

```python
import functools
import math

import jax
import jax.numpy as jnp
from jax import lax
from jax.experimental import pallas as pl
from jax.experimental.pallas import tpu as pltpu

F32 = jnp.float32
BF16 = jnp.bfloat16

D_MODEL = 1024
CHUNK = 64
HEADS = 8
MLA_NOPE = 64
MLA_ROPE = 32
MLA_V = 64
MLA_Q_RANK = 384
MLA_KV_RANK = 256
ROPE_THETA = 10000.0
RWKV_HEAD = 64
RWKV_WIDTH = HEADS * RWKV_HEAD
DECAY_RANK = 64
AAA_RANK = 64
GATE_RANK = 128
GN_EPS = 64e-5
N_GROUPS = 4
EXPERTS_PER_GROUP = 8
N_EXPERTS = N_GROUPS * EXPERTS_PER_GROUP
EXPERT_FF = 256
RMS_EPS = 1e-6
NEG_INF = -1e30
MLA_IN = MLA_Q_RANK + MLA_KV_RANK + MLA_ROPE
RWKV_IN = 3 * RWKV_WIDTH + DECAY_RANK + AAA_RANK + GATE_RANK

LANES = 128
HEAD_PAD = LANES
QK_WIDTH = HEADS * HEAD_PAD
MLA_C_WIDTH = 768
ROUTER_WIDTH = LANES
VMEM_LIMIT = 56 * 1024 * 1024
ATTN_HEADS_PER_STEP = 4


def _dot(a, b):
    return jnp.dot(a, b, preferred_element_type=F32)


def _dot_nt(a, b):
    return lax.dot_general(a, b, (((1,), (1,)), ((), ())), preferred_element_type=F32)


def _dot_tn(a, b):
    return lax.dot_general(a, b, (((0,), (0,)), ((), ())), preferred_element_type=F32)


def _split(x):
    hi = x.astype(BF16)
    lo = (x - hi.astype(F32)).astype(BF16)
    return hi, lo


def _dot3(x, w_hi, w_lo):
    x_hi, x_lo = _split(x)
    return _dot(x_hi, w_hi) + (_dot(x_hi, w_lo) + _dot(x_lo, w_hi))


def _sigmoid(x):
    return 1.0 / (1.0 + jnp.exp(-x))


def _rms(x, g):
    return x * lax.rsqrt(jnp.mean(x * x, axis=-1, keepdims=True) + RMS_EPS) * g


def _ada_kernel(c_ref, w_ref, b_ref, o_ref):
    c = c_ref[...]
    s = c * _sigmoid(c)
    w_hi, w_lo = _split(w_ref[...])
    o_ref[...] = _dot3(s, w_hi, w_lo) + b_ref[...]


def _ada(c, w_ada, b_ada):
    b, d = c.shape
    n = w_ada.shape[1]
    tn = 512
    return pl.pallas_call(
        _ada_kernel,
        grid=(n // tn,),
        in_specs=[pl.BlockSpec((b, d), lambda j: (0, 0)),
                  pl.BlockSpec((d, tn), lambda j: (0, j)),
                  pl.BlockSpec((1, tn), lambda j: (0, j))],
        out_specs=pl.BlockSpec((b, tn), lambda j: (0, j)),
        out_shape=jax.ShapeDtypeStruct((b, n), F32),
        name="ada",
    )(c, w_ada, b_ada.reshape(1, n))


def _inproj_kernel(x_ref, mod_ref, pos_ref, gpre_ref, wc_ref, wr_ref, wg_ref, gcq_ref, gckv_ref,
                   wuq_ref, wuk_ref, wuv_ref, mu_ref, invf_ref,
                   q_ref, k_ref, v_ref, zs_ref, gate_ref, carry_ref):
    i = pl.program_id(1)
    tm = x_ref.shape[0]
    mod = mod_ref[0]
    shift, scale = mod[0:1], mod[1:2]
    h = (_rms(x_ref[...], gpre_ref[...]) * (1.0 + scale) + shift).astype(BF16)

    zc = _dot(h, wc_ref[...])
    nq = _rms(zc[:, :MLA_Q_RANK], gcq_ref[...]).astype(BF16)
    nkv = _rms(zc[:, MLA_Q_RANK:MLA_Q_RANK + MLA_KV_RANK], gckv_ref[...]).astype(BF16)
    kr = zc[:, MLA_Q_RANK + MLA_KV_RANK:]
    qf = _dot(nq, wuq_ref[...])
    kn = _dot(nkv, wuk_ref[...])
    v_ref[...] = _dot(nkv, wuv_ref[...]).astype(BF16)

    ang = pos_ref[...].astype(F32) * invf_ref[...]
    lane = lax.broadcasted_iota(jnp.int32, (1, LANES), 1)
    first_half = lane < MLA_NOPE + MLA_ROPE // 2
    in_rope = (lane >= MLA_NOPE) & (lane < MLA_NOPE + MLA_ROPE)
    cos_t = jnp.where(in_rope, jnp.cos(ang), 1.0)
    sin_a = jnp.sin(ang)
    sin_t = jnp.where(in_rope, jnp.where(first_half, -sin_a, sin_a), 0.0)

    def rope(t):
        rot = jnp.where(first_half, pltpu.roll(t, LANES - MLA_ROPE // 2, 1), pltpu.roll(t, MLA_ROPE // 2, 1))
        return t * cos_t + rot * sin_t

    kr_rot = rope(kr)
    q_scale = 1.0 / math.sqrt(MLA_NOPE + MLA_ROPE)
    for hh in range(HEADS):
        sl = slice(hh * HEAD_PAD, (hh + 1) * HEAD_PAD)
        q_ref[:, sl] = (rope(qf[:, sl]) * q_scale).astype(BF16)
        k_ref[:, sl] = (kn[:, sl] + kr_rot).astype(BF16)

    zr = _dot(h, wr_ref[...])

    @pl.when(i == 0)
    def _():
        carry_ref[...] = jnp.zeros_like(carry_ref)

    row = lax.broadcasted_iota(jnp.int32, (tm, 1), 0)
    prev = jnp.where(row == 0, carry_ref[0:1, :], pltpu.roll(zr, 1, 0))
    carry_ref[0:1, :] = zr[tm - 1:tm, :]
    zs_ref[...] = (zr + (prev - zr) * mu_ref[...]).astype(BF16)

    gate_ref[...] = _sigmoid(_dot(h, wg_ref[...])).astype(BF16)


def _const_spec(shape):
    nd = len(shape)
    return pl.BlockSpec(shape, lambda *_: (0,) * nd, pipeline_mode=pl.Buffered(1))


def _inproj(x2, mod3, pos2, g_pre, w_c, w_r, w_g, g_cq, g_ckv, w_uq, w_uk, w_uv, mu, invf, *, batch, seq, tm):
    t, d = x2.shape
    ns = seq // tm
    row = lambda b, i: (b * ns + i, 0)
    outs = [(QK_WIDTH, BF16), (QK_WIDTH, BF16), (HEADS * MLA_V, BF16), (RWKV_IN, BF16), (2 * d, BF16)]
    return pl.pallas_call(
        _inproj_kernel,
        grid=(batch, ns),
        in_specs=[pl.BlockSpec((tm, d), row),
                  pl.BlockSpec((1, 6, d), lambda b, i: (b, 0, 0)),
                  pl.BlockSpec((tm, 1), row),
                  _const_spec(g_pre.shape), _const_spec(w_c.shape), _const_spec(w_r.shape),
                  _const_spec(w_g.shape), _const_spec(g_cq.shape), _const_spec(g_ckv.shape),
                  _const_spec(w_uq.shape), _const_spec(w_uk.shape), _const_spec(w_uv.shape),
                  _const_spec(mu.shape), _const_spec(invf.shape)],
        out_specs=[pl.BlockSpec((tm, w), row) for w, _ in outs],
        out_shape=[jax.ShapeDtypeStruct((t, w), dt) for w, dt in outs],
        scratch_shapes=[pltpu.VMEM((8, RWKV_IN), F32)],
        compiler_params=pltpu.CompilerParams(dimension_semantics=("parallel", "arbitrary"),
                                             vmem_limit_bytes=VMEM_LIMIT),
        name="inproj",
    )(x2, mod3, pos2, g_pre, w_c, w_r, w_g, g_cq, g_ckv, w_uq, w_uk, w_uv, mu, invf)


def _attn_kernel(q_ref, k_ref, v_ref, o_ref):
    i = pl.program_id(2)
    tq = q_ref.shape[0]
    lane = lax.broadcasted_iota(jnp.int32, (1, LANES), 1)
    rq = lax.broadcasted_iota(jnp.int32, (tq, tq), 0) // CHUNK
    ck = lax.broadcasted_iota(jnp.int32, (tq, tq), 1) // CHUNK
    diag_mask = ck <= rq
    heads = range(ATTN_HEADS_PER_STEP)
    hs = [slice(h * HEAD_PAD, (h + 1) * HEAD_PAD) for h in heads]
    vs = [slice((h // 2) * LANES, (h // 2 + 1) * LANES) for h in heads]
    q = [q_ref[:, s] for s in hs]

    def block(kb, carry, masked):
        rows = pl.ds(pl.multiple_of(kb * tq, tq), tq)
        s = [_dot_nt(q[h], k_ref[rows, hs[h]]) for h in heads]
        if masked:
            s = [jnp.where(diag_mask, t, NEG_INF) for t in s]
        m_new = [jnp.maximum(carry[h][0], jnp.max(s[h], axis=-1, keepdims=True)) for h in heads]
        alpha = [jnp.exp(carry[h][0] - m_new[h]) for h in heads]
        p = [jnp.exp(s[h] - m_new[h]) for h in heads]
        l = [alpha[h] * carry[h][1] + jnp.sum(p[h], axis=-1, keepdims=True) for h in heads]
        acc = [alpha[h] * carry[h][2] + _dot(p[h].astype(BF16), v_ref[rows, vs[h]]) for h in heads]
        return tuple((m_new[h], l[h], acc[h]) for h in heads)

    init = tuple((jnp.full((tq, 1), NEG_INF, F32), jnp.zeros((tq, 1), F32), jnp.zeros((tq, LANES), F32))
                 for _ in heads)
    carry = lax.fori_loop(0, i, functools.partial(block, masked=False), init)
    final = block(i, carry, True)
    outs = [final[h][2] / final[h][1] for h in heads]
    for pr in range(ATTN_HEADS_PER_STEP // 2):
        o_ref[:, pr * LANES:(pr + 1) * LANES] = jnp.where(lane < MLA_V, outs[2 * pr], outs[2 * pr + 1]).astype(BF16)


def _attn(q_all, k_all, v_all, *, batch, seq, tq):
    t = q_all.shape[0]
    nq = seq // tq
    g = ATTN_HEADS_PER_STEP
    return pl.pallas_call(
        _attn_kernel,
        grid=(batch, HEADS // g, nq),
        in_specs=[pl.BlockSpec((tq, g * HEAD_PAD), lambda b, p, i: (b * nq + i, p)),
                  pl.BlockSpec((seq, g * HEAD_PAD), lambda b, p, i: (b, p)),
                  pl.BlockSpec((seq, g * MLA_V), lambda b, p, i: (b, p))],
        out_specs=pl.BlockSpec((tq, g * MLA_V), lambda b, p, i: (b * nq + i, p)),
        out_shape=jax.ShapeDtypeStruct((t, HEADS * MLA_V), BF16),
        compiler_params=pltpu.CompilerParams(dimension_semantics=("parallel", "parallel", "arbitrary"),
                                             vmem_limit_bytes=VMEM_LIMIT),
        name="attn",
    )(q_all, k_all, v_all)


def _rwkv_kernel(zs_ref, wda_ref, wgu_ref, dbias_ref, abias_ref, kk_ref, ka_ref, rk_ref, lnw_ref, lnb_ref,
                 ones_ref, tri_ref, blk_ref,
                 o_ref,
                 st_ref, a_s, r_s, b_s, k_s, bg_s, kg_s, v_s, gc_s, y_s, bonus_s, g_s):
    i = pl.program_id(1)
    tc = zs_ref.shape[0]
    w = RWKV_WIDTH

    @pl.when(i == 0)
    def _():
        st_ref[...] = jnp.zeros_like(st_ref)

    ones_bd = ones_ref[...]

    def head_sum(t):
        return _dot(t.astype(BF16), ones_bd)

    zs = zs_ref[...].astype(F32)
    r, k, v = zs[:, 0:w], zs[:, w:2 * w], zs[:, 2 * w:3 * w]
    da = zs[:, 3 * w:3 * w + LANES]
    gd = zs[:, 3 * w + LANES:]
    lane = lax.broadcasted_iota(jnp.int32, (1, LANES), 1)
    lora_in = jnp.where(lane < DECAY_RANK, jnp.tanh(da), da).astype(BF16)
    pre = _dot(lora_in, wda_ref[...])
    u = -(dbias_ref[...] + pre[:, :w])
    softplus = jnp.maximum(u, 0.0) + jnp.log(1.0 + jnp.exp(-jnp.abs(u)))
    logw = -jnp.exp(-softplus - 0.5)
    eta = _sigmoid(abias_ref[...] + pre[:, w:])
    g_s[...] = _dot(_sigmoid(gd).astype(BF16), wgu_ref[...])
    kk = k * kk_ref[...]
    kk = kk / jnp.maximum(jnp.sqrt(head_sum(kk * kk)), 1e-12)
    kp = k * (1.0 + (eta - 1.0) * ka_ref[...])
    bonus_s[...] = head_sum(r * kp * rk_ref[...]) * v

    lw_hi, lw_lo = _split(logw)
    cum = _dot(tri_ref[...], lw_hi) + _dot(tri_ref[...], lw_lo)
    cum_end = _dot(blk_ref[...], lw_hi) + _dot(blk_ref[...], lw_lo)
    b_in = kk * eta
    a_s[...] = (-kk * jnp.exp(cum - logw)).astype(BF16)
    r_s[...] = (r * jnp.exp(cum)).astype(BF16)
    g_inv = jnp.exp(-cum)
    b_s[...] = (b_in * g_inv).astype(BF16)
    k_s[...] = (kp * g_inv).astype(BF16)
    g_rem = jnp.exp(cum_end - cum)
    bg_s[...] = (b_in * g_rem).astype(BF16)
    kg_s[...] = (kp * g_rem).astype(BF16)
    v_s[...] = v.astype(BF16)
    gc_s[...] = jnp.exp(cum_end)

    n2 = 2 * CHUNK
    ri = lax.broadcasted_iota(jnp.int32, (4 * CHUNK, n2), 0)
    ci = lax.broadcasted_iota(jnp.int32, (4 * CHUNK, n2), 1) % CHUNK
    tri_mask = (ri % CHUNK + ri // n2) > ci
    lane2 = lax.broadcasted_iota(jnp.int32, (CHUNK, n2), 1)
    h1 = lane2 < RWKV_HEAD
    bi = lax.broadcasted_iota(jnp.int32, (n2, n2), 0) // RWKV_HEAD
    bj = lax.broadcasted_iota(jnp.int32, (n2, n2), 1) // RWKV_HEAD
    bd_mask = bi == bj
    zero_bf = jnp.zeros((CHUNK, n2), BF16)
    zero_f = jnp.zeros((CHUNK, n2), F32)

    def chunk(c, carry):
        rows = pl.ds(pl.multiple_of(c * CHUNK, CHUNK), CHUNK)
        pairs = range(HEADS // 2)
        lns = [slice(p * n2, (p + 1) * n2) for p in pairs]
        a_t = [a_s[rows, ln] for ln in lns]
        r_t = [r_s[rows, ln] for ln in lns]
        v_t = [v_s[rows, ln] for ln in lns]
        ml = []
        for p in pairs:
            lhs = jnp.concatenate([jnp.where(h1, a_t[p], zero_bf), jnp.where(h1, zero_bf, a_t[p]),
                                   jnp.where(h1, r_t[p], zero_bf), jnp.where(h1, zero_bf, r_t[p])], axis=0)
            rhs = jnp.concatenate([b_s[rows, lns[p]], k_s[rows, lns[p]]], axis=0)
            ml.append(jnp.where(tri_mask, _dot_nt(lhs, rhs), 0.0))
        ml_b = [m.astype(BF16) for m in ml]
        units = [(p, hh) for p in pairs for hh in range(2)]
        lp = [ml[p][hh * CHUNK:(hh + 1) * CHUNK, 0:CHUNK] for p, hh in units]
        x = [jnp.concatenate([a_t[p].astype(F32),
                              _dot(ml_b[p][hh * CHUNK:(hh + 1) * CHUNK, :],
                                   jnp.concatenate([zero_bf, v_t[p]], axis=0))], axis=1)
             for p, hh in units]
        for j in range(6):
            lb = [t.astype(BF16) for t in lp]
            x = [x[n] + _dot(lb[n], x[n].astype(BF16)) for n in range(len(units))]
            if j < 5:
                lp = [_dot(t, t) for t in lb]
        z, o0, o1 = [], [], []
        for p in pairs:
            x0, x1 = x[2 * p], x[2 * p + 1]
            a_hat = jnp.where(h1, x0[:, :n2], x1[:, :n2])
            u0 = jnp.where(h1, x0[:, n2:], x1[:, n2:])
            z.append(jnp.concatenate([jnp.concatenate([a_hat, u0], axis=1),
                                      jnp.concatenate([zero_f, v_t[p].astype(F32)], axis=1)],
                                     axis=0).astype(BF16))
        for p in pairs:
            o0.append(_dot(ml_b[p][2 * CHUNK:3 * CHUNK, :], z[p]))
            o1.append(_dot(ml_b[p][3 * CHUNK:4 * CHUNK, :], z[p]))
        pm = [_dot_tn(z[p], jnp.concatenate([bg_s[rows, lns[p]], kg_s[rows, lns[p]]], axis=0)) for p in pairs]
        for p in pairs:
            r_hat = r_t[p].astype(F32) + jnp.where(h1, o0[p][:, :n2], o1[p][:, :n2])
            y0 = jnp.where(h1, o0[p][:, n2:], o1[p][:, n2:])
            g_t = jnp.where(bd_mask, pm[p][:n2], 0.0)
            h_t = jnp.where(bd_mask, pm[p][n2:], 0.0)
            s = st_ref[p]
            s_b = s.astype(BF16)
            y_s[rows, lns[p]] = _dot_nt(r_hat.astype(BF16), s_b) + y0
            gc = gc_s[pl.ds(pl.multiple_of(c * CHUNK, CHUNK), 8), lns[p]][0:1]
            st_ref[p] = gc * s + _dot(s_b, g_t.astype(BF16)) + h_t
        return carry

    lax.fori_loop(0, tc // CHUNK, chunk, 0)

    y = y_s[...]
    inv_n = 1.0 / RWKV_HEAD
    dev = y - head_sum(y) * inv_n
    var = head_sum(dev * dev) * inv_n
    out = (dev * lax.rsqrt(var + GN_EPS) * lnw_ref[...] + lnb_ref[...] + bonus_s[...]) * g_s[...]
    o_ref[...] = out.astype(BF16)


def _rwkv(zs, w_da, w_gu, dbias, abias, k_k, k_a, r_k, lnw, lnb, ones_bd, tri, blk, *, batch, seq, tc):
    t = zs.shape[0]
    nc = seq // tc
    w = RWKV_WIDTH
    consts = [w_da, w_gu, dbias, abias, k_k, k_a, r_k, lnw, lnb, ones_bd, tri, blk]
    return pl.pallas_call(
        _rwkv_kernel,
        grid=(batch, nc),
        in_specs=[pl.BlockSpec((tc, RWKV_IN), lambda b, i: (b * nc + i, 0))] + [_const_spec(a.shape) for a in consts],
        out_specs=pl.BlockSpec((tc, w), lambda b, i: (b * nc + i, 0)),
        out_shape=jax.ShapeDtypeStruct((t, w), BF16),
        scratch_shapes=[pltpu.VMEM((HEADS // 2, 2 * RWKV_HEAD, 2 * RWKV_HEAD), F32)]
        + [pltpu.VMEM((tc, w), BF16)] * 7
        + [pltpu.VMEM((tc, w), F32)] * 4,
        compiler_params=pltpu.CompilerParams(dimension_semantics=("parallel", "arbitrary"),
                                             vmem_limit_bytes=VMEM_LIMIT),
        name="rwkv",
    )(zs, *consts)


def _merge_kernel(attn_ref, rw_ref, gate_ref, x_ref, mod_ref, womla_ref, worw_ref, wout_ref,
                  gpost_ref, gffn_ref, wrh_ref, wrl_ref, br_ref,
                  x1_ref, h2_ref, comb_ref):
    d = x_ref.shape[1]
    mod = mod_ref[0]
    gate1, shift2, scale2 = mod[2:3], mod[3:4], mod[4:5]
    gates = gate_ref[...].astype(F32)
    o = gates[:, :d] * _dot(attn_ref[...], womla_ref[...]) + gates[:, d:] * _dot(rw_ref[...], worw_ref[...])
    y = _dot(o.astype(BF16), wout_ref[...])
    x1 = x_ref[...] + gate1 * _rms(y, gpost_ref[...])
    x1_ref[...] = x1
    h2 = _rms(x1, gffn_ref[...]) * (1.0 + scale2) + shift2
    h2_ref[...] = h2.astype(BF16)

    logits = _dot3(h2, wrh_ref[...], wrl_ref[...]) + br_ref[...]
    tm = logits.shape[0]
    lane = lax.broadcasted_iota(jnp.int32, (tm, ROUTER_WIDTH), 1)
    lane_f = lane.astype(F32)
    big = float(ROUTER_WIDTH)
    is_group = (lane >= N_EXPERTS) & (lane < N_EXPERTS + N_GROUPS)
    gl = jnp.where(is_group, logits, NEG_INF)
    g_max = jnp.max(gl, axis=-1, keepdims=True)
    g_sel = jnp.min(jnp.where(gl == g_max, lane_f, big), axis=-1, keepdims=True) - float(N_EXPERTS)
    p_sel = 1.0 / jnp.sum(jnp.where(is_group, jnp.exp(gl - g_max), 0.0), axis=-1, keepdims=True)
    lane_group = (lane // EXPERTS_PER_GROUP).astype(F32)
    in_group = (lane < N_EXPERTS) & (lane_group == g_sel)
    el = jnp.where(in_group, logits, NEG_INF)
    t1 = jnp.max(el, axis=-1, keepdims=True)
    i1 = jnp.min(jnp.where(el == t1, lane_f, big), axis=-1, keepdims=True)
    el2 = jnp.where(lane_f == i1, NEG_INF, el)
    t2 = jnp.max(el2, axis=-1, keepdims=True)
    i2 = jnp.min(jnp.where(el2 == t2, lane_f, big), axis=-1, keepdims=True)
    e21 = jnp.exp(t2 - t1)
    w1 = p_sel / (1.0 + e21)
    w2 = w1 * e21
    comb_ref[...] = jnp.where(lane_f == i1, w1, 0.0) + jnp.where(lane_f == i2, w2, 0.0)


def _merge(attn, rw, gates, x2, mod3, w_o_mla, w_o_rwkv, w_out, g_post, g_ffn, wr_hi, wr_lo, b_r, *, seq, tm):
    t, d = x2.shape
    ns = seq // tm
    row = lambda i: (i, 0)
    consts = [w_o_mla, w_o_rwkv, w_out, g_post, g_ffn, wr_hi, wr_lo, b_r]
    return pl.pallas_call(
        _merge_kernel,
        grid=(t // tm,),
        in_specs=[pl.BlockSpec((tm, attn.shape[1]), row), pl.BlockSpec((tm, rw.shape[1]), row),
                  pl.BlockSpec((tm, 2 * d), row), pl.BlockSpec((tm, d), row),
                  pl.BlockSpec((1, 6, d), lambda i: (i // ns, 0, 0))] + [_const_spec(a.shape) for a in consts],
        out_specs=[pl.BlockSpec((tm, d), row), pl.BlockSpec((tm, d), row), pl.BlockSpec((tm, ROUTER_WIDTH), row)],
        out_shape=[jax.ShapeDtypeStruct((t, d), F32), jax.ShapeDtypeStruct((t, d), BF16),
                   jax.ShapeDtypeStruct((t, ROUTER_WIDTH), F32)],
        compiler_params=pltpu.CompilerParams(dimension_semantics=("parallel",), vmem_limit_bytes=VMEM_LIMIT),
        name="merge",
    )(attn, rw, gates, x2, mod3, *consts)


def _moe_kernel(h_ref, comb_ref, wg_ref, wu_ref, wd_ref, x1_ref, mod_ref, gpost_ref, o_ref, acc_ref):
    e = pl.program_id(1)

    @pl.when(e == 0)
    def _():
        acc_ref[...] = jnp.zeros_like(acc_ref)

    h = h_ref[...]
    a = _dot(h, wg_ref[0])
    act = (a * _sigmoid(a) * _dot(h, wu_ref[0])).astype(BF16)
    comb = comb_ref[...]
    lane = lax.broadcasted_iota(jnp.int32, comb.shape, 1)
    ce = jnp.sum(jnp.where(lane == e, comb, 0.0), axis=-1, keepdims=True)
    acc_ref[...] += ce * _dot(act, wd_ref[0])

    @pl.when(e == pl.num_programs(1) - 1)
    def _():
        gate2 = mod_ref[0][5:6]
        o_ref[...] = x1_ref[...] + gate2 * _rms(acc_ref[...], gpost_ref[...])


def _moe(h2, comb, w_gate, w_up, w_down, x1, mod3, g_post, *, seq, tm):
    t, d = x1.shape
    ns = seq // tm
    row = lambda i, e: (i, 0)
    return pl.pallas_call(
        _moe_kernel,
        grid=(t // tm, N_EXPERTS),
        in_specs=[pl.BlockSpec((tm, d), row), pl.BlockSpec((tm, ROUTER_WIDTH), row),
                  pl.BlockSpec((1, d, EXPERT_FF), lambda i, e: (e, 0, 0)),
                  pl.BlockSpec((1, d, EXPERT_FF), lambda i, e: (e, 0, 0)),
                  pl.BlockSpec((1, EXPERT_FF, d), lambda i, e: (e, 0, 0)),
                  pl.BlockSpec((tm, d), row),
                  pl.BlockSpec((1, 6, d), lambda i, e: (i // ns, 0, 0)),
                  pl.BlockSpec((1, d), lambda i, e: (0, 0))],
        out_specs=pl.BlockSpec((tm, d), row),
        out_shape=jax.ShapeDtypeStruct((t, d), F32),
        scratch_shapes=[pltpu.VMEM((tm, d), F32)],
        compiler_params=pltpu.CompilerParams(dimension_semantics=("parallel", "arbitrary"),
                                             vmem_limit_bytes=VMEM_LIMIT),
        name="moe",
    )(h2, comb, w_gate, w_up, w_down, x1, mod3, g_post)


def _pick_tile(seq, want):
    return want if seq % want == 0 else seq


def _layer(x, c, positions, w_ada, b_ada, g_pre_mix, g_post_mix, g_pre_ffn, g_post_ffn, w_in, g_cq, w_uq,
           g_ckv, w_ukv, w_o_mla, mu_shift, w_decay_up, decay_bias, w_a_up, a_bias, w_g_up, k_k, k_a, r_k,
           lnx_w, lnx_b, w_o_rwkv, w_out, w_router_group, b_router_group, w_router_expert, b_router_expert,
           w_exp_gate, w_exp_up, w_exp_down):
    batch, seq, d = x.shape
    t = batch * seq
    x2 = x.reshape(t, d)
    row1 = lambda a: a.reshape(1, -1)

    mod3 = _ada(c, w_ada, b_ada).reshape(batch, 6, d)

    zeros = lambda n: jnp.zeros((d, n), F32)
    w_c = jnp.concatenate([w_in[:, :MLA_Q_RANK + MLA_KV_RANK], zeros(MLA_NOPE),
                           w_in[:, MLA_Q_RANK + MLA_KV_RANK:MLA_IN], zeros(HEAD_PAD - MLA_NOPE - MLA_ROPE)],
                          axis=1).astype(BF16)
    w_r = w_in[:, MLA_IN:MLA_IN + RWKV_IN].astype(BF16)
    w_g = w_in[:, MLA_IN + RWKV_IN:].astype(BF16)
    w_uq_p = jnp.pad(w_uq.reshape(MLA_Q_RANK, HEADS, MLA_NOPE + MLA_ROPE),
                     ((0, 0), (0, 0), (0, HEAD_PAD - MLA_NOPE - MLA_ROPE))).reshape(MLA_Q_RANK, QK_WIDTH).astype(BF16)
    w_ukv3 = w_ukv.reshape(MLA_KV_RANK, HEADS, MLA_NOPE + MLA_V)
    w_uk_p = jnp.pad(w_ukv3[..., :MLA_NOPE], ((0, 0), (0, 0), (0, HEAD_PAD - MLA_NOPE))
                     ).reshape(MLA_KV_RANK, QK_WIDTH).astype(BF16)
    w_uv = w_ukv3[..., MLA_NOPE:].reshape(MLA_KV_RANK, HEADS * MLA_V).astype(BF16)
    inv_freq = jnp.power(ROPE_THETA, -jnp.arange(0, MLA_ROPE, 2, dtype=F32) / MLA_ROPE)
    invf = jnp.concatenate([jnp.zeros((MLA_NOPE,), F32), inv_freq, inv_freq,
                            jnp.zeros((HEAD_PAD - MLA_NOPE - MLA_ROPE,), F32)]).reshape(1, LANES)

    tm = _pick_tile(seq, 256)
    q_all, k_all, v_all, zs, gates = _inproj(
        x2, mod3, positions.reshape(t, 1), row1(g_pre_mix), w_c, w_r, w_g, row1(g_cq), row1(g_ckv),
        w_uq_p, w_uk_p, w_uv, row1(mu_shift), invf, batch=batch, seq=seq, tm=tm)

    attn = _attn(q_all, k_all, v_all, batch=batch, seq=seq, tq=_pick_tile(seq, 256))

    tc = _pick_tile(seq, 256)
    w_da = jnp.concatenate([
        jnp.concatenate([w_decay_up, jnp.zeros_like(w_decay_up)], axis=1),
        jnp.concatenate([jnp.zeros_like(w_a_up), w_a_up], axis=1)], axis=0).astype(BF16)
    hid = jnp.arange(RWKV_WIDTH) // RWKV_HEAD
    ones_bd = (hid[:, None] == hid[None, :]).astype(BF16)
    tid = jnp.arange(tc)
    same_chunk = (tid[:, None] // CHUNK) == (tid[None, :] // CHUNK)
    tri = (same_chunk & (tid[None, :] <= tid[:, None])).astype(BF16)
    blk = same_chunk.astype(BF16)
    rw = _rwkv(zs, w_da, w_g_up.astype(BF16), row1(decay_bias), row1(a_bias), row1(k_k), row1(k_a), row1(r_k),
               row1(lnx_w), row1(lnx_b), ones_bd, tri, blk, batch=batch, seq=seq, tc=tc)

    w_rt = jnp.concatenate([w_router_expert, w_router_group,
                            jnp.zeros((d, ROUTER_WIDTH - N_EXPERTS - N_GROUPS), F32)], axis=1)
    wr_hi = w_rt.astype(BF16)
    wr_lo = (w_rt - wr_hi.astype(F32)).astype(BF16)
    b_r = jnp.concatenate([b_router_expert, b_router_group,
                           jnp.zeros((ROUTER_WIDTH - N_EXPERTS - N_GROUPS,), F32)]).reshape(1, ROUTER_WIDTH)
    x1, h2, comb = _merge(attn, rw, gates, x2, mod3, w_o_mla.astype(BF16), w_o_rwkv.astype(BF16),
                          w_out.astype(BF16), row1(g_post_mix), row1(g_pre_ffn), wr_hi, wr_lo, b_r,
                          seq=seq, tm=_pick_tile(seq, 256))

    out = _moe(h2, comb, w_exp_gate.astype(BF16), w_exp_up.astype(BF16), w_exp_down.astype(BF16), x1, mod3,
               row1(g_post_ffn), seq=seq, tm=_pick_tile(seq, 512))
    return out.reshape(batch, seq, d)


def kernel(x, c, positions, w_ada, b_ada, g_pre_mix, g_post_mix, g_pre_ffn, g_post_ffn, w_in, g_cq, w_uq, g_ckv, w_ukv, w_o_mla, mu_shift, w_decay_up, decay_bias, w_a_up, a_bias, w_g_up, k_k, k_a, r_k, lnx_w, lnx_b, w_o_rwkv, w_out, w_router_group, b_router_group, w_router_expert, b_router_expert, w_exp_gate, w_exp_up, w_exp_down):
    depth = w_ada.shape[0]
    for l in range(depth):
        x = _layer(x, c, positions, w_ada[l], b_ada[l], g_pre_mix[l], g_post_mix[l], g_pre_ffn[l], g_post_ffn[l],
                   w_in[l], g_cq[l], w_uq[l], g_ckv[l], w_ukv[l], w_o_mla[l], mu_shift[l], w_decay_up[l],
                   decay_bias[l], w_a_up[l], a_bias[l], w_g_up[l], k_k[l], k_a[l], r_k[l], lnx_w[l], lnx_b[l],
                   w_o_rwkv[l], w_out[l], w_router_group[l], b_router_group[l], w_router_expert[l],
                   b_router_expert[l], w_exp_gate[l], w_exp_up[l], w_exp_down[l])
    return x
```

```python
import functools
import math

import jax
import jax.numpy as jnp
from jax import lax
from jax.experimental import pallas as pl
from jax.experimental.pallas import tpu as pltpu

F32 = jnp.float32
BF16 = jnp.bfloat16

D_MODEL = 1024
CHUNK = 64
HEADS = 8
MLA_NOPE = 64
MLA_ROPE = 32
MLA_V = 64
MLA_Q_RANK = 384
MLA_KV_RANK = 256
ROPE_THETA = 10000.0
RWKV_HEAD = 64
RWKV_WIDTH = HEADS * RWKV_HEAD
DECAY_RANK = 64
AAA_RANK = 64
GATE_RANK = 128
GN_EPS = 64e-5
N_GROUPS = 4
EXPERTS_PER_GROUP = 8
N_EXPERTS = N_GROUPS * EXPERTS_PER_GROUP
EXPERT_FF = 256
RMS_EPS = 1e-6
NEG_INF = -1e30
MLA_IN = MLA_Q_RANK + MLA_KV_RANK + MLA_ROPE
RWKV_IN = 3 * RWKV_WIDTH + DECAY_RANK + AAA_RANK + GATE_RANK

LANES = 128
HEAD_PAD = LANES
QK_WIDTH = HEADS * HEAD_PAD
MLA_C_WIDTH = 768
ROUTER_WIDTH = LANES
VMEM_LIMIT = 56 * 1024 * 1024
ATTN_HEADS_PER_STEP = 4


def _dot(a, b):
    return jnp.dot(a, b, preferred_element_type=F32)


def _dot_nt(a, b):
    return lax.dot_general(a, b, (((1,), (1,)), ((), ())), preferred_element_type=F32)


def _dot_tn(a, b):
    return lax.dot_general(a, b, (((0,), (0,)), ((), ())), preferred_element_type=F32)


def _split(x):
    hi = x.astype(BF16)
    lo = (x - hi.astype(F32)).astype(BF16)
    return hi, lo


def _dot3(x, w_hi, w_lo):
    x_hi, x_lo = _split(x)
    return _dot(x_hi, w_hi) + (_dot(x_hi, w_lo) + _dot(x_lo, w_hi))


def _sigmoid(x):
    return 1.0 / (1.0 + jnp.exp(-x))


def _rms(x, g):
    return x * lax.rsqrt(jnp.mean(x * x, axis=-1, keepdims=True) + RMS_EPS) * g


def _ada_kernel(c_ref, w_ref, b_ref, o_ref):
    c = c_ref[...]
    s = c * _sigmoid(c)
    w_hi, w_lo = _split(w_ref[...])
    o_ref[...] = _dot3(s, w_hi, w_lo) + b_ref[...]


def _ada(c, w_ada, b_ada):
    b, d = c.shape
    n = w_ada.shape[1]
    tn = 512
    return pl.pallas_call(
        _ada_kernel,
        grid=(n // tn,),
        in_specs=[pl.BlockSpec((b, d), lambda j: (0, 0)),
                  pl.BlockSpec((d, tn), lambda j: (0, j)),
                  pl.BlockSpec((1, tn), lambda j: (0, j))],
        out_specs=pl.BlockSpec((b, tn), lambda j: (0, j)),
        out_shape=jax.ShapeDtypeStruct((b, n), F32),
        name="ada",
    )(c, w_ada, b_ada.reshape(1, n))


def _inproj_kernel(x_ref, mod_ref, pos_ref, gpre_ref, wc_ref, wr_ref, wg_ref, gcq_ref, gckv_ref,
                   wuq_ref, wuk_ref, wuv_ref, mu_ref, invf_ref,
                   q_ref, k_ref, vt_ref, zs_ref, gate_ref, carry_ref):
    i = pl.program_id(1)
    tm = x_ref.shape[0]
    mod = mod_ref[0]
    shift, scale = mod[0:1], mod[1:2]
    h = (_rms(x_ref[...], gpre_ref[...]) * (1.0 + scale) + shift).astype(BF16)

    zc = _dot(h, wc_ref[...])
    nq = _rms(zc[:, :MLA_Q_RANK], gcq_ref[...]).astype(BF16)
    nkv = _rms(zc[:, MLA_Q_RANK:MLA_Q_RANK + MLA_KV_RANK], gckv_ref[...]).astype(BF16)
    kr = zc[:, MLA_Q_RANK + MLA_KV_RANK:]
    qf = _dot(nq, wuq_ref[...])
    kn = _dot(nkv, wuk_ref[...])
    vt_ref[...] = _dot_nt(wuv_ref[...], nkv).astype(BF16)

    ang = pos_ref[...].astype(F32) * invf_ref[...]
    lane = lax.broadcasted_iota(jnp.int32, (1, LANES), 1)
    first_half = lane < MLA_NOPE + MLA_ROPE // 2
    in_rope = (lane >= MLA_NOPE) & (lane < MLA_NOPE + MLA_ROPE)
    cos_t = jnp.where(in_rope, jnp.cos(ang), 1.0)
    sin_a = jnp.sin(ang)
    sin_t = jnp.where(in_rope, jnp.where(first_half, -sin_a, sin_a), 0.0)

    def rope(t):
        rot = jnp.where(first_half, pltpu.roll(t, LANES - MLA_ROPE // 2, 1), pltpu.roll(t, MLA_ROPE // 2, 1))
        return t * cos_t + rot * sin_t

    kr_rot = rope(kr)
    q_scale = 1.0 / math.sqrt(MLA_NOPE + MLA_ROPE)
    for hh in range(HEADS):
        sl = slice(hh * HEAD_PAD, (hh + 1) * HEAD_PAD)
        q_ref[:, sl] = (rope(qf[:, sl]) * q_scale).astype(BF16)
        k_ref[:, sl] = (kn[:, sl] + kr_rot).astype(BF16)

    zr = _dot(h, wr_ref[...])

    @pl.when(i == 0)
    def _():
        carry_ref[...] = jnp.zeros_like(carry_ref)

    row = lax.broadcasted_iota(jnp.int32, (tm, 1), 0)
    prev = jnp.where(row == 0, carry_ref[0:1, :], pltpu.roll(zr, 1, 0))
    carry_ref[0:1, :] = zr[tm - 1:tm, :]
    zs_ref[...] = (zr + (prev - zr) * mu_ref[...]).astype(BF16)

    gate_ref[...] = _sigmoid(_dot(h, wg_ref[...])).astype(BF16)


def _const_spec(shape):
    nd = len(shape)
    return pl.BlockSpec(shape, lambda *_: (0,) * nd, pipeline_mode=pl.Buffered(1))


def _inproj(x2, mod3, pos2, g_pre, w_c, w_r, w_g, g_cq, g_ckv, w_uq, w_uk, w_uv, mu, invf, *, batch, seq, tm):
    t, d = x2.shape
    ns = seq // tm
    row = lambda b, i: (b * ns + i, 0)
    widths = [QK_WIDTH, QK_WIDTH, RWKV_IN, 2 * d]
    specs = [pl.BlockSpec((tm, w), row) for w in widths]
    shapes = [jax.ShapeDtypeStruct((t, w), BF16) for w in widths]
    vw = HEADS * MLA_V
    specs.insert(2, pl.BlockSpec((None, None, vw, tm), lambda b, i: (b, i, 0, 0)))
    shapes.insert(2, jax.ShapeDtypeStruct((batch, ns, vw, tm), BF16))
    return pl.pallas_call(
        _inproj_kernel,
        grid=(batch, ns),
        in_specs=[pl.BlockSpec((tm, d), row),
                  pl.BlockSpec((1, 6, d), lambda b, i: (b, 0, 0)),
                  pl.BlockSpec((tm, 1), row),
                  _const_spec(g_pre.shape), _const_spec(w_c.shape), _const_spec(w_r.shape),
                  _const_spec(w_g.shape), _const_spec(g_cq.shape), _const_spec(g_ckv.shape),
                  _const_spec(w_uq.shape), _const_spec(w_uk.shape), _const_spec(w_uv.shape),
                  _const_spec(mu.shape), _const_spec(invf.shape)],
        out_specs=specs,
        out_shape=shapes,
        scratch_shapes=[pltpu.VMEM((8, RWKV_IN), F32)],
        compiler_params=pltpu.CompilerParams(dimension_semantics=("parallel", "arbitrary"),
                                             vmem_limit_bytes=VMEM_LIMIT),
        name="inproj",
    )(x2, mod3, pos2, g_pre, w_c, w_r, w_g, g_cq, g_ckv, w_uq, w_uk, w_uv, mu, invf)


def _attn_kernel(q_ref, k_ref, vt_ref, o_ref):
    i = pl.program_id(2)
    tq = q_ref.shape[0]
    kc = lax.broadcasted_iota(jnp.int32, (tq, tq), 0) // CHUNK
    qc = lax.broadcasted_iota(jnp.int32, (tq, tq), 1) // CHUNK
    diag_mask = kc <= qc
    heads = range(ATTN_HEADS_PER_STEP)
    hs = [slice(h * HEAD_PAD, (h + 1) * HEAD_PAD) for h in heads]
    vr = [slice(h * MLA_V, (h + 1) * MLA_V) for h in heads]
    q = [q_ref[:, s] for s in hs]

    def block(kb, carry, masked):
        rows = pl.ds(pl.multiple_of(kb * tq, tq), tq)
        s = [_dot_nt(k_ref[rows, hs[h]], q[h]) for h in heads]
        if masked:
            s = [jnp.where(diag_mask, t, NEG_INF) for t in s]
        m_new = [jnp.maximum(carry[h][0], jnp.max(s[h], axis=0, keepdims=True)) for h in heads]
        alpha = [jnp.exp(carry[h][0] - m_new[h]) for h in heads]
        p = [jnp.exp(s[h] - m_new[h]) for h in heads]
        l = [alpha[h] * carry[h][1] + jnp.sum(p[h], axis=0, keepdims=True) for h in heads]
        acc = [alpha[h] * carry[h][2] + _dot(vt_ref[kb, vr[h], :], p[h].astype(BF16)) for h in heads]
        return tuple((m_new[h], l[h], acc[h]) for h in heads)

    init = tuple((jnp.full((1, tq), NEG_INF, F32), jnp.zeros((1, tq), F32), jnp.zeros((MLA_V, tq), F32))
                 for _ in heads)
    carry = lax.fori_loop(0, i, functools.partial(block, masked=False), init)
    final = block(i, carry, True)
    outs = [final[h][2] / final[h][1] for h in heads]
    for pr in range(ATTN_HEADS_PER_STEP // 2):
        pair_t = jnp.concatenate([outs[2 * pr], outs[2 * pr + 1]], axis=0)
        o_ref[:, pr * LANES:(pr + 1) * LANES] = pair_t.T.astype(BF16)


def _attn(q_all, k_all, vt_all, *, batch, seq, tq):
    t = q_all.shape[0]
    nq = seq // tq
    g = ATTN_HEADS_PER_STEP
    return pl.pallas_call(
        _attn_kernel,
        grid=(batch, HEADS // g, nq),
        in_specs=[pl.BlockSpec((tq, g * HEAD_PAD), lambda b, p, i: (b * nq + i, p)),
                  pl.BlockSpec((seq, g * HEAD_PAD), lambda b, p, i: (b, p)),
                  pl.BlockSpec((None, nq, g * MLA_V, tq), lambda b, p, i: (b, 0, p, 0))],
        out_specs=pl.BlockSpec((tq, g * MLA_V), lambda b, p, i: (b * nq + i, p)),
        out_shape=jax.ShapeDtypeStruct((t, HEADS * MLA_V), BF16),
        compiler_params=pltpu.CompilerParams(dimension_semantics=("parallel", "parallel", "arbitrary"),
                                             vmem_limit_bytes=VMEM_LIMIT),
        name="attn",
    )(q_all, k_all, vt_all)


def _rwkv_kernel(zs_ref, wda_ref, wgu_ref, dbias_ref, abias_ref, kk_ref, ka_ref, rk_ref, lnw_ref, lnb_ref,
                 ones_ref, tri_ref, blk_ref,
                 o_ref,
                 st_ref, a_s, r_s, b_s, k_s, bg_s, kg_s, v_s, gc_s, y_s, bonus_s, g_s):
    i = pl.program_id(1)
    tc = zs_ref.shape[0]
    w = RWKV_WIDTH

    @pl.when(i == 0)
    def _():
        st_ref[...] = jnp.zeros_like(st_ref)

    ones_bd = ones_ref[...]

    def head_sum(t):
        return _dot(t.astype(BF16), ones_bd)

    zs = zs_ref[...].astype(F32)
    r, k, v = zs[:, 0:w], zs[:, w:2 * w], zs[:, 2 * w:3 * w]
    da = zs[:, 3 * w:3 * w + LANES]
    gd = zs[:, 3 * w + LANES:]
    lane = lax.broadcasted_iota(jnp.int32, (1, LANES), 1)
    lora_in = jnp.where(lane < DECAY_RANK, jnp.tanh(da), da).astype(BF16)
    pre = _dot(lora_in, wda_ref[...])
    u = -(dbias_ref[...] + pre[:, :w])
    softplus = jnp.maximum(u, 0.0) + jnp.log(1.0 + jnp.exp(-jnp.abs(u)))
    logw = -jnp.exp(-softplus - 0.5)
    eta = _sigmoid(abias_ref[...] + pre[:, w:])
    g_s[...] = _dot(_sigmoid(gd).astype(BF16), wgu_ref[...])
    kk = k * kk_ref[...]
    kk = kk / jnp.maximum(jnp.sqrt(head_sum(kk * kk)), 1e-12)
    kp = k * (1.0 + (eta - 1.0) * ka_ref[...])
    bonus_s[...] = head_sum(r * kp * rk_ref[...]) * v

    lw_hi, lw_lo = _split(logw)
    cum = _dot(tri_ref[...], lw_hi) + _dot(tri_ref[...], lw_lo)
    cum_end = _dot(blk_ref[...], lw_hi) + _dot(blk_ref[...], lw_lo)
    b_in = kk * eta
    a_s[...] = (-kk * jnp.exp(cum - logw)).astype(BF16)
    r_s[...] = (r * jnp.exp(cum)).astype(BF16)
    g_inv = jnp.exp(-cum)
    b_s[...] = (b_in * g_inv).astype(BF16)
    k_s[...] = (kp * g_inv).astype(BF16)
    g_rem = jnp.exp(cum_end - cum)
    bg_s[...] = (b_in * g_rem).astype(BF16)
    kg_s[...] = (kp * g_rem).astype(BF16)
    v_s[...] = v.astype(BF16)
    gc_s[...] = jnp.exp(cum_end)

    n2 = 2 * CHUNK
    ri = lax.broadcasted_iota(jnp.int32, (4 * CHUNK, n2), 0)
    ci = lax.broadcasted_iota(jnp.int32, (4 * CHUNK, n2), 1) % CHUNK
    tri_mask = (ri % CHUNK + ri // n2) > ci
    lane2 = lax.broadcasted_iota(jnp.int32, (CHUNK, n2), 1)
    h1 = lane2 < RWKV_HEAD
    bi = lax.broadcasted_iota(jnp.int32, (n2, n2), 0) // RWKV_HEAD
    bj = lax.broadcasted_iota(jnp.int32, (n2, n2), 1) // RWKV_HEAD
    bd_mask = bi == bj
    zero_bf = jnp.zeros((CHUNK, n2), BF16)
    zero_f = jnp.zeros((CHUNK, n2), F32)

    def chunk(c, carry):
        rows = pl.ds(pl.multiple_of(c * CHUNK, CHUNK), CHUNK)
        pairs = range(HEADS // 2)
        lns = [slice(p * n2, (p + 1) * n2) for p in pairs]
        a_t = [a_s[rows, ln] for ln in lns]
        r_t = [r_s[rows, ln] for ln in lns]
        v_t = [v_s[rows, ln] for ln in lns]
        ml = []
        for p in pairs:
            lhs = jnp.concatenate([jnp.where(h1, a_t[p], zero_bf), jnp.where(h1, zero_bf, a_t[p]),
                                   jnp.where(h1, r_t[p], zero_bf), jnp.where(h1, zero_bf, r_t[p])], axis=0)
            rhs = jnp.concatenate([b_s[rows, lns[p]], k_s[rows, lns[p]]], axis=0)
            ml.append(jnp.where(tri_mask, _dot_nt(lhs, rhs), 0.0))
        ml_b = [m.astype(BF16) for m in ml]
        units = [(p, hh) for p in pairs for hh in range(2)]
        lp = [ml[p][hh * CHUNK:(hh + 1) * CHUNK, 0:CHUNK] for p, hh in units]
        x = [jnp.concatenate([a_t[p].astype(F32),
                              _dot(ml_b[p][hh * CHUNK:(hh + 1) * CHUNK, :],
                                   jnp.concatenate([zero_bf, v_t[p]], axis=0))], axis=1)
             for p, hh in units]
        for j in range(6):
            lb = [t.astype(BF16) for t in lp]
            x = [x[n] + _dot(lb[n], x[n].astype(BF16)) for n in range(len(units))]
            if j < 5:
                lp = [_dot(t, t) for t in lb]
        z, o0, o1 = [], [], []
        for p in pairs:
            x0, x1 = x[2 * p], x[2 * p + 1]
            a_hat = jnp.where(h1, x0[:, :n2], x1[:, :n2])
            u0 = jnp.where(h1, x0[:, n2:], x1[:, n2:])
            z.append(jnp.concatenate([jnp.concatenate([a_hat, u0], axis=1),
                                      jnp.concatenate([zero_f, v_t[p].astype(F32)], axis=1)],
                                     axis=0).astype(BF16))
        for p in pairs:
            o0.append(_dot(ml_b[p][2 * CHUNK:3 * CHUNK, :], z[p]))
            o1.append(_dot(ml_b[p][3 * CHUNK:4 * CHUNK, :], z[p]))
        pm = [_dot_tn(z[p], jnp.concatenate([bg_s[rows, lns[p]], kg_s[rows, lns[p]]], axis=0)) for p in pairs]
        for p in pairs:
            r_hat = r_t[p].astype(F32) + jnp.where(h1, o0[p][:, :n2], o1[p][:, :n2])
            y0 = jnp.where(h1, o0[p][:, n2:], o1[p][:, n2:])
            g_t = jnp.where(bd_mask, pm[p][:n2], 0.0)
            h_t = jnp.where(bd_mask, pm[p][n2:], 0.0)
            s = st_ref[p]
            s_b = s.astype(BF16)
            y_s[rows, lns[p]] = _dot_nt(r_hat.astype(BF16), s_b) + y0
            gc = gc_s[pl.ds(pl.multiple_of(c * CHUNK, CHUNK), 8), lns[p]][0:1]
            st_ref[p] = gc * s + _dot(s_b, g_t.astype(BF16)) + h_t
        return carry

    lax.fori_loop(0, tc // CHUNK, chunk, 0)

    y = y_s[...]
    inv_n = 1.0 / RWKV_HEAD
    dev = y - head_sum(y) * inv_n
    var = head_sum(dev * dev) * inv_n
    out = (dev * lax.rsqrt(var + GN_EPS) * lnw_ref[...] + lnb_ref[...] + bonus_s[...]) * g_s[...]
    o_ref[...] = out.astype(BF16)


def _rwkv(zs, w_da, w_gu, dbias, abias, k_k, k_a, r_k, lnw, lnb, ones_bd, tri, blk, *, batch, seq, tc):
    t = zs.shape[0]
    nc = seq // tc
    w = RWKV_WIDTH
    consts = [w_da, w_gu, dbias, abias, k_k, k_a, r_k, lnw, lnb, ones_bd, tri, blk]
    return pl.pallas_call(
        _rwkv_kernel,
        grid=(batch, nc),
        in_specs=[pl.BlockSpec((tc, RWKV_IN), lambda b, i: (b * nc + i, 0))] + [_const_spec(a.shape) for a in consts],
        out_specs=pl.BlockSpec((tc, w), lambda b, i: (b * nc + i, 0)),
        out_shape=jax.ShapeDtypeStruct((t, w), BF16),
        scratch_shapes=[pltpu.VMEM((HEADS // 2, 2 * RWKV_HEAD, 2 * RWKV_HEAD), F32)]
        + [pltpu.VMEM((tc, w), BF16)] * 7
        + [pltpu.VMEM((tc, w), F32)] * 4,
        compiler_params=pltpu.CompilerParams(dimension_semantics=("parallel", "arbitrary"),
                                             vmem_limit_bytes=VMEM_LIMIT),
        name="rwkv",
    )(zs, *consts)


def _merge_kernel(attn_ref, rw_ref, gate_ref, x_ref, mod_ref, womla_ref, worw_ref, wout_ref,
                  gpost_ref, gffn_ref, wrh_ref, wrl_ref, br_ref, stril_ref,
                  x1_ref, hp_ref, dest_ref, cnt_ref, carry_ref, *, group_capacity):
    i = pl.program_id(0)
    d = x_ref.shape[1]
    mod = mod_ref[0]
    gate1, shift2, scale2 = mod[2:3], mod[3:4], mod[4:5]
    gates = gate_ref[...].astype(F32)
    o = gates[:, :d] * _dot(attn_ref[...], womla_ref[...]) + gates[:, d:] * _dot(rw_ref[...], worw_ref[...])
    y = _dot(o.astype(BF16), wout_ref[...])
    x1 = x_ref[...] + gate1 * _rms(y, gpost_ref[...])
    x1_ref[...] = x1
    h2 = _rms(x1, gffn_ref[...]) * (1.0 + scale2) + shift2

    logits = _dot3(h2, wrh_ref[...], wrl_ref[...]) + br_ref[...]
    tm = logits.shape[0]
    lane = lax.broadcasted_iota(jnp.int32, (tm, ROUTER_WIDTH), 1)
    lane_f = lane.astype(F32)
    big = float(ROUTER_WIDTH)
    is_group = (lane >= N_EXPERTS) & (lane < N_EXPERTS + N_GROUPS)
    gl = jnp.where(is_group, logits, NEG_INF)
    g_max = jnp.max(gl, axis=-1, keepdims=True)
    g_sel = jnp.min(jnp.where(gl == g_max, lane_f, big), axis=-1, keepdims=True) - float(N_EXPERTS)
    p_sel = 1.0 / jnp.sum(jnp.where(is_group, jnp.exp(gl - g_max), 0.0), axis=-1, keepdims=True)
    lane_group = (lane // EXPERTS_PER_GROUP).astype(F32)
    in_group = (lane < N_EXPERTS) & (lane_group == g_sel)
    el = jnp.where(in_group, logits, NEG_INF)
    t1 = jnp.max(el, axis=-1, keepdims=True)
    i1 = jnp.min(jnp.where(el == t1, lane_f, big), axis=-1, keepdims=True)
    el2 = jnp.where(lane_f == i1, NEG_INF, el)
    t2 = jnp.max(el2, axis=-1, keepdims=True)
    i2 = jnp.min(jnp.where(el2 == t2, lane_f, big), axis=-1, keepdims=True)
    e21 = jnp.exp(t2 - t1)
    w1 = p_sel / (1.0 + e21)
    w2 = w1 * e21
    first = g_sel * float(EXPERTS_PER_GROUP)
    cw = jnp.where(lane_f == i1 - first, w1, 0.0) + jnp.where(lane_f == i2 - first, w2, 0.0)

    @pl.when(i == 0)
    def _():
        carry_ref[...] = jnp.zeros_like(carry_ref)

    onehot = jnp.where(lane_f == g_sel, 1.0, 0.0)
    earlier = _dot(stril_ref[...], onehot.astype(BF16))
    carry = carry_ref[0:1, :]
    rank = jnp.sum(jnp.where(lane_f == g_sel, carry + earlier, 0.0), axis=-1, keepdims=True)
    dest_ref[...] = (g_sel * float(group_capacity) + rank).astype(jnp.int32)
    total = jnp.broadcast_to(carry + jnp.sum(onehot, axis=0, keepdims=True), carry_ref.shape)
    carry_ref[...] = total
    cnt_ref[...] = total

    hp_ref[:, :d] = h2
    hp_ref[:, d:] = cw


def _merge(attn, rw, gates, x2, mod3, w_o_mla, w_o_rwkv, w_out, g_post, g_ffn, wr_hi, wr_lo, b_r, *, seq, tm):
    t, d = x2.shape
    ns = seq // tm
    row = lambda i: (i, 0)
    tid = jnp.arange(tm)
    stril = (tid[None, :] < tid[:, None]).astype(BF16)
    consts = [w_o_mla, w_o_rwkv, w_out, g_post, g_ffn, wr_hi, wr_lo, b_r, stril]
    return pl.pallas_call(
        functools.partial(_merge_kernel, group_capacity=t),
        grid=(t // tm,),
        in_specs=[pl.BlockSpec((tm, attn.shape[1]), row), pl.BlockSpec((tm, rw.shape[1]), row),
                  pl.BlockSpec((tm, 2 * d), row), pl.BlockSpec((tm, d), row),
                  pl.BlockSpec((1, 6, d), lambda i: (i // ns, 0, 0))] + [_const_spec(a.shape) for a in consts],
        out_specs=[pl.BlockSpec((tm, d), row), pl.BlockSpec((tm, d + ROUTER_WIDTH), row),
                   pl.BlockSpec((tm, 1), row), pl.BlockSpec((8, ROUTER_WIDTH), lambda i: (0, 0))],
        out_shape=[jax.ShapeDtypeStruct((t, d), F32), jax.ShapeDtypeStruct((t, d + ROUTER_WIDTH), F32),
                   jax.ShapeDtypeStruct((t, 1), jnp.int32), jax.ShapeDtypeStruct((8, ROUTER_WIDTH), F32)],
        scratch_shapes=[pltpu.VMEM((8, ROUTER_WIDTH), F32)],
        compiler_params=pltpu.CompilerParams(dimension_semantics=("arbitrary",), vmem_limit_bytes=VMEM_LIMIT),
        name="merge",
    )(attn, rw, gates, x2, mod3, *consts)


def _dispatch_kernel(dest_ref, hp_ref, xs_ref, sem):
    td = hp_ref.shape[0]
    base = pl.program_id(0) * td

    def issue(r, c):
        pltpu.make_async_copy(hp_ref.at[pl.ds(r, 1)], xs_ref.at[pl.ds(dest_ref[base + r], 1)], sem).start()
        return c

    lax.fori_loop(0, td, issue, 0, unroll=8)
    pltpu.make_async_copy(hp_ref, xs_ref.at[pl.ds(0, td)], sem).wait()


def _dispatch(dest, hp, *, rows_out, td):
    t, w = hp.shape
    return pl.pallas_call(
        _dispatch_kernel,
        grid_spec=pltpu.PrefetchScalarGridSpec(
            num_scalar_prefetch=1, grid=(t // td,),
            in_specs=[pl.BlockSpec((td, w), lambda i, dest: (i, 0))],
            out_specs=pl.BlockSpec(memory_space=pl.ANY),
            scratch_shapes=[pltpu.SemaphoreType.DMA]),
        out_shape=jax.ShapeDtypeStruct((rows_out, w), F32),
        compiler_params=pltpu.CompilerParams(dimension_semantics=("arbitrary",), vmem_limit_bytes=VMEM_LIMIT),
        name="dispatch",
    )(dest, hp)


def _experts_kernel(tg_ref, tb_ref, tr_ref, xs_ref, wg_ref, wu_ref, wd_ref, ex_ref, ys_ref):
    del tg_ref, tb_ref
    nrows = tr_ref[pl.program_id(0)]

    @pl.when(nrows > 0)
    def _():
        tmx = xs_ref.shape[0]
        valid = lax.broadcasted_iota(jnp.int32, (tmx, 1), 0) < nrows
        tile = jnp.where(valid, xs_ref[...], 0.0)
        x = tile[:, :D_MODEL].astype(BF16)
        cw_hi, cw_lo = _split(tile[:, D_MODEL:])
        cwx = _dot(cw_hi, ex_ref[...]) + _dot(cw_lo, ex_ref[...])
        acc = jnp.zeros((tmx, D_MODEL), F32)
        for e in range(EXPERTS_PER_GROUP):
            a = _dot(x, wg_ref[e])
            act = a * _sigmoid(a) * _dot(x, wu_ref[e]) * cwx[:, e * EXPERT_FF:(e + 1) * EXPERT_FF]
            acc = acc + _dot(act.astype(BF16), wd_ref[e])
        ys_ref[...] = acc


def _experts(tile_group, tile_blk, tile_rows, xs, w_gate, w_up, w_down, expand, *, tmx):
    rows, w = xs.shape
    nt = tile_group.shape[0]
    d = D_MODEL
    grp = lambda j, tg, tb, tr: (tg[j], 0, 0)
    return pl.pallas_call(
        _experts_kernel,
        grid_spec=pltpu.PrefetchScalarGridSpec(
            num_scalar_prefetch=3, grid=(nt,),
            in_specs=[pl.BlockSpec((tmx, w), lambda j, tg, tb, tr: (tb[j], 0)),
                      pl.BlockSpec((EXPERTS_PER_GROUP, d, EXPERT_FF), grp),
                      pl.BlockSpec((EXPERTS_PER_GROUP, d, EXPERT_FF), grp),
                      pl.BlockSpec((EXPERTS_PER_GROUP, EXPERT_FF, d), grp),
                      pl.BlockSpec(expand.shape, lambda j, tg, tb, tr: (0, 0))],
            out_specs=pl.BlockSpec((tmx, d), lambda j, tg, tb, tr: (tb[j], 0))),
        out_shape=jax.ShapeDtypeStruct((rows, d), F32),
        compiler_params=pltpu.CompilerParams(dimension_semantics=("arbitrary",), vmem_limit_bytes=VMEM_LIMIT),
        name="experts",
    )(tile_group, tile_blk, tile_rows, xs, w_gate, w_up, w_down, expand)


def _final_kernel(dest_ref, x1_ref, mod_ref, gpost_ref, ys_ref, o_ref, ybuf, sem):
    tmf = x1_ref.shape[0]
    base = pl.program_id(0) * tmf

    def issue(r, c):
        pltpu.make_async_copy(ys_ref.at[pl.ds(dest_ref[base + r], 1)], ybuf.at[pl.ds(r, 1)], sem).start()
        return c

    lax.fori_loop(0, tmf, issue, 0, unroll=8)
    pltpu.make_async_copy(ys_ref.at[pl.ds(0, tmf)], ybuf, sem).wait()
    gate2 = mod_ref[0][5:6]
    o_ref[...] = x1_ref[...] + gate2 * _rms(ybuf[...], gpost_ref[...])


def _final(dest, x1, mod3, g_post, ys, *, seq, tmf):
    t, d = x1.shape
    ns = seq // tmf
    return pl.pallas_call(
        _final_kernel,
        grid_spec=pltpu.PrefetchScalarGridSpec(
            num_scalar_prefetch=1, grid=(t // tmf,),
            in_specs=[pl.BlockSpec((tmf, d), lambda i, dest: (i, 0)),
                      pl.BlockSpec((1, 6, d), lambda i, dest: (i // ns, 0, 0)),
                      pl.BlockSpec((1, d), lambda i, dest: (0, 0)),
                      pl.BlockSpec(memory_space=pl.ANY)],
            out_specs=pl.BlockSpec((tmf, d), lambda i, dest: (i, 0)),
            scratch_shapes=[pltpu.VMEM((tmf, d), F32), pltpu.SemaphoreType.DMA]),
        out_shape=jax.ShapeDtypeStruct((t, d), F32),
        compiler_params=pltpu.CompilerParams(dimension_semantics=("arbitrary",), vmem_limit_bytes=VMEM_LIMIT),
        name="final",
    )(dest, x1, mod3, g_post, ys)


def _tile_map(counts, *, capacity, tmx):
    ntile = (counts + tmx - 1) // tmx
    ends = jnp.cumsum(ntile)
    starts = ends - ntile
    j = jnp.arange(capacity // tmx + N_GROUPS, dtype=jnp.int32)
    g = jnp.minimum(jnp.sum((j[:, None] >= ends[None, :]).astype(jnp.int32), axis=1), N_GROUPS - 1)
    local = j - starts[g]
    valid = j < ends[-1]
    blk = g * (capacity // tmx) + local
    rows = jnp.clip(counts[g] - local * tmx, 0, tmx)
    last = ends[-1] - 1
    return (jnp.where(valid, g, g[last]).astype(jnp.int32), jnp.where(valid, blk, blk[last]).astype(jnp.int32),
            jnp.where(valid, rows, 0).astype(jnp.int32))


def _pick_tile(seq, want):
    return want if seq % want == 0 else seq


def _layer(x, c, positions, w_ada, b_ada, g_pre_mix, g_post_mix, g_pre_ffn, g_post_ffn, w_in, g_cq, w_uq,
           g_ckv, w_ukv, w_o_mla, mu_shift, w_decay_up, decay_bias, w_a_up, a_bias, w_g_up, k_k, k_a, r_k,
           lnx_w, lnx_b, w_o_rwkv, w_out, w_router_group, b_router_group, w_router_expert, b_router_expert,
           w_exp_gate, w_exp_up, w_exp_down):
    batch, seq, d = x.shape
    t = batch * seq
    x2 = x.reshape(t, d)
    row1 = lambda a: a.reshape(1, -1)

    mod3 = _ada(c, w_ada, b_ada).reshape(batch, 6, d)

    zeros = lambda n: jnp.zeros((d, n), F32)
    w_c = jnp.concatenate([w_in[:, :MLA_Q_RANK + MLA_KV_RANK], zeros(MLA_NOPE),
                           w_in[:, MLA_Q_RANK + MLA_KV_RANK:MLA_IN], zeros(HEAD_PAD - MLA_NOPE - MLA_ROPE)],
                          axis=1).astype(BF16)
    w_r = w_in[:, MLA_IN:MLA_IN + RWKV_IN].astype(BF16)
    w_g = w_in[:, MLA_IN + RWKV_IN:].astype(BF16)
    w_uq_p = jnp.pad(w_uq.reshape(MLA_Q_RANK, HEADS, MLA_NOPE + MLA_ROPE),
                     ((0, 0), (0, 0), (0, HEAD_PAD - MLA_NOPE - MLA_ROPE))).reshape(MLA_Q_RANK, QK_WIDTH).astype(BF16)
    w_ukv3 = w_ukv.reshape(MLA_KV_RANK, HEADS, MLA_NOPE + MLA_V)
    w_uk_p = jnp.pad(w_ukv3[..., :MLA_NOPE], ((0, 0), (0, 0), (0, HEAD_PAD - MLA_NOPE))
                     ).reshape(MLA_KV_RANK, QK_WIDTH).astype(BF16)
    w_uv_t = w_ukv3[..., MLA_NOPE:].reshape(MLA_KV_RANK, HEADS * MLA_V).T.astype(BF16)
    inv_freq = jnp.power(ROPE_THETA, -jnp.arange(0, MLA_ROPE, 2, dtype=F32) / MLA_ROPE)
    invf = jnp.concatenate([jnp.zeros((MLA_NOPE,), F32), inv_freq, inv_freq,
                            jnp.zeros((HEAD_PAD - MLA_NOPE - MLA_ROPE,), F32)]).reshape(1, LANES)

    tm = _pick_tile(seq, 256)
    q_all, k_all, vt_all, zs, gates = _inproj(
        x2, mod3, positions.reshape(t, 1), row1(g_pre_mix), w_c, w_r, w_g, row1(g_cq), row1(g_ckv),
        w_uq_p, w_uk_p, w_uv_t, row1(mu_shift), invf, batch=batch, seq=seq, tm=tm)

    attn = _attn(q_all, k_all, vt_all, batch=batch, seq=seq, tq=tm)

    tc = _pick_tile(seq, 256)
    w_da = jnp.concatenate([
        jnp.concatenate([w_decay_up, jnp.zeros_like(w_decay_up)], axis=1),
        jnp.concatenate([jnp.zeros_like(w_a_up), w_a_up], axis=1)], axis=0).astype(BF16)
    hid = jnp.arange(RWKV_WIDTH) // RWKV_HEAD
    ones_bd = (hid[:, None] == hid[None, :]).astype(BF16)
    tid = jnp.arange(tc)
    same_chunk = (tid[:, None] // CHUNK) == (tid[None, :] // CHUNK)
    tri = (same_chunk & (tid[None, :] <= tid[:, None])).astype(BF16)
    blk = same_chunk.astype(BF16)
    rw = _rwkv(zs, w_da, w_g_up.astype(BF16), row1(decay_bias), row1(a_bias), row1(k_k), row1(k_a), row1(r_k),
               row1(lnx_w), row1(lnx_b), ones_bd, tri, blk, batch=batch, seq=seq, tc=tc)

    w_rt = jnp.concatenate([w_router_expert, w_router_group,
                            jnp.zeros((d, ROUTER_WIDTH - N_EXPERTS - N_GROUPS), F32)], axis=1)
    wr_hi = w_rt.astype(BF16)
    wr_lo = (w_rt - wr_hi.astype(F32)).astype(BF16)
    b_r = jnp.concatenate([b_router_expert, b_router_group,
                           jnp.zeros((ROUTER_WIDTH - N_EXPERTS - N_GROUPS,), F32)]).reshape(1, ROUTER_WIDTH)
    x1, hp, dest, cnt = _merge(attn, rw, gates, x2, mod3, w_o_mla.astype(BF16), w_o_rwkv.astype(BF16),
                               w_out.astype(BF16), row1(g_post_mix), row1(g_pre_ffn), wr_hi, wr_lo, b_r,
                               seq=seq, tm=_pick_tile(seq, 256))

    tmx = _pick_tile(seq, 512)
    dest = dest.reshape(t)
    xs = _dispatch(dest, hp, rows_out=N_GROUPS * t, td=_pick_tile(seq, 1024))
    tile_group, tile_blk, tile_rows = _tile_map(cnt[0, :N_GROUPS].astype(jnp.int32), capacity=t, tmx=tmx)
    eid = jnp.arange(EXPERTS_PER_GROUP * EXPERT_FF) // EXPERT_FF
    expand = (jnp.arange(ROUTER_WIDTH)[:, None] == eid[None, :]).astype(BF16)
    ys = _experts(tile_group, tile_blk, tile_rows, xs, w_exp_gate.astype(BF16), w_exp_up.astype(BF16),
                  w_exp_down.astype(BF16), expand, tmx=tmx)
    out = _final(dest, x1, mod3, row1(g_post_ffn), ys, seq=seq, tmf=_pick_tile(seq, 512))
    return out.reshape(batch, seq, d)


def kernel(x, c, positions, w_ada, b_ada, g_pre_mix, g_post_mix, g_pre_ffn, g_post_ffn, w_in, g_cq, w_uq, g_ckv, w_ukv, w_o_mla, mu_shift, w_decay_up, decay_bias, w_a_up, a_bias, w_g_up, k_k, k_a, r_k, lnx_w, lnx_b, w_o_rwkv, w_out, w_router_group, b_router_group, w_router_expert, b_router_expert, w_exp_gate, w_exp_up, w_exp_down):
    depth = w_ada.shape[0]
    for l in range(depth):
        x = _layer(x, c, positions, w_ada[l], b_ada[l], g_pre_mix[l], g_post_mix[l], g_pre_ffn[l], g_post_ffn[l],
                   w_in[l], g_cq[l], w_uq[l], g_ckv[l], w_ukv[l], w_o_mla[l], mu_shift[l], w_decay_up[l],
                   decay_bias[l], w_a_up[l], a_bias[l], w_g_up[l], k_k[l], k_a[l], r_k[l], lnx_w[l], lnx_b[l],
                   w_o_rwkv[l], w_out[l], w_router_group[l], b_router_group[l], w_router_expert[l],
                   b_router_expert[l], w_exp_gate[l], w_exp_up[l], w_exp_down[l])
    return x
```

```python
import functools
import math

import jax
import jax.numpy as jnp
from jax import lax
from jax.experimental import pallas as pl
from jax.experimental.pallas import tpu as pltpu

F32 = jnp.float32
BF16 = jnp.bfloat16

D_MODEL = 1024
CHUNK = 64
HEADS = 8
MLA_NOPE = 64
MLA_ROPE = 32
MLA_V = 64
MLA_Q_RANK = 384
MLA_KV_RANK = 256
ROPE_THETA = 10000.0
RWKV_HEAD = 64
RWKV_WIDTH = HEADS * RWKV_HEAD
DECAY_RANK = 64
AAA_RANK = 64
GATE_RANK = 128
GN_EPS = 64e-5
N_GROUPS = 4
EXPERTS_PER_GROUP = 8
N_EXPERTS = N_GROUPS * EXPERTS_PER_GROUP
EXPERT_FF = 256
RMS_EPS = 1e-6
NEG_INF = -1e30
MLA_IN = MLA_Q_RANK + MLA_KV_RANK + MLA_ROPE
RWKV_IN = 3 * RWKV_WIDTH + DECAY_RANK + AAA_RANK + GATE_RANK

LANES = 128
HEAD_PAD = LANES
QK_WIDTH = HEADS * HEAD_PAD
MLA_C_WIDTH = 768
ROUTER_WIDTH = LANES
VMEM_LIMIT = 56 * 1024 * 1024
ATTN_HEADS_PER_STEP = 8
ATTN_BLOCK = 256
V_ROWS = 80
RWKV_CHUNKS_PER_STEP = 4


def _dot(a, b):
    return jnp.dot(a, b, preferred_element_type=F32)


def _dot_nt(a, b):
    return lax.dot_general(a, b, (((1,), (1,)), ((), ())), preferred_element_type=F32)


def _dot_tn(a, b):
    return lax.dot_general(a, b, (((0,), (0,)), ((), ())), preferred_element_type=F32)


def _split(x):
    hi = x.astype(BF16)
    lo = (x - hi.astype(F32)).astype(BF16)
    return hi, lo


def _dot3(x, w_hi, w_lo):
    x_hi, x_lo = _split(x)
    return _dot(x_hi, w_hi) + (_dot(x_hi, w_lo) + _dot(x_lo, w_hi))


def _sigmoid(x):
    return 1.0 / (1.0 + jnp.exp(-x))


def _rms(x, g):
    return x * lax.rsqrt(jnp.mean(x * x, axis=-1, keepdims=True) + RMS_EPS) * g


def _ada_kernel(c_ref, w_ref, b_ref, o_ref):
    c = c_ref[...]
    s = c * _sigmoid(c)
    w_hi, w_lo = _split(w_ref[...])
    o_ref[...] = _dot3(s, w_hi, w_lo) + b_ref[...]


def _ada(c, w_ada, b_ada):
    b, d = c.shape
    n = w_ada.shape[1]
    tn = 512
    return pl.pallas_call(
        _ada_kernel,
        grid=(n // tn,),
        in_specs=[pl.BlockSpec((b, d), lambda j: (0, 0)),
                  pl.BlockSpec((d, tn), lambda j: (0, j)),
                  pl.BlockSpec((1, tn), lambda j: (0, j))],
        out_specs=pl.BlockSpec((b, tn), lambda j: (0, j)),
        out_shape=jax.ShapeDtypeStruct((b, n), F32),
        name="ada",
    )(c, w_ada, b_ada.reshape(1, n))


def _inproj_kernel(x_ref, mod_ref, pos_ref, gpre_ref, wc_ref, wr_ref, wg_ref, gcq_ref, gckv_ref,
                   wuq_ref, wuk_ref, wuv_ref, mu_ref, invf_ref,
                   q_ref, k_ref, vt_ref, zs_ref, gate_ref, carry_ref):
    i = pl.program_id(1)
    tm = x_ref.shape[0]
    mod = mod_ref[0]
    shift, scale = mod[0:1], mod[1:2]
    h = (_rms(x_ref[...], gpre_ref[...]) * (1.0 + scale) + shift).astype(BF16)

    zc = _dot(h, wc_ref[...])
    nq = _rms(zc[:, :MLA_Q_RANK], gcq_ref[...]).astype(BF16)
    nkv = _rms(zc[:, MLA_Q_RANK:MLA_Q_RANK + MLA_KV_RANK], gckv_ref[...]).astype(BF16)
    kr = zc[:, MLA_Q_RANK + MLA_KV_RANK:]
    qf = _dot(nq, wuq_ref[...])
    kn = _dot(nkv, wuk_ref[...])
    vt = _dot_nt(wuv_ref[...], nkv)
    vrow = lax.broadcasted_iota(jnp.int32, vt.shape, 0) % V_ROWS
    vt = jnp.where(vrow == MLA_V, 1.0, vt).astype(BF16)
    for j in range(vt_ref.shape[0]):
        vt_ref[j] = vt[:, j * ATTN_BLOCK:(j + 1) * ATTN_BLOCK]

    ang = pos_ref[...].astype(F32) * invf_ref[...]
    lane = lax.broadcasted_iota(jnp.int32, (1, LANES), 1)
    first_half = lane < MLA_NOPE + MLA_ROPE // 2
    in_rope = (lane >= MLA_NOPE) & (lane < MLA_NOPE + MLA_ROPE)
    cos_t = jnp.where(in_rope, jnp.cos(ang), 1.0)
    sin_a = jnp.sin(ang)
    sin_t = jnp.where(in_rope, jnp.where(first_half, -sin_a, sin_a), 0.0)

    def rope(t):
        rot = jnp.where(first_half, pltpu.roll(t, LANES - MLA_ROPE // 2, 1), pltpu.roll(t, MLA_ROPE // 2, 1))
        return t * cos_t + rot * sin_t

    kr_rot = rope(kr)
    q_scale = math.log2(math.e) / math.sqrt(MLA_NOPE + MLA_ROPE)
    for hh in range(HEADS):
        sl = slice(hh * HEAD_PAD, (hh + 1) * HEAD_PAD)
        q_ref[:, sl] = (rope(qf[:, sl]) * q_scale).astype(BF16)
        k_ref[:, sl] = (kn[:, sl] + kr_rot).astype(BF16)

    zr = _dot(h, wr_ref[...])

    @pl.when(i == 0)
    def _():
        carry_ref[...] = jnp.zeros_like(carry_ref)

    row = lax.broadcasted_iota(jnp.int32, (tm, 1), 0)
    prev = jnp.where(row == 0, carry_ref[0:1, :], pltpu.roll(zr, 1, 0))
    carry_ref[0:1, :] = zr[tm - 1:tm, :]
    zs_ref[...] = (zr + (prev - zr) * mu_ref[...]).astype(BF16)

    gate_ref[...] = _sigmoid(_dot(h, wg_ref[...])).astype(BF16)


def _const_spec(shape):
    nd = len(shape)
    return pl.BlockSpec(shape, lambda *_: (0,) * nd, pipeline_mode=pl.Buffered(1))


def _inproj(x2, mod3, pos2, g_pre, w_c, w_r, w_g, g_cq, g_ckv, w_uq, w_uk, w_uv, mu, invf, *, batch, seq, tm):
    t, d = x2.shape
    ns = seq // tm
    row = lambda b, i: (b * ns + i, 0)
    widths = [QK_WIDTH, QK_WIDTH, RWKV_IN, 2 * d]
    specs = [pl.BlockSpec((tm, w), row) for w in widths]
    shapes = [jax.ShapeDtypeStruct((t, w), BF16) for w in widths]
    vw = HEADS * V_ROWS
    nb = tm // ATTN_BLOCK
    specs.insert(2, pl.BlockSpec((None, nb, vw, ATTN_BLOCK), lambda b, i: (b, i, 0, 0)))
    shapes.insert(2, jax.ShapeDtypeStruct((batch, ns * nb, vw, ATTN_BLOCK), BF16))
    return pl.pallas_call(
        _inproj_kernel,
        grid=(batch, ns),
        in_specs=[pl.BlockSpec((tm, d), row),
                  pl.BlockSpec((1, 6, d), lambda b, i: (b, 0, 0)),
                  pl.BlockSpec((tm, 1), row),
                  _const_spec(g_pre.shape), _const_spec(w_c.shape), _const_spec(w_r.shape),
                  _const_spec(w_g.shape), _const_spec(g_cq.shape), _const_spec(g_ckv.shape),
                  _const_spec(w_uq.shape), _const_spec(w_uk.shape), _const_spec(w_uv.shape),
                  _const_spec(mu.shape), _const_spec(invf.shape)],
        out_specs=specs,
        out_shape=shapes,
        scratch_shapes=[pltpu.VMEM((8, RWKV_IN), F32)],
        compiler_params=pltpu.CompilerParams(dimension_semantics=("parallel", "arbitrary"),
                                             vmem_limit_bytes=VMEM_LIMIT),
        name="inproj",
    )(x2, mod3, pos2, g_pre, w_c, w_r, w_g, g_cq, g_ckv, w_uq, w_uk, w_uv, mu, invf)


def _attn_kernel(q_ref, k_ref, vt_ref, o_ref):
    i = pl.program_id(2)
    tq = q_ref.shape[0]
    kc = lax.broadcasted_iota(jnp.int32, (tq, tq), 0) // CHUNK
    qc = lax.broadcasted_iota(jnp.int32, (tq, tq), 1) // CHUNK
    diag_mask = kc <= qc
    heads = range(ATTN_HEADS_PER_STEP)
    hs = [slice(h * HEAD_PAD, (h + 1) * HEAD_PAD) for h in heads]
    vr = [slice(h * V_ROWS, (h + 1) * V_ROWS) for h in heads]
    q = [q_ref[:, s] for s in hs]

    def block(kb, carry, masked):
        rows = pl.ds(pl.multiple_of(kb * tq, tq), tq)
        s = [_dot_nt(k_ref[rows, hs[h]], q[h]) for h in heads]
        if masked:
            s = [jnp.where(diag_mask, t, NEG_INF) for t in s]
        m_new = [jnp.maximum(carry[h][0], jnp.max(s[h], axis=0, keepdims=True)) for h in heads]
        alpha = [jnp.exp2(carry[h][0] - m_new[h]) for h in heads]
        p = [jnp.exp2(s[h] - m_new[h]).astype(BF16) for h in heads]
        acc = [alpha[h] * carry[h][1] + _dot(vt_ref[kb, vr[h], :], p[h]) for h in heads]
        return tuple((m_new[h], acc[h]) for h in heads)

    init = tuple((jnp.full((1, tq), NEG_INF, F32), jnp.zeros((V_ROWS, tq), F32)) for _ in heads)
    carry = lax.fori_loop(0, i, functools.partial(block, masked=False), init)
    final = block(i, carry, True)
    outs = [final[h][1][:MLA_V] / final[h][1][MLA_V:MLA_V + 1] for h in heads]
    for pr in range(ATTN_HEADS_PER_STEP // 2):
        pair_t = jnp.concatenate([outs[2 * pr], outs[2 * pr + 1]], axis=0)
        o_ref[:, pr * LANES:(pr + 1) * LANES] = pair_t.T.astype(BF16)


def _attn(q_all, k_all, vt_all, *, batch, seq, tq):
    t = q_all.shape[0]
    nq = seq // tq
    g = ATTN_HEADS_PER_STEP
    return pl.pallas_call(
        _attn_kernel,
        grid=(batch, HEADS // g, nq),
        in_specs=[pl.BlockSpec((tq, g * HEAD_PAD), lambda b, p, i: (b * nq + i, p)),
                  pl.BlockSpec((seq, g * HEAD_PAD), lambda b, p, i: (b, p)),
                  pl.BlockSpec((None, nq, g * V_ROWS, tq), lambda b, p, i: (b, 0, p, 0))],
        out_specs=pl.BlockSpec((tq, g * MLA_V), lambda b, p, i: (b * nq + i, p)),
        out_shape=jax.ShapeDtypeStruct((t, HEADS * MLA_V), BF16),
        compiler_params=pltpu.CompilerParams(dimension_semantics=("parallel", "parallel", "arbitrary"),
                                             vmem_limit_bytes=VMEM_LIMIT),
        name="attn",
    )(q_all, k_all, vt_all)


def _rwkv_kernel(zs_ref, wda_ref, wgu_ref, dbias_ref, abias_ref, kk_ref, ka_ref, rk_ref, lnw_ref, lnb_ref,
                 ones_ref, tri_ref, blk_ref,
                 o_ref,
                 st_ref, a_s, r_s, b_s, k_s, bg_s, kg_s, v_s, gc_s, y_s, bonus_s, g_s):
    i = pl.program_id(1)
    tc = zs_ref.shape[0]
    w = RWKV_WIDTH

    @pl.when(i == 0)
    def _():
        st_ref[...] = jnp.zeros_like(st_ref)

    ones_bd = ones_ref[...]

    def head_sum(t):
        return _dot(t.astype(BF16), ones_bd)

    zs = zs_ref[...].astype(F32)
    r, k, v = zs[:, 0:w], zs[:, w:2 * w], zs[:, 2 * w:3 * w]
    da = zs[:, 3 * w:3 * w + LANES]
    gd = zs[:, 3 * w + LANES:]
    lane = lax.broadcasted_iota(jnp.int32, (1, LANES), 1)
    lora_in = jnp.where(lane < DECAY_RANK, jnp.tanh(da), da).astype(BF16)
    pre = _dot(lora_in, wda_ref[...])
    u = -(dbias_ref[...] + pre[:, :w])
    softplus = jnp.maximum(u, 0.0) + jnp.log(1.0 + jnp.exp(-jnp.abs(u)))
    logw = -jnp.exp(-softplus - 0.5)
    eta = _sigmoid(abias_ref[...] + pre[:, w:])
    g_s[...] = _dot(_sigmoid(gd).astype(BF16), wgu_ref[...])
    kk = k * kk_ref[...]
    kk = kk / jnp.maximum(jnp.sqrt(head_sum(kk * kk)), 1e-12)
    kp = k * (1.0 + (eta - 1.0) * ka_ref[...])
    bonus_s[...] = head_sum(r * kp * rk_ref[...]) * v

    lw_hi, lw_lo = _split(logw)
    cum = _dot(tri_ref[...], lw_hi) + _dot(tri_ref[...], lw_lo)
    cum_end = _dot(blk_ref[...], lw_hi) + _dot(blk_ref[...], lw_lo)
    b_in = kk * eta
    a_s[...] = (-kk * jnp.exp(cum - logw)).astype(BF16)
    r_s[...] = (r * jnp.exp(cum)).astype(BF16)
    g_inv = jnp.exp(-cum)
    b_s[...] = (b_in * g_inv).astype(BF16)
    k_s[...] = (kp * g_inv).astype(BF16)
    g_rem = jnp.exp(cum_end - cum)
    bg_s[...] = (b_in * g_rem).astype(BF16)
    kg_s[...] = (kp * g_rem).astype(BF16)
    v_s[...] = v.astype(BF16)
    gc_s[...] = jnp.exp(cum_end)

    n2 = 2 * CHUNK
    ri = lax.broadcasted_iota(jnp.int32, (4 * CHUNK, n2), 0)
    ci = lax.broadcasted_iota(jnp.int32, (4 * CHUNK, n2), 1) % CHUNK
    tri_mask = (ri % CHUNK + ri // n2) > ci
    lane2 = lax.broadcasted_iota(jnp.int32, (CHUNK, n2), 1)
    h1 = lane2 < RWKV_HEAD
    bi = lax.broadcasted_iota(jnp.int32, (n2, n2), 0) // RWKV_HEAD
    bj = lax.broadcasted_iota(jnp.int32, (n2, n2), 1) // RWKV_HEAD
    bd_mask = bi == bj
    zero_bf = jnp.zeros((CHUNK, n2), BF16)
    zero_f = jnp.zeros((CHUNK, n2), F32)

    def chunk_group(c, carry):
        nsub = RWKV_CHUNKS_PER_STEP
        blocks = [(sub, p) for sub in range(nsub) for p in range(HEADS // 2)]
        rows = [pl.ds(pl.multiple_of((c * nsub + sub) * CHUNK, CHUNK), CHUNK) for sub in range(nsub)]
        lns = [slice(p * n2, (p + 1) * n2) for p in range(HEADS // 2)]
        a_t = [a_s[rows[sub], lns[p]] for sub, p in blocks]
        r_t = [r_s[rows[sub], lns[p]] for sub, p in blocks]
        v_t = [v_s[rows[sub], lns[p]] for sub, p in blocks]
        ml = []
        for n, (sub, p) in enumerate(blocks):
            lhs = jnp.concatenate([jnp.where(h1, a_t[n], zero_bf), jnp.where(h1, zero_bf, a_t[n]),
                                   jnp.where(h1, r_t[n], zero_bf), jnp.where(h1, zero_bf, r_t[n])], axis=0)
            rhs = jnp.concatenate([b_s[rows[sub], lns[p]], k_s[rows[sub], lns[p]]], axis=0)
            ml.append(jnp.where(tri_mask, _dot_nt(lhs, rhs), 0.0))
        ml_b = [m.astype(BF16) for m in ml]
        units = [(n, hh) for n in range(len(blocks)) for hh in range(2)]
        w_t = [_dot(ml_b[n][hh * CHUNK:(hh + 1) * CHUNK, :], jnp.concatenate([zero_bf, v_t[n]], axis=0))
               for n, hh in units]
        a_f = [t.astype(F32) for t in a_t]
        a_sw = [pltpu.roll(t, RWKV_HEAD, 1) for t in a_f]
        x = [jnp.where(h1, a_f[n], pltpu.roll(w_t[2 * n], RWKV_HEAD, 1)) if hh == 0
             else jnp.where(h1, a_sw[n], w_t[2 * n + 1]) for n, hh in units]
        lsq = [jnp.where(h1, ml[n][hh * CHUNK:(hh + 1) * CHUNK, :], 0.0) for n, hh in units]
        for j in range(6):
            out = [_dot(lsq[u][:, :CHUNK].astype(BF16),
                        jnp.concatenate([x[u], lsq[u]], axis=1).astype(BF16)) for u in range(len(units))]
            x = [x[u] + out[u][:, :n2] for u in range(len(units))]
            lsq = [t[:, n2:] for t in out]
        z = []
        for n in range(len(blocks)):
            x0, x1 = x[2 * n], x[2 * n + 1]
            a_hat = jnp.where(h1, x0, pltpu.roll(x1, RWKV_HEAD, 1))
            u0 = jnp.where(h1, pltpu.roll(x0, RWKV_HEAD, 1), x1)
            z.append(jnp.concatenate([jnp.concatenate([a_hat, u0], axis=1),
                                      jnp.concatenate([zero_f, v_t[n].astype(F32)], axis=1)],
                                     axis=0).astype(BF16))
        o0 = [_dot(ml_b[n][2 * CHUNK:3 * CHUNK, :], z[n]) for n in range(len(blocks))]
        o1 = [_dot(ml_b[n][3 * CHUNK:4 * CHUNK, :], z[n]) for n in range(len(blocks))]
        pm = [_dot_tn(z[n], jnp.concatenate([bg_s[rows[sub], lns[p]], kg_s[rows[sub], lns[p]]], axis=0))
              for n, (sub, p) in enumerate(blocks)]
        r_hat = [(r_t[n].astype(F32) + jnp.where(h1, o0[n][:, :n2], o1[n][:, :n2])).astype(BF16)
                 for n in range(len(blocks))]
        for n, (sub, p) in enumerate(blocks):
            y0 = jnp.where(h1, o0[n][:, n2:], o1[n][:, n2:])
            g_t = jnp.where(bd_mask, pm[n][:n2], 0.0).astype(BF16)
            h_t = jnp.where(bd_mask, pm[n][n2:], 0.0)
            s = st_ref[p]
            s_b = s.astype(BF16)
            y_s[rows[sub], lns[p]] = _dot_nt(r_hat[n], s_b) + y0
            gc = gc_s[pl.ds(pl.multiple_of((c * nsub + sub) * CHUNK, CHUNK), 8), lns[p]][0:1]
            st_ref[p] = gc * s + _dot(s_b, g_t) + h_t
        return carry

    lax.fori_loop(0, tc // (CHUNK * RWKV_CHUNKS_PER_STEP), chunk_group, 0)

    y = y_s[...]
    inv_n = 1.0 / RWKV_HEAD
    dev = y - head_sum(y) * inv_n
    var = head_sum(dev * dev) * inv_n
    out = (dev * lax.rsqrt(var + GN_EPS) * lnw_ref[...] + lnb_ref[...] + bonus_s[...]) * g_s[...]
    o_ref[...] = out.astype(BF16)


def _rwkv(zs, w_da, w_gu, dbias, abias, k_k, k_a, r_k, lnw, lnb, ones_bd, tri, blk, *, batch, seq, tc):
    t = zs.shape[0]
    nc = seq // tc
    w = RWKV_WIDTH
    consts = [w_da, w_gu, dbias, abias, k_k, k_a, r_k, lnw, lnb, ones_bd, tri, blk]
    return pl.pallas_call(
        _rwkv_kernel,
        grid=(batch, nc),
        in_specs=[pl.BlockSpec((tc, RWKV_IN), lambda b, i: (b * nc + i, 0))] + [_const_spec(a.shape) for a in consts],
        out_specs=pl.BlockSpec((tc, w), lambda b, i: (b * nc + i, 0)),
        out_shape=jax.ShapeDtypeStruct((t, w), BF16),
        scratch_shapes=[pltpu.VMEM((HEADS // 2, 2 * RWKV_HEAD, 2 * RWKV_HEAD), F32)]
        + [pltpu.VMEM((tc, w), BF16)] * 7
        + [pltpu.VMEM((tc, w), F32)] * 4,
        compiler_params=pltpu.CompilerParams(dimension_semantics=("parallel", "arbitrary"),
                                             vmem_limit_bytes=VMEM_LIMIT),
        name="rwkv",
    )(zs, *consts)


def _merge_kernel(attn_ref, rw_ref, gate_ref, x_ref, mod_ref, womla_ref, worw_ref, wout_ref,
                  gpost_ref, gffn_ref, wrh_ref, wrl_ref, br_ref, stril_ref,
                  x1_ref, hp_ref, dest_ref, cnt_ref, carry_ref, *, group_capacity):
    i = pl.program_id(0)
    d = x_ref.shape[1]
    mod = mod_ref[0]
    gate1, shift2, scale2 = mod[2:3], mod[3:4], mod[4:5]
    gates = gate_ref[...].astype(F32)
    o = gates[:, :d] * _dot(attn_ref[...], womla_ref[...]) + gates[:, d:] * _dot(rw_ref[...], worw_ref[...])
    y = _dot(o.astype(BF16), wout_ref[...])
    x1 = x_ref[...] + gate1 * _rms(y, gpost_ref[...])
    x1_ref[...] = x1
    h2 = _rms(x1, gffn_ref[...]) * (1.0 + scale2) + shift2

    logits = _dot3(h2, wrh_ref[...], wrl_ref[...]) + br_ref[...]
    tm = logits.shape[0]
    lane = lax.broadcasted_iota(jnp.int32, (tm, ROUTER_WIDTH), 1)
    lane_f = lane.astype(F32)
    big = float(ROUTER_WIDTH)
    is_group = (lane >= N_EXPERTS) & (lane < N_EXPERTS + N_GROUPS)
    gl = jnp.where(is_group, logits, NEG_INF)
    g_max = jnp.max(gl, axis=-1, keepdims=True)
    g_sel = jnp.min(jnp.where(gl == g_max, lane_f, big), axis=-1, keepdims=True) - float(N_EXPERTS)
    p_sel = 1.0 / jnp.sum(jnp.where(is_group, jnp.exp(gl - g_max), 0.0), axis=-1, keepdims=True)
    lane_group = (lane // EXPERTS_PER_GROUP).astype(F32)
    in_group = (lane < N_EXPERTS) & (lane_group == g_sel)
    el = jnp.where(in_group, logits, NEG_INF)
    t1 = jnp.max(el, axis=-1, keepdims=True)
    i1 = jnp.min(jnp.where(el == t1, lane_f, big), axis=-1, keepdims=True)
    el2 = jnp.where(lane_f == i1, NEG_INF, el)
    t2 = jnp.max(el2, axis=-1, keepdims=True)
    i2 = jnp.min(jnp.where(el2 == t2, lane_f, big), axis=-1, keepdims=True)
    e21 = jnp.exp(t2 - t1)
    w1 = p_sel / (1.0 + e21)
    w2 = w1 * e21
    first = g_sel * float(EXPERTS_PER_GROUP)
    cw = jnp.where(lane_f == i1 - first, w1, 0.0) + jnp.where(lane_f == i2 - first, w2, 0.0)

    @pl.when(i == 0)
    def _():
        carry_ref[...] = jnp.zeros_like(carry_ref)

    onehot = jnp.where(lane_f == g_sel, 1.0, 0.0)
    earlier = _dot(stril_ref[...], onehot.astype(BF16))
    carry = carry_ref[0:1, :]
    rank = jnp.sum(jnp.where(lane_f == g_sel, carry + earlier, 0.0), axis=-1, keepdims=True)
    dest_ref[...] = (g_sel * float(group_capacity) + rank).astype(jnp.int32)
    total = jnp.broadcast_to(carry + jnp.sum(onehot, axis=0, keepdims=True), carry_ref.shape)
    carry_ref[...] = total
    cnt_ref[...] = total

    hp_ref[:, :d] = h2
    hp_ref[:, d:] = cw


def _merge(attn, rw, gates, x2, mod3, w_o_mla, w_o_rwkv, w_out, g_post, g_ffn, wr_hi, wr_lo, b_r, *, seq, tm):
    t, d = x2.shape
    ns = seq // tm
    row = lambda i: (i, 0)
    tid = jnp.arange(tm)
    stril = (tid[None, :] < tid[:, None]).astype(BF16)
    consts = [w_o_mla, w_o_rwkv, w_out, g_post, g_ffn, wr_hi, wr_lo, b_r, stril]
    return pl.pallas_call(
        functools.partial(_merge_kernel, group_capacity=t),
        grid=(t // tm,),
        in_specs=[pl.BlockSpec((tm, attn.shape[1]), row), pl.BlockSpec((tm, rw.shape[1]), row),
                  pl.BlockSpec((tm, 2 * d), row), pl.BlockSpec((tm, d), row),
                  pl.BlockSpec((1, 6, d), lambda i: (i // ns, 0, 0))] + [_const_spec(a.shape) for a in consts],
        out_specs=[pl.BlockSpec((tm, d), row), pl.BlockSpec((tm, d + ROUTER_WIDTH), row),
                   pl.BlockSpec((tm, 1), row), pl.BlockSpec((8, ROUTER_WIDTH), lambda i: (0, 0))],
        out_shape=[jax.ShapeDtypeStruct((t, d), F32), jax.ShapeDtypeStruct((t, d + ROUTER_WIDTH), F32),
                   jax.ShapeDtypeStruct((t, 1), jnp.int32), jax.ShapeDtypeStruct((8, ROUTER_WIDTH), F32)],
        scratch_shapes=[pltpu.VMEM((8, ROUTER_WIDTH), F32)],
        compiler_params=pltpu.CompilerParams(dimension_semantics=("arbitrary",), vmem_limit_bytes=VMEM_LIMIT),
        name="merge",
    )(attn, rw, gates, x2, mod3, *consts)


def _dispatch_kernel(dest_ref, hp_ref, xs_ref, sem):
    td = hp_ref.shape[0]
    base = pl.program_id(0) * td

    def issue(r, c):
        pltpu.make_async_copy(hp_ref.at[pl.ds(r, 1)], xs_ref.at[pl.ds(dest_ref[base + r], 1)], sem).start()
        return c

    lax.fori_loop(0, td, issue, 0, unroll=8)
    pltpu.make_async_copy(hp_ref, xs_ref.at[pl.ds(0, td)], sem).wait()


def _dispatch(dest, hp, *, rows_out, td):
    t, w = hp.shape
    return pl.pallas_call(
        _dispatch_kernel,
        grid_spec=pltpu.PrefetchScalarGridSpec(
            num_scalar_prefetch=1, grid=(t // td,),
            in_specs=[pl.BlockSpec((td, w), lambda i, dest: (i, 0))],
            out_specs=pl.BlockSpec(memory_space=pl.ANY),
            scratch_shapes=[pltpu.SemaphoreType.DMA]),
        out_shape=jax.ShapeDtypeStruct((rows_out, w), F32),
        compiler_params=pltpu.CompilerParams(dimension_semantics=("arbitrary",), vmem_limit_bytes=VMEM_LIMIT),
        name="dispatch",
    )(dest, hp)


def _experts_kernel(tg_ref, tb_ref, tr_ref, xs_ref, wg_ref, wu_ref, wd_ref, ex_ref, ys_ref):
    del tg_ref, tb_ref
    nrows = tr_ref[pl.program_id(0)]

    @pl.when(nrows > 0)
    def _():
        tmx = xs_ref.shape[0]
        valid = lax.broadcasted_iota(jnp.int32, (tmx, 1), 0) < nrows
        tile = jnp.where(valid, xs_ref[...], 0.0)
        x = tile[:, :D_MODEL].astype(BF16)
        cw_hi, cw_lo = _split(tile[:, D_MODEL:])
        cwx = _dot(cw_hi, ex_ref[...]) + _dot(cw_lo, ex_ref[...])
        acc = jnp.zeros((tmx, D_MODEL), F32)
        for e in range(EXPERTS_PER_GROUP):
            a = _dot(x, wg_ref[e])
            act = a * _sigmoid(a) * _dot(x, wu_ref[e]) * cwx[:, e * EXPERT_FF:(e + 1) * EXPERT_FF]
            acc = acc + _dot(act.astype(BF16), wd_ref[e])
        ys_ref[...] = acc


def _experts(tile_group, tile_blk, tile_rows, xs, w_gate, w_up, w_down, expand, *, tmx):
    rows, w = xs.shape
    nt = tile_group.shape[0]
    d = D_MODEL
    grp = lambda j, tg, tb, tr: (tg[j], 0, 0)
    return pl.pallas_call(
        _experts_kernel,
        grid_spec=pltpu.PrefetchScalarGridSpec(
            num_scalar_prefetch=3, grid=(nt,),
            in_specs=[pl.BlockSpec((tmx, w), lambda j, tg, tb, tr: (tb[j], 0)),
                      pl.BlockSpec((EXPERTS_PER_GROUP, d, EXPERT_FF), grp),
                      pl.BlockSpec((EXPERTS_PER_GROUP, d, EXPERT_FF), grp),
                      pl.BlockSpec((EXPERTS_PER_GROUP, EXPERT_FF, d), grp),
                      pl.BlockSpec(expand.shape, lambda j, tg, tb, tr: (0, 0))],
            out_specs=pl.BlockSpec((tmx, d), lambda j, tg, tb, tr: (tb[j], 0))),
        out_shape=jax.ShapeDtypeStruct((rows, d), F32),
        compiler_params=pltpu.CompilerParams(dimension_semantics=("arbitrary",), vmem_limit_bytes=VMEM_LIMIT),
        name="experts",
    )(tile_group, tile_blk, tile_rows, xs, w_gate, w_up, w_down, expand)


def _final_kernel(dest_ref, x1_ref, mod_ref, gpost_ref, ys_ref, o_ref, ybuf, sem):
    tmf = x1_ref.shape[0]
    base = pl.program_id(0) * tmf

    def issue(r, c):
        pltpu.make_async_copy(ys_ref.at[pl.ds(dest_ref[base + r], 1)], ybuf.at[pl.ds(r, 1)], sem).start()
        return c

    lax.fori_loop(0, tmf, issue, 0, unroll=8)
    pltpu.make_async_copy(ys_ref.at[pl.ds(0, tmf)], ybuf, sem).wait()
    gate2 = mod_ref[0][5:6]
    o_ref[...] = x1_ref[...] + gate2 * _rms(ybuf[...], gpost_ref[...])


def _final(dest, x1, mod3, g_post, ys, *, seq, tmf):
    t, d = x1.shape
    ns = seq // tmf
    return pl.pallas_call(
        _final_kernel,
        grid_spec=pltpu.PrefetchScalarGridSpec(
            num_scalar_prefetch=1, grid=(t // tmf,),
            in_specs=[pl.BlockSpec((tmf, d), lambda i, dest: (i, 0)),
                      pl.BlockSpec((1, 6, d), lambda i, dest: (i // ns, 0, 0)),
                      pl.BlockSpec((1, d), lambda i, dest: (0, 0)),
                      pl.BlockSpec(memory_space=pl.ANY)],
            out_specs=pl.BlockSpec((tmf, d), lambda i, dest: (i, 0)),
            scratch_shapes=[pltpu.VMEM((tmf, d), F32), pltpu.SemaphoreType.DMA]),
        out_shape=jax.ShapeDtypeStruct((t, d), F32),
        compiler_params=pltpu.CompilerParams(dimension_semantics=("arbitrary",), vmem_limit_bytes=VMEM_LIMIT),
        name="final",
    )(dest, x1, mod3, g_post, ys)


def _tile_map(counts, *, capacity, tmx):
    ntile = (counts + tmx - 1) // tmx
    ends = jnp.cumsum(ntile)
    starts = ends - ntile
    j = jnp.arange(capacity // tmx + N_GROUPS, dtype=jnp.int32)
    g = jnp.minimum(jnp.sum((j[:, None] >= ends[None, :]).astype(jnp.int32), axis=1), N_GROUPS - 1)
    local = j - starts[g]
    valid = j < ends[-1]
    blk = g * (capacity // tmx) + local
    rows = jnp.clip(counts[g] - local * tmx, 0, tmx)
    last = ends[-1] - 1
    return (jnp.where(valid, g, g[last]).astype(jnp.int32), jnp.where(valid, blk, blk[last]).astype(jnp.int32),
            jnp.where(valid, rows, 0).astype(jnp.int32))


def _pick_tile(seq, want):
    return want if seq % want == 0 else seq


def _layer(x, c, positions, w_ada, b_ada, g_pre_mix, g_post_mix, g_pre_ffn, g_post_ffn, w_in, g_cq, w_uq,
           g_ckv, w_ukv, w_o_mla, mu_shift, w_decay_up, decay_bias, w_a_up, a_bias, w_g_up, k_k, k_a, r_k,
           lnx_w, lnx_b, w_o_rwkv, w_out, w_router_group, b_router_group, w_router_expert, b_router_expert,
           w_exp_gate, w_exp_up, w_exp_down):
    batch, seq, d = x.shape
    t = batch * seq
    x2 = x.reshape(t, d)
    row1 = lambda a: a.reshape(1, -1)

    mod3 = _ada(c, w_ada, b_ada).reshape(batch, 6, d)

    zeros = lambda n: jnp.zeros((d, n), F32)
    w_c = jnp.concatenate([w_in[:, :MLA_Q_RANK + MLA_KV_RANK], zeros(MLA_NOPE),
                           w_in[:, MLA_Q_RANK + MLA_KV_RANK:MLA_IN], zeros(HEAD_PAD - MLA_NOPE - MLA_ROPE)],
                          axis=1).astype(BF16)
    w_r = w_in[:, MLA_IN:MLA_IN + RWKV_IN].astype(BF16)
    w_g = w_in[:, MLA_IN + RWKV_IN:].astype(BF16)
    w_uq_p = jnp.pad(w_uq.reshape(MLA_Q_RANK, HEADS, MLA_NOPE + MLA_ROPE),
                     ((0, 0), (0, 0), (0, HEAD_PAD - MLA_NOPE - MLA_ROPE))).reshape(MLA_Q_RANK, QK_WIDTH).astype(BF16)
    w_ukv3 = w_ukv.reshape(MLA_KV_RANK, HEADS, MLA_NOPE + MLA_V)
    w_uk_p = jnp.pad(w_ukv3[..., :MLA_NOPE], ((0, 0), (0, 0), (0, HEAD_PAD - MLA_NOPE))
                     ).reshape(MLA_KV_RANK, QK_WIDTH).astype(BF16)
    w_uv_t = jnp.pad(w_ukv3[..., MLA_NOPE:], ((0, 0), (0, 0), (0, V_ROWS - MLA_V))
                     ).reshape(MLA_KV_RANK, HEADS * V_ROWS).T.astype(BF16)
    inv_freq = jnp.power(ROPE_THETA, -jnp.arange(0, MLA_ROPE, 2, dtype=F32) / MLA_ROPE)
    invf = jnp.concatenate([jnp.zeros((MLA_NOPE,), F32), inv_freq, inv_freq,
                            jnp.zeros((HEAD_PAD - MLA_NOPE - MLA_ROPE,), F32)]).reshape(1, LANES)

    tm = _pick_tile(seq, 512)
    q_all, k_all, vt_all, zs, gates = _inproj(
        x2, mod3, positions.reshape(t, 1), row1(g_pre_mix), w_c, w_r, w_g, row1(g_cq), row1(g_ckv),
        w_uq_p, w_uk_p, w_uv_t, row1(mu_shift), invf, batch=batch, seq=seq, tm=tm)

    attn = _attn(q_all, k_all, vt_all, batch=batch, seq=seq, tq=ATTN_BLOCK)

    tc = _pick_tile(seq, 256)
    w_da = jnp.concatenate([
        jnp.concatenate([w_decay_up, jnp.zeros_like(w_decay_up)], axis=1),
        jnp.concatenate([jnp.zeros_like(w_a_up), w_a_up], axis=1)], axis=0).astype(BF16)
    hid = jnp.arange(RWKV_WIDTH) // RWKV_HEAD
    ones_bd = (hid[:, None] == hid[None, :]).astype(BF16)
    tid = jnp.arange(tc)
    same_chunk = (tid[:, None] // CHUNK) == (tid[None, :] // CHUNK)
    tri = (same_chunk & (tid[None, :] <= tid[:, None])).astype(BF16)
    blk = same_chunk.astype(BF16)
    rw = _rwkv(zs, w_da, w_g_up.astype(BF16), row1(decay_bias), row1(a_bias), row1(k_k), row1(k_a), row1(r_k),
               row1(lnx_w), row1(lnx_b), ones_bd, tri, blk, batch=batch, seq=seq, tc=tc)

    w_rt = jnp.concatenate([w_router_expert, w_router_group,
                            jnp.zeros((d, ROUTER_WIDTH - N_EXPERTS - N_GROUPS), F32)], axis=1)
    wr_hi = w_rt.astype(BF16)
    wr_lo = (w_rt - wr_hi.astype(F32)).astype(BF16)
    b_r = jnp.concatenate([b_router_expert, b_router_group,
                           jnp.zeros((ROUTER_WIDTH - N_EXPERTS - N_GROUPS,), F32)]).reshape(1, ROUTER_WIDTH)
    x1, hp, dest, cnt = _merge(attn, rw, gates, x2, mod3, w_o_mla.astype(BF16), w_o_rwkv.astype(BF16),
                               w_out.astype(BF16), row1(g_post_mix), row1(g_pre_ffn), wr_hi, wr_lo, b_r,
                               seq=seq, tm=_pick_tile(seq, 512))

    tmx = _pick_tile(seq, 512)
    dest = dest.reshape(t)
    xs = _dispatch(dest, hp, rows_out=N_GROUPS * t, td=_pick_tile(seq, 1024))
    tile_group, tile_blk, tile_rows = _tile_map(cnt[0, :N_GROUPS].astype(jnp.int32), capacity=t, tmx=tmx)
    eid = jnp.arange(EXPERTS_PER_GROUP * EXPERT_FF) // EXPERT_FF
    expand = (jnp.arange(ROUTER_WIDTH)[:, None] == eid[None, :]).astype(BF16)
    ys = _experts(tile_group, tile_blk, tile_rows, xs, w_exp_gate.astype(BF16), w_exp_up.astype(BF16),
                  w_exp_down.astype(BF16), expand, tmx=tmx)
    out = _final(dest, x1, mod3, row1(g_post_ffn), ys, seq=seq, tmf=_pick_tile(seq, 512))
    return out.reshape(batch, seq, d)


def kernel(x, c, positions, w_ada, b_ada, g_pre_mix, g_post_mix, g_pre_ffn, g_post_ffn, w_in, g_cq, w_uq, g_ckv, w_ukv, w_o_mla, mu_shift, w_decay_up, decay_bias, w_a_up, a_bias, w_g_up, k_k, k_a, r_k, lnx_w, lnx_b, w_o_rwkv, w_out, w_router_group, b_router_group, w_router_expert, b_router_expert, w_exp_gate, w_exp_up, w_exp_down):
    depth = w_ada.shape[0]
    for l in range(depth):
        x = _layer(x, c, positions, w_ada[l], b_ada[l], g_pre_mix[l], g_post_mix[l], g_pre_ffn[l], g_post_ffn[l],
                   w_in[l], g_cq[l], w_uq[l], g_ckv[l], w_ukv[l], w_o_mla[l], mu_shift[l], w_decay_up[l],
                   decay_bias[l], w_a_up[l], a_bias[l], w_g_up[l], k_k[l], k_a[l], r_k[l], lnx_w[l], lnx_b[l],
                   w_o_rwkv[l], w_out[l], w_router_group[l], b_router_group[l], w_router_expert[l],
                   b_router_expert[l], w_exp_gate[l], w_exp_up[l], w_exp_down[l])
    return x
```

```python
import functools
import math

import jax
import jax.numpy as jnp
from jax import lax
from jax.experimental import pallas as pl
from jax.experimental.pallas import tpu as pltpu

F32 = jnp.float32
BF16 = jnp.bfloat16

D_MODEL = 1024
CHUNK = 64
HEADS = 8
MLA_NOPE = 64
MLA_ROPE = 32
MLA_V = 64
MLA_Q_RANK = 384
MLA_KV_RANK = 256
ROPE_THETA = 10000.0
RWKV_HEAD = 64
RWKV_WIDTH = HEADS * RWKV_HEAD
DECAY_RANK = 64
AAA_RANK = 64
GATE_RANK = 128
GN_EPS = 64e-5
N_GROUPS = 4
EXPERTS_PER_GROUP = 8
N_EXPERTS = N_GROUPS * EXPERTS_PER_GROUP
EXPERT_FF = 256
RMS_EPS = 1e-6
NEG_INF = -1e30
MLA_IN = MLA_Q_RANK + MLA_KV_RANK + MLA_ROPE
RWKV_IN = 3 * RWKV_WIDTH + DECAY_RANK + AAA_RANK + GATE_RANK

LANES = 128
HEAD_PAD = LANES
QK_WIDTH = HEADS * HEAD_PAD
MLA_C_WIDTH = 768
ROUTER_WIDTH = LANES
VMEM_LIMIT = 56 * 1024 * 1024
ATTN_HEADS_PER_STEP = 8
ATTN_BLOCK = 256
V_ROWS = 80
RWKV_CHUNKS_PER_STEP = 4
MERGE_SUBTILES = 2


def _dot(a, b):
    return jnp.dot(a, b, preferred_element_type=F32)


def _dot_nt(a, b):
    return lax.dot_general(a, b, (((1,), (1,)), ((), ())), preferred_element_type=F32)


def _dot_tn(a, b):
    return lax.dot_general(a, b, (((0,), (0,)), ((), ())), preferred_element_type=F32)


def _split(x):
    hi = x.astype(BF16)
    lo = (x - hi.astype(F32)).astype(BF16)
    return hi, lo


def _dot3(x, w_hi, w_lo):
    x_hi, x_lo = _split(x)
    return _dot(x_hi, w_hi) + (_dot(x_hi, w_lo) + _dot(x_lo, w_hi))


def _sigmoid(x):
    return 1.0 / (1.0 + jnp.exp(-x))


def _rms(x, g):
    return x * lax.rsqrt(jnp.mean(x * x, axis=-1, keepdims=True) + RMS_EPS) * g


def _ada_kernel(c_ref, w_ref, b_ref, o_ref):
    c = c_ref[...]
    s = c * _sigmoid(c)
    w_hi, w_lo = _split(w_ref[...])
    o_ref[...] = _dot3(s, w_hi, w_lo) + b_ref[...]


def _ada(c, w_ada, b_ada):
    b, d = c.shape
    n = w_ada.shape[1]
    tn = 512
    return pl.pallas_call(
        _ada_kernel,
        grid=(n // tn,),
        in_specs=[pl.BlockSpec((b, d), lambda j: (0, 0)),
                  pl.BlockSpec((d, tn), lambda j: (0, j)),
                  pl.BlockSpec((1, tn), lambda j: (0, j))],
        out_specs=pl.BlockSpec((b, tn), lambda j: (0, j)),
        out_shape=jax.ShapeDtypeStruct((b, n), F32),
        name="ada",
    )(c, w_ada, b_ada.reshape(1, n))


def _inproj_kernel(x_ref, mod_ref, pos_ref, gpre_ref, wc_ref, wr_ref, wg_ref, gcq_ref, gckv_ref,
                   wuq_ref, wuk_ref, wuv_ref, mu_ref, invf_ref,
                   q_ref, k_ref, vt_ref, zs_ref, gate_ref, carry_ref):
    i = pl.program_id(1)
    tm = x_ref.shape[0]

    @pl.when(i == 0)
    def _():
        carry_ref[...] = jnp.zeros_like(carry_ref)

    mod = mod_ref[0]
    shift, scale = mod[0:1], mod[1:2]
    h = (_rms(x_ref[...], gpre_ref[...]) * (1.0 + scale) + shift).astype(BF16)

    zc = _dot(h, wc_ref[...])
    zg = _dot(h, wg_ref[...])
    nq = _rms(zc[:, :MLA_Q_RANK], gcq_ref[...]).astype(BF16)
    nkv = _rms(zc[:, MLA_Q_RANK:MLA_Q_RANK + MLA_KV_RANK], gckv_ref[...]).astype(BF16)
    kr = zc[:, MLA_Q_RANK + MLA_KV_RANK:]
    qf = _dot(nq, wuq_ref[...])
    kn = _dot(nkv, wuk_ref[...])
    vt = _dot_nt(wuv_ref[...], nkv)
    zr = _dot(h, wr_ref[...])

    ang = pos_ref[...].astype(F32) * invf_ref[...]
    lane = lax.broadcasted_iota(jnp.int32, (1, LANES), 1)
    first_half = lane < MLA_NOPE + MLA_ROPE // 2
    in_rope = (lane >= MLA_NOPE) & (lane < MLA_NOPE + MLA_ROPE)
    cos_t = jnp.where(in_rope, jnp.cos(ang), 1.0)
    sin_a = jnp.sin(ang)
    sin_t = jnp.where(in_rope, jnp.where(first_half, -sin_a, sin_a), 0.0)

    def rope(t):
        rot = jnp.where(first_half, pltpu.roll(t, LANES - MLA_ROPE // 2, 1), pltpu.roll(t, MLA_ROPE // 2, 1))
        return t * cos_t + rot * sin_t

    kr_rot = rope(kr)
    q_scale = math.log2(math.e) / math.sqrt(MLA_NOPE + MLA_ROPE)
    for hh in range(HEADS):
        sl = slice(hh * HEAD_PAD, (hh + 1) * HEAD_PAD)
        q_ref[:, sl] = (rope(qf[:, sl]) * q_scale).astype(BF16)
        k_ref[:, sl] = (kn[:, sl] + kr_rot).astype(BF16)
    vrow = lax.broadcasted_iota(jnp.int32, vt.shape, 0) % V_ROWS
    vt = jnp.where(vrow == MLA_V, 1.0, vt).astype(BF16)
    for j in range(vt_ref.shape[0]):
        vt_ref[j] = vt[:, j * ATTN_BLOCK:(j + 1) * ATTN_BLOCK]

    gate_ref[...] = _sigmoid(zg).astype(BF16)

    row = lax.broadcasted_iota(jnp.int32, (tm, 1), 0)
    prev = jnp.where(row == 0, carry_ref[0:1, :], pltpu.roll(zr, 1, 0))
    carry_ref[0:1, :] = zr[tm - 1:tm, :]
    zs_ref[...] = (zr + (prev - zr) * mu_ref[...]).astype(BF16)


def _const_spec(shape):
    nd = len(shape)
    return pl.BlockSpec(shape, lambda *_: (0,) * nd, pipeline_mode=pl.Buffered(1))


def _inproj(x2, mod3, pos2, g_pre, w_c, w_r, w_g, g_cq, g_ckv, w_uq, w_uk, w_uv, mu, invf, *, batch, seq, tm):
    t, d = x2.shape
    ns = seq // tm
    row = lambda b, i: (b * ns + i, 0)
    widths = [QK_WIDTH, QK_WIDTH, RWKV_IN, 2 * d]
    specs = [pl.BlockSpec((tm, w), row) for w in widths]
    shapes = [jax.ShapeDtypeStruct((t, w), BF16) for w in widths]
    vw = HEADS * V_ROWS
    nb = tm // ATTN_BLOCK
    specs.insert(2, pl.BlockSpec((None, nb, vw, ATTN_BLOCK), lambda b, i: (b, i, 0, 0)))
    shapes.insert(2, jax.ShapeDtypeStruct((batch, ns * nb, vw, ATTN_BLOCK), BF16))
    return pl.pallas_call(
        _inproj_kernel,
        grid=(batch, ns),
        in_specs=[pl.BlockSpec((tm, d), row),
                  pl.BlockSpec((1, 6, d), lambda b, i: (b, 0, 0)),
                  pl.BlockSpec((tm, 1), row),
                  _const_spec(g_pre.shape), _const_spec(w_c.shape), _const_spec(w_r.shape),
                  _const_spec(w_g.shape), _const_spec(g_cq.shape), _const_spec(g_ckv.shape),
                  _const_spec(w_uq.shape), _const_spec(w_uk.shape), _const_spec(w_uv.shape),
                  _const_spec(mu.shape), _const_spec(invf.shape)],
        out_specs=specs,
        out_shape=shapes,
        scratch_shapes=[pltpu.VMEM((8, RWKV_IN), F32)],
        compiler_params=pltpu.CompilerParams(dimension_semantics=("parallel", "arbitrary"),
                                             vmem_limit_bytes=VMEM_LIMIT),
        name="inproj",
    )(x2, mod3, pos2, g_pre, w_c, w_r, w_g, g_cq, g_ckv, w_uq, w_uk, w_uv, mu, invf)


def _attn_kernel(q_ref, k_ref, vt_ref, o_ref):
    i = pl.program_id(2)
    tq = q_ref.shape[0]
    kc = lax.broadcasted_iota(jnp.int32, (tq, tq), 0) // CHUNK
    qc = lax.broadcasted_iota(jnp.int32, (tq, tq), 1) // CHUNK
    diag_mask = kc <= qc
    heads = range(ATTN_HEADS_PER_STEP)
    hs = [slice(h * HEAD_PAD, (h + 1) * HEAD_PAD) for h in heads]
    vr = [slice(h * V_ROWS, (h + 1) * V_ROWS) for h in heads]
    q = [q_ref[:, s] for s in hs]

    def block(kb, carry, masked):
        rows = pl.ds(pl.multiple_of(kb * tq, tq), tq)
        s = [_dot_nt(k_ref[rows, hs[h]], q[h]) for h in heads]
        if masked:
            s = [jnp.where(diag_mask, t, NEG_INF) for t in s]
        m_new = [jnp.maximum(carry[h][0], jnp.max(s[h], axis=0, keepdims=True)) for h in heads]
        alpha = [jnp.exp2(carry[h][0] - m_new[h]) for h in heads]
        p = [jnp.exp2(s[h] - m_new[h]).astype(BF16) for h in heads]
        acc = [alpha[h] * carry[h][1] + _dot(vt_ref[kb, vr[h], :], p[h]) for h in heads]
        return tuple((m_new[h], acc[h]) for h in heads)

    init = tuple((jnp.full((1, tq), NEG_INF, F32), jnp.zeros((V_ROWS, tq), F32)) for _ in heads)
    carry = lax.fori_loop(0, i, functools.partial(block, masked=False), init)
    final = block(i, carry, True)
    outs = [final[h][1][:MLA_V] / final[h][1][MLA_V:MLA_V + 1] for h in heads]
    for pr in range(ATTN_HEADS_PER_STEP // 2):
        pair_t = jnp.concatenate([outs[2 * pr], outs[2 * pr + 1]], axis=0)
        o_ref[:, pr * LANES:(pr + 1) * LANES] = pair_t.T.astype(BF16)


def _attn(q_all, k_all, vt_all, *, batch, seq, tq):
    t = q_all.shape[0]
    nq = seq // tq
    g = ATTN_HEADS_PER_STEP
    return pl.pallas_call(
        _attn_kernel,
        grid=(batch, HEADS // g, nq),
        in_specs=[pl.BlockSpec((tq, g * HEAD_PAD), lambda b, p, i: (b * nq + i, p)),
                  pl.BlockSpec((seq, g * HEAD_PAD), lambda b, p, i: (b, p)),
                  pl.BlockSpec((None, nq, g * V_ROWS, tq), lambda b, p, i: (b, 0, p, 0))],
        out_specs=pl.BlockSpec((tq, g * MLA_V), lambda b, p, i: (b * nq + i, p)),
        out_shape=jax.ShapeDtypeStruct((t, HEADS * MLA_V), BF16),
        compiler_params=pltpu.CompilerParams(dimension_semantics=("parallel", "parallel", "arbitrary"),
                                             vmem_limit_bytes=VMEM_LIMIT),
        name="attn",
    )(q_all, k_all, vt_all)


def _rwkv_kernel(zs_ref, wda_ref, wgu_ref, dbias_ref, abias_ref, kk_ref, ka_ref, rk_ref, lnw_ref, lnb_ref,
                 ones_ref, tri_ref, blk_ref,
                 o_ref,
                 st_ref, a_s, r_s, b_s, k_s, bg_s, kg_s, v_s, gc_s, y_s, bonus_s, g_s):
    i = pl.program_id(1)
    tc = zs_ref.shape[0]
    w = RWKV_WIDTH

    @pl.when(i == 0)
    def _():
        st_ref[...] = jnp.zeros_like(st_ref)

    ones_bd = ones_ref[...]

    def head_sum(t):
        return _dot(t.astype(BF16), ones_bd)

    zs = zs_ref[...].astype(F32)
    r, k, v = zs[:, 0:w], zs[:, w:2 * w], zs[:, 2 * w:3 * w]
    da = zs[:, 3 * w:3 * w + LANES]
    gd = zs[:, 3 * w + LANES:]
    lane = lax.broadcasted_iota(jnp.int32, (1, LANES), 1)
    lora_in = jnp.where(lane < DECAY_RANK, jnp.tanh(da), da).astype(BF16)
    pre = _dot(lora_in, wda_ref[...])
    u = -(dbias_ref[...] + pre[:, :w])
    softplus = jnp.maximum(u, 0.0) + jnp.log(1.0 + jnp.exp(-jnp.abs(u)))
    logw = -jnp.exp(-softplus - 0.5)
    eta = _sigmoid(abias_ref[...] + pre[:, w:])
    g_s[...] = _dot(_sigmoid(gd).astype(BF16), wgu_ref[...])
    kk = k * kk_ref[...]
    kk = kk * jnp.minimum(lax.rsqrt(head_sum(kk * kk)), 1e12)
    kp = k * (1.0 + (eta - 1.0) * ka_ref[...])
    bonus_s[...] = head_sum(r * kp * rk_ref[...]) * v

    lw_hi, lw_lo = _split(logw)
    cum = _dot(tri_ref[...], lw_hi) + _dot(tri_ref[...], lw_lo)
    cum_end = _dot(blk_ref[...], lw_hi) + _dot(blk_ref[...], lw_lo)
    b_in = kk * eta
    a_s[...] = (-kk * jnp.exp(cum - logw)).astype(BF16)
    r_s[...] = (r * jnp.exp(cum)).astype(BF16)
    g_inv = jnp.exp(-cum)
    b_s[...] = (b_in * g_inv).astype(BF16)
    k_s[...] = (kp * g_inv).astype(BF16)
    g_rem = jnp.exp(cum_end - cum)
    bg_s[...] = (b_in * g_rem).astype(BF16)
    kg_s[...] = (kp * g_rem).astype(BF16)
    v_s[...] = v.astype(BF16)
    gc_s[...] = jnp.exp(cum_end)

    n2 = 2 * CHUNK
    ri = lax.broadcasted_iota(jnp.int32, (4 * CHUNK, n2), 0)
    ci = lax.broadcasted_iota(jnp.int32, (4 * CHUNK, n2), 1) % CHUNK
    tri_mask = (ri % CHUNK + ri // n2) > ci
    lane2 = lax.broadcasted_iota(jnp.int32, (CHUNK, n2), 1)
    h1 = lane2 < RWKV_HEAD
    bi = lax.broadcasted_iota(jnp.int32, (n2, n2), 0) // RWKV_HEAD
    bj = lax.broadcasted_iota(jnp.int32, (n2, n2), 1) // RWKV_HEAD
    bd_mask = bi == bj
    zero_bf = jnp.zeros((CHUNK, n2), BF16)
    zero_f = jnp.zeros((CHUNK, n2), F32)

    def chunk_group(c, carry):
        nsub = RWKV_CHUNKS_PER_STEP
        blocks = [(sub, p) for sub in range(nsub) for p in range(HEADS // 2)]
        rows = [pl.ds(pl.multiple_of((c * nsub + sub) * CHUNK, CHUNK), CHUNK) for sub in range(nsub)]
        lns = [slice(p * n2, (p + 1) * n2) for p in range(HEADS // 2)]
        a_t = [a_s[rows[sub], lns[p]] for sub, p in blocks]
        r_t = [r_s[rows[sub], lns[p]] for sub, p in blocks]
        v_t = [v_s[rows[sub], lns[p]] for sub, p in blocks]
        ml = []
        for n, (sub, p) in enumerate(blocks):
            lhs = jnp.concatenate([jnp.where(h1, a_t[n], zero_bf), jnp.where(h1, zero_bf, a_t[n]),
                                   jnp.where(h1, r_t[n], zero_bf), jnp.where(h1, zero_bf, r_t[n])], axis=0)
            rhs = jnp.concatenate([b_s[rows[sub], lns[p]], k_s[rows[sub], lns[p]]], axis=0)
            ml.append(jnp.where(tri_mask, _dot_nt(lhs, rhs), 0.0))
        ml_b = [m.astype(BF16) for m in ml]
        units = [(n, hh) for n in range(len(blocks)) for hh in range(2)]
        w_t = [_dot(ml_b[n][hh * CHUNK:(hh + 1) * CHUNK, :], jnp.concatenate([zero_bf, v_t[n]], axis=0))
               for n, hh in units]
        a_f = [t.astype(F32) for t in a_t]
        a_sw = [pltpu.roll(t, RWKV_HEAD, 1) for t in a_f]
        x = [jnp.where(h1, a_f[n], pltpu.roll(w_t[2 * n], RWKV_HEAD, 1)) if hh == 0
             else jnp.where(h1, a_sw[n], w_t[2 * n + 1]) for n, hh in units]
        lsq = [jnp.where(h1, ml[n][hh * CHUNK:(hh + 1) * CHUNK, :], 0.0) for n, hh in units]
        for j in range(6):
            out = [_dot(lsq[u][:, :CHUNK].astype(BF16),
                        jnp.concatenate([x[u], lsq[u]], axis=1).astype(BF16)) for u in range(len(units))]
            x = [x[u] + out[u][:, :n2] for u in range(len(units))]
            lsq = [t[:, n2:] for t in out]
        z = []
        for n in range(len(blocks)):
            x0, x1 = x[2 * n], x[2 * n + 1]
            a_hat = jnp.where(h1, x0, pltpu.roll(x1, RWKV_HEAD, 1))
            u0 = jnp.where(h1, pltpu.roll(x0, RWKV_HEAD, 1), x1)
            z.append(jnp.concatenate([jnp.concatenate([a_hat, u0], axis=1),
                                      jnp.concatenate([zero_f, v_t[n].astype(F32)], axis=1)],
                                     axis=0).astype(BF16))
        o0 = [_dot(ml_b[n][2 * CHUNK:3 * CHUNK, :], z[n]) for n in range(len(blocks))]
        o1 = [_dot(ml_b[n][3 * CHUNK:4 * CHUNK, :], z[n]) for n in range(len(blocks))]
        pm = [_dot_tn(z[n], jnp.concatenate([bg_s[rows[sub], lns[p]], kg_s[rows[sub], lns[p]]], axis=0))
              for n, (sub, p) in enumerate(blocks)]
        r_hat = [(r_t[n].astype(F32) + jnp.where(h1, o0[n][:, :n2], o1[n][:, :n2])).astype(BF16)
                 for n in range(len(blocks))]
        for n, (sub, p) in enumerate(blocks):
            y0 = jnp.where(h1, o0[n][:, n2:], o1[n][:, n2:])
            g_t = jnp.where(bd_mask, pm[n][:n2], 0.0).astype(BF16)
            h_t = jnp.where(bd_mask, pm[n][n2:], 0.0)
            s = st_ref[p]
            s_b = s.astype(BF16)
            y_s[rows[sub], lns[p]] = _dot_nt(r_hat[n], s_b) + y0
            gc = gc_s[pl.ds(pl.multiple_of((c * nsub + sub) * CHUNK, CHUNK), 8), lns[p]][0:1]
            st_ref[p] = gc * s + _dot(s_b, g_t) + h_t
        return carry

    lax.fori_loop(0, tc // (CHUNK * RWKV_CHUNKS_PER_STEP), chunk_group, 0)

    y = y_s[...]
    inv_n = 1.0 / RWKV_HEAD
    dev = y - head_sum(y) * inv_n
    var = head_sum(dev * dev) * inv_n
    out = (dev * lax.rsqrt(var + GN_EPS) * lnw_ref[...] + lnb_ref[...] + bonus_s[...]) * g_s[...]
    o_ref[...] = out.astype(BF16)


def _rwkv(zs, w_da, w_gu, dbias, abias, k_k, k_a, r_k, lnw, lnb, ones_bd, tri, blk, *, batch, seq, tc):
    t = zs.shape[0]
    nc = seq // tc
    w = RWKV_WIDTH
    consts = [w_da, w_gu, dbias, abias, k_k, k_a, r_k, lnw, lnb, ones_bd, tri, blk]
    return pl.pallas_call(
        _rwkv_kernel,
        grid=(batch, nc),
        in_specs=[pl.BlockSpec((tc, RWKV_IN), lambda b, i: (b * nc + i, 0))] + [_const_spec(a.shape) for a in consts],
        out_specs=pl.BlockSpec((tc, w), lambda b, i: (b * nc + i, 0)),
        out_shape=jax.ShapeDtypeStruct((t, w), BF16),
        scratch_shapes=[pltpu.VMEM((HEADS // 2, 2 * RWKV_HEAD, 2 * RWKV_HEAD), F32)]
        + [pltpu.VMEM((tc, w), BF16)] * 7
        + [pltpu.VMEM((tc, w), F32)] * 4,
        compiler_params=pltpu.CompilerParams(dimension_semantics=("parallel", "arbitrary"),
                                             vmem_limit_bytes=VMEM_LIMIT),
        name="rwkv",
    )(zs, *consts)


def _route(logits):
    lane = lax.broadcasted_iota(jnp.int32, logits.shape, 1)
    lane_f = lane.astype(F32)
    big = float(ROUTER_WIDTH)
    is_group = (lane >= N_EXPERTS) & (lane < N_EXPERTS + N_GROUPS)
    gl = jnp.where(is_group, logits, NEG_INF)
    g_max = jnp.max(gl, axis=-1, keepdims=True)
    g_sel = jnp.min(jnp.where(gl == g_max, lane_f, big), axis=-1, keepdims=True) - float(N_EXPERTS)
    p_sel = 1.0 / jnp.sum(jnp.where(is_group, jnp.exp(gl - g_max), 0.0), axis=-1, keepdims=True)
    in_group = (lane < N_EXPERTS) & ((lane // EXPERTS_PER_GROUP).astype(F32) == g_sel)
    el = jnp.where(in_group, logits, NEG_INF)
    t1 = jnp.max(el, axis=-1, keepdims=True)
    i1 = jnp.min(jnp.where(el == t1, lane_f, big), axis=-1, keepdims=True)
    el2 = jnp.where(lane_f == i1, NEG_INF, el)
    t2 = jnp.max(el2, axis=-1, keepdims=True)
    i2 = jnp.min(jnp.where(el2 == t2, lane_f, big), axis=-1, keepdims=True)
    e21 = jnp.exp(t2 - t1)
    w1 = p_sel / (1.0 + e21)
    w2 = w1 * e21
    first = g_sel * float(EXPERTS_PER_GROUP)
    cw = jnp.where(lane_f == i1 - first, w1, 0.0) + jnp.where(lane_f == i2 - first, w2, 0.0)
    return g_sel, cw


def _merge_kernel(attn_ref, rw_ref, gate_ref, x_ref, mod_ref, womla_ref, worw_ref, wout_ref,
                  gpost_ref, gffn_ref, wrh_ref, wrl_ref, br_ref, stril_ref,
                  x1_ref, hp_ref, dest_ref, cnt_ref, carry_ref, *, group_capacity):
    i = pl.program_id(0)
    tm, d = x_ref.shape

    @pl.when(i == 0)
    def _():
        carry_ref[...] = jnp.zeros_like(carry_ref)

    mod = mod_ref[0]
    gate1, shift2, scale2 = mod[2:3], mod[3:4], mod[4:5]

    ts = tm // MERGE_SUBTILES
    subs = [slice(n * ts, (n + 1) * ts) for n in range(MERGE_SUBTILES)]
    o_mla = [_dot(attn_ref[r, :], womla_ref[...]) for r in subs]
    o_rw = [_dot(rw_ref[r, :], worw_ref[...]) for r in subs]
    o = [(gate_ref[r, :d].astype(F32) * o_mla[n] + gate_ref[r, d:].astype(F32) * o_rw[n]).astype(BF16)
         for n, r in enumerate(subs)]
    y = [_dot(t, wout_ref[...]) for t in o]
    x1 = [x_ref[r, :] + gate1 * _rms(y[n], gpost_ref[...]) for n, r in enumerate(subs)]
    h2 = [_rms(t, gffn_ref[...]) * (1.0 + scale2) + shift2 for t in x1]
    logits = [_dot3(t, wrh_ref[...], wrl_ref[...]) + br_ref[...] for t in h2]
    for n, r in enumerate(subs):
        x1_ref[r, :] = x1[n]
        hp_ref[r, :d] = h2[n]

    lane_f = lax.broadcasted_iota(jnp.int32, (ts, ROUTER_WIDTH), 1).astype(F32)
    routes = [_route(t) for t in logits]
    g_sels = [g for g, _ in routes]
    for n, r in enumerate(subs):
        hp_ref[r, d:] = routes[n][1]
    onehots = [jnp.where(lane_f == g, 1.0, 0.0) for g in g_sels]

    onehot = jnp.concatenate(onehots, axis=0)
    earlier = _dot(stril_ref[...], onehot.astype(BF16))
    carry = carry_ref[0:1, :]
    for n, r in enumerate(subs):
        rank = jnp.sum(jnp.where(lane_f == g_sels[n], carry + earlier[r, :], 0.0), axis=-1, keepdims=True)
        dest_ref[r, :] = (g_sels[n] * float(group_capacity) + rank).astype(jnp.int32)
    total = jnp.broadcast_to(carry + jnp.sum(onehot, axis=0, keepdims=True), carry_ref.shape)
    carry_ref[...] = total
    cnt_ref[...] = total


def _merge(attn, rw, gates, x2, mod3, w_o_mla, w_o_rwkv, w_out, g_post, g_ffn, wr_hi, wr_lo, b_r, *, seq, tm):
    t, d = x2.shape
    ns = seq // tm
    row = lambda i: (i, 0)
    tid = jnp.arange(tm)
    stril = (tid[None, :] < tid[:, None]).astype(BF16)
    consts = [w_o_mla, w_o_rwkv, w_out, g_post, g_ffn, wr_hi, wr_lo, b_r, stril]
    return pl.pallas_call(
        functools.partial(_merge_kernel, group_capacity=t),
        grid=(t // tm,),
        in_specs=[pl.BlockSpec((tm, attn.shape[1]), row), pl.BlockSpec((tm, rw.shape[1]), row),
                  pl.BlockSpec((tm, 2 * d), row), pl.BlockSpec((tm, d), row),
                  pl.BlockSpec((1, 6, d), lambda i: (i // ns, 0, 0))] + [_const_spec(a.shape) for a in consts],
        out_specs=[pl.BlockSpec((tm, d), row), pl.BlockSpec((tm, d + ROUTER_WIDTH), row),
                   pl.BlockSpec((tm, 1), row), pl.BlockSpec((8, ROUTER_WIDTH), lambda i: (0, 0))],
        out_shape=[jax.ShapeDtypeStruct((t, d), F32), jax.ShapeDtypeStruct((t, d + ROUTER_WIDTH), F32),
                   jax.ShapeDtypeStruct((t, 1), jnp.int32), jax.ShapeDtypeStruct((8, ROUTER_WIDTH), F32)],
        scratch_shapes=[pltpu.VMEM((8, ROUTER_WIDTH), F32)],
        compiler_params=pltpu.CompilerParams(dimension_semantics=("arbitrary",), vmem_limit_bytes=VMEM_LIMIT),
        name="merge",
    )(attn, rw, gates, x2, mod3, *consts)


def _dispatch_kernel(dest_ref, hp_ref, xs_ref, sem):
    td = hp_ref.shape[0]
    base = pl.program_id(0) * td

    def issue(r, c):
        pltpu.make_async_copy(hp_ref.at[pl.ds(r, 1)], xs_ref.at[pl.ds(dest_ref[base + r], 1)], sem).start()
        return c

    lax.fori_loop(0, td, issue, 0, unroll=8)
    pltpu.make_async_copy(hp_ref, xs_ref.at[pl.ds(0, td)], sem).wait()


def _dispatch(dest, hp, *, rows_out, td):
    t, w = hp.shape
    return pl.pallas_call(
        _dispatch_kernel,
        grid_spec=pltpu.PrefetchScalarGridSpec(
            num_scalar_prefetch=1, grid=(t // td,),
            in_specs=[pl.BlockSpec((td, w), lambda i, dest: (i, 0))],
            out_specs=pl.BlockSpec(memory_space=pl.ANY),
            scratch_shapes=[pltpu.SemaphoreType.DMA]),
        out_shape=jax.ShapeDtypeStruct((rows_out, w), F32),
        compiler_params=pltpu.CompilerParams(dimension_semantics=("arbitrary",), vmem_limit_bytes=VMEM_LIMIT),
        name="dispatch",
    )(dest, hp)


def _experts_kernel(tg_ref, tb_ref, tr_ref, xs_ref, wg_ref, wu_ref, wd_ref, ex_ref, ys_ref):
    del tg_ref, tb_ref
    nrows = tr_ref[pl.program_id(0)]

    @pl.when(nrows > 0)
    def _():
        tmx = xs_ref.shape[0]
        valid = lax.broadcasted_iota(jnp.int32, (tmx, 1), 0) < nrows
        tile = jnp.where(valid, xs_ref[...], 0.0)
        x = tile[:, :D_MODEL].astype(BF16)
        cw_hi, cw_lo = _split(tile[:, D_MODEL:])
        cwx = _dot(cw_hi, ex_ref[...]) + _dot(cw_lo, ex_ref[...])
        acc = jnp.zeros((tmx, D_MODEL), F32)
        for e in range(EXPERTS_PER_GROUP):
            a = _dot(x, wg_ref[e])
            act = a * _sigmoid(a) * _dot(x, wu_ref[e]) * cwx[:, e * EXPERT_FF:(e + 1) * EXPERT_FF]
            acc = acc + _dot(act.astype(BF16), wd_ref[e])
        ys_ref[...] = acc


def _experts(tile_group, tile_blk, tile_rows, xs, w_gate, w_up, w_down, expand, *, tmx):
    rows, w = xs.shape
    nt = tile_group.shape[0]
    d = D_MODEL
    grp = lambda j, tg, tb, tr: (tg[j], 0, 0)
    return pl.pallas_call(
        _experts_kernel,
        grid_spec=pltpu.PrefetchScalarGridSpec(
            num_scalar_prefetch=3, grid=(nt,),
            in_specs=[pl.BlockSpec((tmx, w), lambda j, tg, tb, tr: (tb[j], 0)),
                      pl.BlockSpec((EXPERTS_PER_GROUP, d, EXPERT_FF), grp),
                      pl.BlockSpec((EXPERTS_PER_GROUP, d, EXPERT_FF), grp),
                      pl.BlockSpec((EXPERTS_PER_GROUP, EXPERT_FF, d), grp),
                      pl.BlockSpec(expand.shape, lambda j, tg, tb, tr: (0, 0))],
            out_specs=pl.BlockSpec((tmx, d), lambda j, tg, tb, tr: (tb[j], 0))),
        out_shape=jax.ShapeDtypeStruct((rows, d), F32),
        compiler_params=pltpu.CompilerParams(dimension_semantics=("arbitrary",), vmem_limit_bytes=VMEM_LIMIT),
        name="experts",
    )(tile_group, tile_blk, tile_rows, xs, w_gate, w_up, w_down, expand)


def _final_kernel(dest_ref, x1_ref, mod_ref, gpost_ref, ys_ref, o_ref, ybuf, sem):
    tmf = x1_ref.shape[0]
    base = pl.program_id(0) * tmf

    def issue(r, c):
        pltpu.make_async_copy(ys_ref.at[pl.ds(dest_ref[base + r], 1)], ybuf.at[pl.ds(r, 1)], sem).start()
        return c

    lax.fori_loop(0, tmf, issue, 0, unroll=8)
    pltpu.make_async_copy(ys_ref.at[pl.ds(0, tmf)], ybuf, sem).wait()
    gate2 = mod_ref[0][5:6]
    o_ref[...] = x1_ref[...] + gate2 * _rms(ybuf[...], gpost_ref[...])


def _final(dest, x1, mod3, g_post, ys, *, seq, tmf):
    t, d = x1.shape
    ns = seq // tmf
    return pl.pallas_call(
        _final_kernel,
        grid_spec=pltpu.PrefetchScalarGridSpec(
            num_scalar_prefetch=1, grid=(t // tmf,),
            in_specs=[pl.BlockSpec((tmf, d), lambda i, dest: (i, 0)),
                      pl.BlockSpec((1, 6, d), lambda i, dest: (i // ns, 0, 0)),
                      pl.BlockSpec((1, d), lambda i, dest: (0, 0)),
                      pl.BlockSpec(memory_space=pl.ANY)],
            out_specs=pl.BlockSpec((tmf, d), lambda i, dest: (i, 0)),
            scratch_shapes=[pltpu.VMEM((tmf, d), F32), pltpu.SemaphoreType.DMA]),
        out_shape=jax.ShapeDtypeStruct((t, d), F32),
        compiler_params=pltpu.CompilerParams(dimension_semantics=("arbitrary",), vmem_limit_bytes=VMEM_LIMIT),
        name="final",
    )(dest, x1, mod3, g_post, ys)


def _tile_map(counts, *, capacity, tmx):
    ntile = (counts + tmx - 1) // tmx
    ends = jnp.cumsum(ntile)
    starts = ends - ntile
    j = jnp.arange(capacity // tmx + N_GROUPS, dtype=jnp.int32)
    g = jnp.minimum(jnp.sum((j[:, None] >= ends[None, :]).astype(jnp.int32), axis=1), N_GROUPS - 1)
    local = j - starts[g]
    valid = j < ends[-1]
    blk = g * (capacity // tmx) + local
    rows = jnp.clip(counts[g] - local * tmx, 0, tmx)
    last = ends[-1] - 1
    return (jnp.where(valid, g, g[last]).astype(jnp.int32), jnp.where(valid, blk, blk[last]).astype(jnp.int32),
            jnp.where(valid, rows, 0).astype(jnp.int32))


def _pick_tile(seq, want):
    return want if seq % want == 0 else seq


def _layer(x, c, positions, w_ada, b_ada, g_pre_mix, g_post_mix, g_pre_ffn, g_post_ffn, w_in, g_cq, w_uq,
           g_ckv, w_ukv, w_o_mla, mu_shift, w_decay_up, decay_bias, w_a_up, a_bias, w_g_up, k_k, k_a, r_k,
           lnx_w, lnx_b, w_o_rwkv, w_out, w_router_group, b_router_group, w_router_expert, b_router_expert,
           w_exp_gate, w_exp_up, w_exp_down):
    batch, seq, d = x.shape
    t = batch * seq
    x2 = x.reshape(t, d)
    row1 = lambda a: a.reshape(1, -1)

    mod3 = _ada(c, w_ada, b_ada).reshape(batch, 6, d)

    zeros = lambda n: jnp.zeros((d, n), F32)
    w_c = jnp.concatenate([w_in[:, :MLA_Q_RANK + MLA_KV_RANK], zeros(MLA_NOPE),
                           w_in[:, MLA_Q_RANK + MLA_KV_RANK:MLA_IN], zeros(HEAD_PAD - MLA_NOPE - MLA_ROPE)],
                          axis=1).astype(BF16)
    w_r = w_in[:, MLA_IN:MLA_IN + RWKV_IN].astype(BF16)
    w_g = w_in[:, MLA_IN + RWKV_IN:].astype(BF16)
    w_uq_p = jnp.pad(w_uq.reshape(MLA_Q_RANK, HEADS, MLA_NOPE + MLA_ROPE),
                     ((0, 0), (0, 0), (0, HEAD_PAD - MLA_NOPE - MLA_ROPE))).reshape(MLA_Q_RANK, QK_WIDTH).astype(BF16)
    w_ukv3 = w_ukv.reshape(MLA_KV_RANK, HEADS, MLA_NOPE + MLA_V)
    w_uk_p = jnp.pad(w_ukv3[..., :MLA_NOPE], ((0, 0), (0, 0), (0, HEAD_PAD - MLA_NOPE))
                     ).reshape(MLA_KV_RANK, QK_WIDTH).astype(BF16)
    w_uv_t = jnp.pad(w_ukv3[..., MLA_NOPE:], ((0, 0), (0, 0), (0, V_ROWS - MLA_V))
                     ).reshape(MLA_KV_RANK, HEADS * V_ROWS).T.astype(BF16)
    inv_freq = jnp.power(ROPE_THETA, -jnp.arange(0, MLA_ROPE, 2, dtype=F32) / MLA_ROPE)
    invf = jnp.concatenate([jnp.zeros((MLA_NOPE,), F32), inv_freq, inv_freq,
                            jnp.zeros((HEAD_PAD - MLA_NOPE - MLA_ROPE,), F32)]).reshape(1, LANES)

    tm = _pick_tile(seq, 512)
    q_all, k_all, vt_all, zs, gates = _inproj(
        x2, mod3, positions.reshape(t, 1), row1(g_pre_mix), w_c, w_r, w_g, row1(g_cq), row1(g_ckv),
        w_uq_p, w_uk_p, w_uv_t, row1(mu_shift), invf, batch=batch, seq=seq, tm=tm)

    attn = _attn(q_all, k_all, vt_all, batch=batch, seq=seq, tq=ATTN_BLOCK)

    tc = _pick_tile(seq, 256)
    w_da = jnp.concatenate([
        jnp.concatenate([w_decay_up, jnp.zeros_like(w_decay_up)], axis=1),
        jnp.concatenate([jnp.zeros_like(w_a_up), w_a_up], axis=1)], axis=0).astype(BF16)
    hid = jnp.arange(RWKV_WIDTH) // RWKV_HEAD
    ones_bd = (hid[:, None] == hid[None, :]).astype(BF16)
    tid = jnp.arange(tc)
    same_chunk = (tid[:, None] // CHUNK) == (tid[None, :] // CHUNK)
    tri = (same_chunk & (tid[None, :] <= tid[:, None])).astype(BF16)
    blk = same_chunk.astype(BF16)
    rw = _rwkv(zs, w_da, w_g_up.astype(BF16), row1(decay_bias), row1(a_bias), row1(k_k), row1(k_a), row1(r_k),
               row1(lnx_w), row1(lnx_b), ones_bd, tri, blk, batch=batch, seq=seq, tc=tc)

    w_rt = jnp.concatenate([w_router_expert, w_router_group,
                            jnp.zeros((d, ROUTER_WIDTH - N_EXPERTS - N_GROUPS), F32)], axis=1)
    wr_hi = w_rt.astype(BF16)
    wr_lo = (w_rt - wr_hi.astype(F32)).astype(BF16)
    b_r = jnp.concatenate([b_router_expert, b_router_group,
                           jnp.zeros((ROUTER_WIDTH - N_EXPERTS - N_GROUPS,), F32)]).reshape(1, ROUTER_WIDTH)
    x1, hp, dest, cnt = _merge(attn, rw, gates, x2, mod3, w_o_mla.astype(BF16), w_o_rwkv.astype(BF16),
                               w_out.astype(BF16), row1(g_post_mix), row1(g_pre_ffn), wr_hi, wr_lo, b_r,
                               seq=seq, tm=_pick_tile(seq, 512))

    tmx = _pick_tile(seq, 512)
    dest = dest.reshape(t)
    xs = _dispatch(dest, hp, rows_out=N_GROUPS * t, td=_pick_tile(seq, 1024))
    tile_group, tile_blk, tile_rows = _tile_map(cnt[0, :N_GROUPS].astype(jnp.int32), capacity=t, tmx=tmx)
    eid = jnp.arange(EXPERTS_PER_GROUP * EXPERT_FF) // EXPERT_FF
    expand = (jnp.arange(ROUTER_WIDTH)[:, None] == eid[None, :]).astype(BF16)
    ys = _experts(tile_group, tile_blk, tile_rows, xs, w_exp_gate.astype(BF16), w_exp_up.astype(BF16),
                  w_exp_down.astype(BF16), expand, tmx=tmx)
    out = _final(dest, x1, mod3, row1(g_post_ffn), ys, seq=seq, tmf=_pick_tile(seq, 512))
    return out.reshape(batch, seq, d)


def kernel(x, c, positions, w_ada, b_ada, g_pre_mix, g_post_mix, g_pre_ffn, g_post_ffn, w_in, g_cq, w_uq, g_ckv, w_ukv, w_o_mla, mu_shift, w_decay_up, decay_bias, w_a_up, a_bias, w_g_up, k_k, k_a, r_k, lnx_w, lnx_b, w_o_rwkv, w_out, w_router_group, b_router_group, w_router_expert, b_router_expert, w_exp_gate, w_exp_up, w_exp_down):
    depth = w_ada.shape[0]
    for l in range(depth):
        x = _layer(x, c, positions, w_ada[l], b_ada[l], g_pre_mix[l], g_post_mix[l], g_pre_ffn[l], g_post_ffn[l],
                   w_in[l], g_cq[l], w_uq[l], g_ckv[l], w_ukv[l], w_o_mla[l], mu_shift[l], w_decay_up[l],
                   decay_bias[l], w_a_up[l], a_bias[l], w_g_up[l], k_k[l], k_a[l], r_k[l], lnx_w[l], lnx_b[l],
                   w_o_rwkv[l], w_out[l], w_router_group[l], b_router_group[l], w_router_expert[l],
                   b_router_expert[l], w_exp_gate[l], w_exp_up[l], w_exp_down[l])
    return x
```

```python
import functools
import math

import jax
import jax.numpy as jnp
from jax import lax
from jax.experimental import pallas as pl
from jax.experimental.pallas import tpu as pltpu

F32 = jnp.float32
BF16 = jnp.bfloat16

D_MODEL = 1024
CHUNK = 64
HEADS = 8
MLA_NOPE = 64
MLA_ROPE = 32
MLA_V = 64
MLA_Q_RANK = 384
MLA_KV_RANK = 256
ROPE_THETA = 10000.0
RWKV_HEAD = 64
RWKV_WIDTH = HEADS * RWKV_HEAD
DECAY_RANK = 64
AAA_RANK = 64
GATE_RANK = 128
GN_EPS = 64e-5
N_GROUPS = 4
EXPERTS_PER_GROUP = 8
N_EXPERTS = N_GROUPS * EXPERTS_PER_GROUP
EXPERT_FF = 256
RMS_EPS = 1e-6
NEG_INF = -1e30
MLA_IN = MLA_Q_RANK + MLA_KV_RANK + MLA_ROPE
RWKV_IN = 3 * RWKV_WIDTH + DECAY_RANK + AAA_RANK + GATE_RANK

LANES = 128
SUBLANES = 8
HEAD_PAD = LANES
QK_WIDTH = HEADS * HEAD_PAD
MLA_C_WIDTH = 768
ROUTER_WIDTH = LANES
VMEM_LIMIT = 56 * 1024 * 1024
ATTN_HEADS_PER_STEP = 8
ATTN_BLOCK = 256
V_ROWS = 80
RWKV_CHUNKS_PER_STEP = 4
MERGE_SUBTILES = 2


def _dot(a, b):
    return jnp.dot(a, b, preferred_element_type=F32)


def _dot_nt(a, b):
    return lax.dot_general(a, b, (((1,), (1,)), ((), ())), preferred_element_type=F32)


def _dot_tn(a, b):
    return lax.dot_general(a, b, (((0,), (0,)), ((), ())), preferred_element_type=F32)


def _split(x):
    hi = x.astype(BF16)
    lo = (x - hi.astype(F32)).astype(BF16)
    return hi, lo


def _dot3(x, w_hi, w_lo):
    x_hi, x_lo = _split(x)
    return _dot(x_hi, w_hi) + (_dot(x_hi, w_lo) + _dot(x_lo, w_hi))


def _sigmoid(x):
    return 1.0 / (1.0 + jnp.exp(-x))


def _rms(x, g):
    return x * lax.rsqrt(jnp.mean(x * x, axis=-1, keepdims=True) + RMS_EPS) * g


def _ada_kernel(c_ref, w_ref, b_ref, o_ref):
    c = c_ref[...]
    s = c * _sigmoid(c)
    w_hi, w_lo = _split(w_ref[...])
    o_ref[...] = _dot3(s, w_hi, w_lo) + b_ref[...]


def _ada(c, w_ada, b_ada):
    b, d = c.shape
    n = w_ada.shape[1]
    tn = 512
    return pl.pallas_call(
        _ada_kernel,
        grid=(n // tn,),
        in_specs=[pl.BlockSpec((b, d), lambda j: (0, 0)),
                  pl.BlockSpec((d, tn), lambda j: (0, j)),
                  pl.BlockSpec((1, tn), lambda j: (0, j))],
        out_specs=pl.BlockSpec((b, tn), lambda j: (0, j)),
        out_shape=jax.ShapeDtypeStruct((b, n), F32),
        name="ada",
    )(c, w_ada, b_ada.reshape(1, n))


def _inproj_kernel(x_ref, mod_ref, pos_ref, gpre_ref, wc_ref, wr_ref, wg_ref, gcq_ref, gckv_ref,
                   wuq_ref, wuk_ref, wuv_ref, mu_ref, invf_ref,
                   q_ref, k_ref, vt_ref, zs_ref, gate_ref, carry_ref):
    i = pl.program_id(1)
    tm = x_ref.shape[0]

    @pl.when(i == 0)
    def _():
        carry_ref[...] = jnp.zeros_like(carry_ref)

    mod = mod_ref[0]
    shift, scale = mod[0:1], mod[1:2]
    h = (_rms(x_ref[...], gpre_ref[...]) * (1.0 + scale) + shift).astype(BF16)

    zc = _dot(h, wc_ref[...])
    zg = _dot(h, wg_ref[...])
    nq = _rms(zc[:, :MLA_Q_RANK], gcq_ref[...]).astype(BF16)
    nkv = _rms(zc[:, MLA_Q_RANK:MLA_Q_RANK + MLA_KV_RANK], gckv_ref[...]).astype(BF16)
    kr = zc[:, MLA_Q_RANK + MLA_KV_RANK:]
    qf = _dot(nq, wuq_ref[...])
    kn = _dot(nkv, wuk_ref[...])
    vt = _dot_nt(wuv_ref[...], nkv)
    zr = _dot(h, wr_ref[...])

    ang = pos_ref[...].astype(F32) * invf_ref[...]
    lane = lax.broadcasted_iota(jnp.int32, (1, LANES), 1)
    first_half = lane < MLA_NOPE + MLA_ROPE // 2
    in_rope = (lane >= MLA_NOPE) & (lane < MLA_NOPE + MLA_ROPE)
    cos_t = jnp.where(in_rope, jnp.cos(ang), 1.0)
    sin_a = jnp.sin(ang)
    sin_t = jnp.where(in_rope, jnp.where(first_half, -sin_a, sin_a), 0.0)

    def rope(t):
        rot = jnp.where(first_half, pltpu.roll(t, LANES - MLA_ROPE // 2, 1), pltpu.roll(t, MLA_ROPE // 2, 1))
        return t * cos_t + rot * sin_t

    kr_rot = rope(kr)
    q_scale = math.log2(math.e) / math.sqrt(MLA_NOPE + MLA_ROPE)
    for hh in range(HEADS):
        sl = slice(hh * HEAD_PAD, (hh + 1) * HEAD_PAD)
        q_ref[:, sl] = (rope(qf[:, sl]) * q_scale).astype(BF16)
        k_ref[:, sl] = (kn[:, sl] + kr_rot).astype(BF16)
    vrow = lax.broadcasted_iota(jnp.int32, vt.shape, 0) % V_ROWS
    vt = jnp.where(vrow == MLA_V, 1.0, vt).astype(BF16)
    for j in range(vt_ref.shape[0]):
        vt_ref[j] = vt[:, j * ATTN_BLOCK:(j + 1) * ATTN_BLOCK]

    gate_ref[...] = _sigmoid(zg).astype(BF16)

    row = lax.broadcasted_iota(jnp.int32, (tm, 1), 0)
    prev = jnp.where(row == 0, carry_ref[0:1, :], pltpu.roll(zr, 1, 0))
    carry_ref[0:1, :] = zr[tm - 1:tm, :]
    zs_ref[...] = (zr + (prev - zr) * mu_ref[...]).astype(BF16)


def _const_spec(shape):
    nd = len(shape)
    return pl.BlockSpec(shape, lambda *_: (0,) * nd, pipeline_mode=pl.Buffered(1))


def _inproj(x2, mod3, pos2, g_pre, w_c, w_r, w_g, g_cq, g_ckv, w_uq, w_uk, w_uv, mu, invf, *, batch, seq, tm):
    t, d = x2.shape
    ns = seq // tm
    row = lambda b, i: (b * ns + i, 0)
    widths = [QK_WIDTH, QK_WIDTH, RWKV_IN, 2 * d]
    specs = [pl.BlockSpec((tm, w), row) for w in widths]
    shapes = [jax.ShapeDtypeStruct((t, w), BF16) for w in widths]
    vw = HEADS * V_ROWS
    nb = tm // ATTN_BLOCK
    specs.insert(2, pl.BlockSpec((None, nb, vw, ATTN_BLOCK), lambda b, i: (b, i, 0, 0)))
    shapes.insert(2, jax.ShapeDtypeStruct((batch, ns * nb, vw, ATTN_BLOCK), BF16))
    return pl.pallas_call(
        _inproj_kernel,
        grid=(batch, ns),
        in_specs=[pl.BlockSpec((tm, d), row),
                  pl.BlockSpec((1, 6, d), lambda b, i: (b, 0, 0)),
                  pl.BlockSpec((tm, 1), row),
                  _const_spec(g_pre.shape), _const_spec(w_c.shape), _const_spec(w_r.shape),
                  _const_spec(w_g.shape), _const_spec(g_cq.shape), _const_spec(g_ckv.shape),
                  _const_spec(w_uq.shape), _const_spec(w_uk.shape), _const_spec(w_uv.shape),
                  _const_spec(mu.shape), _const_spec(invf.shape)],
        out_specs=specs,
        out_shape=shapes,
        scratch_shapes=[pltpu.VMEM((8, RWKV_IN), F32)],
        compiler_params=pltpu.CompilerParams(dimension_semantics=("parallel", "arbitrary"),
                                             vmem_limit_bytes=VMEM_LIMIT),
        name="inproj",
    )(x2, mod3, pos2, g_pre, w_c, w_r, w_g, g_cq, g_ckv, w_uq, w_uk, w_uv, mu, invf)


def _attn_kernel(q_ref, k_ref, vt_ref, o_ref):
    i = pl.program_id(2)
    tq = q_ref.shape[0]
    kc = lax.broadcasted_iota(jnp.int32, (tq, tq), 0) // CHUNK
    qc = lax.broadcasted_iota(jnp.int32, (tq, tq), 1) // CHUNK
    diag_mask = kc <= qc
    heads = range(ATTN_HEADS_PER_STEP)
    hs = [slice(h * HEAD_PAD, (h + 1) * HEAD_PAD) for h in heads]
    vr = [slice(h * V_ROWS, (h + 1) * V_ROWS) for h in heads]
    q = [q_ref[:, s] for s in hs]

    def block(kb, carry, masked):
        rows = pl.ds(pl.multiple_of(kb * tq, tq), tq)
        s = [_dot_nt(k_ref[rows, hs[h]], q[h]) for h in heads]
        if masked:
            s = [jnp.where(diag_mask, t, NEG_INF) for t in s]
        m_new = [jnp.maximum(carry[h][0], jnp.max(s[h], axis=0, keepdims=True)) for h in heads]
        alpha = [jnp.exp2(carry[h][0] - m_new[h]) for h in heads]
        p = [jnp.exp2(s[h] - m_new[h]).astype(BF16) for h in heads]
        acc = [alpha[h] * carry[h][1] + _dot(vt_ref[kb, vr[h], :], p[h]) for h in heads]
        return tuple((m_new[h], acc[h]) for h in heads)

    init = tuple((jnp.full((1, tq), NEG_INF, F32), jnp.zeros((V_ROWS, tq), F32)) for _ in heads)
    carry = lax.fori_loop(0, i, functools.partial(block, masked=False), init)
    final = block(i, carry, True)
    outs = [final[h][1][:MLA_V] / final[h][1][MLA_V:MLA_V + 1] for h in heads]
    for pr in range(ATTN_HEADS_PER_STEP // 2):
        pair_t = jnp.concatenate([outs[2 * pr], outs[2 * pr + 1]], axis=0)
        o_ref[:, pr * LANES:(pr + 1) * LANES] = pair_t.T.astype(BF16)


def _attn(q_all, k_all, vt_all, *, batch, seq, tq):
    t = q_all.shape[0]
    nq = seq // tq
    g = ATTN_HEADS_PER_STEP
    return pl.pallas_call(
        _attn_kernel,
        grid=(batch, HEADS // g, nq),
        in_specs=[pl.BlockSpec((tq, g * HEAD_PAD), lambda b, p, i: (b * nq + i, p)),
                  pl.BlockSpec((seq, g * HEAD_PAD), lambda b, p, i: (b, p)),
                  pl.BlockSpec((None, nq, g * V_ROWS, tq), lambda b, p, i: (b, 0, p, 0))],
        out_specs=pl.BlockSpec((tq, g * MLA_V), lambda b, p, i: (b * nq + i, p)),
        out_shape=jax.ShapeDtypeStruct((t, HEADS * MLA_V), BF16),
        compiler_params=pltpu.CompilerParams(dimension_semantics=("parallel", "parallel", "arbitrary"),
                                             vmem_limit_bytes=VMEM_LIMIT),
        name="attn",
    )(q_all, k_all, vt_all)


def _rwkv_kernel(zs_ref, wda_ref, wgu_ref, dbias_ref, abias_ref, kk_ref, ka_ref, rk_ref, lnw_ref, lnb_ref,
                 ones_ref, tri_ref,
                 o_ref,
                 st_ref, a_s, r_s, b_s, k_s, bg_s, kg_s, v_s, gc_s, bonus_s, g_s, y_s):
    tc = zs_ref.shape[0]
    w = RWKV_WIDTH
    staged = (a_s, r_s, b_s, k_s, bg_s, kg_s, v_s, gc_s, bonus_s, g_s)

    @pl.when(pl.program_id(1) == 0)
    def _():
        st_ref[...] = jnp.zeros_like(st_ref)

    ones_bd = ones_ref[...]

    def head_sum(t):
        tb = t.astype(BF16)
        hw = ones_bd.shape[0]
        return jnp.concatenate([_dot(tb[:, n * hw:(n + 1) * hw], ones_bd) for n in range(w // hw)], axis=1)

    nsub = tc // CHUNK
    n2 = 2 * CHUNK
    zs = zs_ref[...].astype(F32)
    r, k, v = zs[:, 0:w], zs[:, w:2 * w], zs[:, 2 * w:3 * w]
    da = zs[:, 3 * w:3 * w + LANES]
    gd = zs[:, 3 * w + LANES:]
    lane = lax.broadcasted_iota(jnp.int32, (1, LANES), 1)
    lora_in = jnp.where(lane < DECAY_RANK, jnp.tanh(da), da).astype(BF16)
    pre = _dot(lora_in, wda_ref[...])
    u = -(dbias_ref[...] + pre[:, :w])
    softplus = jnp.maximum(u, 0.0) + jnp.log(1.0 + jnp.exp(-jnp.abs(u)))
    logw = -jnp.exp(-softplus - 0.5)
    eta = _sigmoid(abias_ref[...] + pre[:, w:])
    g_new = _dot(_sigmoid(gd).astype(BF16), wgu_ref[...])
    kk = k * kk_ref[...]
    kk = kk * jnp.minimum(lax.rsqrt(head_sum(kk * kk)), 1e12)
    kp = k * (1.0 + (eta - 1.0) * ka_ref[...])
    bonus_new = head_sum(r * kp * rk_ref[...]) * v

    lw_hi, lw_lo = _split(logw)
    cum = _dot(tri_ref[...], lw_hi) + _dot(tri_ref[...], lw_lo)
    cum_end = jnp.concatenate([jnp.broadcast_to(cum[(c + 1) * CHUNK - 1:(c + 1) * CHUNK, :], (CHUNK, w))
                               for c in range(nsub)], axis=0)
    b_in = kk * eta
    g_inv = jnp.exp(-cum)
    g_rem = jnp.exp(cum_end - cum)
    staged_new = ((-kk * jnp.exp(cum - logw)).astype(BF16), (r * jnp.exp(cum)).astype(BF16),
                  (b_in * g_inv).astype(BF16), (kp * g_inv).astype(BF16),
                  (b_in * g_rem).astype(BF16), (kp * g_rem).astype(BF16), v.astype(BF16),
                  jnp.exp(cum_end), bonus_new, g_new)
    for ref, val in zip(staged, staged_new):
        ref[...] = val

    blocks = [(sub, p) for sub in range(nsub) for p in range(HEADS // 2)]
    rows = [slice(sub * CHUNK, (sub + 1) * CHUNK) for sub in range(nsub)]
    lns = [slice(p * n2, (p + 1) * n2) for p in range(HEADS // 2)]
    a_t = [a_s[rows[sub], lns[p]] for sub, p in blocks]
    r_t = [r_s[rows[sub], lns[p]] for sub, p in blocks]
    v_t = [v_s[rows[sub], lns[p]] for sub, p in blocks]
    bk_t = [jnp.concatenate([b_s[rows[sub], lns[p]], k_s[rows[sub], lns[p]]], axis=0) for sub, p in blocks]
    bkg_t = [jnp.concatenate([bg_s[rows[sub], lns[p]], kg_s[rows[sub], lns[p]]], axis=0) for sub, p in blocks]
    gc_t = [gc_s[sub * CHUNK:sub * CHUNK + 1, lns[p]] for sub, p in blocks]
    ri = lax.broadcasted_iota(jnp.int32, (4 * CHUNK, n2), 0)
    ci = lax.broadcasted_iota(jnp.int32, (4 * CHUNK, n2), 1) % CHUNK
    tri_mask = (ri % CHUNK + ri // n2) > ci
    lane2 = lax.broadcasted_iota(jnp.int32, (CHUNK, n2), 1)
    h1 = lane2 < RWKV_HEAD
    bi = lax.broadcasted_iota(jnp.int32, (n2, n2), 0) // RWKV_HEAD
    bj = lax.broadcasted_iota(jnp.int32, (n2, n2), 1) // RWKV_HEAD
    bd_mask = bi == bj
    zero_bf = jnp.zeros((CHUNK, n2), BF16)
    zero_f = jnp.zeros((CHUNK, n2), F32)

    ml = []
    for n, (sub, p) in enumerate(blocks):
        lhs = jnp.concatenate([jnp.where(h1, a_t[n], zero_bf), jnp.where(h1, zero_bf, a_t[n]),
                               jnp.where(h1, r_t[n], zero_bf), jnp.where(h1, zero_bf, r_t[n])], axis=0)
        ml.append(jnp.where(tri_mask, _dot_nt(lhs, bk_t[n]), 0.0))
    ml_b = [m.astype(BF16) for m in ml]
    units = [(n, hh) for n in range(len(blocks)) for hh in range(2)]
    w_t = [_dot(ml_b[n][hh * CHUNK:(hh + 1) * CHUNK, :], jnp.concatenate([zero_bf, v_t[n]], axis=0))
           for n, hh in units]
    a_f = [t.astype(F32) for t in a_t]
    a_sw = [pltpu.roll(t, RWKV_HEAD, 1) for t in a_f]
    x = [jnp.where(h1, a_f[n], pltpu.roll(w_t[2 * n], RWKV_HEAD, 1)) if hh == 0
         else jnp.where(h1, a_sw[n], w_t[2 * n + 1]) for n, hh in units]
    lsq = [jnp.where(h1, ml[n][hh * CHUNK:(hh + 1) * CHUNK, :], 0.0) for n, hh in units]
    for _ in range(6):
        out = [_dot(lsq[u][:, :CHUNK].astype(BF16),
                    jnp.concatenate([x[u], lsq[u]], axis=1).astype(BF16)) for u in range(len(units))]
        x = [x[u] + out[u][:, :n2] for u in range(len(units))]
        lsq = [t[:, n2:] for t in out]
    z = []
    for n in range(len(blocks)):
        x0, x1 = x[2 * n], x[2 * n + 1]
        a_hat = jnp.where(h1, x0, pltpu.roll(x1, RWKV_HEAD, 1))
        u0 = jnp.where(h1, pltpu.roll(x0, RWKV_HEAD, 1), x1)
        z.append(jnp.concatenate([jnp.concatenate([a_hat, u0], axis=1),
                                  jnp.concatenate([zero_f, v_t[n].astype(F32)], axis=1)],
                                 axis=0).astype(BF16))
    o0 = [_dot(ml_b[n][2 * CHUNK:3 * CHUNK, :], z[n]) for n in range(len(blocks))]
    o1 = [_dot(ml_b[n][3 * CHUNK:4 * CHUNK, :], z[n]) for n in range(len(blocks))]
    pm = [_dot_tn(z[n], bkg_t[n]) for n in range(len(blocks))]
    r_hat = [(r_t[n].astype(F32) + jnp.where(h1, o0[n][:, :n2], o1[n][:, :n2])).astype(BF16)
             for n in range(len(blocks))]
    state = [st_ref[p] for p in range(HEADS // 2)]
    for n, (sub, p) in enumerate(blocks):
        y0 = jnp.where(h1, o0[n][:, n2:], o1[n][:, n2:])
        g_t = jnp.where(bd_mask, pm[n][:n2], 0.0).astype(BF16)
        h_t = jnp.where(bd_mask, pm[n][n2:], 0.0)
        s_b = state[p].astype(BF16)
        y_s[rows[sub], lns[p]] = _dot_nt(r_hat[n], s_b) + y0
        state[p] = gc_t[n] * state[p] + _dot(s_b, g_t) + h_t
    for p in range(HEADS // 2):
        st_ref[p] = state[p]

    y = y_s[...]
    inv_n = 1.0 / RWKV_HEAD
    dev = y - head_sum(y) * inv_n
    var = head_sum(dev * dev) * inv_n
    out = (dev * lax.rsqrt(var + GN_EPS) * lnw_ref[...] + lnb_ref[...] + bonus_s[...]) * g_s[...]
    o_ref[...] = out.astype(BF16)


def _rwkv(zs, w_da, w_gu, dbias, abias, k_k, k_a, r_k, lnw, lnb, ones_bd, tri, *, batch, seq, tc):
    t = zs.shape[0]
    nc = seq // tc
    w = RWKV_WIDTH
    consts = [w_da, w_gu, dbias, abias, k_k, k_a, r_k, lnw, lnb, ones_bd, tri]
    return pl.pallas_call(
        _rwkv_kernel,
        grid=(batch, nc),
        in_specs=[pl.BlockSpec((tc, RWKV_IN), lambda b, i: (b * nc + i, 0))] + [_const_spec(a.shape) for a in consts],
        out_specs=pl.BlockSpec((tc, w), lambda b, i: (b * nc + i, 0)),
        out_shape=jax.ShapeDtypeStruct((t, w), BF16),
        scratch_shapes=[pltpu.VMEM((HEADS // 2, 2 * RWKV_HEAD, 2 * RWKV_HEAD), F32)]
        + [pltpu.VMEM((tc, w), BF16)] * 7
        + [pltpu.VMEM((tc, w), F32)] * 4,
        compiler_params=pltpu.CompilerParams(dimension_semantics=("parallel", "arbitrary"),
                                             vmem_limit_bytes=VMEM_LIMIT),
        name="rwkv",
    )(zs, *consts)


def _route(logits):
    lane = lax.broadcasted_iota(jnp.int32, logits.shape, 1)
    lane_f = lane.astype(F32)
    big = float(ROUTER_WIDTH)
    is_group = (lane >= N_EXPERTS) & (lane < N_EXPERTS + N_GROUPS)
    gl = jnp.where(is_group, logits, NEG_INF)
    g_max = jnp.max(gl, axis=-1, keepdims=True)
    g_sel = jnp.min(jnp.where(gl == g_max, lane_f, big), axis=-1, keepdims=True) - float(N_EXPERTS)
    p_sel = 1.0 / jnp.sum(jnp.where(is_group, jnp.exp(gl - g_max), 0.0), axis=-1, keepdims=True)
    in_group = (lane < N_EXPERTS) & ((lane // EXPERTS_PER_GROUP).astype(F32) == g_sel)
    el = jnp.where(in_group, logits, NEG_INF)
    t1 = jnp.max(el, axis=-1, keepdims=True)
    i1 = jnp.min(jnp.where(el == t1, lane_f, big), axis=-1, keepdims=True)
    el2 = jnp.where(lane_f == i1, NEG_INF, el)
    t2 = jnp.max(el2, axis=-1, keepdims=True)
    i2 = jnp.min(jnp.where(el2 == t2, lane_f, big), axis=-1, keepdims=True)
    e21 = jnp.exp(t2 - t1)
    w1 = p_sel / (1.0 + e21)
    w2 = w1 * e21
    first = g_sel * float(EXPERTS_PER_GROUP)
    cw = jnp.where(lane_f == i1 - first, w1, 0.0) + jnp.where(lane_f == i2 - first, w2, 0.0)
    return g_sel, cw


def _merge_kernel(attn_ref, rw_ref, gate_ref, x_ref, mod_ref, womla_ref, worw_ref, wout_ref,
                  gpost_ref, gffn_ref, wrh_ref, wrl_ref, br_ref, stril_ref,
                  x1_ref, hp_ref, dest_ref, cnt_ref, carry_ref, *, group_capacity):
    i = pl.program_id(0)
    tm, d = x_ref.shape

    @pl.when(i == 0)
    def _():
        carry_ref[...] = jnp.zeros_like(carry_ref)

    mod = mod_ref[0]
    gate1, shift2, scale2 = mod[2:3], mod[3:4], mod[4:5]

    ts = tm // MERGE_SUBTILES
    subs = [slice(n * ts, (n + 1) * ts) for n in range(MERGE_SUBTILES)]
    o_mla = [_dot(attn_ref[r, :], womla_ref[...]) for r in subs]
    o_rw = [_dot(rw_ref[r, :], worw_ref[...]) for r in subs]
    o = [(gate_ref[r, :d].astype(F32) * o_mla[n] + gate_ref[r, d:].astype(F32) * o_rw[n]).astype(BF16)
         for n, r in enumerate(subs)]
    y = [_dot(t, wout_ref[...]) for t in o]
    x1 = [x_ref[r, :] + gate1 * _rms(y[n], gpost_ref[...]) for n, r in enumerate(subs)]
    h2 = [_rms(t, gffn_ref[...]) * (1.0 + scale2) + shift2 for t in x1]
    logits = [_dot3(t, wrh_ref[...], wrl_ref[...]) + br_ref[...] for t in h2]
    for n, r in enumerate(subs):
        x1_ref[r, :] = x1[n]
        hp_ref[r, :d] = h2[n]

    lane_f = lax.broadcasted_iota(jnp.int32, (ts, ROUTER_WIDTH), 1).astype(F32)
    routes = [_route(t) for t in logits]
    g_sels = [g for g, _ in routes]
    for n, r in enumerate(subs):
        hp_ref[r, d:] = routes[n][1]
    onehots = [jnp.where(lane_f == g, 1.0, 0.0) for g in g_sels]

    onehot = jnp.concatenate(onehots, axis=0)
    earlier = _dot(stril_ref[...], onehot.astype(BF16))
    carry = carry_ref[0:1, :]
    for n, r in enumerate(subs):
        rank = jnp.sum(jnp.where(lane_f == g_sels[n], carry + earlier[r, :], 0.0), axis=-1, keepdims=True)
        dest_ref[r, :] = (g_sels[n] * float(group_capacity) + rank).astype(jnp.int32)
    total = jnp.broadcast_to(carry + jnp.sum(onehot, axis=0, keepdims=True), carry_ref.shape)
    carry_ref[...] = total
    cnt_ref[...] = total


def _merge(attn, rw, gates, x2, mod3, w_o_mla, w_o_rwkv, w_out, g_post, g_ffn, wr_hi, wr_lo, b_r, *, seq, tm):
    t, d = x2.shape
    ns = seq // tm
    row = lambda i: (i, 0)
    tid = jnp.arange(tm)
    stril = (tid[None, :] < tid[:, None]).astype(BF16)
    consts = [w_o_mla, w_o_rwkv, w_out, g_post, g_ffn, wr_hi, wr_lo, b_r, stril]
    return pl.pallas_call(
        functools.partial(_merge_kernel, group_capacity=t),
        grid=(t // tm,),
        in_specs=[pl.BlockSpec((tm, attn.shape[1]), row), pl.BlockSpec((tm, rw.shape[1]), row),
                  pl.BlockSpec((tm, 2 * d), row), pl.BlockSpec((tm, d), row),
                  pl.BlockSpec((1, 6, d), lambda i: (i // ns, 0, 0))] + [_const_spec(a.shape) for a in consts],
        out_specs=[pl.BlockSpec((tm, d), row), pl.BlockSpec((tm, d + ROUTER_WIDTH), row),
                   pl.BlockSpec((tm, 1), row), pl.BlockSpec((8, ROUTER_WIDTH), lambda i: (0, 0))],
        out_shape=[jax.ShapeDtypeStruct((t, d), F32), jax.ShapeDtypeStruct((t, d + ROUTER_WIDTH), F32),
                   jax.ShapeDtypeStruct((t, 1), jnp.int32), jax.ShapeDtypeStruct((8, ROUTER_WIDTH), F32)],
        scratch_shapes=[pltpu.VMEM((8, ROUTER_WIDTH), F32)],
        compiler_params=pltpu.CompilerParams(dimension_semantics=("arbitrary",), vmem_limit_bytes=VMEM_LIMIT),
        name="merge",
    )(attn, rw, gates, x2, mod3, *consts)


def _dispatch_kernel(dest_ref, hp_ref, xs_ref, sem):
    td = hp_ref.shape[0] * SUBLANES
    base = pl.program_id(0) * td

    def issue(r8, c):
        for j in range(SUBLANES):
            d = dest_ref[base + r8 * SUBLANES + j]
            pltpu.make_async_copy(hp_ref.at[r8, pl.ds(j, 1)],
                                  xs_ref.at[d >> 3, pl.ds(d & (SUBLANES - 1), 1)], sem).start()
        return c

    lax.fori_loop(0, td // SUBLANES, issue, 0)
    pltpu.make_async_copy(hp_ref, xs_ref.at[pl.ds(0, td // SUBLANES)], sem).wait()


def _dispatch(dest, hp, *, rows_out, td):
    t, w = hp.shape
    xs = pl.pallas_call(
        _dispatch_kernel,
        grid_spec=pltpu.PrefetchScalarGridSpec(
            num_scalar_prefetch=1, grid=(t // td,),
            in_specs=[pl.BlockSpec((td // SUBLANES, SUBLANES, w), lambda i, dest: (i, 0, 0))],
            out_specs=pl.BlockSpec(memory_space=pl.ANY),
            scratch_shapes=[pltpu.SemaphoreType.DMA]),
        out_shape=jax.ShapeDtypeStruct((rows_out // SUBLANES, SUBLANES, w), F32),
        compiler_params=pltpu.CompilerParams(dimension_semantics=("arbitrary",), vmem_limit_bytes=VMEM_LIMIT),
        name="dispatch",
    )(dest, hp.reshape(t // SUBLANES, SUBLANES, w))
    return xs.reshape(rows_out, w)


def _experts_kernel(tg_ref, tb_ref, tr_ref, xs_ref, wg_ref, wu_ref, wd_ref, ex_ref, ys_ref):
    del tg_ref, tb_ref
    nrows = tr_ref[pl.program_id(0)]

    @pl.when(nrows > 0)
    def _():
        tmx = xs_ref.shape[0]
        valid = lax.broadcasted_iota(jnp.int32, (tmx, 1), 0) < nrows
        tile = jnp.where(valid, xs_ref[...], 0.0)
        x = tile[:, :D_MODEL].astype(BF16)
        cw_hi, cw_lo = _split(tile[:, D_MODEL:])
        cwx = _dot(cw_hi, ex_ref[...]) + _dot(cw_lo, ex_ref[...])
        acc = jnp.zeros((tmx, D_MODEL), F32)
        for e in range(EXPERTS_PER_GROUP):
            a = _dot(x, wg_ref[e])
            act = a * _sigmoid(a) * _dot(x, wu_ref[e]) * cwx[:, e * EXPERT_FF:(e + 1) * EXPERT_FF]
            acc = acc + _dot(act.astype(BF16), wd_ref[e])
        ys_ref[...] = acc


def _experts(tile_group, tile_blk, tile_rows, xs, w_gate, w_up, w_down, expand, *, tmx):
    rows, w = xs.shape
    nt = tile_group.shape[0]
    d = D_MODEL
    grp = lambda j, tg, tb, tr: (tg[j], 0, 0)
    return pl.pallas_call(
        _experts_kernel,
        grid_spec=pltpu.PrefetchScalarGridSpec(
            num_scalar_prefetch=3, grid=(nt,),
            in_specs=[pl.BlockSpec((tmx, w), lambda j, tg, tb, tr: (tb[j], 0)),
                      pl.BlockSpec((EXPERTS_PER_GROUP, d, EXPERT_FF), grp),
                      pl.BlockSpec((EXPERTS_PER_GROUP, d, EXPERT_FF), grp),
                      pl.BlockSpec((EXPERTS_PER_GROUP, EXPERT_FF, d), grp),
                      pl.BlockSpec(expand.shape, lambda j, tg, tb, tr: (0, 0))],
            out_specs=pl.BlockSpec((tmx, d), lambda j, tg, tb, tr: (tb[j], 0))),
        out_shape=jax.ShapeDtypeStruct((rows, d), F32),
        compiler_params=pltpu.CompilerParams(dimension_semantics=("arbitrary",), vmem_limit_bytes=VMEM_LIMIT),
        name="experts",
    )(tile_group, tile_blk, tile_rows, xs, w_gate, w_up, w_down, expand)


def _final_kernel(dest_ref, x1_ref, mod_ref, gpost_ref, ys_ref, o_ref, ybuf, sem):
    tmf = x1_ref.shape[0]
    base = pl.program_id(0) * tmf

    def issue(r8, c):
        for j in range(SUBLANES):
            d = dest_ref[base + r8 * SUBLANES + j]
            pltpu.make_async_copy(ys_ref.at[d >> 3, pl.ds(d & (SUBLANES - 1), 1)],
                                  ybuf.at[r8, pl.ds(j, 1)], sem).start()
        return c

    lax.fori_loop(0, tmf // SUBLANES, issue, 0)
    pltpu.make_async_copy(ys_ref.at[pl.ds(0, tmf // SUBLANES)], ybuf, sem).wait()
    gate2 = mod_ref[0][5:6]
    y = ybuf[...].reshape(tmf, x1_ref.shape[1])
    o_ref[...] = x1_ref[...] + gate2 * _rms(y, gpost_ref[...])


def _final(dest, x1, mod3, g_post, ys, *, seq, tmf):
    t, d = x1.shape
    ns = seq // tmf
    return pl.pallas_call(
        _final_kernel,
        grid_spec=pltpu.PrefetchScalarGridSpec(
            num_scalar_prefetch=1, grid=(t // tmf,),
            in_specs=[pl.BlockSpec((tmf, d), lambda i, dest: (i, 0)),
                      pl.BlockSpec((1, 6, d), lambda i, dest: (i // ns, 0, 0)),
                      pl.BlockSpec((1, d), lambda i, dest: (0, 0)),
                      pl.BlockSpec(memory_space=pl.ANY)],
            out_specs=pl.BlockSpec((tmf, d), lambda i, dest: (i, 0)),
            scratch_shapes=[pltpu.VMEM((tmf // SUBLANES, SUBLANES, d), F32), pltpu.SemaphoreType.DMA]),
        out_shape=jax.ShapeDtypeStruct((t, d), F32),
        compiler_params=pltpu.CompilerParams(dimension_semantics=("arbitrary",), vmem_limit_bytes=VMEM_LIMIT),
        name="final",
    )(dest, x1, mod3, g_post, ys.reshape(ys.shape[0] // SUBLANES, SUBLANES, d))


def _tile_map(counts, *, capacity, tmx):
    ntile = (counts + tmx - 1) // tmx
    ends = jnp.cumsum(ntile)
    starts = ends - ntile
    j = jnp.arange(capacity // tmx + N_GROUPS, dtype=jnp.int32)
    g = jnp.minimum(jnp.sum((j[:, None] >= ends[None, :]).astype(jnp.int32), axis=1), N_GROUPS - 1)
    local = j - starts[g]
    valid = j < ends[-1]
    blk = g * (capacity // tmx) + local
    rows = jnp.clip(counts[g] - local * tmx, 0, tmx)
    last = ends[-1] - 1
    return (jnp.where(valid, g, g[last]).astype(jnp.int32), jnp.where(valid, blk, blk[last]).astype(jnp.int32),
            jnp.where(valid, rows, 0).astype(jnp.int32))


def _pick_tile(seq, want):
    return want if seq % want == 0 else seq


def _layer(x, c, positions, w_ada, b_ada, g_pre_mix, g_post_mix, g_pre_ffn, g_post_ffn, w_in, g_cq, w_uq,
           g_ckv, w_ukv, w_o_mla, mu_shift, w_decay_up, decay_bias, w_a_up, a_bias, w_g_up, k_k, k_a, r_k,
           lnx_w, lnx_b, w_o_rwkv, w_out, w_router_group, b_router_group, w_router_expert, b_router_expert,
           w_exp_gate, w_exp_up, w_exp_down):
    batch, seq, d = x.shape
    t = batch * seq
    x2 = x.reshape(t, d)
    row1 = lambda a: a.reshape(1, -1)

    mod3 = _ada(c, w_ada, b_ada).reshape(batch, 6, d)

    zeros = lambda n: jnp.zeros((d, n), F32)
    w_c = jnp.concatenate([w_in[:, :MLA_Q_RANK + MLA_KV_RANK], zeros(MLA_NOPE),
                           w_in[:, MLA_Q_RANK + MLA_KV_RANK:MLA_IN], zeros(HEAD_PAD - MLA_NOPE - MLA_ROPE)],
                          axis=1).astype(BF16)
    w_r = w_in[:, MLA_IN:MLA_IN + RWKV_IN].astype(BF16)
    w_g = w_in[:, MLA_IN + RWKV_IN:].astype(BF16)
    w_uq_p = jnp.pad(w_uq.reshape(MLA_Q_RANK, HEADS, MLA_NOPE + MLA_ROPE),
                     ((0, 0), (0, 0), (0, HEAD_PAD - MLA_NOPE - MLA_ROPE))).reshape(MLA_Q_RANK, QK_WIDTH).astype(BF16)
    w_ukv3 = w_ukv.reshape(MLA_KV_RANK, HEADS, MLA_NOPE + MLA_V)
    w_uk_p = jnp.pad(w_ukv3[..., :MLA_NOPE], ((0, 0), (0, 0), (0, HEAD_PAD - MLA_NOPE))
                     ).reshape(MLA_KV_RANK, QK_WIDTH).astype(BF16)
    w_uv_t = jnp.pad(w_ukv3[..., MLA_NOPE:], ((0, 0), (0, 0), (0, V_ROWS - MLA_V))
                     ).reshape(MLA_KV_RANK, HEADS * V_ROWS).T.astype(BF16)
    inv_freq = jnp.power(ROPE_THETA, -jnp.arange(0, MLA_ROPE, 2, dtype=F32) / MLA_ROPE)
    invf = jnp.concatenate([jnp.zeros((MLA_NOPE,), F32), inv_freq, inv_freq,
                            jnp.zeros((HEAD_PAD - MLA_NOPE - MLA_ROPE,), F32)]).reshape(1, LANES)

    tm = _pick_tile(seq, 512)
    q_all, k_all, vt_all, zs, gates = _inproj(
        x2, mod3, positions.reshape(t, 1), row1(g_pre_mix), w_c, w_r, w_g, row1(g_cq), row1(g_ckv),
        w_uq_p, w_uk_p, w_uv_t, row1(mu_shift), invf, batch=batch, seq=seq, tm=tm)

    attn = _attn(q_all, k_all, vt_all, batch=batch, seq=seq, tq=ATTN_BLOCK)

    tc = CHUNK * RWKV_CHUNKS_PER_STEP
    w_da = jnp.concatenate([
        jnp.concatenate([w_decay_up, jnp.zeros_like(w_decay_up)], axis=1),
        jnp.concatenate([jnp.zeros_like(w_a_up), w_a_up], axis=1)], axis=0).astype(BF16)
    hid = jnp.arange(2 * LANES) // RWKV_HEAD
    ones_bd = (hid[:, None] == hid[None, :]).astype(BF16)
    tid = jnp.arange(tc)
    same_chunk = (tid[:, None] // CHUNK) == (tid[None, :] // CHUNK)
    tri = (same_chunk & (tid[None, :] <= tid[:, None])).astype(BF16)
    rw = _rwkv(zs, w_da, w_g_up.astype(BF16), row1(decay_bias), row1(a_bias), row1(k_k), row1(k_a), row1(r_k),
               row1(lnx_w), row1(lnx_b), ones_bd, tri, batch=batch, seq=seq, tc=tc)

    w_rt = jnp.concatenate([w_router_expert, w_router_group,
                            jnp.zeros((d, ROUTER_WIDTH - N_EXPERTS - N_GROUPS), F32)], axis=1)
    wr_hi = w_rt.astype(BF16)
    wr_lo = (w_rt - wr_hi.astype(F32)).astype(BF16)
    b_r = jnp.concatenate([b_router_expert, b_router_group,
                           jnp.zeros((ROUTER_WIDTH - N_EXPERTS - N_GROUPS,), F32)]).reshape(1, ROUTER_WIDTH)
    x1, hp, dest, cnt = _merge(attn, rw, gates, x2, mod3, w_o_mla.astype(BF16), w_o_rwkv.astype(BF16),
                               w_out.astype(BF16), row1(g_post_mix), row1(g_pre_ffn), wr_hi, wr_lo, b_r,
                               seq=seq, tm=_pick_tile(seq, 512))

    tmx = _pick_tile(seq, 512)
    dest = dest.reshape(t)
    xs = _dispatch(dest, hp, rows_out=N_GROUPS * t, td=_pick_tile(seq, 1024))
    tile_group, tile_blk, tile_rows = _tile_map(cnt[0, :N_GROUPS].astype(jnp.int32), capacity=t, tmx=tmx)
    eid = jnp.arange(EXPERTS_PER_GROUP * EXPERT_FF) // EXPERT_FF
    expand = (jnp.arange(ROUTER_WIDTH)[:, None] == eid[None, :]).astype(BF16)
    ys = _experts(tile_group, tile_blk, tile_rows, xs, w_exp_gate.astype(BF16), w_exp_up.astype(BF16),
                  w_exp_down.astype(BF16), expand, tmx=tmx)
    out = _final(dest, x1, mod3, row1(g_post_ffn), ys, seq=seq, tmf=_pick_tile(seq, 512))
    return out.reshape(batch, seq, d)


def kernel(x, c, positions, w_ada, b_ada, g_pre_mix, g_post_mix, g_pre_ffn, g_post_ffn, w_in, g_cq, w_uq, g_ckv, w_ukv, w_o_mla, mu_shift, w_decay_up, decay_bias, w_a_up, a_bias, w_g_up, k_k, k_a, r_k, lnx_w, lnx_b, w_o_rwkv, w_out, w_router_group, b_router_group, w_router_expert, b_router_expert, w_exp_gate, w_exp_up, w_exp_down):
    depth = w_ada.shape[0]
    for l in range(depth):
        x = _layer(x, c, positions, w_ada[l], b_ada[l], g_pre_mix[l], g_post_mix[l], g_pre_ffn[l], g_post_ffn[l],
                   w_in[l], g_cq[l], w_uq[l], g_ckv[l], w_ukv[l], w_o_mla[l], mu_shift[l], w_decay_up[l],
                   decay_bias[l], w_a_up[l], a_bias[l], w_g_up[l], k_k[l], k_a[l], r_k[l], lnx_w[l], lnx_b[l],
                   w_o_rwkv[l], w_out[l], w_router_group[l], b_router_group[l], w_router_expert[l],
                   b_router_expert[l], w_exp_gate[l], w_exp_up[l], w_exp_down[l])
    return x
```

```python
import functools
import math

import jax
import jax.numpy as jnp
from jax import lax
from jax.experimental import pallas as pl
from jax.experimental.pallas import tpu as pltpu

F32 = jnp.float32
BF16 = jnp.bfloat16

D_MODEL = 1024
CHUNK = 64
HEADS = 8
MLA_NOPE = 64
MLA_ROPE = 32
MLA_V = 64
MLA_Q_RANK = 384
MLA_KV_RANK = 256
ROPE_THETA = 10000.0
RWKV_HEAD = 64
RWKV_WIDTH = HEADS * RWKV_HEAD
DECAY_RANK = 64
AAA_RANK = 64
GATE_RANK = 128
GN_EPS = 64e-5
N_GROUPS = 4
EXPERTS_PER_GROUP = 8
N_EXPERTS = N_GROUPS * EXPERTS_PER_GROUP
EXPERT_FF = 256
RMS_EPS = 1e-6
NEG_INF = -1e30
MLA_IN = MLA_Q_RANK + MLA_KV_RANK + MLA_ROPE
RWKV_IN = 3 * RWKV_WIDTH + DECAY_RANK + AAA_RANK + GATE_RANK

LANES = 128
SUBLANES = 8
HEAD_PAD = LANES
QK_WIDTH = HEADS * HEAD_PAD
MLA_C_WIDTH = 768
ROUTER_WIDTH = LANES
VMEM_LIMIT = 56 * 1024 * 1024
ATTN_HEADS_PER_STEP = 8
ATTN_BLOCK = 256
ATTN_BLOCKS_PER_ITER = 2
V_ROWS = 80
RWKV_CHUNKS_PER_STEP = 4
MERGE_SUBTILES = 2


def _dot(a, b):
    return jnp.dot(a, b, preferred_element_type=F32)


def _dot_nt(a, b):
    return lax.dot_general(a, b, (((1,), (1,)), ((), ())), preferred_element_type=F32)


def _dot_tn(a, b):
    return lax.dot_general(a, b, (((0,), (0,)), ((), ())), preferred_element_type=F32)


def _split(x):
    hi = x.astype(BF16)
    lo = (x - hi.astype(F32)).astype(BF16)
    return hi, lo


def _dot3(x, w_hi, w_lo):
    x_hi, x_lo = _split(x)
    return _dot(x_hi, w_hi) + (_dot(x_hi, w_lo) + _dot(x_lo, w_hi))


def _sigmoid(x):
    return 1.0 / (1.0 + jnp.exp(-x))


def _rms(x, g):
    return x * lax.rsqrt(jnp.mean(x * x, axis=-1, keepdims=True) + RMS_EPS) * g


def _ada_kernel(c_ref, w_ref, b_ref, o_ref):
    c = c_ref[...]
    s = c * _sigmoid(c)
    w_hi, w_lo = _split(w_ref[...])
    o_ref[...] = _dot3(s, w_hi, w_lo) + b_ref[...]


def _ada(c, w_ada, b_ada):
    b, d = c.shape
    n = w_ada.shape[1]
    tn = 512
    return pl.pallas_call(
        _ada_kernel,
        grid=(n // tn,),
        in_specs=[pl.BlockSpec((b, d), lambda j: (0, 0)),
                  pl.BlockSpec((d, tn), lambda j: (0, j)),
                  pl.BlockSpec((1, tn), lambda j: (0, j))],
        out_specs=pl.BlockSpec((b, tn), lambda j: (0, j)),
        out_shape=jax.ShapeDtypeStruct((b, n), F32),
        name="ada",
    )(c, w_ada, b_ada.reshape(1, n))


def _inproj_kernel(x_ref, mod_ref, pos_ref, gpre_ref, wc_ref, wr_ref, wg_ref, gcq_ref, gckv_ref,
                   wuq_ref, wuk_ref, wuv_ref, mu_ref, invf_ref,
                   q_ref, k_ref, vt_ref, zs_ref, gate_ref, carry_ref):
    i = pl.program_id(1)
    tm = x_ref.shape[0]

    @pl.when(i == 0)
    def _():
        carry_ref[...] = jnp.zeros_like(carry_ref)

    mod = mod_ref[0]
    shift, scale = mod[0:1], mod[1:2]
    h = (_rms(x_ref[...], gpre_ref[...]) * (1.0 + scale) + shift).astype(BF16)

    zc = _dot(h, wc_ref[...])
    zg = _dot(h, wg_ref[...])
    nq = _rms(zc[:, :MLA_Q_RANK], gcq_ref[...]).astype(BF16)
    nkv = _rms(zc[:, MLA_Q_RANK:MLA_Q_RANK + MLA_KV_RANK], gckv_ref[...]).astype(BF16)
    kr = zc[:, MLA_Q_RANK + MLA_KV_RANK:]
    qf = _dot(nq, wuq_ref[...])
    kn = _dot(nkv, wuk_ref[...])
    vt = _dot_nt(wuv_ref[...], nkv)
    zr = _dot(h, wr_ref[...])

    ang = pos_ref[...].astype(F32) * invf_ref[...]
    lane = lax.broadcasted_iota(jnp.int32, (1, LANES), 1)
    first_half = lane < MLA_NOPE + MLA_ROPE // 2
    in_rope = (lane >= MLA_NOPE) & (lane < MLA_NOPE + MLA_ROPE)
    cos_t = jnp.where(in_rope, jnp.cos(ang), 1.0)
    sin_a = jnp.sin(ang)
    sin_t = jnp.where(in_rope, jnp.where(first_half, -sin_a, sin_a), 0.0)

    def rope(t):
        rot = jnp.where(first_half, pltpu.roll(t, LANES - MLA_ROPE // 2, 1), pltpu.roll(t, MLA_ROPE // 2, 1))
        return t * cos_t + rot * sin_t

    kr_rot = rope(kr)
    q_scale = math.log2(math.e) / math.sqrt(MLA_NOPE + MLA_ROPE)
    for hh in range(HEADS):
        sl = slice(hh * HEAD_PAD, (hh + 1) * HEAD_PAD)
        q_ref[:, sl] = (rope(qf[:, sl]) * q_scale).astype(BF16)
        k_ref[:, sl] = (kn[:, sl] + kr_rot).astype(BF16)
    vrow = lax.broadcasted_iota(jnp.int32, vt.shape, 0) % V_ROWS
    vt = jnp.where(vrow == MLA_V, 1.0, vt).astype(BF16)
    for j in range(vt_ref.shape[0]):
        vt_ref[j] = vt[:, j * ATTN_BLOCK:(j + 1) * ATTN_BLOCK]

    gate_ref[...] = _sigmoid(zg).astype(BF16)

    row = lax.broadcasted_iota(jnp.int32, (tm, 1), 0)
    prev = jnp.where(row == 0, carry_ref[0:1, :], pltpu.roll(zr, 1, 0))
    carry_ref[0:1, :] = zr[tm - 1:tm, :]
    zs_ref[...] = (zr + (prev - zr) * mu_ref[...]).astype(BF16)


def _const_spec(shape):
    nd = len(shape)
    return pl.BlockSpec(shape, lambda *_: (0,) * nd, pipeline_mode=pl.Buffered(1))


def _inproj(x2, mod3, pos2, g_pre, w_c, w_r, w_g, g_cq, g_ckv, w_uq, w_uk, w_uv, mu, invf, *, batch, seq, tm):
    t, d = x2.shape
    ns = seq // tm
    row = lambda b, i: (b * ns + i, 0)
    widths = [QK_WIDTH, QK_WIDTH, RWKV_IN, 2 * d]
    specs = [pl.BlockSpec((tm, w), row) for w in widths]
    shapes = [jax.ShapeDtypeStruct((t, w), BF16) for w in widths]
    vw = HEADS * V_ROWS
    nb = tm // ATTN_BLOCK
    specs.insert(2, pl.BlockSpec((None, nb, vw, ATTN_BLOCK), lambda b, i: (b, i, 0, 0)))
    shapes.insert(2, jax.ShapeDtypeStruct((batch, ns * nb, vw, ATTN_BLOCK), BF16))
    return pl.pallas_call(
        _inproj_kernel,
        grid=(batch, ns),
        in_specs=[pl.BlockSpec((tm, d), row),
                  pl.BlockSpec((1, 6, d), lambda b, i: (b, 0, 0)),
                  pl.BlockSpec((tm, 1), row),
                  _const_spec(g_pre.shape), _const_spec(w_c.shape), _const_spec(w_r.shape),
                  _const_spec(w_g.shape), _const_spec(g_cq.shape), _const_spec(g_ckv.shape),
                  _const_spec(w_uq.shape), _const_spec(w_uk.shape), _const_spec(w_uv.shape),
                  _const_spec(mu.shape), _const_spec(invf.shape)],
        out_specs=specs,
        out_shape=shapes,
        scratch_shapes=[pltpu.VMEM((8, RWKV_IN), F32)],
        compiler_params=pltpu.CompilerParams(dimension_semantics=("parallel", "arbitrary"),
                                             vmem_limit_bytes=VMEM_LIMIT),
        name="inproj",
    )(x2, mod3, pos2, g_pre, w_c, w_r, w_g, g_cq, g_ckv, w_uq, w_uk, w_uv, mu, invf)


def _attn_kernel(q_ref, k_ref, vt_ref, o_ref):
    i = pl.program_id(2)
    tq = q_ref.shape[0]
    kc = lax.broadcasted_iota(jnp.int32, (tq, tq), 0) // CHUNK
    qc = lax.broadcasted_iota(jnp.int32, (tq, tq), 1) // CHUNK
    diag_mask = kc <= qc
    heads = range(ATTN_HEADS_PER_STEP)
    hs = [slice(h * HEAD_PAD, (h + 1) * HEAD_PAD) for h in heads]
    vr = [slice(h * V_ROWS, (h + 1) * V_ROWS) for h in heads]
    q = [q_ref[:, s] for s in hs]

    def block(kb, carry, nblk, masked):
        kbs = [kb + n for n in range(nblk)]
        rows = [pl.ds(pl.multiple_of(b * tq, tq), tq) for b in kbs]
        s = [[_dot_nt(k_ref[r, hs[h]], q[h]) for r in rows] for h in heads]
        if masked:
            s = [[jnp.where(diag_mask, t, NEG_INF) for t in sh] for sh in s]
        m_new = []
        for h in heads:
            m = carry[h][0]
            for t in s[h]:
                m = jnp.maximum(m, jnp.max(t, axis=0, keepdims=True))
            m_new.append(m)
        alpha = [jnp.exp2(carry[h][0] - m_new[h]) for h in heads]
        p = [jnp.concatenate([jnp.exp2(t - m_new[h]).astype(BF16) for t in s[h]], axis=0) for h in heads]
        acc = [alpha[h] * carry[h][1]
               + _dot(jnp.concatenate([vt_ref[b, vr[h], :] for b in kbs], axis=1), p[h]) for h in heads]
        return tuple((m_new[h], acc[h]) for h in heads)

    init = tuple((jnp.full((1, tq), NEG_INF, F32), jnp.zeros((V_ROWS, tq), F32)) for _ in heads)
    nb = ATTN_BLOCKS_PER_ITER
    carry = lax.fori_loop(0, i // nb, lambda n, c: block(n * nb, c, nb, False), init)
    carry = lax.fori_loop((i // nb) * nb, i, lambda kb, c: block(kb, c, 1, False), carry)
    final = block(i, carry, 1, True)
    outs = [final[h][1][:MLA_V] / final[h][1][MLA_V:MLA_V + 1] for h in heads]
    for pr in range(ATTN_HEADS_PER_STEP // 2):
        pair_t = jnp.concatenate([outs[2 * pr], outs[2 * pr + 1]], axis=0)
        o_ref[:, pr * LANES:(pr + 1) * LANES] = pair_t.T.astype(BF16)


def _attn(q_all, k_all, vt_all, *, batch, seq, tq):
    t = q_all.shape[0]
    nq = seq // tq
    g = ATTN_HEADS_PER_STEP
    return pl.pallas_call(
        _attn_kernel,
        grid=(batch, HEADS // g, nq),
        in_specs=[pl.BlockSpec((tq, g * HEAD_PAD), lambda b, p, i: (b * nq + i, p)),
                  pl.BlockSpec((seq, g * HEAD_PAD), lambda b, p, i: (b, p)),
                  pl.BlockSpec((None, nq, g * V_ROWS, tq), lambda b, p, i: (b, 0, p, 0))],
        out_specs=pl.BlockSpec((tq, g * MLA_V), lambda b, p, i: (b * nq + i, p)),
        out_shape=jax.ShapeDtypeStruct((t, HEADS * MLA_V), BF16),
        compiler_params=pltpu.CompilerParams(dimension_semantics=("parallel", "parallel", "arbitrary"),
                                             vmem_limit_bytes=VMEM_LIMIT),
        name="attn",
    )(q_all, k_all, vt_all)


def _rwkv_kernel(zs_ref, wda_ref, wgu_ref, dbias_ref, abias_ref, kk_ref, ka_ref, rk_ref, lnw_ref, lnb_ref,
                 ones_ref, tri_ref,
                 o_ref,
                 st_ref, a_s, r_s, b_s, k_s, bg_s, kg_s, v_s, gc_s, bonus_s, g_s, y_s):
    w = RWKV_WIDTH
    nsub = RWKV_CHUNKS_PER_STEP
    th = nsub * CHUNK
    n2 = 2 * CHUNK
    staged = (a_s, r_s, b_s, k_s, bg_s, kg_s, v_s, gc_s, bonus_s, g_s)

    @pl.when(pl.program_id(1) == 0)
    def _():
        st_ref[...] = jnp.zeros_like(st_ref)

    ones_bd = ones_ref[...]

    def head_sum(t):
        tb = t.astype(BF16)
        hw = ones_bd.shape[0]
        return jnp.concatenate([_dot(tb[:, n * hw:(n + 1) * hw], ones_bd) for n in range(w // hw)], axis=1)

    ri = lax.broadcasted_iota(jnp.int32, (4 * CHUNK, n2), 0)
    ci = lax.broadcasted_iota(jnp.int32, (4 * CHUNK, n2), 1) % CHUNK
    tri_mask = (ri % CHUNK + ri // n2) > ci
    lane2 = lax.broadcasted_iota(jnp.int32, (CHUNK, n2), 1)
    h1 = lane2 < RWKV_HEAD
    bi = lax.broadcasted_iota(jnp.int32, (n2, n2), 0) // RWKV_HEAD
    bj = lax.broadcasted_iota(jnp.int32, (n2, n2), 1) // RWKV_HEAD
    bd_mask = bi == bj
    zero_bf = jnp.zeros((CHUNK, n2), BF16)
    zero_f = jnp.zeros((CHUNK, n2), F32)
    lane = lax.broadcasted_iota(jnp.int32, (1, LANES), 1)

    def prepare(h):
        hr = slice(h * th, (h + 1) * th)
        zs = zs_ref[hr, :].astype(F32)
        r, k, v = zs[:, 0:w], zs[:, w:2 * w], zs[:, 2 * w:3 * w]
        da = zs[:, 3 * w:3 * w + LANES]
        gd = zs[:, 3 * w + LANES:]
        lora_in = jnp.where(lane < DECAY_RANK, jnp.tanh(da), da).astype(BF16)
        pre = _dot(lora_in, wda_ref[...])
        g_new = _dot(_sigmoid(gd).astype(BF16), wgu_ref[...])
        yield
        u = -(dbias_ref[...] + pre[:, :w])
        softplus = jnp.maximum(u, 0.0) + jnp.log(1.0 + jnp.exp(-jnp.abs(u)))
        logw = -jnp.exp(-softplus - 0.5)
        eta = _sigmoid(abias_ref[...] + pre[:, w:])
        kk = k * kk_ref[...]
        kk_ss = head_sum(kk * kk)
        lw_hi, lw_lo = _split(logw)
        cum = _dot(tri_ref[...], lw_hi) + _dot(tri_ref[...], lw_lo)
        yield
        kk = kk * jnp.minimum(lax.rsqrt(kk_ss), 1e12)
        kp = k * (1.0 + (eta - 1.0) * ka_ref[...])
        bonus_new = head_sum(r * kp * rk_ref[...]) * v
        yield
        cum_end = jnp.concatenate([jnp.broadcast_to(cum[(c + 1) * CHUNK - 1:(c + 1) * CHUNK, :], (CHUNK, w))
                                   for c in range(nsub)], axis=0)
        b_in = kk * eta
        g_inv = jnp.exp(-cum)
        g_rem = jnp.exp(cum_end - cum)
        staged_new = ((-kk * jnp.exp(cum - logw)).astype(BF16), (r * jnp.exp(cum)).astype(BF16),
                      (b_in * g_inv).astype(BF16), (kp * g_inv).astype(BF16),
                      (b_in * g_rem).astype(BF16), (kp * g_rem).astype(BF16), v.astype(BF16),
                      jnp.exp(cum_end), bonus_new, g_new)
        for ref, val in zip(staged, staged_new):
            ref[hr, :] = val
        yield

    def solve(h):
        blocks = [(sub, p) for sub in range(nsub) for p in range(HEADS // 2)]
        rows = [slice(h * th + sub * CHUNK, h * th + (sub + 1) * CHUNK) for sub in range(nsub)]
        lns = [slice(p * n2, (p + 1) * n2) for p in range(HEADS // 2)]
        a_t = [a_s[rows[sub], lns[p]] for sub, p in blocks]
        r_t = [r_s[rows[sub], lns[p]] for sub, p in blocks]
        v_t = [v_s[rows[sub], lns[p]] for sub, p in blocks]
        ml = []
        for n, (sub, p) in enumerate(blocks):
            lhs = jnp.concatenate([jnp.where(h1, a_t[n], zero_bf), jnp.where(h1, zero_bf, a_t[n]),
                                   jnp.where(h1, r_t[n], zero_bf), jnp.where(h1, zero_bf, r_t[n])], axis=0)
            rhs = jnp.concatenate([b_s[rows[sub], lns[p]], k_s[rows[sub], lns[p]]], axis=0)
            ml.append(jnp.where(tri_mask, _dot_nt(lhs, rhs), 0.0))
        ml_b = [m.astype(BF16) for m in ml]
        yield
        units = [(n, hh) for n in range(len(blocks)) for hh in range(2)]
        w_t = [_dot(ml_b[n][hh * CHUNK:(hh + 1) * CHUNK, :], jnp.concatenate([zero_bf, v_t[n]], axis=0))
               for n, hh in units]
        a_f = [t.astype(F32) for t in a_t]
        a_sw = [pltpu.roll(t, RWKV_HEAD, 1) for t in a_f]
        x = [jnp.where(h1, a_f[n], pltpu.roll(w_t[2 * n], RWKV_HEAD, 1)) if hh == 0
             else jnp.where(h1, a_sw[n], w_t[2 * n + 1]) for n, hh in units]
        lsq = [jnp.where(h1, ml[n][hh * CHUNK:(hh + 1) * CHUNK, :], 0.0) for n, hh in units]
        yield
        for _ in range(6):
            out = [_dot(lsq[u][:, :CHUNK].astype(BF16),
                        jnp.concatenate([x[u], lsq[u]], axis=1).astype(BF16)) for u in range(len(units))]
            x = [x[u] + out[u][:, :n2] for u in range(len(units))]
            lsq = [t[:, n2:] for t in out]
            yield
        z = []
        for n in range(len(blocks)):
            x0, x1 = x[2 * n], x[2 * n + 1]
            a_hat = jnp.where(h1, x0, pltpu.roll(x1, RWKV_HEAD, 1))
            u0 = jnp.where(h1, pltpu.roll(x0, RWKV_HEAD, 1), x1)
            z.append(jnp.concatenate([jnp.concatenate([a_hat, u0], axis=1),
                                      jnp.concatenate([zero_f, v_t[n].astype(F32)], axis=1)],
                                     axis=0).astype(BF16))
        o0 = [_dot(ml_b[n][2 * CHUNK:3 * CHUNK, :], z[n]) for n in range(len(blocks))]
        o1 = [_dot(ml_b[n][3 * CHUNK:4 * CHUNK, :], z[n]) for n in range(len(blocks))]
        pm = [_dot_tn(z[n], jnp.concatenate([bg_s[rows[sub], lns[p]], kg_s[rows[sub], lns[p]]], axis=0))
              for n, (sub, p) in enumerate(blocks)]
        r_hat = [(r_t[n].astype(F32) + jnp.where(h1, o0[n][:, :n2], o1[n][:, :n2])).astype(BF16)
                 for n in range(len(blocks))]
        yield
        state = [st_ref[p] for p in range(HEADS // 2)]
        for n, (sub, p) in enumerate(blocks):
            y0 = jnp.where(h1, o0[n][:, n2:], o1[n][:, n2:])
            g_t = jnp.where(bd_mask, pm[n][:n2], 0.0).astype(BF16)
            h_t = jnp.where(bd_mask, pm[n][n2:], 0.0)
            s_b = state[p].astype(BF16)
            y_s[rows[sub], lns[p]] = _dot_nt(r_hat[n], s_b) + y0
            gc = gc_s[rows[sub].start:rows[sub].start + 1, lns[p]]
            state[p] = gc * state[p] + _dot(s_b, g_t) + h_t
        for p in range(HEADS // 2):
            st_ref[p] = state[p]
        yield

    def finish(h):
        hr = slice(h * th, (h + 1) * th)
        y = y_s[hr, :]
        inv_n = 1.0 / RWKV_HEAD
        dev = y - head_sum(y) * inv_n
        yield
        var = head_sum(dev * dev) * inv_n
        yield
        out = (dev * lax.rsqrt(var + GN_EPS) * lnw_ref[...] + lnb_ref[...] + bonus_s[hr, :]) * g_s[hr, :]
        o_ref[hr, :] = out.astype(BF16)
        yield

    def emit(main, side=None, every=1):
        n = 0
        for _ in main:
            n += 1
            if side is not None and n % every == 0:
                next(side, None)
        if side is not None:
            for _ in side:
                pass

    emit(prepare(0))
    emit(solve(0), prepare(1), every=2)
    emit(solve(1), finish(0), every=3)
    emit(finish(1))


def _rwkv(zs, w_da, w_gu, dbias, abias, k_k, k_a, r_k, lnw, lnb, ones_bd, tri, *, batch, seq, tc):
    t = zs.shape[0]
    nc = seq // tc
    w = RWKV_WIDTH
    consts = [w_da, w_gu, dbias, abias, k_k, k_a, r_k, lnw, lnb, ones_bd, tri]
    return pl.pallas_call(
        _rwkv_kernel,
        grid=(batch, nc),
        in_specs=[pl.BlockSpec((tc, RWKV_IN), lambda b, i: (b * nc + i, 0))] + [_const_spec(a.shape) for a in consts],
        out_specs=pl.BlockSpec((tc, w), lambda b, i: (b * nc + i, 0)),
        out_shape=jax.ShapeDtypeStruct((t, w), BF16),
        scratch_shapes=[pltpu.VMEM((HEADS // 2, 2 * RWKV_HEAD, 2 * RWKV_HEAD), F32)]
        + [pltpu.VMEM((tc, w), BF16)] * 7
        + [pltpu.VMEM((tc, w), F32)] * 4,
        compiler_params=pltpu.CompilerParams(dimension_semantics=("parallel", "arbitrary"),
                                             vmem_limit_bytes=VMEM_LIMIT),
        name="rwkv",
    )(zs, *consts)


def _route(logits):
    lane = lax.broadcasted_iota(jnp.int32, logits.shape, 1)
    lane_f = lane.astype(F32)
    big = float(ROUTER_WIDTH)
    is_group = (lane >= N_EXPERTS) & (lane < N_EXPERTS + N_GROUPS)
    gl = jnp.where(is_group, logits, NEG_INF)
    g_max = jnp.max(gl, axis=-1, keepdims=True)
    g_sel = jnp.min(jnp.where(gl == g_max, lane_f, big), axis=-1, keepdims=True) - float(N_EXPERTS)
    p_sel = 1.0 / jnp.sum(jnp.where(is_group, jnp.exp(gl - g_max), 0.0), axis=-1, keepdims=True)
    in_group = (lane < N_EXPERTS) & ((lane // EXPERTS_PER_GROUP).astype(F32) == g_sel)
    el = jnp.where(in_group, logits, NEG_INF)
    t1 = jnp.max(el, axis=-1, keepdims=True)
    i1 = jnp.min(jnp.where(el == t1, lane_f, big), axis=-1, keepdims=True)
    el2 = jnp.where(lane_f == i1, NEG_INF, el)
    t2 = jnp.max(el2, axis=-1, keepdims=True)
    i2 = jnp.min(jnp.where(el2 == t2, lane_f, big), axis=-1, keepdims=True)
    e21 = jnp.exp(t2 - t1)
    w1 = p_sel / (1.0 + e21)
    w2 = w1 * e21
    first = g_sel * float(EXPERTS_PER_GROUP)
    cw = jnp.where(lane_f == i1 - first, w1, 0.0) + jnp.where(lane_f == i2 - first, w2, 0.0)
    return g_sel, cw


def _merge_kernel(attn_ref, rw_ref, gate_ref, x_ref, mod_ref, womla_ref, worw_ref, wout_ref,
                  gpost_ref, gffn_ref, wrh_ref, wrl_ref, br_ref, stril_ref,
                  x1_ref, hp_ref, dest_ref, cnt_ref, carry_ref, *, group_capacity):
    i = pl.program_id(0)
    tm, d = x_ref.shape

    @pl.when(i == 0)
    def _():
        carry_ref[...] = jnp.zeros_like(carry_ref)

    mod = mod_ref[0]
    gate1, shift2, scale2 = mod[2:3], mod[3:4], mod[4:5]

    ts = tm // MERGE_SUBTILES
    subs = [slice(n * ts, (n + 1) * ts) for n in range(MERGE_SUBTILES)]
    o_mla = [_dot(attn_ref[r, :], womla_ref[...]) for r in subs]
    o_rw = [_dot(rw_ref[r, :], worw_ref[...]) for r in subs]
    o = [(gate_ref[r, :d].astype(F32) * o_mla[n] + gate_ref[r, d:].astype(F32) * o_rw[n]).astype(BF16)
         for n, r in enumerate(subs)]
    y = [_dot(t, wout_ref[...]) for t in o]
    x1 = [x_ref[r, :] + gate1 * _rms(y[n], gpost_ref[...]) for n, r in enumerate(subs)]
    h2 = [_rms(t, gffn_ref[...]) * (1.0 + scale2) + shift2 for t in x1]
    logits = [_dot3(t, wrh_ref[...], wrl_ref[...]) + br_ref[...] for t in h2]
    for n, r in enumerate(subs):
        x1_ref[r, :] = x1[n]
        hp_ref[r, :d] = h2[n]

    lane_f = lax.broadcasted_iota(jnp.int32, (ts, ROUTER_WIDTH), 1).astype(F32)
    routes = [_route(t) for t in logits]
    g_sels = [g for g, _ in routes]
    for n, r in enumerate(subs):
        hp_ref[r, d:] = routes[n][1]
    onehots = [jnp.where(lane_f == g, 1.0, 0.0) for g in g_sels]

    onehot = jnp.concatenate(onehots, axis=0)
    earlier = _dot(stril_ref[...], onehot.astype(BF16))
    carry = carry_ref[0:1, :]
    for n, r in enumerate(subs):
        rank = jnp.sum(jnp.where(lane_f == g_sels[n], carry + earlier[r, :], 0.0), axis=-1, keepdims=True)
        dest_ref[r, :] = (g_sels[n] * float(group_capacity) + rank).astype(jnp.int32)
    total = jnp.broadcast_to(carry + jnp.sum(onehot, axis=0, keepdims=True), carry_ref.shape)
    carry_ref[...] = total
    cnt_ref[...] = total


def _merge(attn, rw, gates, x2, mod3, w_o_mla, w_o_rwkv, w_out, g_post, g_ffn, wr_hi, wr_lo, b_r, *, seq, tm):
    t, d = x2.shape
    ns = seq // tm
    row = lambda i: (i, 0)
    tid = jnp.arange(tm)
    stril = (tid[None, :] < tid[:, None]).astype(BF16)
    consts = [w_o_mla, w_o_rwkv, w_out, g_post, g_ffn, wr_hi, wr_lo, b_r, stril]
    return pl.pallas_call(
        functools.partial(_merge_kernel, group_capacity=t),
        grid=(t // tm,),
        in_specs=[pl.BlockSpec((tm, attn.shape[1]), row), pl.BlockSpec((tm, rw.shape[1]), row),
                  pl.BlockSpec((tm, 2 * d), row), pl.BlockSpec((tm, d), row),
                  pl.BlockSpec((1, 6, d), lambda i: (i // ns, 0, 0))] + [_const_spec(a.shape) for a in consts],
        out_specs=[pl.BlockSpec((tm, d), row), pl.BlockSpec((tm, d + ROUTER_WIDTH), row),
                   pl.BlockSpec((tm, 1), row), pl.BlockSpec((8, ROUTER_WIDTH), lambda i: (0, 0))],
        out_shape=[jax.ShapeDtypeStruct((t, d), F32), jax.ShapeDtypeStruct((t, d + ROUTER_WIDTH), F32),
                   jax.ShapeDtypeStruct((t, 1), jnp.int32), jax.ShapeDtypeStruct((8, ROUTER_WIDTH), F32)],
        scratch_shapes=[pltpu.VMEM((8, ROUTER_WIDTH), F32)],
        compiler_params=pltpu.CompilerParams(dimension_semantics=("arbitrary",), vmem_limit_bytes=VMEM_LIMIT),
        name="merge",
    )(attn, rw, gates, x2, mod3, *consts)


def _dispatch_kernel(dest_ref, hp_ref, xs_ref, sem):
    td = hp_ref.shape[0] * SUBLANES
    base = pl.program_id(0) * td

    def issue(r8, c):
        for j in range(SUBLANES):
            d = dest_ref[base + r8 * SUBLANES + j]
            pltpu.make_async_copy(hp_ref.at[r8, pl.ds(j, 1)],
                                  xs_ref.at[d >> 3, pl.ds(d & (SUBLANES - 1), 1)], sem).start()
        return c

    lax.fori_loop(0, td // SUBLANES, issue, 0)
    pltpu.make_async_copy(hp_ref, xs_ref.at[pl.ds(0, td // SUBLANES)], sem).wait()


def _dispatch(dest, hp, *, rows_out, td):
    t, w = hp.shape
    xs = pl.pallas_call(
        _dispatch_kernel,
        grid_spec=pltpu.PrefetchScalarGridSpec(
            num_scalar_prefetch=1, grid=(t // td,),
            in_specs=[pl.BlockSpec((td // SUBLANES, SUBLANES, w), lambda i, dest: (i, 0, 0))],
            out_specs=pl.BlockSpec(memory_space=pl.ANY),
            scratch_shapes=[pltpu.SemaphoreType.DMA]),
        out_shape=jax.ShapeDtypeStruct((rows_out // SUBLANES, SUBLANES, w), F32),
        compiler_params=pltpu.CompilerParams(dimension_semantics=("arbitrary",), vmem_limit_bytes=VMEM_LIMIT),
        name="dispatch",
    )(dest, hp.reshape(t // SUBLANES, SUBLANES, w))
    return xs.reshape(rows_out, w)


def _experts_kernel(tg_ref, tb_ref, tr_ref, xs_ref, wg_ref, wu_ref, wd_ref, ex_ref, ys_ref):
    del tg_ref, tb_ref
    nrows = tr_ref[pl.program_id(0)]

    @pl.when(nrows > 0)
    def _():
        tmx = xs_ref.shape[0]
        valid = lax.broadcasted_iota(jnp.int32, (tmx, 1), 0) < nrows
        tile = jnp.where(valid, xs_ref[...], 0.0)
        x = tile[:, :D_MODEL].astype(BF16)
        cw_hi, cw_lo = _split(tile[:, D_MODEL:])
        cwx = _dot(cw_hi, ex_ref[...]) + _dot(cw_lo, ex_ref[...])
        acc = jnp.zeros((tmx, D_MODEL), F32)
        for e in range(EXPERTS_PER_GROUP):
            a = _dot(x, wg_ref[e])
            act = a * _sigmoid(a) * _dot(x, wu_ref[e]) * cwx[:, e * EXPERT_FF:(e + 1) * EXPERT_FF]
            acc = acc + _dot(act.astype(BF16), wd_ref[e])
        ys_ref[...] = acc


def _experts(tile_group, tile_blk, tile_rows, xs, w_gate, w_up, w_down, expand, *, tmx):
    rows, w = xs.shape
    nt = tile_group.shape[0]
    d = D_MODEL
    grp = lambda j, tg, tb, tr: (tg[j], 0, 0)
    return pl.pallas_call(
        _experts_kernel,
        grid_spec=pltpu.PrefetchScalarGridSpec(
            num_scalar_prefetch=3, grid=(nt,),
            in_specs=[pl.BlockSpec((tmx, w), lambda j, tg, tb, tr: (tb[j], 0)),
                      pl.BlockSpec((EXPERTS_PER_GROUP, d, EXPERT_FF), grp),
                      pl.BlockSpec((EXPERTS_PER_GROUP, d, EXPERT_FF), grp),
                      pl.BlockSpec((EXPERTS_PER_GROUP, EXPERT_FF, d), grp),
                      pl.BlockSpec(expand.shape, lambda j, tg, tb, tr: (0, 0))],
            out_specs=pl.BlockSpec((tmx, d), lambda j, tg, tb, tr: (tb[j], 0))),
        out_shape=jax.ShapeDtypeStruct((rows, d), F32),
        compiler_params=pltpu.CompilerParams(dimension_semantics=("arbitrary",), vmem_limit_bytes=VMEM_LIMIT),
        name="experts",
    )(tile_group, tile_blk, tile_rows, xs, w_gate, w_up, w_down, expand)


def _final_kernel(dest_ref, x1_ref, mod_ref, gpost_ref, ys_ref, o_ref, ybuf, sem):
    tmf = x1_ref.shape[0]
    base = pl.program_id(0) * tmf

    def issue(r8, c):
        for j in range(SUBLANES):
            d = dest_ref[base + r8 * SUBLANES + j]
            pltpu.make_async_copy(ys_ref.at[d >> 3, pl.ds(d & (SUBLANES - 1), 1)],
                                  ybuf.at[r8, pl.ds(j, 1)], sem).start()
        return c

    lax.fori_loop(0, tmf // SUBLANES, issue, 0)
    pltpu.make_async_copy(ys_ref.at[pl.ds(0, tmf // SUBLANES)], ybuf, sem).wait()
    gate2 = mod_ref[0][5:6]
    y = ybuf[...].reshape(tmf, x1_ref.shape[1])
    o_ref[...] = x1_ref[...] + gate2 * _rms(y, gpost_ref[...])


def _final(dest, x1, mod3, g_post, ys, *, seq, tmf):
    t, d = x1.shape
    ns = seq // tmf
    return pl.pallas_call(
        _final_kernel,
        grid_spec=pltpu.PrefetchScalarGridSpec(
            num_scalar_prefetch=1, grid=(t // tmf,),
            in_specs=[pl.BlockSpec((tmf, d), lambda i, dest: (i, 0)),
                      pl.BlockSpec((1, 6, d), lambda i, dest: (i // ns, 0, 0)),
                      pl.BlockSpec((1, d), lambda i, dest: (0, 0)),
                      pl.BlockSpec(memory_space=pl.ANY)],
            out_specs=pl.BlockSpec((tmf, d), lambda i, dest: (i, 0)),
            scratch_shapes=[pltpu.VMEM((tmf // SUBLANES, SUBLANES, d), F32), pltpu.SemaphoreType.DMA]),
        out_shape=jax.ShapeDtypeStruct((t, d), F32),
        compiler_params=pltpu.CompilerParams(dimension_semantics=("arbitrary",), vmem_limit_bytes=VMEM_LIMIT),
        name="final",
    )(dest, x1, mod3, g_post, ys.reshape(ys.shape[0] // SUBLANES, SUBLANES, d))


def _tile_map(counts, *, capacity, tmx):
    ntile = (counts + tmx - 1) // tmx
    ends = jnp.cumsum(ntile)
    starts = ends - ntile
    j = jnp.arange(capacity // tmx + N_GROUPS, dtype=jnp.int32)
    g = jnp.minimum(jnp.sum((j[:, None] >= ends[None, :]).astype(jnp.int32), axis=1), N_GROUPS - 1)
    local = j - starts[g]
    valid = j < ends[-1]
    blk = g * (capacity // tmx) + local
    rows = jnp.clip(counts[g] - local * tmx, 0, tmx)
    last = ends[-1] - 1
    return (jnp.where(valid, g, g[last]).astype(jnp.int32), jnp.where(valid, blk, blk[last]).astype(jnp.int32),
            jnp.where(valid, rows, 0).astype(jnp.int32))


def _pick_tile(seq, want):
    return want if seq % want == 0 else seq


def _layer(x, c, positions, w_ada, b_ada, g_pre_mix, g_post_mix, g_pre_ffn, g_post_ffn, w_in, g_cq, w_uq,
           g_ckv, w_ukv, w_o_mla, mu_shift, w_decay_up, decay_bias, w_a_up, a_bias, w_g_up, k_k, k_a, r_k,
           lnx_w, lnx_b, w_o_rwkv, w_out, w_router_group, b_router_group, w_router_expert, b_router_expert,
           w_exp_gate, w_exp_up, w_exp_down):
    batch, seq, d = x.shape
    t = batch * seq
    x2 = x.reshape(t, d)
    row1 = lambda a: a.reshape(1, -1)

    mod3 = _ada(c, w_ada, b_ada).reshape(batch, 6, d)

    zeros = lambda n: jnp.zeros((d, n), F32)
    w_c = jnp.concatenate([w_in[:, :MLA_Q_RANK + MLA_KV_RANK], zeros(MLA_NOPE),
                           w_in[:, MLA_Q_RANK + MLA_KV_RANK:MLA_IN], zeros(HEAD_PAD - MLA_NOPE - MLA_ROPE)],
                          axis=1).astype(BF16)
    w_r = w_in[:, MLA_IN:MLA_IN + RWKV_IN].astype(BF16)
    w_g = w_in[:, MLA_IN + RWKV_IN:].astype(BF16)
    w_uq_p = jnp.pad(w_uq.reshape(MLA_Q_RANK, HEADS, MLA_NOPE + MLA_ROPE),
                     ((0, 0), (0, 0), (0, HEAD_PAD - MLA_NOPE - MLA_ROPE))).reshape(MLA_Q_RANK, QK_WIDTH).astype(BF16)
    w_ukv3 = w_ukv.reshape(MLA_KV_RANK, HEADS, MLA_NOPE + MLA_V)
    w_uk_p = jnp.pad(w_ukv3[..., :MLA_NOPE], ((0, 0), (0, 0), (0, HEAD_PAD - MLA_NOPE))
                     ).reshape(MLA_KV_RANK, QK_WIDTH).astype(BF16)
    w_uv_t = jnp.pad(w_ukv3[..., MLA_NOPE:], ((0, 0), (0, 0), (0, V_ROWS - MLA_V))
                     ).reshape(MLA_KV_RANK, HEADS * V_ROWS).T.astype(BF16)
    inv_freq = jnp.power(ROPE_THETA, -jnp.arange(0, MLA_ROPE, 2, dtype=F32) / MLA_ROPE)
    invf = jnp.concatenate([jnp.zeros((MLA_NOPE,), F32), inv_freq, inv_freq,
                            jnp.zeros((HEAD_PAD - MLA_NOPE - MLA_ROPE,), F32)]).reshape(1, LANES)

    tm = _pick_tile(seq, 512)
    q_all, k_all, vt_all, zs, gates = _inproj(
        x2, mod3, positions.reshape(t, 1), row1(g_pre_mix), w_c, w_r, w_g, row1(g_cq), row1(g_ckv),
        w_uq_p, w_uk_p, w_uv_t, row1(mu_shift), invf, batch=batch, seq=seq, tm=tm)

    attn = _attn(q_all, k_all, vt_all, batch=batch, seq=seq, tq=ATTN_BLOCK)

    tc = 2 * CHUNK * RWKV_CHUNKS_PER_STEP
    w_da = jnp.concatenate([
        jnp.concatenate([w_decay_up, jnp.zeros_like(w_decay_up)], axis=1),
        jnp.concatenate([jnp.zeros_like(w_a_up), w_a_up], axis=1)], axis=0).astype(BF16)
    hid = jnp.arange(2 * LANES) // RWKV_HEAD
    ones_bd = (hid[:, None] == hid[None, :]).astype(BF16)
    tid = jnp.arange(tc // 2)
    same_chunk = (tid[:, None] // CHUNK) == (tid[None, :] // CHUNK)
    tri = (same_chunk & (tid[None, :] <= tid[:, None])).astype(BF16)
    rw = _rwkv(zs, w_da, w_g_up.astype(BF16), row1(decay_bias), row1(a_bias), row1(k_k), row1(k_a), row1(r_k),
               row1(lnx_w), row1(lnx_b), ones_bd, tri, batch=batch, seq=seq, tc=tc)

    w_rt = jnp.concatenate([w_router_expert, w_router_group,
                            jnp.zeros((d, ROUTER_WIDTH - N_EXPERTS - N_GROUPS), F32)], axis=1)
    wr_hi = w_rt.astype(BF16)
    wr_lo = (w_rt - wr_hi.astype(F32)).astype(BF16)
    b_r = jnp.concatenate([b_router_expert, b_router_group,
                           jnp.zeros((ROUTER_WIDTH - N_EXPERTS - N_GROUPS,), F32)]).reshape(1, ROUTER_WIDTH)
    x1, hp, dest, cnt = _merge(attn, rw, gates, x2, mod3, w_o_mla.astype(BF16), w_o_rwkv.astype(BF16),
                               w_out.astype(BF16), row1(g_post_mix), row1(g_pre_ffn), wr_hi, wr_lo, b_r,
                               seq=seq, tm=_pick_tile(seq, 512))

    tmx = _pick_tile(seq, 512)
    dest = dest.reshape(t)
    xs = _dispatch(dest, hp, rows_out=N_GROUPS * t, td=_pick_tile(seq, 1024))
    tile_group, tile_blk, tile_rows = _tile_map(cnt[0, :N_GROUPS].astype(jnp.int32), capacity=t, tmx=tmx)
    eid = jnp.arange(EXPERTS_PER_GROUP * EXPERT_FF) // EXPERT_FF
    expand = (jnp.arange(ROUTER_WIDTH)[:, None] == eid[None, :]).astype(BF16)
    ys = _experts(tile_group, tile_blk, tile_rows, xs, w_exp_gate.astype(BF16), w_exp_up.astype(BF16),
                  w_exp_down.astype(BF16), expand, tmx=tmx)
    out = _final(dest, x1, mod3, row1(g_post_ffn), ys, seq=seq, tmf=_pick_tile(seq, 512))
    return out.reshape(batch, seq, d)


def kernel(x, c, positions, w_ada, b_ada, g_pre_mix, g_post_mix, g_pre_ffn, g_post_ffn, w_in, g_cq, w_uq, g_ckv, w_ukv, w_o_mla, mu_shift, w_decay_up, decay_bias, w_a_up, a_bias, w_g_up, k_k, k_a, r_k, lnx_w, lnx_b, w_o_rwkv, w_out, w_router_group, b_router_group, w_router_expert, b_router_expert, w_exp_gate, w_exp_up, w_exp_down):
    depth = w_ada.shape[0]
    for l in range(depth):
        x = _layer(x, c, positions, w_ada[l], b_ada[l], g_pre_mix[l], g_post_mix[l], g_pre_ffn[l], g_post_ffn[l],
                   w_in[l], g_cq[l], w_uq[l], g_ckv[l], w_ukv[l], w_o_mla[l], mu_shift[l], w_decay_up[l],
                   decay_bias[l], w_a_up[l], a_bias[l], w_g_up[l], k_k[l], k_a[l], r_k[l], lnx_w[l], lnx_b[l],
                   w_o_rwkv[l], w_out[l], w_router_group[l], b_router_group[l], w_router_expert[l],
                   b_router_expert[l], w_exp_gate[l], w_exp_up[l], w_exp_down[l])
    return x
```

```python
import functools
import math

import jax
import jax.numpy as jnp
from jax import lax
from jax.experimental import pallas as pl
from jax.experimental.pallas import tpu as pltpu

F32 = jnp.float32
BF16 = jnp.bfloat16

D_MODEL = 1024
CHUNK = 64
HEADS = 8
MLA_NOPE = 64
MLA_ROPE = 32
MLA_V = 64
MLA_Q_RANK = 384
MLA_KV_RANK = 256
ROPE_THETA = 10000.0
RWKV_HEAD = 64
RWKV_WIDTH = HEADS * RWKV_HEAD
DECAY_RANK = 64
AAA_RANK = 64
GATE_RANK = 128
GN_EPS = 64e-5
N_GROUPS = 4
EXPERTS_PER_GROUP = 8
N_EXPERTS = N_GROUPS * EXPERTS_PER_GROUP
EXPERT_FF = 256
RMS_EPS = 1e-6
NEG_INF = -1e30
MLA_IN = MLA_Q_RANK + MLA_KV_RANK + MLA_ROPE
RWKV_IN = 3 * RWKV_WIDTH + DECAY_RANK + AAA_RANK + GATE_RANK

LANES = 128
ROW_CHUNKS = 9
HEAD_PAD = LANES
QK_WIDTH = HEADS * HEAD_PAD
MLA_C_WIDTH = 768
ROUTER_WIDTH = LANES
VMEM_LIMIT = 56 * 1024 * 1024
ATTN_HEADS_PER_STEP = 8
ATTN_BLOCK = 256
ATTN_BLOCKS_PER_ITER = 2
V_ROWS = 80
RWKV_CHUNKS_PER_STEP = 4
MERGE_SUBTILES = 2


def _dot(a, b):
    return jnp.dot(a, b, preferred_element_type=F32)


def _dot_nt(a, b):
    return lax.dot_general(a, b, (((1,), (1,)), ((), ())), preferred_element_type=F32)


def _dot_tn(a, b):
    return lax.dot_general(a, b, (((0,), (0,)), ((), ())), preferred_element_type=F32)


def _split(x):
    hi = x.astype(BF16)
    lo = (x - hi.astype(F32)).astype(BF16)
    return hi, lo


def _dot3(x, w_hi, w_lo):
    x_hi, x_lo = _split(x)
    return _dot(x_hi, w_hi) + (_dot(x_hi, w_lo) + _dot(x_lo, w_hi))


def _sigmoid(x):
    return 1.0 / (1.0 + jnp.exp(-x))


def _rms(x, g):
    return x * lax.rsqrt(jnp.mean(x * x, axis=-1, keepdims=True) + RMS_EPS) * g


def _ada_kernel(c_ref, w_ref, b_ref, o_ref):
    c = c_ref[...]
    s = c * _sigmoid(c)
    w_hi, w_lo = _split(w_ref[...])
    o_ref[...] = _dot3(s, w_hi, w_lo) + b_ref[...]


def _ada(c, w_ada, b_ada):
    b, d = c.shape
    n = w_ada.shape[1]
    tn = 512
    return pl.pallas_call(
        _ada_kernel,
        grid=(n // tn,),
        in_specs=[pl.BlockSpec((b, d), lambda j: (0, 0)),
                  pl.BlockSpec((d, tn), lambda j: (0, j)),
                  pl.BlockSpec((1, tn), lambda j: (0, j))],
        out_specs=pl.BlockSpec((b, tn), lambda j: (0, j)),
        out_shape=jax.ShapeDtypeStruct((b, n), F32),
        name="ada",
    )(c, w_ada, b_ada.reshape(1, n))


def _inproj_kernel(x_ref, mod_ref, pos_ref, gpre_ref, wc_ref, wr_ref, wg_ref, gcq_ref, gckv_ref,
                   wuq_ref, wuk_ref, wuv_ref, mu_ref, invf_ref,
                   q_ref, k_ref, vt_ref, zs_ref, gate_ref, carry_ref):
    i = pl.program_id(1)
    tm = x_ref.shape[0]

    @pl.when(i == 0)
    def _():
        carry_ref[...] = jnp.zeros_like(carry_ref)

    mod = mod_ref[0]
    shift, scale = mod[0:1], mod[1:2]
    h = (_rms(x_ref[...], gpre_ref[...]) * (1.0 + scale) + shift).astype(BF16)

    zc = _dot(h, wc_ref[...])
    zg = _dot(h, wg_ref[...])
    nq = _rms(zc[:, :MLA_Q_RANK], gcq_ref[...]).astype(BF16)
    nkv = _rms(zc[:, MLA_Q_RANK:MLA_Q_RANK + MLA_KV_RANK], gckv_ref[...]).astype(BF16)
    kr = zc[:, MLA_Q_RANK + MLA_KV_RANK:]
    qf = _dot(nq, wuq_ref[...])
    kn = _dot(nkv, wuk_ref[...])
    vt = _dot_nt(wuv_ref[...], nkv)
    zr = _dot(h, wr_ref[...])

    ang = pos_ref[...].astype(F32) * invf_ref[...]
    lane = lax.broadcasted_iota(jnp.int32, (1, LANES), 1)
    first_half = lane < MLA_NOPE + MLA_ROPE // 2
    in_rope = (lane >= MLA_NOPE) & (lane < MLA_NOPE + MLA_ROPE)
    cos_t = jnp.where(in_rope, jnp.cos(ang), 1.0)
    sin_a = jnp.sin(ang)
    sin_t = jnp.where(in_rope, jnp.where(first_half, -sin_a, sin_a), 0.0)

    def rope(t):
        rot = jnp.where(first_half, pltpu.roll(t, LANES - MLA_ROPE // 2, 1), pltpu.roll(t, MLA_ROPE // 2, 1))
        return t * cos_t + rot * sin_t

    kr_rot = rope(kr)
    q_scale = math.log2(math.e) / math.sqrt(MLA_NOPE + MLA_ROPE)
    for hh in range(HEADS):
        sl = slice(hh * HEAD_PAD, (hh + 1) * HEAD_PAD)
        q_ref[:, sl] = (rope(qf[:, sl]) * q_scale).astype(BF16)
        k_ref[:, sl] = (kn[:, sl] + kr_rot).astype(BF16)
    vrow = lax.broadcasted_iota(jnp.int32, vt.shape, 0) % V_ROWS
    vt = jnp.where(vrow == MLA_V, 1.0, vt).astype(BF16)
    for j in range(vt_ref.shape[0]):
        vt_ref[j] = vt[:, j * ATTN_BLOCK:(j + 1) * ATTN_BLOCK]

    gate_ref[...] = _sigmoid(zg).astype(BF16)

    row = lax.broadcasted_iota(jnp.int32, (tm, 1), 0)
    prev = jnp.where(row == 0, carry_ref[0:1, :], pltpu.roll(zr, 1, 0))
    carry_ref[0:1, :] = zr[tm - 1:tm, :]
    zs_ref[...] = (zr + (prev - zr) * mu_ref[...]).astype(BF16)


def _const_spec(shape):
    nd = len(shape)
    return pl.BlockSpec(shape, lambda *_: (0,) * nd, pipeline_mode=pl.Buffered(1))


def _inproj(x2, mod3, pos2, g_pre, w_c, w_r, w_g, g_cq, g_ckv, w_uq, w_uk, w_uv, mu, invf, *, batch, seq, tm):
    t, d = x2.shape
    ns = seq // tm
    row = lambda b, i: (b * ns + i, 0)
    widths = [QK_WIDTH, QK_WIDTH, RWKV_IN, 2 * d]
    specs = [pl.BlockSpec((tm, w), row) for w in widths]
    shapes = [jax.ShapeDtypeStruct((t, w), BF16) for w in widths]
    vw = HEADS * V_ROWS
    nb = tm // ATTN_BLOCK
    specs.insert(2, pl.BlockSpec((None, nb, vw, ATTN_BLOCK), lambda b, i: (b, i, 0, 0)))
    shapes.insert(2, jax.ShapeDtypeStruct((batch, ns * nb, vw, ATTN_BLOCK), BF16))
    return pl.pallas_call(
        _inproj_kernel,
        grid=(batch, ns),
        in_specs=[pl.BlockSpec((tm, d), row),
                  pl.BlockSpec((1, 6, d), lambda b, i: (b, 0, 0)),
                  pl.BlockSpec((tm, 1), row),
                  _const_spec(g_pre.shape), _const_spec(w_c.shape), _const_spec(w_r.shape),
                  _const_spec(w_g.shape), _const_spec(g_cq.shape), _const_spec(g_ckv.shape),
                  _const_spec(w_uq.shape), _const_spec(w_uk.shape), _const_spec(w_uv.shape),
                  _const_spec(mu.shape), _const_spec(invf.shape)],
        out_specs=specs,
        out_shape=shapes,
        scratch_shapes=[pltpu.VMEM((8, RWKV_IN), F32)],
        compiler_params=pltpu.CompilerParams(dimension_semantics=("parallel", "arbitrary"),
                                             vmem_limit_bytes=VMEM_LIMIT),
        name="inproj",
    )(x2, mod3, pos2, g_pre, w_c, w_r, w_g, g_cq, g_ckv, w_uq, w_uk, w_uv, mu, invf)


def _attn_kernel(q_ref, k_ref, vt_ref, o_ref):
    i = pl.program_id(2)
    tq = q_ref.shape[0]
    kc = lax.broadcasted_iota(jnp.int32, (tq, tq), 0) // CHUNK
    qc = lax.broadcasted_iota(jnp.int32, (tq, tq), 1) // CHUNK
    diag_mask = kc <= qc
    heads = range(ATTN_HEADS_PER_STEP)
    hs = [slice(h * HEAD_PAD, (h + 1) * HEAD_PAD) for h in heads]
    vr = [slice(h * V_ROWS, (h + 1) * V_ROWS) for h in heads]
    q = [q_ref[:, s] for s in hs]

    def block(kb, carry, nblk, masked):
        kbs = [kb + n for n in range(nblk)]
        rows = [pl.ds(pl.multiple_of(b * tq, tq), tq) for b in kbs]
        s = [[_dot_nt(k_ref[r, hs[h]], q[h]) for r in rows] for h in heads]
        if masked:
            s = [[jnp.where(diag_mask, t, NEG_INF) for t in sh] for sh in s]
        m_new = []
        for h in heads:
            m = carry[h][0]
            for t in s[h]:
                m = jnp.maximum(m, jnp.max(t, axis=0, keepdims=True))
            m_new.append(m)
        alpha = [jnp.exp2(carry[h][0] - m_new[h]) for h in heads]
        p = [jnp.concatenate([jnp.exp2(t - m_new[h]).astype(BF16) for t in s[h]], axis=0) for h in heads]
        acc = [alpha[h] * carry[h][1]
               + _dot(jnp.concatenate([vt_ref[b, vr[h], :] for b in kbs], axis=1), p[h]) for h in heads]
        return tuple((m_new[h], acc[h]) for h in heads)

    init = tuple((jnp.full((1, tq), NEG_INF, F32), jnp.zeros((V_ROWS, tq), F32)) for _ in heads)
    nb = ATTN_BLOCKS_PER_ITER
    carry = lax.fori_loop(0, i // nb, lambda n, c: block(n * nb, c, nb, False), init)
    carry = lax.fori_loop((i // nb) * nb, i, lambda kb, c: block(kb, c, 1, False), carry)
    final = block(i, carry, 1, True)
    outs = [final[h][1][:MLA_V] / final[h][1][MLA_V:MLA_V + 1] for h in heads]
    for pr in range(ATTN_HEADS_PER_STEP // 2):
        pair_t = jnp.concatenate([outs[2 * pr], outs[2 * pr + 1]], axis=0)
        o_ref[:, pr * LANES:(pr + 1) * LANES] = pair_t.T.astype(BF16)


def _attn(q_all, k_all, vt_all, *, batch, seq, tq):
    t = q_all.shape[0]
    nq = seq // tq
    g = ATTN_HEADS_PER_STEP
    return pl.pallas_call(
        _attn_kernel,
        grid=(batch, HEADS // g, nq),
        in_specs=[pl.BlockSpec((tq, g * HEAD_PAD), lambda b, p, i: (b * nq + i, p)),
                  pl.BlockSpec((seq, g * HEAD_PAD), lambda b, p, i: (b, p)),
                  pl.BlockSpec((None, nq, g * V_ROWS, tq), lambda b, p, i: (b, 0, p, 0))],
        out_specs=pl.BlockSpec((tq, g * MLA_V), lambda b, p, i: (b * nq + i, p)),
        out_shape=jax.ShapeDtypeStruct((t, HEADS * MLA_V), BF16),
        compiler_params=pltpu.CompilerParams(dimension_semantics=("parallel", "parallel", "arbitrary"),
                                             vmem_limit_bytes=VMEM_LIMIT),
        name="attn",
    )(q_all, k_all, vt_all)


def _rwkv_kernel(zs_ref, wda_ref, wgu_ref, dbias_ref, abias_ref, kk_ref, ka_ref, rk_ref, lnw_ref, lnb_ref,
                 ones_ref, tri_ref,
                 o_ref,
                 st_ref, a_s, r_s, b_s, k_s, bg_s, kg_s, v_s, gc_s, bonus_s, g_s, y_s):
    w = RWKV_WIDTH
    nsub = RWKV_CHUNKS_PER_STEP
    th = nsub * CHUNK
    n2 = 2 * CHUNK
    staged = (a_s, r_s, b_s, k_s, bg_s, kg_s, v_s, gc_s, bonus_s, g_s)

    @pl.when(pl.program_id(1) == 0)
    def _():
        st_ref[...] = jnp.zeros_like(st_ref)

    ones_bd = ones_ref[...]

    def head_sum(t):
        tb = t.astype(BF16)
        hw = ones_bd.shape[0]
        return jnp.concatenate([_dot(tb[:, n * hw:(n + 1) * hw], ones_bd) for n in range(w // hw)], axis=1)

    ri = lax.broadcasted_iota(jnp.int32, (4 * CHUNK, n2), 0)
    ci = lax.broadcasted_iota(jnp.int32, (4 * CHUNK, n2), 1) % CHUNK
    tri_mask = (ri % CHUNK + ri // n2) > ci
    lane2 = lax.broadcasted_iota(jnp.int32, (CHUNK, n2), 1)
    h1 = lane2 < RWKV_HEAD
    bi = lax.broadcasted_iota(jnp.int32, (n2, n2), 0) // RWKV_HEAD
    bj = lax.broadcasted_iota(jnp.int32, (n2, n2), 1) // RWKV_HEAD
    bd_mask = bi == bj
    zero_bf = jnp.zeros((CHUNK, n2), BF16)
    zero_f = jnp.zeros((CHUNK, n2), F32)
    lane = lax.broadcasted_iota(jnp.int32, (1, LANES), 1)

    def prepare(h):
        hr = slice(h * th, (h + 1) * th)
        zs = zs_ref[hr, :].astype(F32)
        r, k, v = zs[:, 0:w], zs[:, w:2 * w], zs[:, 2 * w:3 * w]
        da = zs[:, 3 * w:3 * w + LANES]
        gd = zs[:, 3 * w + LANES:]
        lora_in = jnp.where(lane < DECAY_RANK, jnp.tanh(da), da).astype(BF16)
        pre = _dot(lora_in, wda_ref[...])
        g_new = _dot(_sigmoid(gd).astype(BF16), wgu_ref[...])
        yield
        u = -(dbias_ref[...] + pre[:, :w])
        softplus = jnp.maximum(u, 0.0) + jnp.log(1.0 + jnp.exp(-jnp.abs(u)))
        logw = -jnp.exp(-softplus - 0.5)
        eta = _sigmoid(abias_ref[...] + pre[:, w:])
        kk = k * kk_ref[...]
        kk_ss = head_sum(kk * kk)
        lw_hi, lw_lo = _split(logw)
        cum = _dot(tri_ref[...], lw_hi) + _dot(tri_ref[...], lw_lo)
        yield
        kk = kk * jnp.minimum(lax.rsqrt(kk_ss), 1e12)
        kp = k * (1.0 + (eta - 1.0) * ka_ref[...])
        bonus_new = head_sum(r * kp * rk_ref[...]) * v
        yield
        cum_end = jnp.concatenate([jnp.broadcast_to(cum[(c + 1) * CHUNK - 1:(c + 1) * CHUNK, :], (CHUNK, w))
                                   for c in range(nsub)], axis=0)
        b_in = kk * eta
        g_inv = jnp.exp(-cum)
        g_rem = jnp.exp(cum_end - cum)
        staged_new = ((-kk * jnp.exp(cum - logw)).astype(BF16), (r * jnp.exp(cum)).astype(BF16),
                      (b_in * g_inv).astype(BF16), (kp * g_inv).astype(BF16),
                      (b_in * g_rem).astype(BF16), (kp * g_rem).astype(BF16), v.astype(BF16),
                      jnp.exp(cum_end), bonus_new, g_new)
        for ref, val in zip(staged, staged_new):
            ref[hr, :] = val
        yield

    def solve(h):
        blocks = [(sub, p) for sub in range(nsub) for p in range(HEADS // 2)]
        rows = [slice(h * th + sub * CHUNK, h * th + (sub + 1) * CHUNK) for sub in range(nsub)]
        lns = [slice(p * n2, (p + 1) * n2) for p in range(HEADS // 2)]
        a_t = [a_s[rows[sub], lns[p]] for sub, p in blocks]
        r_t = [r_s[rows[sub], lns[p]] for sub, p in blocks]
        v_t = [v_s[rows[sub], lns[p]] for sub, p in blocks]
        ml = []
        for n, (sub, p) in enumerate(blocks):
            lhs = jnp.concatenate([jnp.where(h1, a_t[n], zero_bf), jnp.where(h1, zero_bf, a_t[n]),
                                   jnp.where(h1, r_t[n], zero_bf), jnp.where(h1, zero_bf, r_t[n])], axis=0)
            rhs = jnp.concatenate([b_s[rows[sub], lns[p]], k_s[rows[sub], lns[p]]], axis=0)
            ml.append(jnp.where(tri_mask, _dot_nt(lhs, rhs), 0.0))
        ml_b = [m.astype(BF16) for m in ml]
        yield
        units = [(n, hh) for n in range(len(blocks)) for hh in range(2)]
        w_t = [_dot(ml_b[n][hh * CHUNK:(hh + 1) * CHUNK, :], jnp.concatenate([zero_bf, v_t[n]], axis=0))
               for n, hh in units]
        a_f = [t.astype(F32) for t in a_t]
        a_sw = [pltpu.roll(t, RWKV_HEAD, 1) for t in a_f]
        x = [jnp.where(h1, a_f[n], pltpu.roll(w_t[2 * n], RWKV_HEAD, 1)) if hh == 0
             else jnp.where(h1, a_sw[n], w_t[2 * n + 1]) for n, hh in units]
        lsq = [jnp.where(h1, ml[n][hh * CHUNK:(hh + 1) * CHUNK, :], 0.0) for n, hh in units]
        yield
        for _ in range(6):
            out = [_dot(lsq[u][:, :CHUNK].astype(BF16),
                        jnp.concatenate([x[u], lsq[u]], axis=1).astype(BF16)) for u in range(len(units))]
            x = [x[u] + out[u][:, :n2] for u in range(len(units))]
            lsq = [t[:, n2:] for t in out]
            yield
        z = []
        for n in range(len(blocks)):
            x0, x1 = x[2 * n], x[2 * n + 1]
            a_hat = jnp.where(h1, x0, pltpu.roll(x1, RWKV_HEAD, 1))
            u0 = jnp.where(h1, pltpu.roll(x0, RWKV_HEAD, 1), x1)
            z.append(jnp.concatenate([jnp.concatenate([a_hat, u0], axis=1),
                                      jnp.concatenate([zero_f, v_t[n].astype(F32)], axis=1)],
                                     axis=0).astype(BF16))
        o0 = [_dot(ml_b[n][2 * CHUNK:3 * CHUNK, :], z[n]) for n in range(len(blocks))]
        o1 = [_dot(ml_b[n][3 * CHUNK:4 * CHUNK, :], z[n]) for n in range(len(blocks))]
        pm = [_dot_tn(z[n], jnp.concatenate([bg_s[rows[sub], lns[p]], kg_s[rows[sub], lns[p]]], axis=0))
              for n, (sub, p) in enumerate(blocks)]
        r_hat = [(r_t[n].astype(F32) + jnp.where(h1, o0[n][:, :n2], o1[n][:, :n2])).astype(BF16)
                 for n in range(len(blocks))]
        yield
        state = [st_ref[p] for p in range(HEADS // 2)]
        for n, (sub, p) in enumerate(blocks):
            y0 = jnp.where(h1, o0[n][:, n2:], o1[n][:, n2:])
            g_t = jnp.where(bd_mask, pm[n][:n2], 0.0).astype(BF16)
            h_t = jnp.where(bd_mask, pm[n][n2:], 0.0)
            s_b = state[p].astype(BF16)
            y_s[rows[sub], lns[p]] = _dot_nt(r_hat[n], s_b) + y0
            gc = gc_s[rows[sub].start:rows[sub].start + 1, lns[p]]
            state[p] = gc * state[p] + _dot(s_b, g_t) + h_t
        for p in range(HEADS // 2):
            st_ref[p] = state[p]
        yield

    def finish(h):
        hr = slice(h * th, (h + 1) * th)
        y = y_s[hr, :]
        inv_n = 1.0 / RWKV_HEAD
        dev = y - head_sum(y) * inv_n
        yield
        var = head_sum(dev * dev) * inv_n
        yield
        out = (dev * lax.rsqrt(var + GN_EPS) * lnw_ref[...] + lnb_ref[...] + bonus_s[hr, :]) * g_s[hr, :]
        o_ref[hr, :] = out.astype(BF16)
        yield

    def emit(main, side=None, every=1):
        n = 0
        for _ in main:
            n += 1
            if side is not None and n % every == 0:
                next(side, None)
        if side is not None:
            for _ in side:
                pass

    emit(prepare(0))
    emit(solve(0), prepare(1), every=2)
    emit(solve(1), finish(0), every=3)
    emit(finish(1))


def _rwkv(zs, w_da, w_gu, dbias, abias, k_k, k_a, r_k, lnw, lnb, ones_bd, tri, *, batch, seq, tc):
    t = zs.shape[0]
    nc = seq // tc
    w = RWKV_WIDTH
    consts = [w_da, w_gu, dbias, abias, k_k, k_a, r_k, lnw, lnb, ones_bd, tri]
    return pl.pallas_call(
        _rwkv_kernel,
        grid=(batch, nc),
        in_specs=[pl.BlockSpec((tc, RWKV_IN), lambda b, i: (b * nc + i, 0))] + [_const_spec(a.shape) for a in consts],
        out_specs=pl.BlockSpec((tc, w), lambda b, i: (b * nc + i, 0)),
        out_shape=jax.ShapeDtypeStruct((t, w), BF16),
        scratch_shapes=[pltpu.VMEM((HEADS // 2, 2 * RWKV_HEAD, 2 * RWKV_HEAD), F32)]
        + [pltpu.VMEM((tc, w), BF16)] * 7
        + [pltpu.VMEM((tc, w), F32)] * 4,
        compiler_params=pltpu.CompilerParams(dimension_semantics=("parallel", "arbitrary"),
                                             vmem_limit_bytes=VMEM_LIMIT),
        name="rwkv",
    )(zs, *consts)


def _route(logits):
    lane = lax.broadcasted_iota(jnp.int32, logits.shape, 1)
    lane_f = lane.astype(F32)
    big = float(ROUTER_WIDTH)
    is_group = (lane >= N_EXPERTS) & (lane < N_EXPERTS + N_GROUPS)
    gl = jnp.where(is_group, logits, NEG_INF)
    g_max = jnp.max(gl, axis=-1, keepdims=True)
    g_sel = jnp.min(jnp.where(gl == g_max, lane_f, big), axis=-1, keepdims=True) - float(N_EXPERTS)
    p_sel = 1.0 / jnp.sum(jnp.where(is_group, jnp.exp(gl - g_max), 0.0), axis=-1, keepdims=True)
    in_group = (lane < N_EXPERTS) & ((lane // EXPERTS_PER_GROUP).astype(F32) == g_sel)
    el = jnp.where(in_group, logits, NEG_INF)
    t1 = jnp.max(el, axis=-1, keepdims=True)
    i1 = jnp.min(jnp.where(el == t1, lane_f, big), axis=-1, keepdims=True)
    el2 = jnp.where(lane_f == i1, NEG_INF, el)
    t2 = jnp.max(el2, axis=-1, keepdims=True)
    i2 = jnp.min(jnp.where(el2 == t2, lane_f, big), axis=-1, keepdims=True)
    e21 = jnp.exp(t2 - t1)
    w1 = p_sel / (1.0 + e21)
    w2 = w1 * e21
    first = g_sel * float(EXPERTS_PER_GROUP)
    cw = jnp.where(lane_f == i1 - first, w1, 0.0) + jnp.where(lane_f == i2 - first, w2, 0.0)
    return g_sel, cw


def _merge_kernel(attn_ref, rw_ref, gate_ref, x_ref, mod_ref, womla_ref, worw_ref, wout_ref,
                  gpost_ref, gffn_ref, wrh_ref, wrl_ref, br_ref, stril_ref,
                  x1_ref, hp_ref, dest_ref, cnt_ref, carry_ref, *, group_capacity):
    i = pl.program_id(0)
    tm, d = x_ref.shape

    @pl.when(i == 0)
    def _():
        carry_ref[...] = jnp.zeros_like(carry_ref)

    mod = mod_ref[0]
    gate1, shift2, scale2 = mod[2:3], mod[3:4], mod[4:5]

    ts = tm // MERGE_SUBTILES
    subs = [slice(n * ts, (n + 1) * ts) for n in range(MERGE_SUBTILES)]
    o_mla = [_dot(attn_ref[r, :], womla_ref[...]) for r in subs]
    o_rw = [_dot(rw_ref[r, :], worw_ref[...]) for r in subs]
    o = [(gate_ref[r, :d].astype(F32) * o_mla[n] + gate_ref[r, d:].astype(F32) * o_rw[n]).astype(BF16)
         for n, r in enumerate(subs)]
    y = [_dot(t, wout_ref[...]) for t in o]
    x1 = [x_ref[r, :] + gate1 * _rms(y[n], gpost_ref[...]) for n, r in enumerate(subs)]
    h2 = [_rms(t, gffn_ref[...]) * (1.0 + scale2) + shift2 for t in x1]
    logits = [_dot3(t, wrh_ref[...], wrl_ref[...]) + br_ref[...] for t in h2]
    for n, r in enumerate(subs):
        x1_ref[r, :] = x1[n]
        for c in range(d // LANES):
            hp_ref[pl.ds(n * ts * ROW_CHUNKS + c, ts, stride=ROW_CHUNKS), :] = h2[n][:, c * LANES:(c + 1) * LANES]

    lane_f = lax.broadcasted_iota(jnp.int32, (ts, ROUTER_WIDTH), 1).astype(F32)
    routes = [_route(t) for t in logits]
    g_sels = [g for g, _ in routes]
    for n, r in enumerate(subs):
        hp_ref[pl.ds(n * ts * ROW_CHUNKS + d // LANES, ts, stride=ROW_CHUNKS), :] = routes[n][1]
    onehots = [jnp.where(lane_f == g, 1.0, 0.0) for g in g_sels]

    onehot = jnp.concatenate(onehots, axis=0)
    earlier = _dot(stril_ref[...], onehot.astype(BF16))
    carry = carry_ref[0:1, :]
    for n, r in enumerate(subs):
        rank = jnp.sum(jnp.where(lane_f == g_sels[n], carry + earlier[r, :], 0.0), axis=-1, keepdims=True)
        dest_ref[r, :] = (g_sels[n] * float(group_capacity) + rank).astype(jnp.int32)
    total = jnp.broadcast_to(carry + jnp.sum(onehot, axis=0, keepdims=True), carry_ref.shape)
    carry_ref[...] = total
    cnt_ref[...] = total


def _merge(attn, rw, gates, x2, mod3, w_o_mla, w_o_rwkv, w_out, g_post, g_ffn, wr_hi, wr_lo, b_r, *, seq, tm):
    t, d = x2.shape
    ns = seq // tm
    row = lambda i: (i, 0)
    tid = jnp.arange(tm)
    stril = (tid[None, :] < tid[:, None]).astype(BF16)
    consts = [w_o_mla, w_o_rwkv, w_out, g_post, g_ffn, wr_hi, wr_lo, b_r, stril]
    return pl.pallas_call(
        functools.partial(_merge_kernel, group_capacity=t),
        grid=(t // tm,),
        in_specs=[pl.BlockSpec((tm, attn.shape[1]), row), pl.BlockSpec((tm, rw.shape[1]), row),
                  pl.BlockSpec((tm, 2 * d), row), pl.BlockSpec((tm, d), row),
                  pl.BlockSpec((1, 6, d), lambda i: (i // ns, 0, 0))] + [_const_spec(a.shape) for a in consts],
        out_specs=[pl.BlockSpec((tm, d), row), pl.BlockSpec((tm * ROW_CHUNKS, LANES), row),
                   pl.BlockSpec((tm, 1), row), pl.BlockSpec((8, ROUTER_WIDTH), lambda i: (0, 0))],
        out_shape=[jax.ShapeDtypeStruct((t, d), F32), jax.ShapeDtypeStruct((t * ROW_CHUNKS, LANES), F32),
                   jax.ShapeDtypeStruct((t, 1), jnp.int32), jax.ShapeDtypeStruct((8, ROUTER_WIDTH), F32)],
        scratch_shapes=[pltpu.VMEM((8, ROUTER_WIDTH), F32)],
        compiler_params=pltpu.CompilerParams(dimension_semantics=("arbitrary",), vmem_limit_bytes=VMEM_LIMIT),
        name="merge",
    )(attn, rw, gates, x2, mod3, *consts)


def _token_rows(i):
    return pl.ds(i * ROW_CHUNKS, ROW_CHUNKS)


def _chunk(ref, c, n):
    return ref[pl.ds(c, n, stride=ROW_CHUNKS), :]


def _dispatch_kernel(dest_ref, hp_ref, xs_ref, sem):
    td = hp_ref.shape[0] // ROW_CHUNKS
    base = pl.program_id(0) * td

    def issue(r, c):
        pltpu.make_async_copy(hp_ref.at[_token_rows(r)], xs_ref.at[_token_rows(dest_ref[base + r])], sem).start()
        return c

    lax.fori_loop(0, td, issue, 0, unroll=8)
    pltpu.make_async_copy(hp_ref, xs_ref.at[pl.ds(0, td * ROW_CHUNKS)], sem).wait()


def _dispatch(dest, hp, *, tokens_out, td):
    t = hp.shape[0] // ROW_CHUNKS
    return pl.pallas_call(
        _dispatch_kernel,
        grid_spec=pltpu.PrefetchScalarGridSpec(
            num_scalar_prefetch=1, grid=(t // td,),
            in_specs=[pl.BlockSpec((td * ROW_CHUNKS, LANES), lambda i, dest: (i, 0))],
            out_specs=pl.BlockSpec(memory_space=pl.ANY),
            scratch_shapes=[pltpu.SemaphoreType.DMA]),
        out_shape=jax.ShapeDtypeStruct((tokens_out * ROW_CHUNKS, LANES), F32),
        compiler_params=pltpu.CompilerParams(dimension_semantics=("arbitrary",), vmem_limit_bytes=VMEM_LIMIT),
        name="dispatch",
    )(dest, hp)


def _experts_kernel(tg_ref, tb_ref, tr_ref, xs_ref, wg_ref, wu_ref, wd_ref, ex_ref, ys_ref):
    del tg_ref, tb_ref
    nrows = tr_ref[pl.program_id(0)]

    @pl.when(nrows > 0)
    def _():
        tmx = xs_ref.shape[0] // ROW_CHUNKS
        nchunk = D_MODEL // LANES
        valid = lax.broadcasted_iota(jnp.int32, (tmx, 1), 0) < nrows
        x = jnp.concatenate([jnp.where(valid, _chunk(xs_ref, c, tmx), 0.0).astype(BF16) for c in range(nchunk)],
                            axis=1)
        cw_hi, cw_lo = _split(jnp.where(valid, _chunk(xs_ref, nchunk, tmx), 0.0))
        cwx = _dot(cw_hi, ex_ref[...]) + _dot(cw_lo, ex_ref[...])
        acc = jnp.zeros((tmx, D_MODEL), F32)
        for e in range(EXPERTS_PER_GROUP):
            a = _dot(x, wg_ref[e])
            act = a * _sigmoid(a) * _dot(x, wu_ref[e]) * cwx[:, e * EXPERT_FF:(e + 1) * EXPERT_FF]
            acc = acc + _dot(act.astype(BF16), wd_ref[e])
        for c in range(nchunk):
            ys_ref[pl.ds(c, tmx, stride=ROW_CHUNKS), :] = acc[:, c * LANES:(c + 1) * LANES]
        ys_ref[pl.ds(nchunk, tmx, stride=ROW_CHUNKS), :] = jnp.zeros((tmx, LANES), F32)


def _experts(tile_group, tile_blk, tile_rows, xs, w_gate, w_up, w_down, expand, *, tmx):
    nt = tile_group.shape[0]
    d = D_MODEL
    grp = lambda j, tg, tb, tr: (tg[j], 0, 0)
    tile = pl.BlockSpec((tmx * ROW_CHUNKS, LANES), lambda j, tg, tb, tr: (tb[j], 0))
    return pl.pallas_call(
        _experts_kernel,
        grid_spec=pltpu.PrefetchScalarGridSpec(
            num_scalar_prefetch=3, grid=(nt,),
            in_specs=[tile,
                      pl.BlockSpec((EXPERTS_PER_GROUP, d, EXPERT_FF), grp),
                      pl.BlockSpec((EXPERTS_PER_GROUP, d, EXPERT_FF), grp),
                      pl.BlockSpec((EXPERTS_PER_GROUP, EXPERT_FF, d), grp),
                      pl.BlockSpec(expand.shape, lambda j, tg, tb, tr: (0, 0))],
            out_specs=tile),
        out_shape=jax.ShapeDtypeStruct(xs.shape, F32),
        compiler_params=pltpu.CompilerParams(dimension_semantics=("arbitrary",), vmem_limit_bytes=VMEM_LIMIT),
        name="experts",
    )(tile_group, tile_blk, tile_rows, xs, w_gate, w_up, w_down, expand)


def _final_kernel(dest_ref, x1_ref, mod_ref, gpost_ref, ys_ref, o_ref, ybuf, sem):
    tmf, d = x1_ref.shape
    base = pl.program_id(0) * tmf

    def issue(r, c):
        pltpu.make_async_copy(ys_ref.at[_token_rows(dest_ref[base + r])], ybuf.at[_token_rows(r)], sem).start()
        return c

    lax.fori_loop(0, tmf, issue, 0, unroll=8)
    pltpu.make_async_copy(ys_ref.at[pl.ds(0, tmf * ROW_CHUNKS)], ybuf, sem).wait()
    gate2 = mod_ref[0][5:6]
    y = jnp.concatenate([_chunk(ybuf, c, tmf) for c in range(d // LANES)], axis=1)
    o_ref[...] = x1_ref[...] + gate2 * _rms(y, gpost_ref[...])


def _final(dest, x1, mod3, g_post, ys, *, seq, tmf):
    t, d = x1.shape
    ns = seq // tmf
    return pl.pallas_call(
        _final_kernel,
        grid_spec=pltpu.PrefetchScalarGridSpec(
            num_scalar_prefetch=1, grid=(t // tmf,),
            in_specs=[pl.BlockSpec((tmf, d), lambda i, dest: (i, 0)),
                      pl.BlockSpec((1, 6, d), lambda i, dest: (i // ns, 0, 0)),
                      pl.BlockSpec((1, d), lambda i, dest: (0, 0)),
                      pl.BlockSpec(memory_space=pl.ANY)],
            out_specs=pl.BlockSpec((tmf, d), lambda i, dest: (i, 0)),
            scratch_shapes=[pltpu.VMEM((tmf * ROW_CHUNKS, LANES), F32), pltpu.SemaphoreType.DMA]),
        out_shape=jax.ShapeDtypeStruct((t, d), F32),
        compiler_params=pltpu.CompilerParams(dimension_semantics=("arbitrary",), vmem_limit_bytes=VMEM_LIMIT),
        name="final",
    )(dest, x1, mod3, g_post, ys)


def _tile_map(counts, *, capacity, tmx):
    ntile = (counts + tmx - 1) // tmx
    ends = jnp.cumsum(ntile)
    starts = ends - ntile
    j = jnp.arange(capacity // tmx + N_GROUPS, dtype=jnp.int32)
    g = jnp.minimum(jnp.sum((j[:, None] >= ends[None, :]).astype(jnp.int32), axis=1), N_GROUPS - 1)
    local = j - starts[g]
    valid = j < ends[-1]
    blk = g * (capacity // tmx) + local
    rows = jnp.clip(counts[g] - local * tmx, 0, tmx)
    last = ends[-1] - 1
    return (jnp.where(valid, g, g[last]).astype(jnp.int32), jnp.where(valid, blk, blk[last]).astype(jnp.int32),
            jnp.where(valid, rows, 0).astype(jnp.int32))


def _pick_tile(seq, want):
    return want if seq % want == 0 else seq


def _layer(x, c, positions, w_ada, b_ada, g_pre_mix, g_post_mix, g_pre_ffn, g_post_ffn, w_in, g_cq, w_uq,
           g_ckv, w_ukv, w_o_mla, mu_shift, w_decay_up, decay_bias, w_a_up, a_bias, w_g_up, k_k, k_a, r_k,
           lnx_w, lnx_b, w_o_rwkv, w_out, w_router_group, b_router_group, w_router_expert, b_router_expert,
           w_exp_gate, w_exp_up, w_exp_down):
    batch, seq, d = x.shape
    t = batch * seq
    x2 = x.reshape(t, d)
    row1 = lambda a: a.reshape(1, -1)

    mod3 = _ada(c, w_ada, b_ada).reshape(batch, 6, d)

    zeros = lambda n: jnp.zeros((d, n), F32)
    w_c = jnp.concatenate([w_in[:, :MLA_Q_RANK + MLA_KV_RANK], zeros(MLA_NOPE),
                           w_in[:, MLA_Q_RANK + MLA_KV_RANK:MLA_IN], zeros(HEAD_PAD - MLA_NOPE - MLA_ROPE)],
                          axis=1).astype(BF16)
    w_r = w_in[:, MLA_IN:MLA_IN + RWKV_IN].astype(BF16)
    w_g = w_in[:, MLA_IN + RWKV_IN:].astype(BF16)
    w_uq_p = jnp.pad(w_uq.reshape(MLA_Q_RANK, HEADS, MLA_NOPE + MLA_ROPE),
                     ((0, 0), (0, 0), (0, HEAD_PAD - MLA_NOPE - MLA_ROPE))).reshape(MLA_Q_RANK, QK_WIDTH).astype(BF16)
    w_ukv3 = w_ukv.reshape(MLA_KV_RANK, HEADS, MLA_NOPE + MLA_V)
    w_uk_p = jnp.pad(w_ukv3[..., :MLA_NOPE], ((0, 0), (0, 0), (0, HEAD_PAD - MLA_NOPE))
                     ).reshape(MLA_KV_RANK, QK_WIDTH).astype(BF16)
    w_uv_t = jnp.pad(w_ukv3[..., MLA_NOPE:], ((0, 0), (0, 0), (0, V_ROWS - MLA_V))
                     ).reshape(MLA_KV_RANK, HEADS * V_ROWS).T.astype(BF16)
    inv_freq = jnp.power(ROPE_THETA, -jnp.arange(0, MLA_ROPE, 2, dtype=F32) / MLA_ROPE)
    invf = jnp.concatenate([jnp.zeros((MLA_NOPE,), F32), inv_freq, inv_freq,
                            jnp.zeros((HEAD_PAD - MLA_NOPE - MLA_ROPE,), F32)]).reshape(1, LANES)

    tm = _pick_tile(seq, 512)
    q_all, k_all, vt_all, zs, gates = _inproj(
        x2, mod3, positions.reshape(t, 1), row1(g_pre_mix), w_c, w_r, w_g, row1(g_cq), row1(g_ckv),
        w_uq_p, w_uk_p, w_uv_t, row1(mu_shift), invf, batch=batch, seq=seq, tm=tm)

    attn = _attn(q_all, k_all, vt_all, batch=batch, seq=seq, tq=ATTN_BLOCK)

    tc = 2 * CHUNK * RWKV_CHUNKS_PER_STEP
    w_da = jnp.concatenate([
        jnp.concatenate([w_decay_up, jnp.zeros_like(w_decay_up)], axis=1),
        jnp.concatenate([jnp.zeros_like(w_a_up), w_a_up], axis=1)], axis=0).astype(BF16)
    hid = jnp.arange(2 * LANES) // RWKV_HEAD
    ones_bd = (hid[:, None] == hid[None, :]).astype(BF16)
    tid = jnp.arange(tc // 2)
    same_chunk = (tid[:, None] // CHUNK) == (tid[None, :] // CHUNK)
    tri = (same_chunk & (tid[None, :] <= tid[:, None])).astype(BF16)
    rw = _rwkv(zs, w_da, w_g_up.astype(BF16), row1(decay_bias), row1(a_bias), row1(k_k), row1(k_a), row1(r_k),
               row1(lnx_w), row1(lnx_b), ones_bd, tri, batch=batch, seq=seq, tc=tc)

    w_rt = jnp.concatenate([w_router_expert, w_router_group,
                            jnp.zeros((d, ROUTER_WIDTH - N_EXPERTS - N_GROUPS), F32)], axis=1)
    wr_hi = w_rt.astype(BF16)
    wr_lo = (w_rt - wr_hi.astype(F32)).astype(BF16)
    b_r = jnp.concatenate([b_router_expert, b_router_group,
                           jnp.zeros((ROUTER_WIDTH - N_EXPERTS - N_GROUPS,), F32)]).reshape(1, ROUTER_WIDTH)
    x1, hp, dest, cnt = _merge(attn, rw, gates, x2, mod3, w_o_mla.astype(BF16), w_o_rwkv.astype(BF16),
                               w_out.astype(BF16), row1(g_post_mix), row1(g_pre_ffn), wr_hi, wr_lo, b_r,
                               seq=seq, tm=_pick_tile(seq, 512))

    tmx = _pick_tile(seq, 512)
    dest = dest.reshape(t)
    xs = _dispatch(dest, hp, tokens_out=N_GROUPS * t, td=_pick_tile(seq, 1024))
    tile_group, tile_blk, tile_rows = _tile_map(cnt[0, :N_GROUPS].astype(jnp.int32), capacity=t, tmx=tmx)
    eid = jnp.arange(EXPERTS_PER_GROUP * EXPERT_FF) // EXPERT_FF
    expand = (jnp.arange(ROUTER_WIDTH)[:, None] == eid[None, :]).astype(BF16)
    ys = _experts(tile_group, tile_blk, tile_rows, xs, w_exp_gate.astype(BF16), w_exp_up.astype(BF16),
                  w_exp_down.astype(BF16), expand, tmx=tmx)
    out = _final(dest, x1, mod3, row1(g_post_ffn), ys, seq=seq, tmf=_pick_tile(seq, 512))
    return out.reshape(batch, seq, d)


def kernel(x, c, positions, w_ada, b_ada, g_pre_mix, g_post_mix, g_pre_ffn, g_post_ffn, w_in, g_cq, w_uq, g_ckv, w_ukv, w_o_mla, mu_shift, w_decay_up, decay_bias, w_a_up, a_bias, w_g_up, k_k, k_a, r_k, lnx_w, lnx_b, w_o_rwkv, w_out, w_router_group, b_router_group, w_router_expert, b_router_expert, w_exp_gate, w_exp_up, w_exp_down):
    depth = w_ada.shape[0]
    for l in range(depth):
        x = _layer(x, c, positions, w_ada[l], b_ada[l], g_pre_mix[l], g_post_mix[l], g_pre_ffn[l], g_post_ffn[l],
                   w_in[l], g_cq[l], w_uq[l], g_ckv[l], w_ukv[l], w_o_mla[l], mu_shift[l], w_decay_up[l],
                   decay_bias[l], w_a_up[l], a_bias[l], w_g_up[l], k_k[l], k_a[l], r_k[l], lnx_w[l], lnx_b[l],
                   w_o_rwkv[l], w_out[l], w_router_group[l], b_router_group[l], w_router_expert[l],
                   b_router_expert[l], w_exp_gate[l], w_exp_up[l], w_exp_down[l])
    return x
```

```python
import functools
import math

import jax
import jax.numpy as jnp
from jax import lax
from jax.experimental import pallas as pl
from jax.experimental.pallas import tpu as pltpu

F32 = jnp.float32
BF16 = jnp.bfloat16

D_MODEL = 1024
CHUNK = 64
HEADS = 8
MLA_NOPE = 64
MLA_ROPE = 32
MLA_V = 64
MLA_Q_RANK = 384
MLA_KV_RANK = 256
ROPE_THETA = 10000.0
RWKV_HEAD = 64
RWKV_WIDTH = HEADS * RWKV_HEAD
DECAY_RANK = 64
AAA_RANK = 64
GATE_RANK = 128
GN_EPS = 64e-5
N_GROUPS = 4
EXPERTS_PER_GROUP = 8
N_EXPERTS = N_GROUPS * EXPERTS_PER_GROUP
EXPERT_FF = 256
RMS_EPS = 1e-6
NEG_INF = -1e30
MLA_IN = MLA_Q_RANK + MLA_KV_RANK + MLA_ROPE
RWKV_IN = 3 * RWKV_WIDTH + DECAY_RANK + AAA_RANK + GATE_RANK

LANES = 128
DMA_UNROLL = 8
ROW_CHUNKS = 9
HEAD_PAD = LANES
QK_WIDTH = HEADS * HEAD_PAD
MLA_C_WIDTH = 768
ROUTER_WIDTH = LANES
VMEM_LIMIT = 56 * 1024 * 1024
ATTN_HEADS_PER_STEP = 8
ATTN_BLOCK = 256
ATTN_BLOCKS_PER_ITER = 2
V_ROWS = 80
RWKV_CHUNKS_PER_STEP = 4
MERGE_SUBTILES = 2


def _dot(a, b):
    return jnp.dot(a, b, preferred_element_type=F32)


def _dot_nt(a, b):
    return lax.dot_general(a, b, (((1,), (1,)), ((), ())), preferred_element_type=F32)


def _dot_tn(a, b):
    return lax.dot_general(a, b, (((0,), (0,)), ((), ())), preferred_element_type=F32)


def _split(x):
    hi = x.astype(BF16)
    lo = (x - hi.astype(F32)).astype(BF16)
    return hi, lo


def _dot3(x, w_hi, w_lo):
    x_hi, x_lo = _split(x)
    return _dot(x_hi, w_hi) + (_dot(x_hi, w_lo) + _dot(x_lo, w_hi))


def _sigmoid(x):
    return 1.0 / (1.0 + jnp.exp(-x))


def _rms(x, g):
    return x * lax.rsqrt(jnp.mean(x * x, axis=-1, keepdims=True) + RMS_EPS) * g


def _ada_kernel(c_ref, w_ref, b_ref, o_ref):
    c = c_ref[...]
    s = c * _sigmoid(c)
    w_hi, w_lo = _split(w_ref[...])
    o_ref[...] = _dot3(s, w_hi, w_lo) + b_ref[...]


def _ada(c, w_ada, b_ada):
    b, d = c.shape
    n = w_ada.shape[1]
    tn = 512
    return pl.pallas_call(
        _ada_kernel,
        grid=(n // tn,),
        in_specs=[pl.BlockSpec((b, d), lambda j: (0, 0)),
                  pl.BlockSpec((d, tn), lambda j: (0, j)),
                  pl.BlockSpec((1, tn), lambda j: (0, j))],
        out_specs=pl.BlockSpec((b, tn), lambda j: (0, j)),
        out_shape=jax.ShapeDtypeStruct((b, n), F32),
        name="ada",
    )(c, w_ada, b_ada.reshape(1, n))


def _inproj_kernel(x_ref, mod_ref, pos_ref, gpre_ref, wc_ref, wr_ref, wg_ref, gcq_ref, gckv_ref,
                   wuq_ref, wuk_ref, wuv_ref, mu_ref, invf_ref,
                   q_ref, k_ref, vt_ref, zs_ref, gate_ref, carry_ref):
    i = pl.program_id(1)
    tm = x_ref.shape[0]

    @pl.when(i == 0)
    def _():
        carry_ref[...] = jnp.zeros_like(carry_ref)

    mod = mod_ref[0]
    shift, scale = mod[0:1], mod[1:2]
    h = (_rms(x_ref[...], gpre_ref[...]) * (1.0 + scale) + shift).astype(BF16)

    zc = _dot(h, wc_ref[...])
    zg = _dot(h, wg_ref[...])
    nq = _rms(zc[:, :MLA_Q_RANK], gcq_ref[...]).astype(BF16)
    nkv = _rms(zc[:, MLA_Q_RANK:MLA_Q_RANK + MLA_KV_RANK], gckv_ref[...]).astype(BF16)
    kr = zc[:, MLA_Q_RANK + MLA_KV_RANK:]
    qf = _dot(nq, wuq_ref[...])
    kn = _dot(nkv, wuk_ref[...])
    vt = _dot_nt(wuv_ref[...], nkv)
    zr = _dot(h, wr_ref[...])

    ang = pos_ref[...].astype(F32) * invf_ref[...]
    lane = lax.broadcasted_iota(jnp.int32, (1, LANES), 1)
    first_half = lane < MLA_NOPE + MLA_ROPE // 2
    in_rope = (lane >= MLA_NOPE) & (lane < MLA_NOPE + MLA_ROPE)
    cos_t = jnp.where(in_rope, jnp.cos(ang), 1.0)
    sin_a = jnp.sin(ang)
    sin_t = jnp.where(in_rope, jnp.where(first_half, -sin_a, sin_a), 0.0)

    def rope(t):
        rot = jnp.where(first_half, pltpu.roll(t, LANES - MLA_ROPE // 2, 1), pltpu.roll(t, MLA_ROPE // 2, 1))
        return t * cos_t + rot * sin_t

    kr_rot = rope(kr)
    q_scale = math.log2(math.e) / math.sqrt(MLA_NOPE + MLA_ROPE)
    for hh in range(HEADS):
        sl = slice(hh * HEAD_PAD, (hh + 1) * HEAD_PAD)
        q_ref[:, sl] = (rope(qf[:, sl]) * q_scale).astype(BF16)
        k_ref[:, sl] = (kn[:, sl] + kr_rot).astype(BF16)
    vrow = lax.broadcasted_iota(jnp.int32, vt.shape, 0) % V_ROWS
    vt = jnp.where(vrow == MLA_V, 1.0, vt).astype(BF16)
    for j in range(vt_ref.shape[0]):
        vt_ref[j] = vt[:, j * ATTN_BLOCK:(j + 1) * ATTN_BLOCK]

    gate_ref[...] = _sigmoid(zg).astype(BF16)

    row = lax.broadcasted_iota(jnp.int32, (tm, 1), 0)
    prev = jnp.where(row == 0, carry_ref[0:1, :], pltpu.roll(zr, 1, 0))
    carry_ref[0:1, :] = zr[tm - 1:tm, :]
    zs_ref[...] = (zr + (prev - zr) * mu_ref[...]).astype(BF16)


def _const_spec(shape):
    nd = len(shape)
    return pl.BlockSpec(shape, lambda *_: (0,) * nd, pipeline_mode=pl.Buffered(1))


def _inproj(x2, mod3, pos2, g_pre, w_c, w_r, w_g, g_cq, g_ckv, w_uq, w_uk, w_uv, mu, invf, *, batch, seq, tm):
    t, d = x2.shape
    ns = seq // tm
    row = lambda b, i: (b * ns + i, 0)
    widths = [QK_WIDTH, QK_WIDTH, RWKV_IN, 2 * d]
    specs = [pl.BlockSpec((tm, w), row) for w in widths]
    shapes = [jax.ShapeDtypeStruct((t, w), BF16) for w in widths]
    vw = HEADS * V_ROWS
    nb = tm // ATTN_BLOCK
    specs.insert(2, pl.BlockSpec((None, nb, vw, ATTN_BLOCK), lambda b, i: (b, i, 0, 0)))
    shapes.insert(2, jax.ShapeDtypeStruct((batch, ns * nb, vw, ATTN_BLOCK), BF16))
    return pl.pallas_call(
        _inproj_kernel,
        grid=(batch, ns),
        in_specs=[pl.BlockSpec((tm, d), row),
                  pl.BlockSpec((1, 6, d), lambda b, i: (b, 0, 0)),
                  pl.BlockSpec((tm, 1), row),
                  _const_spec(g_pre.shape), _const_spec(w_c.shape), _const_spec(w_r.shape),
                  _const_spec(w_g.shape), _const_spec(g_cq.shape), _const_spec(g_ckv.shape),
                  _const_spec(w_uq.shape), _const_spec(w_uk.shape), _const_spec(w_uv.shape),
                  _const_spec(mu.shape), _const_spec(invf.shape)],
        out_specs=specs,
        out_shape=shapes,
        scratch_shapes=[pltpu.VMEM((8, RWKV_IN), F32)],
        compiler_params=pltpu.CompilerParams(dimension_semantics=("parallel", "arbitrary"),
                                             vmem_limit_bytes=VMEM_LIMIT),
        name="inproj",
    )(x2, mod3, pos2, g_pre, w_c, w_r, w_g, g_cq, g_ckv, w_uq, w_uk, w_uv, mu, invf)


def _attn_kernel(q_ref, k_ref, vt_ref, o_ref):
    i = pl.program_id(2)
    tq = q_ref.shape[0]
    kc = lax.broadcasted_iota(jnp.int32, (tq, tq), 0) // CHUNK
    qc = lax.broadcasted_iota(jnp.int32, (tq, tq), 1) // CHUNK
    diag_mask = kc <= qc
    heads = range(ATTN_HEADS_PER_STEP)
    hs = [slice(h * HEAD_PAD, (h + 1) * HEAD_PAD) for h in heads]
    vr = [slice(h * V_ROWS, (h + 1) * V_ROWS) for h in heads]
    q = [q_ref[:, s] for s in hs]

    def block(kb, carry, nblk, masked):
        kbs = [kb + n for n in range(nblk)]
        rows = [pl.ds(pl.multiple_of(b * tq, tq), tq) for b in kbs]
        s = [[_dot_nt(k_ref[r, hs[h]], q[h]) for r in rows] for h in heads]
        if masked:
            s = [[jnp.where(diag_mask, t, NEG_INF) for t in sh] for sh in s]
        m_new = []
        for h in heads:
            m = carry[h][0]
            for t in s[h]:
                m = jnp.maximum(m, jnp.max(t, axis=0, keepdims=True))
            m_new.append(m)
        alpha = [jnp.exp2(carry[h][0] - m_new[h]) for h in heads]
        p = [jnp.concatenate([jnp.exp2(t - m_new[h]).astype(BF16) for t in s[h]], axis=0) for h in heads]
        acc = [alpha[h] * carry[h][1]
               + _dot(jnp.concatenate([vt_ref[b, vr[h], :] for b in kbs], axis=1), p[h]) for h in heads]
        return tuple((m_new[h], acc[h]) for h in heads)

    init = tuple((jnp.full((1, tq), NEG_INF, F32), jnp.zeros((V_ROWS, tq), F32)) for _ in heads)
    nb = ATTN_BLOCKS_PER_ITER
    carry = lax.fori_loop(0, i // nb, lambda n, c: block(n * nb, c, nb, False), init)
    carry = lax.fori_loop((i // nb) * nb, i, lambda kb, c: block(kb, c, 1, False), carry)
    final = block(i, carry, 1, True)
    outs = [final[h][1][:MLA_V] / final[h][1][MLA_V:MLA_V + 1] for h in heads]
    for pr in range(ATTN_HEADS_PER_STEP // 2):
        pair_t = jnp.concatenate([outs[2 * pr], outs[2 * pr + 1]], axis=0)
        o_ref[:, pr * LANES:(pr + 1) * LANES] = pair_t.T.astype(BF16)


def _attn(q_all, k_all, vt_all, *, batch, seq, tq):
    t = q_all.shape[0]
    nq = seq // tq
    g = ATTN_HEADS_PER_STEP
    return pl.pallas_call(
        _attn_kernel,
        grid=(batch, HEADS // g, nq),
        in_specs=[pl.BlockSpec((tq, g * HEAD_PAD), lambda b, p, i: (b * nq + i, p)),
                  pl.BlockSpec((seq, g * HEAD_PAD), lambda b, p, i: (b, p)),
                  pl.BlockSpec((None, nq, g * V_ROWS, tq), lambda b, p, i: (b, 0, p, 0))],
        out_specs=pl.BlockSpec((tq, g * MLA_V), lambda b, p, i: (b * nq + i, p)),
        out_shape=jax.ShapeDtypeStruct((t, HEADS * MLA_V), BF16),
        compiler_params=pltpu.CompilerParams(dimension_semantics=("parallel", "parallel", "arbitrary"),
                                             vmem_limit_bytes=VMEM_LIMIT),
        name="attn",
    )(q_all, k_all, vt_all)


def _rwkv_kernel(zs_ref, wda_ref, wgu_ref, dbias_ref, abias_ref, kk_ref, ka_ref, rk_ref, lnw_ref, lnb_ref,
                 ones_ref, tri_ref,
                 o_ref,
                 st_ref, a_s, r_s, b_s, k_s, bg_s, kg_s, v_s, gc_s, bonus_s, g_s, y_s):
    w = RWKV_WIDTH
    nsub = RWKV_CHUNKS_PER_STEP
    th = nsub * CHUNK
    n2 = 2 * CHUNK
    staged = (a_s, r_s, b_s, k_s, bg_s, kg_s, v_s, gc_s, bonus_s, g_s)

    @pl.when(pl.program_id(1) == 0)
    def _():
        st_ref[...] = jnp.zeros_like(st_ref)

    ones_bd = ones_ref[...]

    def head_sum(t):
        tb = t.astype(BF16)
        hw = ones_bd.shape[0]
        return jnp.concatenate([_dot(tb[:, n * hw:(n + 1) * hw], ones_bd) for n in range(w // hw)], axis=1)

    ri = lax.broadcasted_iota(jnp.int32, (4 * CHUNK, n2), 0)
    ci = lax.broadcasted_iota(jnp.int32, (4 * CHUNK, n2), 1) % CHUNK
    tri_mask = (ri % CHUNK + ri // n2) > ci
    lane2 = lax.broadcasted_iota(jnp.int32, (CHUNK, n2), 1)
    h1 = lane2 < RWKV_HEAD
    bi = lax.broadcasted_iota(jnp.int32, (n2, n2), 0) // RWKV_HEAD
    bj = lax.broadcasted_iota(jnp.int32, (n2, n2), 1) // RWKV_HEAD
    bd_mask = bi == bj
    zero_bf = jnp.zeros((CHUNK, n2), BF16)
    zero_f = jnp.zeros((CHUNK, n2), F32)
    lane = lax.broadcasted_iota(jnp.int32, (1, LANES), 1)

    def prepare(h):
        hr = slice(h * th, (h + 1) * th)
        zs = zs_ref[hr, :].astype(F32)
        r, k, v = zs[:, 0:w], zs[:, w:2 * w], zs[:, 2 * w:3 * w]
        da = zs[:, 3 * w:3 * w + LANES]
        gd = zs[:, 3 * w + LANES:]
        lora_in = jnp.where(lane < DECAY_RANK, jnp.tanh(da), da).astype(BF16)
        pre = _dot(lora_in, wda_ref[...])
        g_new = _dot(_sigmoid(gd).astype(BF16), wgu_ref[...])
        yield
        u = -(dbias_ref[...] + pre[:, :w])
        softplus = jnp.maximum(u, 0.0) + jnp.log(1.0 + jnp.exp(-jnp.abs(u)))
        logw = -jnp.exp(-softplus - 0.5)
        eta = _sigmoid(abias_ref[...] + pre[:, w:])
        kk = k * kk_ref[...]
        kk_ss = head_sum(kk * kk)
        lw_hi, lw_lo = _split(logw)
        cum = _dot(tri_ref[...], lw_hi) + _dot(tri_ref[...], lw_lo)
        yield
        kk = kk * jnp.minimum(lax.rsqrt(kk_ss), 1e12)
        kp = k * (1.0 + (eta - 1.0) * ka_ref[...])
        bonus_new = head_sum(r * kp * rk_ref[...]) * v
        yield
        cum_end = jnp.concatenate([jnp.broadcast_to(cum[(c + 1) * CHUNK - 1:(c + 1) * CHUNK, :], (CHUNK, w))
                                   for c in range(nsub)], axis=0)
        b_in = kk * eta
        g_inv = jnp.exp(-cum)
        g_rem = jnp.exp(cum_end - cum)
        staged_new = ((-kk * jnp.exp(cum - logw)).astype(BF16), (r * jnp.exp(cum)).astype(BF16),
                      (b_in * g_inv).astype(BF16), (kp * g_inv).astype(BF16),
                      (b_in * g_rem).astype(BF16), (kp * g_rem).astype(BF16), v.astype(BF16),
                      jnp.exp(cum_end), bonus_new, g_new)
        for ref, val in zip(staged, staged_new):
            ref[hr, :] = val
        yield

    def solve(h):
        blocks = [(sub, p) for sub in range(nsub) for p in range(HEADS // 2)]
        rows = [slice(h * th + sub * CHUNK, h * th + (sub + 1) * CHUNK) for sub in range(nsub)]
        lns = [slice(p * n2, (p + 1) * n2) for p in range(HEADS // 2)]
        a_t = [a_s[rows[sub], lns[p]] for sub, p in blocks]
        r_t = [r_s[rows[sub], lns[p]] for sub, p in blocks]
        v_t = [v_s[rows[sub], lns[p]] for sub, p in blocks]
        ml = []
        for n, (sub, p) in enumerate(blocks):
            lhs = jnp.concatenate([jnp.where(h1, a_t[n], zero_bf), jnp.where(h1, zero_bf, a_t[n]),
                                   jnp.where(h1, r_t[n], zero_bf), jnp.where(h1, zero_bf, r_t[n])], axis=0)
            rhs = jnp.concatenate([b_s[rows[sub], lns[p]], k_s[rows[sub], lns[p]]], axis=0)
            ml.append(jnp.where(tri_mask, _dot_nt(lhs, rhs), 0.0))
        ml_b = [m.astype(BF16) for m in ml]
        yield
        units = [(n, hh) for n in range(len(blocks)) for hh in range(2)]
        w_t = [_dot(ml_b[n][hh * CHUNK:(hh + 1) * CHUNK, :], jnp.concatenate([zero_bf, v_t[n]], axis=0))
               for n, hh in units]
        a_f = [t.astype(F32) for t in a_t]
        a_sw = [pltpu.roll(t, RWKV_HEAD, 1) for t in a_f]
        x = [jnp.where(h1, a_f[n], pltpu.roll(w_t[2 * n], RWKV_HEAD, 1)) if hh == 0
             else jnp.where(h1, a_sw[n], w_t[2 * n + 1]) for n, hh in units]
        lsq = [jnp.where(h1, ml[n][hh * CHUNK:(hh + 1) * CHUNK, :], 0.0) for n, hh in units]
        yield
        for _ in range(6):
            out = [_dot(lsq[u][:, :CHUNK].astype(BF16),
                        jnp.concatenate([x[u], lsq[u]], axis=1).astype(BF16)) for u in range(len(units))]
            x = [x[u] + out[u][:, :n2] for u in range(len(units))]
            lsq = [t[:, n2:] for t in out]
            yield
        z = []
        for n in range(len(blocks)):
            x0, x1 = x[2 * n], x[2 * n + 1]
            a_hat = jnp.where(h1, x0, pltpu.roll(x1, RWKV_HEAD, 1))
            u0 = jnp.where(h1, pltpu.roll(x0, RWKV_HEAD, 1), x1)
            z.append(jnp.concatenate([jnp.concatenate([a_hat, u0], axis=1),
                                      jnp.concatenate([zero_f, v_t[n].astype(F32)], axis=1)],
                                     axis=0).astype(BF16))
        o0 = [_dot(ml_b[n][2 * CHUNK:3 * CHUNK, :], z[n]) for n in range(len(blocks))]
        o1 = [_dot(ml_b[n][3 * CHUNK:4 * CHUNK, :], z[n]) for n in range(len(blocks))]
        pm = [_dot_tn(z[n], jnp.concatenate([bg_s[rows[sub], lns[p]], kg_s[rows[sub], lns[p]]], axis=0))
              for n, (sub, p) in enumerate(blocks)]
        r_hat = [(r_t[n].astype(F32) + jnp.where(h1, o0[n][:, :n2], o1[n][:, :n2])).astype(BF16)
                 for n in range(len(blocks))]
        yield
        state = [st_ref[p] for p in range(HEADS // 2)]
        for n, (sub, p) in enumerate(blocks):
            y0 = jnp.where(h1, o0[n][:, n2:], o1[n][:, n2:])
            g_t = jnp.where(bd_mask, pm[n][:n2], 0.0).astype(BF16)
            h_t = jnp.where(bd_mask, pm[n][n2:], 0.0)
            s_b = state[p].astype(BF16)
            y_s[rows[sub], lns[p]] = _dot_nt(r_hat[n], s_b) + y0
            gc = gc_s[rows[sub].start:rows[sub].start + 1, lns[p]]
            state[p] = gc * state[p] + _dot(s_b, g_t) + h_t
        for p in range(HEADS // 2):
            st_ref[p] = state[p]
        yield

    def finish(h):
        hr = slice(h * th, (h + 1) * th)
        y = y_s[hr, :]
        inv_n = 1.0 / RWKV_HEAD
        dev = y - head_sum(y) * inv_n
        yield
        var = head_sum(dev * dev) * inv_n
        yield
        out = (dev * lax.rsqrt(var + GN_EPS) * lnw_ref[...] + lnb_ref[...] + bonus_s[hr, :]) * g_s[hr, :]
        o_ref[hr, :] = out.astype(BF16)
        yield

    def emit(main, side=None, every=1):
        n = 0
        for _ in main:
            n += 1
            if side is not None and n % every == 0:
                next(side, None)
        if side is not None:
            for _ in side:
                pass

    emit(prepare(0))
    emit(solve(0), prepare(1), every=2)
    emit(solve(1), finish(0), every=3)
    emit(finish(1))


def _rwkv(zs, w_da, w_gu, dbias, abias, k_k, k_a, r_k, lnw, lnb, ones_bd, tri, *, batch, seq, tc):
    t = zs.shape[0]
    nc = seq // tc
    w = RWKV_WIDTH
    consts = [w_da, w_gu, dbias, abias, k_k, k_a, r_k, lnw, lnb, ones_bd, tri]
    return pl.pallas_call(
        _rwkv_kernel,
        grid=(batch, nc),
        in_specs=[pl.BlockSpec((tc, RWKV_IN), lambda b, i: (b * nc + i, 0))] + [_const_spec(a.shape) for a in consts],
        out_specs=pl.BlockSpec((tc, w), lambda b, i: (b * nc + i, 0)),
        out_shape=jax.ShapeDtypeStruct((t, w), BF16),
        scratch_shapes=[pltpu.VMEM((HEADS // 2, 2 * RWKV_HEAD, 2 * RWKV_HEAD), F32)]
        + [pltpu.VMEM((tc, w), BF16)] * 7
        + [pltpu.VMEM((tc, w), F32)] * 4,
        compiler_params=pltpu.CompilerParams(dimension_semantics=("parallel", "arbitrary"),
                                             vmem_limit_bytes=VMEM_LIMIT),
        name="rwkv",
    )(zs, *consts)


def _route(logits):
    lane = lax.broadcasted_iota(jnp.int32, logits.shape, 1)
    lane_f = lane.astype(F32)
    big = float(ROUTER_WIDTH)
    is_group = (lane >= N_EXPERTS) & (lane < N_EXPERTS + N_GROUPS)
    gl = jnp.where(is_group, logits, NEG_INF)
    g_max = jnp.max(gl, axis=-1, keepdims=True)
    g_sel = jnp.min(jnp.where(gl == g_max, lane_f, big), axis=-1, keepdims=True) - float(N_EXPERTS)
    p_sel = 1.0 / jnp.sum(jnp.where(is_group, jnp.exp(gl - g_max), 0.0), axis=-1, keepdims=True)
    in_group = (lane < N_EXPERTS) & ((lane // EXPERTS_PER_GROUP).astype(F32) == g_sel)
    el = jnp.where(in_group, logits, NEG_INF)
    t1 = jnp.max(el, axis=-1, keepdims=True)
    i1 = jnp.min(jnp.where(el == t1, lane_f, big), axis=-1, keepdims=True)
    el2 = jnp.where(lane_f == i1, NEG_INF, el)
    t2 = jnp.max(el2, axis=-1, keepdims=True)
    i2 = jnp.min(jnp.where(el2 == t2, lane_f, big), axis=-1, keepdims=True)
    e21 = jnp.exp(t2 - t1)
    w1 = p_sel / (1.0 + e21)
    w2 = w1 * e21
    first = g_sel * float(EXPERTS_PER_GROUP)
    cw = jnp.where(lane_f == i1 - first, w1, 0.0) + jnp.where(lane_f == i2 - first, w2, 0.0)
    return g_sel, cw


def _merge_kernel(attn_ref, rw_ref, gate_ref, x_ref, mod_ref, womla_ref, worw_ref, wout_ref,
                  gpost_ref, gffn_ref, wrh_ref, wrl_ref, br_ref, stril_ref,
                  x1_ref, hp_ref, dest_ref, cnt_ref, carry_ref, *, group_capacity):
    i = pl.program_id(0)
    tm, d = x_ref.shape

    @pl.when(i == 0)
    def _():
        carry_ref[...] = jnp.zeros_like(carry_ref)

    mod = mod_ref[0]
    gate1, shift2, scale2 = mod[2:3], mod[3:4], mod[4:5]

    ts = tm // MERGE_SUBTILES
    subs = [slice(n * ts, (n + 1) * ts) for n in range(MERGE_SUBTILES)]
    o_mla = [_dot(attn_ref[r, :], womla_ref[...]) for r in subs]
    o_rw = [_dot(rw_ref[r, :], worw_ref[...]) for r in subs]
    o = [(gate_ref[r, :d].astype(F32) * o_mla[n] + gate_ref[r, d:].astype(F32) * o_rw[n]).astype(BF16)
         for n, r in enumerate(subs)]
    y = [_dot(t, wout_ref[...]) for t in o]
    x1 = [x_ref[r, :] + gate1 * _rms(y[n], gpost_ref[...]) for n, r in enumerate(subs)]
    h2 = [_rms(t, gffn_ref[...]) * (1.0 + scale2) + shift2 for t in x1]
    logits = [_dot3(t, wrh_ref[...], wrl_ref[...]) + br_ref[...] for t in h2]
    for n, r in enumerate(subs):
        x1_ref[r, :] = x1[n]
        for c in range(d // LANES):
            hp_ref[pl.ds(n * ts * ROW_CHUNKS + c, ts, stride=ROW_CHUNKS), :] = h2[n][:, c * LANES:(c + 1) * LANES]

    lane_f = lax.broadcasted_iota(jnp.int32, (ts, ROUTER_WIDTH), 1).astype(F32)
    routes = [_route(t) for t in logits]
    g_sels = [g for g, _ in routes]
    for n, r in enumerate(subs):
        hp_ref[pl.ds(n * ts * ROW_CHUNKS + d // LANES, ts, stride=ROW_CHUNKS), :] = routes[n][1]
    onehots = [jnp.where(lane_f == g, 1.0, 0.0) for g in g_sels]

    onehot = jnp.concatenate(onehots, axis=0)
    earlier = _dot(stril_ref[...], onehot.astype(BF16))
    carry = carry_ref[0:1, :]
    for n, r in enumerate(subs):
        rank = jnp.sum(jnp.where(lane_f == g_sels[n], carry + earlier[r, :], 0.0), axis=-1, keepdims=True)
        dest_ref[r, :] = (g_sels[n] * float(group_capacity) + rank).astype(jnp.int32)
    total = jnp.broadcast_to(carry + jnp.sum(onehot, axis=0, keepdims=True), carry_ref.shape)
    carry_ref[...] = total
    cnt_ref[...] = total


def _merge(attn, rw, gates, x2, mod3, w_o_mla, w_o_rwkv, w_out, g_post, g_ffn, wr_hi, wr_lo, b_r, *, seq, tm):
    t, d = x2.shape
    ns = seq // tm
    row = lambda i: (i, 0)
    tid = jnp.arange(tm)
    stril = (tid[None, :] < tid[:, None]).astype(BF16)
    consts = [w_o_mla, w_o_rwkv, w_out, g_post, g_ffn, wr_hi, wr_lo, b_r, stril]
    return pl.pallas_call(
        functools.partial(_merge_kernel, group_capacity=t),
        grid=(t // tm,),
        in_specs=[pl.BlockSpec((tm, attn.shape[1]), row), pl.BlockSpec((tm, rw.shape[1]), row),
                  pl.BlockSpec((tm, 2 * d), row), pl.BlockSpec((tm, d), row),
                  pl.BlockSpec((1, 6, d), lambda i: (i // ns, 0, 0))] + [_const_spec(a.shape) for a in consts],
        out_specs=[pl.BlockSpec((tm, d), row), pl.BlockSpec((tm * ROW_CHUNKS, LANES), row),
                   pl.BlockSpec((tm, 1), row), pl.BlockSpec((8, ROUTER_WIDTH), lambda i: (0, 0))],
        out_shape=[jax.ShapeDtypeStruct((t, d), F32), jax.ShapeDtypeStruct((t * ROW_CHUNKS, LANES), F32),
                   jax.ShapeDtypeStruct((t, 1), jnp.int32), jax.ShapeDtypeStruct((8, ROUTER_WIDTH), F32)],
        scratch_shapes=[pltpu.VMEM((8, ROUTER_WIDTH), F32)],
        compiler_params=pltpu.CompilerParams(dimension_semantics=("arbitrary",), vmem_limit_bytes=VMEM_LIMIT),
        name="merge",
    )(attn, rw, gates, x2, mod3, *consts)


def _token_rows(i):
    return pl.ds(i * ROW_CHUNKS, ROW_CHUNKS)


def _chunk(ref, c, n):
    return ref[pl.ds(c, n, stride=ROW_CHUNKS), :]


def _dispatch_kernel(dest_ref, hp_ref, xs_ref, sem):
    td = hp_ref.shape[0] // ROW_CHUNKS
    base = pl.program_id(0) * td

    def issue(r0, c):
        for j in range(DMA_UNROLL):
            r = r0 * DMA_UNROLL + j
            pltpu.make_async_copy(hp_ref.at[_token_rows(r)], xs_ref.at[_token_rows(dest_ref[base + r])],
                                  sem).start(priority=j % 2)
        return c

    lax.fori_loop(0, td // DMA_UNROLL, issue, 0)
    pltpu.make_async_copy(hp_ref, xs_ref.at[pl.ds(0, td * ROW_CHUNKS)], sem).wait()


def _dispatch(dest, hp, *, tokens_out, td):
    t = hp.shape[0] // ROW_CHUNKS
    return pl.pallas_call(
        _dispatch_kernel,
        grid_spec=pltpu.PrefetchScalarGridSpec(
            num_scalar_prefetch=1, grid=(t // td,),
            in_specs=[pl.BlockSpec((td * ROW_CHUNKS, LANES), lambda i, dest: (i, 0))],
            out_specs=pl.BlockSpec(memory_space=pl.ANY),
            scratch_shapes=[pltpu.SemaphoreType.DMA]),
        out_shape=jax.ShapeDtypeStruct((tokens_out * ROW_CHUNKS, LANES), F32),
        compiler_params=pltpu.CompilerParams(dimension_semantics=("arbitrary",), vmem_limit_bytes=VMEM_LIMIT),
        name="dispatch",
    )(dest, hp)


def _experts_kernel(tg_ref, tb_ref, tr_ref, xs_ref, wg_ref, wu_ref, wd_ref, ex_ref, ys_ref):
    del tg_ref, tb_ref
    nrows = tr_ref[pl.program_id(0)]

    @pl.when(nrows > 0)
    def _():
        tmx = xs_ref.shape[0] // ROW_CHUNKS
        nchunk = D_MODEL // LANES
        valid = lax.broadcasted_iota(jnp.int32, (tmx, 1), 0) < nrows
        x = jnp.concatenate([jnp.where(valid, _chunk(xs_ref, c, tmx), 0.0).astype(BF16) for c in range(nchunk)],
                            axis=1)
        cw_hi, cw_lo = _split(jnp.where(valid, _chunk(xs_ref, nchunk, tmx), 0.0))
        cwx = _dot(cw_hi, ex_ref[...]) + _dot(cw_lo, ex_ref[...])
        acc = jnp.zeros((tmx, D_MODEL), F32)
        for e in range(EXPERTS_PER_GROUP):
            a = _dot(x, wg_ref[e])
            act = a * _sigmoid(a) * _dot(x, wu_ref[e]) * cwx[:, e * EXPERT_FF:(e + 1) * EXPERT_FF]
            acc = acc + _dot(act.astype(BF16), wd_ref[e])
        for c in range(nchunk):
            ys_ref[pl.ds(c, tmx, stride=ROW_CHUNKS), :] = acc[:, c * LANES:(c + 1) * LANES]
        ys_ref[pl.ds(nchunk, tmx, stride=ROW_CHUNKS), :] = jnp.zeros((tmx, LANES), F32)


def _experts(tile_group, tile_blk, tile_rows, xs, w_gate, w_up, w_down, expand, *, tmx):
    nt = tile_group.shape[0]
    d = D_MODEL
    grp = lambda j, tg, tb, tr: (tg[j], 0, 0)
    tile = pl.BlockSpec((tmx * ROW_CHUNKS, LANES), lambda j, tg, tb, tr: (tb[j], 0))
    return pl.pallas_call(
        _experts_kernel,
        grid_spec=pltpu.PrefetchScalarGridSpec(
            num_scalar_prefetch=3, grid=(nt,),
            in_specs=[tile,
                      pl.BlockSpec((EXPERTS_PER_GROUP, d, EXPERT_FF), grp),
                      pl.BlockSpec((EXPERTS_PER_GROUP, d, EXPERT_FF), grp),
                      pl.BlockSpec((EXPERTS_PER_GROUP, EXPERT_FF, d), grp),
                      pl.BlockSpec(expand.shape, lambda j, tg, tb, tr: (0, 0))],
            out_specs=tile),
        out_shape=jax.ShapeDtypeStruct(xs.shape, F32),
        compiler_params=pltpu.CompilerParams(dimension_semantics=("arbitrary",), vmem_limit_bytes=VMEM_LIMIT),
        name="experts",
    )(tile_group, tile_blk, tile_rows, xs, w_gate, w_up, w_down, expand)


def _final_kernel(dest_ref, x1_ref, mod_ref, gpost_ref, ys_ref, o_ref, ybuf, sems):
    tmf, d = x1_ref.shape
    i = pl.program_id(0)
    slot = i % 2

    def gather(tile, into):
        base = tile * tmf

        def issue(r0, c):
            for j in range(DMA_UNROLL):
                r = r0 * DMA_UNROLL + j
                pltpu.make_async_copy(ys_ref.at[_token_rows(dest_ref[base + r])], ybuf.at[into, _token_rows(r)],
                                      sems.at[into]).start(priority=j % 2)
            return c

        lax.fori_loop(0, tmf // DMA_UNROLL, issue, 0)

    @pl.when(i == 0)
    def _():
        gather(i, slot)

    @pl.when(i + 1 < pl.num_programs(0))
    def _():
        gather(i + 1, 1 - slot)

    pltpu.make_async_copy(ys_ref.at[pl.ds(0, tmf * ROW_CHUNKS)], ybuf.at[slot], sems.at[slot]).wait()
    gate2 = mod_ref[0][5:6]
    y = jnp.concatenate([ybuf[slot, pl.ds(c, tmf, stride=ROW_CHUNKS), :] for c in range(d // LANES)], axis=1)
    o_ref[...] = x1_ref[...] + gate2 * _rms(y, gpost_ref[...])


def _final(dest, x1, mod3, g_post, ys, *, seq, tmf):
    t, d = x1.shape
    ns = seq // tmf
    return pl.pallas_call(
        _final_kernel,
        grid_spec=pltpu.PrefetchScalarGridSpec(
            num_scalar_prefetch=1, grid=(t // tmf,),
            in_specs=[pl.BlockSpec((tmf, d), lambda i, dest: (i, 0)),
                      pl.BlockSpec((1, 6, d), lambda i, dest: (i // ns, 0, 0)),
                      pl.BlockSpec((1, d), lambda i, dest: (0, 0)),
                      pl.BlockSpec(memory_space=pl.ANY)],
            out_specs=pl.BlockSpec((tmf, d), lambda i, dest: (i, 0)),
            scratch_shapes=[pltpu.VMEM((2, tmf * ROW_CHUNKS, LANES), F32), pltpu.SemaphoreType.DMA((2,))]),
        out_shape=jax.ShapeDtypeStruct((t, d), F32),
        compiler_params=pltpu.CompilerParams(dimension_semantics=("arbitrary",), vmem_limit_bytes=VMEM_LIMIT),
        name="final",
    )(dest, x1, mod3, g_post, ys)


def _tile_map(counts, *, capacity, tmx):
    ntile = (counts + tmx - 1) // tmx
    ends = jnp.cumsum(ntile)
    starts = ends - ntile
    j = jnp.arange(capacity // tmx + N_GROUPS, dtype=jnp.int32)
    g = jnp.minimum(jnp.sum((j[:, None] >= ends[None, :]).astype(jnp.int32), axis=1), N_GROUPS - 1)
    local = j - starts[g]
    valid = j < ends[-1]
    blk = g * (capacity // tmx) + local
    rows = jnp.clip(counts[g] - local * tmx, 0, tmx)
    last = ends[-1] - 1
    return (jnp.where(valid, g, g[last]).astype(jnp.int32), jnp.where(valid, blk, blk[last]).astype(jnp.int32),
            jnp.where(valid, rows, 0).astype(jnp.int32))


def _pick_tile(seq, want):
    return want if seq % want == 0 else seq


def _layer(x, c, positions, w_ada, b_ada, g_pre_mix, g_post_mix, g_pre_ffn, g_post_ffn, w_in, g_cq, w_uq,
           g_ckv, w_ukv, w_o_mla, mu_shift, w_decay_up, decay_bias, w_a_up, a_bias, w_g_up, k_k, k_a, r_k,
           lnx_w, lnx_b, w_o_rwkv, w_out, w_router_group, b_router_group, w_router_expert, b_router_expert,
           w_exp_gate, w_exp_up, w_exp_down):
    batch, seq, d = x.shape
    t = batch * seq
    x2 = x.reshape(t, d)
    row1 = lambda a: a.reshape(1, -1)

    mod3 = _ada(c, w_ada, b_ada).reshape(batch, 6, d)

    zeros = lambda n: jnp.zeros((d, n), F32)
    w_c = jnp.concatenate([w_in[:, :MLA_Q_RANK + MLA_KV_RANK], zeros(MLA_NOPE),
                           w_in[:, MLA_Q_RANK + MLA_KV_RANK:MLA_IN], zeros(HEAD_PAD - MLA_NOPE - MLA_ROPE)],
                          axis=1).astype(BF16)
    w_r = w_in[:, MLA_IN:MLA_IN + RWKV_IN].astype(BF16)
    w_g = w_in[:, MLA_IN + RWKV_IN:].astype(BF16)
    w_uq_p = jnp.pad(w_uq.reshape(MLA_Q_RANK, HEADS, MLA_NOPE + MLA_ROPE),
                     ((0, 0), (0, 0), (0, HEAD_PAD - MLA_NOPE - MLA_ROPE))).reshape(MLA_Q_RANK, QK_WIDTH).astype(BF16)
    w_ukv3 = w_ukv.reshape(MLA_KV_RANK, HEADS, MLA_NOPE + MLA_V)
    w_uk_p = jnp.pad(w_ukv3[..., :MLA_NOPE], ((0, 0), (0, 0), (0, HEAD_PAD - MLA_NOPE))
                     ).reshape(MLA_KV_RANK, QK_WIDTH).astype(BF16)
    w_uv_t = jnp.pad(w_ukv3[..., MLA_NOPE:], ((0, 0), (0, 0), (0, V_ROWS - MLA_V))
                     ).reshape(MLA_KV_RANK, HEADS * V_ROWS).T.astype(BF16)
    inv_freq = jnp.power(ROPE_THETA, -jnp.arange(0, MLA_ROPE, 2, dtype=F32) / MLA_ROPE)
    invf = jnp.concatenate([jnp.zeros((MLA_NOPE,), F32), inv_freq, inv_freq,
                            jnp.zeros((HEAD_PAD - MLA_NOPE - MLA_ROPE,), F32)]).reshape(1, LANES)

    tm = _pick_tile(seq, 512)
    q_all, k_all, vt_all, zs, gates = _inproj(
        x2, mod3, positions.reshape(t, 1), row1(g_pre_mix), w_c, w_r, w_g, row1(g_cq), row1(g_ckv),
        w_uq_p, w_uk_p, w_uv_t, row1(mu_shift), invf, batch=batch, seq=seq, tm=tm)

    attn = _attn(q_all, k_all, vt_all, batch=batch, seq=seq, tq=ATTN_BLOCK)

    tc = 2 * CHUNK * RWKV_CHUNKS_PER_STEP
    w_da = jnp.concatenate([
        jnp.concatenate([w_decay_up, jnp.zeros_like(w_decay_up)], axis=1),
        jnp.concatenate([jnp.zeros_like(w_a_up), w_a_up], axis=1)], axis=0).astype(BF16)
    hid = jnp.arange(2 * LANES) // RWKV_HEAD
    ones_bd = (hid[:, None] == hid[None, :]).astype(BF16)
    tid = jnp.arange(tc // 2)
    same_chunk = (tid[:, None] // CHUNK) == (tid[None, :] // CHUNK)
    tri = (same_chunk & (tid[None, :] <= tid[:, None])).astype(BF16)
    rw = _rwkv(zs, w_da, w_g_up.astype(BF16), row1(decay_bias), row1(a_bias), row1(k_k), row1(k_a), row1(r_k),
               row1(lnx_w), row1(lnx_b), ones_bd, tri, batch=batch, seq=seq, tc=tc)

    w_rt = jnp.concatenate([w_router_expert, w_router_group,
                            jnp.zeros((d, ROUTER_WIDTH - N_EXPERTS - N_GROUPS), F32)], axis=1)
    wr_hi = w_rt.astype(BF16)
    wr_lo = (w_rt - wr_hi.astype(F32)).astype(BF16)
    b_r = jnp.concatenate([b_router_expert, b_router_group,
                           jnp.zeros((ROUTER_WIDTH - N_EXPERTS - N_GROUPS,), F32)]).reshape(1, ROUTER_WIDTH)
    x1, hp, dest, cnt = _merge(attn, rw, gates, x2, mod3, w_o_mla.astype(BF16), w_o_rwkv.astype(BF16),
                               w_out.astype(BF16), row1(g_post_mix), row1(g_pre_ffn), wr_hi, wr_lo, b_r,
                               seq=seq, tm=_pick_tile(seq, 512))

    tmx = _pick_tile(seq, 512)
    dest = dest.reshape(t)
    xs = _dispatch(dest, hp, tokens_out=N_GROUPS * t, td=_pick_tile(seq, 1024))
    tile_group, tile_blk, tile_rows = _tile_map(cnt[0, :N_GROUPS].astype(jnp.int32), capacity=t, tmx=tmx)
    eid = jnp.arange(EXPERTS_PER_GROUP * EXPERT_FF) // EXPERT_FF
    expand = (jnp.arange(ROUTER_WIDTH)[:, None] == eid[None, :]).astype(BF16)
    ys = _experts(tile_group, tile_blk, tile_rows, xs, w_exp_gate.astype(BF16), w_exp_up.astype(BF16),
                  w_exp_down.astype(BF16), expand, tmx=tmx)
    out = _final(dest, x1, mod3, row1(g_post_ffn), ys, seq=seq, tmf=_pick_tile(seq, 512))
    return out.reshape(batch, seq, d)


def kernel(x, c, positions, w_ada, b_ada, g_pre_mix, g_post_mix, g_pre_ffn, g_post_ffn, w_in, g_cq, w_uq, g_ckv, w_ukv, w_o_mla, mu_shift, w_decay_up, decay_bias, w_a_up, a_bias, w_g_up, k_k, k_a, r_k, lnx_w, lnx_b, w_o_rwkv, w_out, w_router_group, b_router_group, w_router_expert, b_router_expert, w_exp_gate, w_exp_up, w_exp_down):
    depth = w_ada.shape[0]
    for l in range(depth):
        x = _layer(x, c, positions, w_ada[l], b_ada[l], g_pre_mix[l], g_post_mix[l], g_pre_ffn[l], g_post_ffn[l],
                   w_in[l], g_cq[l], w_uq[l], g_ckv[l], w_ukv[l], w_o_mla[l], mu_shift[l], w_decay_up[l],
                   decay_bias[l], w_a_up[l], a_bias[l], w_g_up[l], k_k[l], k_a[l], r_k[l], lnx_w[l], lnx_b[l],
                   w_o_rwkv[l], w_out[l], w_router_group[l], b_router_group[l], w_router_expert[l],
                   b_router_expert[l], w_exp_gate[l], w_exp_up[l], w_exp_down[l])
    return x
```

```python
import functools
import math

import jax
import jax.numpy as jnp
from jax import lax
from jax.experimental import pallas as pl
from jax.experimental.pallas import tpu as pltpu

F32 = jnp.float32
BF16 = jnp.bfloat16

D_MODEL = 1024
CHUNK = 64
HEADS = 8
MLA_NOPE = 64
MLA_ROPE = 32
MLA_V = 64
MLA_Q_RANK = 384
MLA_KV_RANK = 256
ROPE_THETA = 10000.0
RWKV_HEAD = 64
RWKV_WIDTH = HEADS * RWKV_HEAD
DECAY_RANK = 64
AAA_RANK = 64
GATE_RANK = 128
GN_EPS = 64e-5
N_GROUPS = 4
EXPERTS_PER_GROUP = 8
N_EXPERTS = N_GROUPS * EXPERTS_PER_GROUP
EXPERT_FF = 256
RMS_EPS = 1e-6
NEG_INF = -1e30
MLA_IN = MLA_Q_RANK + MLA_KV_RANK + MLA_ROPE
RWKV_IN = 3 * RWKV_WIDTH + DECAY_RANK + AAA_RANK + GATE_RANK

LANES = 128
DMA_UNROLL = 8
ROW_CHUNKS = 9
HEAD_PAD = LANES
QK_WIDTH = HEADS * HEAD_PAD
MLA_C_WIDTH = 768
ROUTER_WIDTH = LANES
VMEM_LIMIT = 56 * 1024 * 1024
ATTN_HEADS_PER_STEP = 8
ATTN_BLOCK = 256
ATTN_BLOCKS_PER_ITER = 2
V_ROWS = 80
RWKV_CHUNKS_PER_STEP = 4
MERGE_SUBTILES = 2


def _dot(a, b):
    return jnp.dot(a, b, preferred_element_type=F32)


def _dot_nt(a, b):
    return lax.dot_general(a, b, (((1,), (1,)), ((), ())), preferred_element_type=F32)


def _dot_tn(a, b):
    return lax.dot_general(a, b, (((0,), (0,)), ((), ())), preferred_element_type=F32)


def _split(x):
    hi = x.astype(BF16)
    lo = (x - hi.astype(F32)).astype(BF16)
    return hi, lo


def _dot3(x, w_hi, w_lo):
    x_hi, x_lo = _split(x)
    return _dot(x_hi, w_hi) + (_dot(x_hi, w_lo) + _dot(x_lo, w_hi))


def _sigmoid(x):
    return 1.0 / (1.0 + jnp.exp(-x))


def _rms(x, g):
    return x * lax.rsqrt(jnp.mean(x * x, axis=-1, keepdims=True) + RMS_EPS) * g


def _ada_kernel(c_ref, w_ref, b_ref, o_ref):
    c = c_ref[...]
    s = c * _sigmoid(c)
    w_hi, w_lo = _split(w_ref[...])
    o_ref[...] = _dot3(s, w_hi, w_lo) + b_ref[...]


def _ada(c, w_ada, b_ada):
    b, d = c.shape
    n = w_ada.shape[1]
    tn = 512
    return pl.pallas_call(
        _ada_kernel,
        grid=(n // tn,),
        in_specs=[pl.BlockSpec((b, d), lambda j: (0, 0)),
                  pl.BlockSpec((d, tn), lambda j: (0, j)),
                  pl.BlockSpec((1, tn), lambda j: (0, j))],
        out_specs=pl.BlockSpec((b, tn), lambda j: (0, j)),
        out_shape=jax.ShapeDtypeStruct((b, n), F32),
        name="ada",
    )(c, w_ada, b_ada.reshape(1, n))


def _inproj_kernel(x_ref, mod_ref, pos_ref, gpre_ref, wc_ref, wr_ref, wg_ref, gcq_ref, gckv_ref,
                   wuq_ref, wuk_ref, wuv_ref, mu_ref, invf_ref,
                   q_ref, k_ref, vt_ref, zs_ref, gate_ref, carry_ref):
    i = pl.program_id(1)
    tm = x_ref.shape[0]

    @pl.when(i == 0)
    def _():
        carry_ref[...] = jnp.zeros_like(carry_ref)

    mod = mod_ref[0]
    shift, scale = mod[0:1], mod[1:2]
    h = (_rms(x_ref[...], gpre_ref[...]) * (1.0 + scale) + shift).astype(BF16)

    zc = _dot(h, wc_ref[...])
    zg = _dot(h, wg_ref[...])
    nq = _rms(zc[:, :MLA_Q_RANK], gcq_ref[...]).astype(BF16)
    nkv = _rms(zc[:, MLA_Q_RANK:MLA_Q_RANK + MLA_KV_RANK], gckv_ref[...]).astype(BF16)
    kr = zc[:, MLA_Q_RANK + MLA_KV_RANK:]
    qf = _dot(nq, wuq_ref[...])
    kn = _dot(nkv, wuk_ref[...])
    vt = _dot_nt(wuv_ref[...], nkv)
    zr = _dot(h, wr_ref[...])

    pos_col = jnp.broadcast_to(pos_ref[...].astype(F32), (LANES, tm)).T
    ang = pos_col * invf_ref[...]
    lane = lax.broadcasted_iota(jnp.int32, (1, LANES), 1)
    first_half = lane < MLA_NOPE + MLA_ROPE // 2
    in_rope = (lane >= MLA_NOPE) & (lane < MLA_NOPE + MLA_ROPE)
    cos_t = jnp.where(in_rope, jnp.cos(ang), 1.0)
    sin_a = jnp.sin(ang)
    sin_t = jnp.where(in_rope, jnp.where(first_half, -sin_a, sin_a), 0.0)

    def rope(t):
        rot = jnp.where(first_half, pltpu.roll(t, LANES - MLA_ROPE // 2, 1), pltpu.roll(t, MLA_ROPE // 2, 1))
        return t * cos_t + rot * sin_t

    kr_rot = rope(kr)
    q_scale = math.log2(math.e) / math.sqrt(MLA_NOPE + MLA_ROPE)
    for hh in range(HEADS):
        sl = slice(hh * HEAD_PAD, (hh + 1) * HEAD_PAD)
        q_ref[:, sl] = (rope(qf[:, sl]) * q_scale).astype(BF16)
        k_ref[:, sl] = (kn[:, sl] + kr_rot).astype(BF16)
    vrow = lax.broadcasted_iota(jnp.int32, vt.shape, 0) % V_ROWS
    vt = jnp.where(vrow == MLA_V, 1.0, vt).astype(BF16)
    for j in range(vt_ref.shape[0]):
        vt_ref[j] = vt[:, j * ATTN_BLOCK:(j + 1) * ATTN_BLOCK]

    gate_ref[...] = _sigmoid(zg).astype(BF16)

    row = lax.broadcasted_iota(jnp.int32, (tm, 1), 0)
    prev = jnp.where(row == 0, carry_ref[0:1, :], pltpu.roll(zr, 1, 0))
    carry_ref[0:1, :] = zr[tm - 1:tm, :]
    zs_ref[...] = (zr + (prev - zr) * mu_ref[...]).astype(BF16)


def _const_spec(shape):
    nd = len(shape)
    return pl.BlockSpec(shape, lambda *_: (0,) * nd, pipeline_mode=pl.Buffered(1))


def _inproj(x2, mod3, pos2, g_pre, w_c, w_r, w_g, g_cq, g_ckv, w_uq, w_uk, w_uv, mu, invf, *, batch, seq, tm):
    t, d = x2.shape
    ns = seq // tm
    row = lambda b, i: (b * ns + i, 0)
    widths = [QK_WIDTH, QK_WIDTH, RWKV_IN, 2 * d]
    specs = [pl.BlockSpec((tm, w), row) for w in widths]
    shapes = [jax.ShapeDtypeStruct((t, w), BF16) for w in widths]
    vw = HEADS * V_ROWS
    nb = tm // ATTN_BLOCK
    specs.insert(2, pl.BlockSpec((None, nb, vw, ATTN_BLOCK), lambda b, i: (b, i, 0, 0)))
    shapes.insert(2, jax.ShapeDtypeStruct((batch, ns * nb, vw, ATTN_BLOCK), BF16))
    return pl.pallas_call(
        _inproj_kernel,
        grid=(batch, ns),
        in_specs=[pl.BlockSpec((tm, d), row),
                  pl.BlockSpec((1, 6, d), lambda b, i: (b, 0, 0)),
                  pl.BlockSpec((None, 1, tm), lambda b, i: (b, 0, i)),
                  _const_spec(g_pre.shape), _const_spec(w_c.shape), _const_spec(w_r.shape),
                  _const_spec(w_g.shape), _const_spec(g_cq.shape), _const_spec(g_ckv.shape),
                  _const_spec(w_uq.shape), _const_spec(w_uk.shape), _const_spec(w_uv.shape),
                  _const_spec(mu.shape), _const_spec(invf.shape)],
        out_specs=specs,
        out_shape=shapes,
        scratch_shapes=[pltpu.VMEM((8, RWKV_IN), F32)],
        compiler_params=pltpu.CompilerParams(dimension_semantics=("parallel", "arbitrary"),
                                             vmem_limit_bytes=VMEM_LIMIT),
        name="inproj",
    )(x2, mod3, pos2, g_pre, w_c, w_r, w_g, g_cq, g_ckv, w_uq, w_uk, w_uv, mu, invf)


def _attn_kernel(q_ref, k_ref, vt_ref, o_ref):
    i = pl.program_id(2)
    tq = q_ref.shape[0]
    kc = lax.broadcasted_iota(jnp.int32, (tq, tq), 0) // CHUNK
    qc = lax.broadcasted_iota(jnp.int32, (tq, tq), 1) // CHUNK
    diag_mask = kc <= qc
    heads = range(ATTN_HEADS_PER_STEP)
    hs = [slice(h * HEAD_PAD, (h + 1) * HEAD_PAD) for h in heads]
    vr = [slice(h * V_ROWS, (h + 1) * V_ROWS) for h in heads]
    q = [q_ref[:, s] for s in hs]

    def block(kb, carry, nblk, masked):
        kbs = [kb + n for n in range(nblk)]
        rows = [pl.ds(pl.multiple_of(b * tq, tq), tq) for b in kbs]
        s = [[_dot_nt(k_ref[r, hs[h]], q[h]) for r in rows] for h in heads]
        if masked:
            s = [[jnp.where(diag_mask, t, NEG_INF) for t in sh] for sh in s]
        m_new = []
        for h in heads:
            m = carry[h][0]
            for t in s[h]:
                m = jnp.maximum(m, jnp.max(t, axis=0, keepdims=True))
            m_new.append(m)
        alpha = [jnp.exp2(carry[h][0] - m_new[h]) for h in heads]
        p = [jnp.concatenate([jnp.exp2(t - m_new[h]).astype(BF16) for t in s[h]], axis=0) for h in heads]
        acc = [alpha[h] * carry[h][1]
               + _dot(jnp.concatenate([vt_ref[b, vr[h], :] for b in kbs], axis=1), p[h]) for h in heads]
        return tuple((m_new[h], acc[h]) for h in heads)

    init = tuple((jnp.full((1, tq), NEG_INF, F32), jnp.zeros((V_ROWS, tq), F32)) for _ in heads)
    nb = ATTN_BLOCKS_PER_ITER
    carry = lax.fori_loop(0, i // nb, lambda n, c: block(n * nb, c, nb, False), init)
    carry = lax.fori_loop((i // nb) * nb, i, lambda kb, c: block(kb, c, 1, False), carry)
    final = block(i, carry, 1, True)
    outs = [final[h][1][:MLA_V] / final[h][1][MLA_V:MLA_V + 1] for h in heads]
    for pr in range(ATTN_HEADS_PER_STEP // 2):
        pair_t = jnp.concatenate([outs[2 * pr], outs[2 * pr + 1]], axis=0)
        o_ref[:, pr * LANES:(pr + 1) * LANES] = pair_t.T.astype(BF16)


def _attn(q_all, k_all, vt_all, *, batch, seq, tq):
    t = q_all.shape[0]
    nq = seq // tq
    g = ATTN_HEADS_PER_STEP
    return pl.pallas_call(
        _attn_kernel,
        grid=(batch, HEADS // g, nq),
        in_specs=[pl.BlockSpec((tq, g * HEAD_PAD), lambda b, p, i: (b * nq + i, p)),
                  pl.BlockSpec((seq, g * HEAD_PAD), lambda b, p, i: (b, p)),
                  pl.BlockSpec((None, nq, g * V_ROWS, tq), lambda b, p, i: (b, 0, p, 0))],
        out_specs=pl.BlockSpec((tq, g * MLA_V), lambda b, p, i: (b * nq + i, p)),
        out_shape=jax.ShapeDtypeStruct((t, HEADS * MLA_V), BF16),
        compiler_params=pltpu.CompilerParams(dimension_semantics=("parallel", "parallel", "arbitrary"),
                                             vmem_limit_bytes=VMEM_LIMIT),
        name="attn",
    )(q_all, k_all, vt_all)


def _rwkv_kernel(zs_ref, wda_ref, wgu_ref, dbias_ref, abias_ref, kk_ref, ka_ref, rk_ref, lnw_ref, lnb_ref,
                 ones_ref, tri_ref,
                 o_ref,
                 st_ref, a_s, r_s, b_s, k_s, bg_s, kg_s, v_s, gc_s, bonus_s, g_s, y_s):
    w = RWKV_WIDTH
    nsub = RWKV_CHUNKS_PER_STEP
    th = nsub * CHUNK
    n2 = 2 * CHUNK
    staged = (a_s, r_s, b_s, k_s, bg_s, kg_s, v_s, gc_s, bonus_s, g_s)

    @pl.when(pl.program_id(1) == 0)
    def _():
        st_ref[...] = jnp.zeros_like(st_ref)

    ones_bd = ones_ref[...]

    def head_sum(t):
        tb = t.astype(BF16)
        hw = ones_bd.shape[0]
        return jnp.concatenate([_dot(tb[:, n * hw:(n + 1) * hw], ones_bd) for n in range(w // hw)], axis=1)

    ri = lax.broadcasted_iota(jnp.int32, (4 * CHUNK, n2), 0)
    ci = lax.broadcasted_iota(jnp.int32, (4 * CHUNK, n2), 1) % CHUNK
    tri_mask = (ri % CHUNK + ri // n2) > ci
    lane2 = lax.broadcasted_iota(jnp.int32, (CHUNK, n2), 1)
    h1 = lane2 < RWKV_HEAD
    bi = lax.broadcasted_iota(jnp.int32, (n2, n2), 0) // RWKV_HEAD
    bj = lax.broadcasted_iota(jnp.int32, (n2, n2), 1) // RWKV_HEAD
    bd_mask = bi == bj
    zero_bf = jnp.zeros((CHUNK, n2), BF16)
    zero_f = jnp.zeros((CHUNK, n2), F32)
    lane = lax.broadcasted_iota(jnp.int32, (1, LANES), 1)

    def prepare(h):
        hr = slice(h * th, (h + 1) * th)
        zs = zs_ref[hr, :].astype(F32)
        r, k, v = zs[:, 0:w], zs[:, w:2 * w], zs[:, 2 * w:3 * w]
        da = zs[:, 3 * w:3 * w + LANES]
        gd = zs[:, 3 * w + LANES:]
        lora_in = jnp.where(lane < DECAY_RANK, jnp.tanh(da), da).astype(BF16)
        pre = _dot(lora_in, wda_ref[...])
        g_new = _dot(_sigmoid(gd).astype(BF16), wgu_ref[...])
        yield
        u = -(dbias_ref[...] + pre[:, :w])
        softplus = jnp.maximum(u, 0.0) + jnp.log(1.0 + jnp.exp(-jnp.abs(u)))
        logw = -jnp.exp(-softplus - 0.5)
        eta = _sigmoid(abias_ref[...] + pre[:, w:])
        kk = k * kk_ref[...]
        kk_ss = head_sum(kk * kk)
        lw_hi, lw_lo = _split(logw)
        cum = _dot(tri_ref[...], lw_hi) + _dot(tri_ref[...], lw_lo)
        yield
        kk = kk * jnp.minimum(lax.rsqrt(kk_ss), 1e12)
        kp = k * (1.0 + (eta - 1.0) * ka_ref[...])
        bonus_new = head_sum(r * kp * rk_ref[...]) * v
        yield
        cum_end = jnp.concatenate([jnp.broadcast_to(cum[(c + 1) * CHUNK - 1:(c + 1) * CHUNK, :], (CHUNK, w))
                                   for c in range(nsub)], axis=0)
        b_in = kk * eta
        g_inv = jnp.exp(-cum)
        g_rem = jnp.exp(cum_end - cum)
        staged_new = ((-kk * jnp.exp(cum - logw)).astype(BF16), (r * jnp.exp(cum)).astype(BF16),
                      (b_in * g_inv).astype(BF16), (kp * g_inv).astype(BF16),
                      (b_in * g_rem).astype(BF16), (kp * g_rem).astype(BF16), v.astype(BF16),
                      jnp.exp(cum_end), bonus_new, g_new)
        for ref, val in zip(staged, staged_new):
            ref[hr, :] = val
        yield

    def solve(h):
        blocks = [(sub, p) for sub in range(nsub) for p in range(HEADS // 2)]
        rows = [slice(h * th + sub * CHUNK, h * th + (sub + 1) * CHUNK) for sub in range(nsub)]
        lns = [slice(p * n2, (p + 1) * n2) for p in range(HEADS // 2)]
        a_t = [a_s[rows[sub], lns[p]] for sub, p in blocks]
        r_t = [r_s[rows[sub], lns[p]] for sub, p in blocks]
        v_t = [v_s[rows[sub], lns[p]] for sub, p in blocks]
        ml = []
        for n, (sub, p) in enumerate(blocks):
            lhs = jnp.concatenate([jnp.where(h1, a_t[n], zero_bf), jnp.where(h1, zero_bf, a_t[n]),
                                   jnp.where(h1, r_t[n], zero_bf), jnp.where(h1, zero_bf, r_t[n])], axis=0)
            rhs = jnp.concatenate([b_s[rows[sub], lns[p]], k_s[rows[sub], lns[p]]], axis=0)
            ml.append(jnp.where(tri_mask, _dot_nt(lhs, rhs), 0.0))
        ml_b = [m.astype(BF16) for m in ml]
        yield
        units = [(n, hh) for n in range(len(blocks)) for hh in range(2)]
        w_t = [_dot(ml_b[n][hh * CHUNK:(hh + 1) * CHUNK, :], jnp.concatenate([zero_bf, v_t[n]], axis=0))
               for n, hh in units]
        a_f = [t.astype(F32) for t in a_t]
        a_sw = [pltpu.roll(t, RWKV_HEAD, 1) for t in a_f]
        x = [jnp.where(h1, a_f[n], pltpu.roll(w_t[2 * n], RWKV_HEAD, 1)) if hh == 0
             else jnp.where(h1, a_sw[n], w_t[2 * n + 1]) for n, hh in units]
        lsq = [jnp.where(h1, ml[n][hh * CHUNK:(hh + 1) * CHUNK, :], 0.0) for n, hh in units]
        yield
        for _ in range(6):
            out = [_dot(lsq[u][:, :CHUNK].astype(BF16),
                        jnp.concatenate([x[u], lsq[u]], axis=1).astype(BF16)) for u in range(len(units))]
            x = [x[u] + out[u][:, :n2] for u in range(len(units))]
            lsq = [t[:, n2:] for t in out]
            yield
        z = []
        for n in range(len(blocks)):
            x0, x1 = x[2 * n], x[2 * n + 1]
            a_hat = jnp.where(h1, x0, pltpu.roll(x1, RWKV_HEAD, 1))
            u0 = jnp.where(h1, pltpu.roll(x0, RWKV_HEAD, 1), x1)
            z.append(jnp.concatenate([jnp.concatenate([a_hat, u0], axis=1),
                                      jnp.concatenate([zero_f, v_t[n].astype(F32)], axis=1)],
                                     axis=0).astype(BF16))
        o0 = [_dot(ml_b[n][2 * CHUNK:3 * CHUNK, :], z[n]) for n in range(len(blocks))]
        o1 = [_dot(ml_b[n][3 * CHUNK:4 * CHUNK, :], z[n]) for n in range(len(blocks))]
        pm = [_dot_tn(z[n], jnp.concatenate([bg_s[rows[sub], lns[p]], kg_s[rows[sub], lns[p]]], axis=0))
              for n, (sub, p) in enumerate(blocks)]
        r_hat = [(r_t[n].astype(F32) + jnp.where(h1, o0[n][:, :n2], o1[n][:, :n2])).astype(BF16)
                 for n in range(len(blocks))]
        yield
        state = [st_ref[p] for p in range(HEADS // 2)]
        for n, (sub, p) in enumerate(blocks):
            y0 = jnp.where(h1, o0[n][:, n2:], o1[n][:, n2:])
            g_t = jnp.where(bd_mask, pm[n][:n2], 0.0).astype(BF16)
            h_t = jnp.where(bd_mask, pm[n][n2:], 0.0)
            s_b = state[p].astype(BF16)
            y_s[rows[sub], lns[p]] = _dot_nt(r_hat[n], s_b) + y0
            gc = gc_s[rows[sub].start:rows[sub].start + 1, lns[p]]
            state[p] = gc * state[p] + _dot(s_b, g_t) + h_t
        for p in range(HEADS // 2):
            st_ref[p] = state[p]
        yield

    def finish(h):
        hr = slice(h * th, (h + 1) * th)
        y = y_s[hr, :]
        inv_n = 1.0 / RWKV_HEAD
        dev = y - head_sum(y) * inv_n
        yield
        var = head_sum(dev * dev) * inv_n
        yield
        out = (dev * lax.rsqrt(var + GN_EPS) * lnw_ref[...] + lnb_ref[...] + bonus_s[hr, :]) * g_s[hr, :]
        o_ref[hr, :] = out.astype(BF16)
        yield

    def emit(main, side=None, every=1):
        n = 0
        for _ in main:
            n += 1
            if side is not None and n % every == 0:
                next(side, None)
        if side is not None:
            for _ in side:
                pass

    emit(prepare(0))
    emit(solve(0), prepare(1), every=2)
    emit(solve(1), finish(0), every=3)
    emit(finish(1))


def _rwkv(zs, w_da, w_gu, dbias, abias, k_k, k_a, r_k, lnw, lnb, ones_bd, tri, *, batch, seq, tc):
    t = zs.shape[0]
    nc = seq // tc
    w = RWKV_WIDTH
    consts = [w_da, w_gu, dbias, abias, k_k, k_a, r_k, lnw, lnb, ones_bd, tri]
    return pl.pallas_call(
        _rwkv_kernel,
        grid=(batch, nc),
        in_specs=[pl.BlockSpec((tc, RWKV_IN), lambda b, i: (b * nc + i, 0))] + [_const_spec(a.shape) for a in consts],
        out_specs=pl.BlockSpec((tc, w), lambda b, i: (b * nc + i, 0)),
        out_shape=jax.ShapeDtypeStruct((t, w), BF16),
        scratch_shapes=[pltpu.VMEM((HEADS // 2, 2 * RWKV_HEAD, 2 * RWKV_HEAD), F32)]
        + [pltpu.VMEM((tc, w), BF16)] * 7
        + [pltpu.VMEM((tc, w), F32)] * 4,
        compiler_params=pltpu.CompilerParams(dimension_semantics=("parallel", "arbitrary"),
                                             vmem_limit_bytes=VMEM_LIMIT),
        name="rwkv",
    )(zs, *consts)


def _route(logits):
    lane = lax.broadcasted_iota(jnp.int32, logits.shape, 1)
    lane_f = lane.astype(F32)
    big = float(ROUTER_WIDTH)
    is_group = (lane >= N_EXPERTS) & (lane < N_EXPERTS + N_GROUPS)
    gl = jnp.where(is_group, logits, NEG_INF)
    g_max = jnp.max(gl, axis=-1, keepdims=True)
    g_sel = jnp.min(jnp.where(gl == g_max, lane_f, big), axis=-1, keepdims=True) - float(N_EXPERTS)
    p_sel = 1.0 / jnp.sum(jnp.where(is_group, jnp.exp(gl - g_max), 0.0), axis=-1, keepdims=True)
    in_group = (lane < N_EXPERTS) & ((lane // EXPERTS_PER_GROUP).astype(F32) == g_sel)
    el = jnp.where(in_group, logits, NEG_INF)
    t1 = jnp.max(el, axis=-1, keepdims=True)
    i1 = jnp.min(jnp.where(el == t1, lane_f, big), axis=-1, keepdims=True)
    el2 = jnp.where(lane_f == i1, NEG_INF, el)
    t2 = jnp.max(el2, axis=-1, keepdims=True)
    i2 = jnp.min(jnp.where(el2 == t2, lane_f, big), axis=-1, keepdims=True)
    e21 = jnp.exp(t2 - t1)
    w1 = p_sel / (1.0 + e21)
    w2 = w1 * e21
    first = g_sel * float(EXPERTS_PER_GROUP)
    cw = jnp.where(lane_f == i1 - first, w1, 0.0) + jnp.where(lane_f == i2 - first, w2, 0.0)
    return g_sel, cw


def _merge_kernel(attn_ref, rw_ref, gate_ref, x_ref, mod_ref, womla_ref, worw_ref, wout_ref,
                  gpost_ref, gffn_ref, wrh_ref, wrl_ref, br_ref, stril_ref,
                  x1_ref, hp_ref, dest_ref, cnt_ref, carry_ref, *, group_capacity):
    i = pl.program_id(0)
    tm, d = x_ref.shape

    @pl.when(i == 0)
    def _():
        carry_ref[...] = jnp.zeros_like(carry_ref)

    mod = mod_ref[0]
    gate1, shift2, scale2 = mod[2:3], mod[3:4], mod[4:5]

    ts = tm // MERGE_SUBTILES
    subs = [slice(n * ts, (n + 1) * ts) for n in range(MERGE_SUBTILES)]
    o_mla = [_dot(attn_ref[r, :], womla_ref[...]) for r in subs]
    o_rw = [_dot(rw_ref[r, :], worw_ref[...]) for r in subs]
    o = [(gate_ref[r, :d].astype(F32) * o_mla[n] + gate_ref[r, d:].astype(F32) * o_rw[n]).astype(BF16)
         for n, r in enumerate(subs)]
    y = [_dot(t, wout_ref[...]) for t in o]
    x1 = [x_ref[r, :] + gate1 * _rms(y[n], gpost_ref[...]) for n, r in enumerate(subs)]
    h2 = [_rms(t, gffn_ref[...]) * (1.0 + scale2) + shift2 for t in x1]
    logits = [_dot3(t, wrh_ref[...], wrl_ref[...]) + br_ref[...] for t in h2]
    for n, r in enumerate(subs):
        x1_ref[r, :] = x1[n]
        for c in range(d // LANES):
            hp_ref[pl.ds(n * ts * ROW_CHUNKS + c, ts, stride=ROW_CHUNKS), :] = h2[n][:, c * LANES:(c + 1) * LANES]

    lane_f = lax.broadcasted_iota(jnp.int32, (ts, ROUTER_WIDTH), 1).astype(F32)
    routes = [_route(t) for t in logits]
    g_sels = [g for g, _ in routes]
    for n, r in enumerate(subs):
        hp_ref[pl.ds(n * ts * ROW_CHUNKS + d // LANES, ts, stride=ROW_CHUNKS), :] = routes[n][1]
    onehots = [jnp.where(lane_f == g, 1.0, 0.0) for g in g_sels]

    onehot = jnp.concatenate(onehots, axis=0)
    earlier = _dot(stril_ref[...], onehot.astype(BF16))
    carry = carry_ref[0:1, :]
    for n, r in enumerate(subs):
        rank = jnp.sum(jnp.where(lane_f == g_sels[n], carry + earlier[r, :], 0.0), axis=-1, keepdims=True)
        dest_col = jnp.broadcast_to(g_sels[n] * float(group_capacity) + rank, (ts, LANES))
        dest_ref[:, r] = dest_col.T[0:1, :].astype(jnp.int32)
    total = jnp.broadcast_to(carry + jnp.sum(onehot, axis=0, keepdims=True), carry_ref.shape)
    carry_ref[...] = total
    cnt_ref[...] = total


def _merge(attn, rw, gates, x2, mod3, w_o_mla, w_o_rwkv, w_out, g_post, g_ffn, wr_hi, wr_lo, b_r, *, seq, tm):
    t, d = x2.shape
    ns = seq // tm
    row = lambda i: (i, 0)
    tid = jnp.arange(tm)
    stril = (tid[None, :] < tid[:, None]).astype(BF16)
    consts = [w_o_mla, w_o_rwkv, w_out, g_post, g_ffn, wr_hi, wr_lo, b_r, stril]
    return pl.pallas_call(
        functools.partial(_merge_kernel, group_capacity=t),
        grid=(t // tm,),
        in_specs=[pl.BlockSpec((tm, attn.shape[1]), row), pl.BlockSpec((tm, rw.shape[1]), row),
                  pl.BlockSpec((tm, 2 * d), row), pl.BlockSpec((tm, d), row),
                  pl.BlockSpec((1, 6, d), lambda i: (i // ns, 0, 0))] + [_const_spec(a.shape) for a in consts],
        out_specs=[pl.BlockSpec((tm, d), row), pl.BlockSpec((tm * ROW_CHUNKS, LANES), row),
                   pl.BlockSpec((None, 1, tm), lambda i: (i, 0, 0)), pl.BlockSpec((8, ROUTER_WIDTH), lambda i: (0, 0))],
        out_shape=[jax.ShapeDtypeStruct((t, d), F32), jax.ShapeDtypeStruct((t * ROW_CHUNKS, LANES), F32),
                   jax.ShapeDtypeStruct((t // tm, 1, tm), jnp.int32), jax.ShapeDtypeStruct((8, ROUTER_WIDTH), F32)],
        scratch_shapes=[pltpu.VMEM((8, ROUTER_WIDTH), F32)],
        compiler_params=pltpu.CompilerParams(dimension_semantics=("arbitrary",), vmem_limit_bytes=VMEM_LIMIT),
        name="merge",
    )(attn, rw, gates, x2, mod3, *consts)


def _token_rows(i):
    return pl.ds(i * ROW_CHUNKS, ROW_CHUNKS)


def _chunk(ref, c, n):
    return ref[pl.ds(c, n, stride=ROW_CHUNKS), :]


def _dispatch_kernel(dest_ref, hp_ref, xs_ref, sem):
    td = hp_ref.shape[0] // ROW_CHUNKS
    base = pl.program_id(0) * td

    def issue(r0, c):
        for j in range(DMA_UNROLL):
            r = r0 * DMA_UNROLL + j
            pltpu.make_async_copy(hp_ref.at[_token_rows(r)], xs_ref.at[_token_rows(dest_ref[base + r])],
                                  sem).start(priority=j % 2)
        return c

    lax.fori_loop(0, td // DMA_UNROLL, issue, 0)
    pltpu.make_async_copy(hp_ref, xs_ref.at[pl.ds(0, td * ROW_CHUNKS)], sem).wait()


def _dispatch(dest, hp, *, tokens_out, td):
    t = hp.shape[0] // ROW_CHUNKS
    return pl.pallas_call(
        _dispatch_kernel,
        grid_spec=pltpu.PrefetchScalarGridSpec(
            num_scalar_prefetch=1, grid=(t // td,),
            in_specs=[pl.BlockSpec((td * ROW_CHUNKS, LANES), lambda i, dest: (i, 0))],
            out_specs=pl.BlockSpec(memory_space=pl.ANY),
            scratch_shapes=[pltpu.SemaphoreType.DMA]),
        out_shape=jax.ShapeDtypeStruct((tokens_out * ROW_CHUNKS, LANES), F32),
        compiler_params=pltpu.CompilerParams(dimension_semantics=("arbitrary",), vmem_limit_bytes=VMEM_LIMIT),
        name="dispatch",
    )(dest, hp)


def _experts_kernel(tg_ref, tb_ref, tr_ref, xs_ref, wg_ref, wu_ref, wd_ref, ex_ref, ys_ref):
    del tg_ref, tb_ref
    nrows = tr_ref[pl.program_id(0)]

    @pl.when(nrows > 0)
    def _():
        tmx = xs_ref.shape[0] // ROW_CHUNKS
        nchunk = D_MODEL // LANES
        valid = lax.broadcasted_iota(jnp.int32, (tmx, 1), 0) < nrows
        x = jnp.concatenate([jnp.where(valid, _chunk(xs_ref, c, tmx), 0.0).astype(BF16) for c in range(nchunk)],
                            axis=1)
        cw_hi, cw_lo = _split(jnp.where(valid, _chunk(xs_ref, nchunk, tmx), 0.0))
        cwx = _dot(cw_hi, ex_ref[...]) + _dot(cw_lo, ex_ref[...])
        acc = jnp.zeros((tmx, D_MODEL), F32)
        for e in range(EXPERTS_PER_GROUP):
            a = _dot(x, wg_ref[e])
            act = a * _sigmoid(a) * _dot(x, wu_ref[e]) * cwx[:, e * EXPERT_FF:(e + 1) * EXPERT_FF]
            acc = acc + _dot(act.astype(BF16), wd_ref[e])
        for c in range(nchunk):
            ys_ref[pl.ds(c, tmx, stride=ROW_CHUNKS), :] = acc[:, c * LANES:(c + 1) * LANES]
        ys_ref[pl.ds(nchunk, tmx, stride=ROW_CHUNKS), :] = jnp.zeros((tmx, LANES), F32)


def _experts(tile_group, tile_blk, tile_rows, xs, w_gate, w_up, w_down, expand, *, tmx):
    nt = tile_group.shape[0]
    d = D_MODEL
    grp = lambda j, tg, tb, tr: (tg[j], 0, 0)
    tile = pl.BlockSpec((tmx * ROW_CHUNKS, LANES), lambda j, tg, tb, tr: (tb[j], 0))
    return pl.pallas_call(
        _experts_kernel,
        grid_spec=pltpu.PrefetchScalarGridSpec(
            num_scalar_prefetch=3, grid=(nt,),
            in_specs=[tile,
                      pl.BlockSpec((EXPERTS_PER_GROUP, d, EXPERT_FF), grp),
                      pl.BlockSpec((EXPERTS_PER_GROUP, d, EXPERT_FF), grp),
                      pl.BlockSpec((EXPERTS_PER_GROUP, EXPERT_FF, d), grp),
                      pl.BlockSpec(expand.shape, lambda j, tg, tb, tr: (0, 0))],
            out_specs=tile),
        out_shape=jax.ShapeDtypeStruct(xs.shape, F32),
        compiler_params=pltpu.CompilerParams(dimension_semantics=("arbitrary",), vmem_limit_bytes=VMEM_LIMIT),
        name="experts",
    )(tile_group, tile_blk, tile_rows, xs, w_gate, w_up, w_down, expand)


def _final_kernel(dest_ref, x1_ref, mod_ref, gpost_ref, ys_ref, o_ref, ybuf, sems):
    tmf, d = x1_ref.shape
    i = pl.program_id(0)
    slot = i % 2

    def gather(tile, into):
        base = tile * tmf

        def issue(r0, c):
            for j in range(DMA_UNROLL):
                r = r0 * DMA_UNROLL + j
                pltpu.make_async_copy(ys_ref.at[_token_rows(dest_ref[base + r])], ybuf.at[into, _token_rows(r)],
                                      sems.at[into]).start(priority=j % 2)
            return c

        lax.fori_loop(0, tmf // DMA_UNROLL, issue, 0)

    @pl.when(i == 0)
    def _():
        gather(i, slot)

    @pl.when(i + 1 < pl.num_programs(0))
    def _():
        gather(i + 1, 1 - slot)

    pltpu.make_async_copy(ys_ref.at[pl.ds(0, tmf * ROW_CHUNKS)], ybuf.at[slot], sems.at[slot]).wait()
    gate2 = mod_ref[0][5:6]
    y = jnp.concatenate([ybuf[slot, pl.ds(c, tmf, stride=ROW_CHUNKS), :] for c in range(d // LANES)], axis=1)
    o_ref[...] = x1_ref[...] + gate2 * _rms(y, gpost_ref[...])


def _final(dest, x1, mod3, g_post, ys, *, seq, tmf):
    t, d = x1.shape
    ns = seq // tmf
    return pl.pallas_call(
        _final_kernel,
        grid_spec=pltpu.PrefetchScalarGridSpec(
            num_scalar_prefetch=1, grid=(t // tmf,),
            in_specs=[pl.BlockSpec((tmf, d), lambda i, dest: (i, 0)),
                      pl.BlockSpec((1, 6, d), lambda i, dest: (i // ns, 0, 0)),
                      pl.BlockSpec((1, d), lambda i, dest: (0, 0)),
                      pl.BlockSpec(memory_space=pl.ANY)],
            out_specs=pl.BlockSpec((tmf, d), lambda i, dest: (i, 0)),
            scratch_shapes=[pltpu.VMEM((2, tmf * ROW_CHUNKS, LANES), F32), pltpu.SemaphoreType.DMA((2,))]),
        out_shape=jax.ShapeDtypeStruct((t, d), F32),
        compiler_params=pltpu.CompilerParams(dimension_semantics=("arbitrary",), vmem_limit_bytes=VMEM_LIMIT),
        name="final",
    )(dest, x1, mod3, g_post, ys)


def _tile_map(counts, *, capacity, tmx):
    ntile = (counts + tmx - 1) // tmx
    ends = jnp.cumsum(ntile)
    starts = ends - ntile
    j = jnp.arange(capacity // tmx + N_GROUPS, dtype=jnp.int32)
    g = jnp.minimum(jnp.sum((j[:, None] >= ends[None, :]).astype(jnp.int32), axis=1), N_GROUPS - 1)
    local = j - starts[g]
    valid = j < ends[-1]
    blk = g * (capacity // tmx) + local
    rows = jnp.clip(counts[g] - local * tmx, 0, tmx)
    last = ends[-1] - 1
    return (jnp.where(valid, g, g[last]).astype(jnp.int32), jnp.where(valid, blk, blk[last]).astype(jnp.int32),
            jnp.where(valid, rows, 0).astype(jnp.int32))


def _pick_tile(seq, want):
    return want if seq % want == 0 else seq


def _layer(x, c, positions, w_ada, b_ada, g_pre_mix, g_post_mix, g_pre_ffn, g_post_ffn, w_in, g_cq, w_uq,
           g_ckv, w_ukv, w_o_mla, mu_shift, w_decay_up, decay_bias, w_a_up, a_bias, w_g_up, k_k, k_a, r_k,
           lnx_w, lnx_b, w_o_rwkv, w_out, w_router_group, b_router_group, w_router_expert, b_router_expert,
           w_exp_gate, w_exp_up, w_exp_down):
    batch, seq, d = x.shape
    t = batch * seq
    x2 = x.reshape(t, d)
    row1 = lambda a: a.reshape(1, -1)

    mod3 = _ada(c, w_ada, b_ada).reshape(batch, 6, d)

    zeros = lambda n: jnp.zeros((d, n), F32)
    w_c = jnp.concatenate([w_in[:, :MLA_Q_RANK + MLA_KV_RANK], zeros(MLA_NOPE),
                           w_in[:, MLA_Q_RANK + MLA_KV_RANK:MLA_IN], zeros(HEAD_PAD - MLA_NOPE - MLA_ROPE)],
                          axis=1).astype(BF16)
    w_r = w_in[:, MLA_IN:MLA_IN + RWKV_IN].astype(BF16)
    w_g = w_in[:, MLA_IN + RWKV_IN:].astype(BF16)
    w_uq_p = jnp.pad(w_uq.reshape(MLA_Q_RANK, HEADS, MLA_NOPE + MLA_ROPE),
                     ((0, 0), (0, 0), (0, HEAD_PAD - MLA_NOPE - MLA_ROPE))).reshape(MLA_Q_RANK, QK_WIDTH).astype(BF16)
    w_ukv3 = w_ukv.reshape(MLA_KV_RANK, HEADS, MLA_NOPE + MLA_V)
    w_uk_p = jnp.pad(w_ukv3[..., :MLA_NOPE], ((0, 0), (0, 0), (0, HEAD_PAD - MLA_NOPE))
                     ).reshape(MLA_KV_RANK, QK_WIDTH).astype(BF16)
    w_uv_t = jnp.pad(w_ukv3[..., MLA_NOPE:], ((0, 0), (0, 0), (0, V_ROWS - MLA_V))
                     ).reshape(MLA_KV_RANK, HEADS * V_ROWS).T.astype(BF16)
    inv_freq = jnp.power(ROPE_THETA, -jnp.arange(0, MLA_ROPE, 2, dtype=F32) / MLA_ROPE)
    invf = jnp.concatenate([jnp.zeros((MLA_NOPE,), F32), inv_freq, inv_freq,
                            jnp.zeros((HEAD_PAD - MLA_NOPE - MLA_ROPE,), F32)]).reshape(1, LANES)

    tm = _pick_tile(seq, 512)
    q_all, k_all, vt_all, zs, gates = _inproj(
        x2, mod3, positions.reshape(batch, 1, seq), row1(g_pre_mix), w_c, w_r, w_g, row1(g_cq), row1(g_ckv),
        w_uq_p, w_uk_p, w_uv_t, row1(mu_shift), invf, batch=batch, seq=seq, tm=tm)

    attn = _attn(q_all, k_all, vt_all, batch=batch, seq=seq, tq=ATTN_BLOCK)

    tc = 2 * CHUNK * RWKV_CHUNKS_PER_STEP
    w_da = jnp.concatenate([
        jnp.concatenate([w_decay_up, jnp.zeros_like(w_decay_up)], axis=1),
        jnp.concatenate([jnp.zeros_like(w_a_up), w_a_up], axis=1)], axis=0).astype(BF16)
    hid = jnp.arange(2 * LANES) // RWKV_HEAD
    ones_bd = (hid[:, None] == hid[None, :]).astype(BF16)
    tid = jnp.arange(tc // 2)
    same_chunk = (tid[:, None] // CHUNK) == (tid[None, :] // CHUNK)
    tri = (same_chunk & (tid[None, :] <= tid[:, None])).astype(BF16)
    rw = _rwkv(zs, w_da, w_g_up.astype(BF16), row1(decay_bias), row1(a_bias), row1(k_k), row1(k_a), row1(r_k),
               row1(lnx_w), row1(lnx_b), ones_bd, tri, batch=batch, seq=seq, tc=tc)

    w_rt = jnp.concatenate([w_router_expert, w_router_group,
                            jnp.zeros((d, ROUTER_WIDTH - N_EXPERTS - N_GROUPS), F32)], axis=1)
    wr_hi = w_rt.astype(BF16)
    wr_lo = (w_rt - wr_hi.astype(F32)).astype(BF16)
    b_r = jnp.concatenate([b_router_expert, b_router_group,
                           jnp.zeros((ROUTER_WIDTH - N_EXPERTS - N_GROUPS,), F32)]).reshape(1, ROUTER_WIDTH)
    x1, hp, dest, cnt = _merge(attn, rw, gates, x2, mod3, w_o_mla.astype(BF16), w_o_rwkv.astype(BF16),
                               w_out.astype(BF16), row1(g_post_mix), row1(g_pre_ffn), wr_hi, wr_lo, b_r,
                               seq=seq, tm=_pick_tile(seq, 512))

    tmx = _pick_tile(seq, 512)
    dest = dest.reshape(t)
    xs = _dispatch(dest, hp, tokens_out=N_GROUPS * t, td=_pick_tile(seq, 1024))
    tile_group, tile_blk, tile_rows = _tile_map(cnt[0, :N_GROUPS].astype(jnp.int32), capacity=t, tmx=tmx)
    eid = jnp.arange(EXPERTS_PER_GROUP * EXPERT_FF) // EXPERT_FF
    expand = (jnp.arange(ROUTER_WIDTH)[:, None] == eid[None, :]).astype(BF16)
    ys = _experts(tile_group, tile_blk, tile_rows, xs, w_exp_gate.astype(BF16), w_exp_up.astype(BF16),
                  w_exp_down.astype(BF16), expand, tmx=tmx)
    out = _final(dest, x1, mod3, row1(g_post_ffn), ys, seq=seq, tmf=_pick_tile(seq, 512))
    return out.reshape(batch, seq, d)


def kernel(x, c, positions, w_ada, b_ada, g_pre_mix, g_post_mix, g_pre_ffn, g_post_ffn, w_in, g_cq, w_uq, g_ckv, w_ukv, w_o_mla, mu_shift, w_decay_up, decay_bias, w_a_up, a_bias, w_g_up, k_k, k_a, r_k, lnx_w, lnx_b, w_o_rwkv, w_out, w_router_group, b_router_group, w_router_expert, b_router_expert, w_exp_gate, w_exp_up, w_exp_down):
    depth = w_ada.shape[0]
    for l in range(depth):
        x = _layer(x, c, positions, w_ada[l], b_ada[l], g_pre_mix[l], g_post_mix[l], g_pre_ffn[l], g_post_ffn[l],
                   w_in[l], g_cq[l], w_uq[l], g_ckv[l], w_ukv[l], w_o_mla[l], mu_shift[l], w_decay_up[l],
                   decay_bias[l], w_a_up[l], a_bias[l], w_g_up[l], k_k[l], k_a[l], r_k[l], lnx_w[l], lnx_b[l],
                   w_o_rwkv[l], w_out[l], w_router_group[l], b_router_group[l], w_router_expert[l],
                   b_router_expert[l], w_exp_gate[l], w_exp_up[l], w_exp_down[l])
    return x
```

```python
import functools
import math

import jax
import jax.numpy as jnp
from jax import lax
from jax.experimental import pallas as pl
from jax.experimental.pallas import tpu as pltpu

F32 = jnp.float32
BF16 = jnp.bfloat16

D_MODEL = 1024
CHUNK = 64
HEADS = 8
MLA_NOPE = 64
MLA_ROPE = 32
MLA_V = 64
MLA_Q_RANK = 384
MLA_KV_RANK = 256
ROPE_THETA = 10000.0
RWKV_HEAD = 64
RWKV_WIDTH = HEADS * RWKV_HEAD
DECAY_RANK = 64
AAA_RANK = 64
GATE_RANK = 128
GN_EPS = 64e-5
N_GROUPS = 4
EXPERTS_PER_GROUP = 8
N_EXPERTS = N_GROUPS * EXPERTS_PER_GROUP
EXPERT_FF = 256
RMS_EPS = 1e-6
NEG_INF = -1e30
MLA_IN = MLA_Q_RANK + MLA_KV_RANK + MLA_ROPE
RWKV_IN = 3 * RWKV_WIDTH + DECAY_RANK + AAA_RANK + GATE_RANK

LANES = 128
DMA_UNROLL = 8
ROW_CHUNKS = 9
HEAD_PAD = LANES
QK_WIDTH = HEADS * HEAD_PAD
MLA_C_WIDTH = 768
ROUTER_WIDTH = LANES
VMEM_LIMIT = 56 * 1024 * 1024
ATTN_HEADS_PER_STEP = 8
ATTN_BLOCK = 256
ATTN_BLOCKS_PER_ITER = 2
V_ROWS = 80
RWKV_CHUNKS_PER_STEP = 4
MERGE_SUBTILES = 2


def _dot(a, b):
    return jnp.dot(a, b, preferred_element_type=F32)


def _dot_nt(a, b):
    return lax.dot_general(a, b, (((1,), (1,)), ((), ())), preferred_element_type=F32)


def _dot_tn(a, b):
    return lax.dot_general(a, b, (((0,), (0,)), ((), ())), preferred_element_type=F32)


def _split(x):
    hi = x.astype(BF16)
    lo = (x - hi.astype(F32)).astype(BF16)
    return hi, lo


def _dot3(x, w_hi, w_lo):
    x_hi, x_lo = _split(x)
    return _dot(x_hi, w_hi) + (_dot(x_hi, w_lo) + _dot(x_lo, w_hi))


def _sigmoid(x):
    return 1.0 / (1.0 + jnp.exp(-x))


def _rms(x, g):
    return x * lax.rsqrt(jnp.mean(x * x, axis=-1, keepdims=True) + RMS_EPS) * g


def _ada_kernel(c_ref, w_ref, b_ref, o_ref):
    c = c_ref[...]
    s = c * _sigmoid(c)
    w_hi, w_lo = _split(w_ref[...])
    o_ref[...] = _dot3(s, w_hi, w_lo) + b_ref[...]


def _ada(c, w_ada, b_ada):
    b, d = c.shape
    n = w_ada.shape[1]
    tn = 512
    return pl.pallas_call(
        _ada_kernel,
        grid=(n // tn,),
        in_specs=[pl.BlockSpec((b, d), lambda j: (0, 0)),
                  pl.BlockSpec((d, tn), lambda j: (0, j)),
                  pl.BlockSpec((1, tn), lambda j: (0, j))],
        out_specs=pl.BlockSpec((b, tn), lambda j: (0, j)),
        out_shape=jax.ShapeDtypeStruct((b, n), F32),
        name="ada",
    )(c, w_ada, b_ada.reshape(1, n))


def _inproj_kernel(x_ref, mod_ref, pos_ref, gpre_ref, wc_ref, wr_ref, wg_ref, gcq_ref, gckv_ref,
                   wuq_ref, wuk_ref, wuv_ref, mu_ref, invf_ref,
                   q_ref, k_ref, vt_ref, zs_ref, gate_ref, carry_ref):
    i = pl.program_id(1)
    tm = x_ref.shape[0]

    @pl.when(i == 0)
    def _():
        carry_ref[...] = jnp.zeros_like(carry_ref)

    mod = mod_ref[0]
    shift, scale = mod[0:1], mod[1:2]
    h = (_rms(x_ref[...], gpre_ref[...]) * (1.0 + scale) + shift).astype(BF16)

    zc = _dot(h, wc_ref[...])
    zg = _dot(h, wg_ref[...])
    nq = _rms(zc[:, :MLA_Q_RANK], gcq_ref[...]).astype(BF16)
    nkv = _rms(zc[:, MLA_Q_RANK:MLA_Q_RANK + MLA_KV_RANK], gckv_ref[...]).astype(BF16)
    kr = zc[:, MLA_Q_RANK + MLA_KV_RANK:]
    qf = _dot(nq, wuq_ref[...])
    kn = _dot(nkv, wuk_ref[...])
    vt = _dot_nt(wuv_ref[...], nkv)
    zr = _dot(h, wr_ref[...])

    pos_col = jnp.broadcast_to(pos_ref[...].astype(F32), (LANES, tm)).T
    ang = pos_col * invf_ref[...]
    lane = lax.broadcasted_iota(jnp.int32, (1, LANES), 1)
    first_half = lane < MLA_NOPE + MLA_ROPE // 2
    in_rope = (lane >= MLA_NOPE) & (lane < MLA_NOPE + MLA_ROPE)
    cos_t = jnp.where(in_rope, jnp.cos(ang), 1.0)
    sin_a = jnp.sin(ang)
    sin_t = jnp.where(in_rope, jnp.where(first_half, -sin_a, sin_a), 0.0)

    def rope(t):
        rot = jnp.where(first_half, pltpu.roll(t, LANES - MLA_ROPE // 2, 1), pltpu.roll(t, MLA_ROPE // 2, 1))
        return t * cos_t + rot * sin_t

    kr_rot = rope(kr)
    q_scale = math.log2(math.e) / math.sqrt(MLA_NOPE + MLA_ROPE)
    for hh in range(HEADS):
        sl = slice(hh * HEAD_PAD, (hh + 1) * HEAD_PAD)
        q_ref[:, sl] = (rope(qf[:, sl]) * q_scale).astype(BF16)
        k_ref[:, sl] = (kn[:, sl] + kr_rot).astype(BF16)
    vrow = lax.broadcasted_iota(jnp.int32, vt.shape, 0) % V_ROWS
    vt = jnp.where(vrow == MLA_V, 1.0, vt).astype(BF16)
    for j in range(vt_ref.shape[0]):
        vt_ref[j] = vt[:, j * ATTN_BLOCK:(j + 1) * ATTN_BLOCK]

    gate_ref[...] = _sigmoid(zg).astype(BF16)

    row = lax.broadcasted_iota(jnp.int32, (tm, 1), 0)
    prev = jnp.where(row == 0, carry_ref[0:1, :], pltpu.roll(zr, 1, 0))
    carry_ref[0:1, :] = zr[tm - 1:tm, :]
    zs_ref[...] = (zr + (prev - zr) * mu_ref[...]).astype(BF16)


def _const_spec(shape):
    nd = len(shape)
    return pl.BlockSpec(shape, lambda *_: (0,) * nd, pipeline_mode=pl.Buffered(1))


def _inproj(x2, mod3, pos2, g_pre, w_c, w_r, w_g, g_cq, g_ckv, w_uq, w_uk, w_uv, mu, invf, *, batch, seq, tm):
    t, d = x2.shape
    ns = seq // tm
    row = lambda b, i: (b * ns + i, 0)
    widths = [QK_WIDTH, QK_WIDTH, RWKV_IN, 2 * d]
    specs = [pl.BlockSpec((tm, w), row) for w in widths]
    shapes = [jax.ShapeDtypeStruct((t, w), BF16) for w in widths]
    vw = HEADS * V_ROWS
    nb = tm // ATTN_BLOCK
    specs.insert(2, pl.BlockSpec((None, nb, vw, ATTN_BLOCK), lambda b, i: (b, i, 0, 0)))
    shapes.insert(2, jax.ShapeDtypeStruct((batch, ns * nb, vw, ATTN_BLOCK), BF16))
    return pl.pallas_call(
        _inproj_kernel,
        grid=(batch, ns),
        in_specs=[pl.BlockSpec((tm, d), row),
                  pl.BlockSpec((1, 6, d), lambda b, i: (b, 0, 0)),
                  pl.BlockSpec((None, 1, tm), lambda b, i: (b, 0, i)),
                  _const_spec(g_pre.shape), _const_spec(w_c.shape), _const_spec(w_r.shape),
                  _const_spec(w_g.shape), _const_spec(g_cq.shape), _const_spec(g_ckv.shape),
                  _const_spec(w_uq.shape), _const_spec(w_uk.shape), _const_spec(w_uv.shape),
                  _const_spec(mu.shape), _const_spec(invf.shape)],
        out_specs=specs,
        out_shape=shapes,
        scratch_shapes=[pltpu.VMEM((8, RWKV_IN), F32)],
        compiler_params=pltpu.CompilerParams(dimension_semantics=("parallel", "arbitrary"),
                                             vmem_limit_bytes=VMEM_LIMIT),
        name="inproj",
    )(x2, mod3, pos2, g_pre, w_c, w_r, w_g, g_cq, g_ckv, w_uq, w_uk, w_uv, mu, invf)


def _attn_kernel(q_ref, k_ref, vt_ref, o_ref):
    i = pl.program_id(2)
    tq = q_ref.shape[0]
    kc = lax.broadcasted_iota(jnp.int32, (tq, tq), 0) // CHUNK
    qc = lax.broadcasted_iota(jnp.int32, (tq, tq), 1) // CHUNK
    diag_mask = kc <= qc
    heads = range(ATTN_HEADS_PER_STEP)
    hs = [slice(h * HEAD_PAD, (h + 1) * HEAD_PAD) for h in heads]
    vr = [slice(h * V_ROWS, (h + 1) * V_ROWS) for h in heads]
    q = [q_ref[:, s] for s in hs]

    def block(kb, carry, nblk, masked):
        kbs = [kb + n for n in range(nblk)]
        rows = [pl.ds(pl.multiple_of(b * tq, tq), tq) for b in kbs]
        s = [[_dot_nt(k_ref[r, hs[h]], q[h]) for r in rows] for h in heads]
        if masked:
            s = [[jnp.where(diag_mask, t, NEG_INF) for t in sh] for sh in s]
        m_new = []
        for h in heads:
            m = carry[h][0]
            for t in s[h]:
                m = jnp.maximum(m, jnp.max(t, axis=0, keepdims=True))
            m_new.append(m)
        alpha = [jnp.exp2(carry[h][0] - m_new[h]) for h in heads]
        p = [jnp.concatenate([jnp.exp2(t - m_new[h]).astype(BF16) for t in s[h]], axis=0) for h in heads]
        acc = [alpha[h] * carry[h][1]
               + _dot(jnp.concatenate([vt_ref[b, vr[h], :] for b in kbs], axis=1), p[h]) for h in heads]
        return tuple((m_new[h], acc[h]) for h in heads)

    init = tuple((jnp.full((1, tq), NEG_INF, F32), jnp.zeros((V_ROWS, tq), F32)) for _ in heads)
    nb = ATTN_BLOCKS_PER_ITER
    carry = lax.fori_loop(0, i // nb, lambda n, c: block(n * nb, c, nb, False), init)
    carry = lax.fori_loop((i // nb) * nb, i, lambda kb, c: block(kb, c, 1, False), carry)
    final = block(i, carry, 1, True)
    outs = [final[h][1][:MLA_V] / final[h][1][MLA_V:MLA_V + 1] for h in heads]
    for pr in range(ATTN_HEADS_PER_STEP // 2):
        pair_t = jnp.concatenate([outs[2 * pr], outs[2 * pr + 1]], axis=0)
        o_ref[:, pr * LANES:(pr + 1) * LANES] = pair_t.T.astype(BF16)


def _attn(q_all, k_all, vt_all, *, batch, seq, tq):
    t = q_all.shape[0]
    nq = seq // tq
    g = ATTN_HEADS_PER_STEP
    return pl.pallas_call(
        _attn_kernel,
        grid=(batch, HEADS // g, nq),
        in_specs=[pl.BlockSpec((tq, g * HEAD_PAD), lambda b, p, i: (b * nq + i, p)),
                  pl.BlockSpec((seq, g * HEAD_PAD), lambda b, p, i: (b, p)),
                  pl.BlockSpec((None, nq, g * V_ROWS, tq), lambda b, p, i: (b, 0, p, 0))],
        out_specs=pl.BlockSpec((tq, g * MLA_V), lambda b, p, i: (b * nq + i, p)),
        out_shape=jax.ShapeDtypeStruct((t, HEADS * MLA_V), BF16),
        compiler_params=pltpu.CompilerParams(dimension_semantics=("parallel", "parallel", "arbitrary"),
                                             vmem_limit_bytes=VMEM_LIMIT),
        name="attn",
    )(q_all, k_all, vt_all)


def _rwkv_kernel(zs_ref, wda_ref, wgu_ref, dbias_ref, abias_ref, kk_ref, ka_ref, rk_ref, lnw_ref, lnb_ref,
                 ones_ref, tri_ref,
                 o_ref,
                 st_ref, a_s, r_s, b_s, k_s, bg_s, kg_s, v_s, gc_s, bonus_s, g_s, y_s):
    w = RWKV_WIDTH
    nsub = RWKV_CHUNKS_PER_STEP
    th = nsub * CHUNK
    n2 = 2 * CHUNK
    staged = (a_s, r_s, b_s, k_s, bg_s, kg_s, v_s, gc_s, bonus_s, g_s)

    @pl.when(pl.program_id(1) == 0)
    def _():
        st_ref[...] = jnp.zeros_like(st_ref)

    ones_bd = ones_ref[...]

    def head_sum(t):
        tb = t.astype(BF16)
        hw = ones_bd.shape[0]
        return jnp.concatenate([_dot(tb[:, n * hw:(n + 1) * hw], ones_bd) for n in range(w // hw)], axis=1)

    ri = lax.broadcasted_iota(jnp.int32, (4 * CHUNK, n2), 0)
    ci = lax.broadcasted_iota(jnp.int32, (4 * CHUNK, n2), 1) % CHUNK
    tri_mask = (ri % CHUNK + ri // n2) > ci
    lane2 = lax.broadcasted_iota(jnp.int32, (CHUNK, n2), 1)
    h1 = lane2 < RWKV_HEAD
    bi = lax.broadcasted_iota(jnp.int32, (n2, n2), 0) // RWKV_HEAD
    bj = lax.broadcasted_iota(jnp.int32, (n2, n2), 1) // RWKV_HEAD
    bd_mask = bi == bj
    zero_bf = jnp.zeros((CHUNK, n2), BF16)
    zero_f = jnp.zeros((CHUNK, n2), F32)
    lane = lax.broadcasted_iota(jnp.int32, (1, LANES), 1)

    def prepare(h):
        hr = slice(h * th, (h + 1) * th)
        zs = zs_ref[hr, :].astype(F32)
        r, k, v = zs[:, 0:w], zs[:, w:2 * w], zs[:, 2 * w:3 * w]
        da = zs[:, 3 * w:3 * w + LANES]
        gd = zs[:, 3 * w + LANES:]
        lora_in = jnp.where(lane < DECAY_RANK, jnp.tanh(da), da).astype(BF16)
        pre = _dot(lora_in, wda_ref[...])
        g_new = _dot(_sigmoid(gd).astype(BF16), wgu_ref[...])
        yield
        u = -(dbias_ref[...] + pre[:, :w])
        softplus = jnp.maximum(u, 0.0) + jnp.log(1.0 + jnp.exp(-jnp.abs(u)))
        logw = -jnp.exp(-softplus - 0.5)
        eta = _sigmoid(abias_ref[...] + pre[:, w:])
        kk = k * kk_ref[...]
        kk_ss = head_sum(kk * kk)
        lw_hi, lw_lo = _split(logw)
        cum = _dot(tri_ref[...], lw_hi) + _dot(tri_ref[...], lw_lo)
        yield
        kk = kk * jnp.minimum(lax.rsqrt(kk_ss), 1e12)
        kp = k * (1.0 + (eta - 1.0) * ka_ref[...])
        bonus_new = head_sum(r * kp * rk_ref[...]) * v
        yield
        cum_end = jnp.concatenate([jnp.broadcast_to(cum[(c + 1) * CHUNK - 1:(c + 1) * CHUNK, :], (CHUNK, w))
                                   for c in range(nsub)], axis=0)
        b_in = kk * eta
        g_inv = jnp.exp(-cum)
        g_rem = jnp.exp(cum_end - cum)
        staged_new = ((-kk * jnp.exp(cum - logw)).astype(BF16), (r * jnp.exp(cum)).astype(BF16),
                      (b_in * g_inv).astype(BF16), (kp * g_inv).astype(BF16),
                      (b_in * g_rem).astype(BF16), (kp * g_rem).astype(BF16), v.astype(BF16),
                      jnp.exp(cum_end), bonus_new, g_new)
        for ref, val in zip(staged, staged_new):
            ref[hr, :] = val
        yield

    def solve(h):
        blocks = [(sub, p) for sub in range(nsub) for p in range(HEADS // 2)]
        rows = [slice(h * th + sub * CHUNK, h * th + (sub + 1) * CHUNK) for sub in range(nsub)]
        lns = [slice(p * n2, (p + 1) * n2) for p in range(HEADS // 2)]
        a_t = [a_s[rows[sub], lns[p]] for sub, p in blocks]
        r_t = [r_s[rows[sub], lns[p]] for sub, p in blocks]
        v_t = [v_s[rows[sub], lns[p]] for sub, p in blocks]
        ml = []
        for n, (sub, p) in enumerate(blocks):
            lhs = jnp.concatenate([jnp.where(h1, a_t[n], zero_bf), jnp.where(h1, zero_bf, a_t[n]),
                                   jnp.where(h1, r_t[n], zero_bf), jnp.where(h1, zero_bf, r_t[n])], axis=0)
            rhs = jnp.concatenate([b_s[rows[sub], lns[p]], k_s[rows[sub], lns[p]]], axis=0)
            ml.append(jnp.where(tri_mask, _dot_nt(lhs, rhs), 0.0))
        ml_b = [m.astype(BF16) for m in ml]
        yield
        units = [(n, hh) for n in range(len(blocks)) for hh in range(2)]
        w_t = [_dot(ml_b[n][hh * CHUNK:(hh + 1) * CHUNK, :], jnp.concatenate([zero_bf, v_t[n]], axis=0))
               for n, hh in units]
        a_f = [t.astype(F32) for t in a_t]
        a_sw = [pltpu.roll(t, RWKV_HEAD, 1) for t in a_f]
        x = [jnp.where(h1, a_f[n], pltpu.roll(w_t[2 * n], RWKV_HEAD, 1)) if hh == 0
             else jnp.where(h1, a_sw[n], w_t[2 * n + 1]) for n, hh in units]
        lsq = [jnp.where(h1, ml[n][hh * CHUNK:(hh + 1) * CHUNK, :], 0.0) for n, hh in units]
        yield
        for _ in range(6):
            out = [_dot(lsq[u][:, :CHUNK].astype(BF16),
                        jnp.concatenate([x[u], lsq[u]], axis=1).astype(BF16)) for u in range(len(units))]
            x = [x[u] + out[u][:, :n2] for u in range(len(units))]
            lsq = [t[:, n2:] for t in out]
            yield
        z = []
        for n in range(len(blocks)):
            x0, x1 = x[2 * n], x[2 * n + 1]
            a_hat = jnp.where(h1, x0, pltpu.roll(x1, RWKV_HEAD, 1))
            u0 = jnp.where(h1, pltpu.roll(x0, RWKV_HEAD, 1), x1)
            z.append(jnp.concatenate([jnp.concatenate([a_hat, u0], axis=1),
                                      jnp.concatenate([zero_f, v_t[n].astype(F32)], axis=1)],
                                     axis=0).astype(BF16))
        o0 = [_dot(ml_b[n][2 * CHUNK:3 * CHUNK, :], z[n]) for n in range(len(blocks))]
        o1 = [_dot(ml_b[n][3 * CHUNK:4 * CHUNK, :], z[n]) for n in range(len(blocks))]
        pm = [_dot_tn(z[n], jnp.concatenate([bg_s[rows[sub], lns[p]], kg_s[rows[sub], lns[p]]], axis=0))
              for n, (sub, p) in enumerate(blocks)]
        r_hat = [(r_t[n].astype(F32) + jnp.where(h1, o0[n][:, :n2], o1[n][:, :n2])).astype(BF16)
                 for n in range(len(blocks))]
        yield
        state = [st_ref[p] for p in range(HEADS // 2)]
        for n, (sub, p) in enumerate(blocks):
            y0 = jnp.where(h1, o0[n][:, n2:], o1[n][:, n2:])
            g_t = jnp.where(bd_mask, pm[n][:n2], 0.0).astype(BF16)
            h_t = jnp.where(bd_mask, pm[n][n2:], 0.0)
            s_b = state[p].astype(BF16)
            y_s[rows[sub], lns[p]] = _dot_nt(r_hat[n], s_b) + y0
            gc = gc_s[rows[sub].start:rows[sub].start + 1, lns[p]]
            state[p] = gc * state[p] + _dot(s_b, g_t) + h_t
        for p in range(HEADS // 2):
            st_ref[p] = state[p]
        yield

    def finish(h):
        hr = slice(h * th, (h + 1) * th)
        y = y_s[hr, :]
        inv_n = 1.0 / RWKV_HEAD
        dev = y - head_sum(y) * inv_n
        yield
        var = head_sum(dev * dev) * inv_n
        yield
        out = (dev * lax.rsqrt(var + GN_EPS) * lnw_ref[...] + lnb_ref[...] + bonus_s[hr, :]) * g_s[hr, :]
        o_ref[hr, :] = out.astype(BF16)
        yield

    def emit(main, side=None, every=1):
        n = 0
        for _ in main:
            n += 1
            if side is not None and n % every == 0:
                next(side, None)
        if side is not None:
            for _ in side:
                pass

    emit(prepare(0))
    emit(solve(0), prepare(1), every=2)
    emit(solve(1), finish(0), every=3)
    emit(finish(1))


def _rwkv(zs, w_da, w_gu, dbias, abias, k_k, k_a, r_k, lnw, lnb, ones_bd, tri, *, batch, seq, tc):
    t = zs.shape[0]
    nc = seq // tc
    w = RWKV_WIDTH
    consts = [w_da, w_gu, dbias, abias, k_k, k_a, r_k, lnw, lnb, ones_bd, tri]
    return pl.pallas_call(
        _rwkv_kernel,
        grid=(batch, nc),
        in_specs=[pl.BlockSpec((tc, RWKV_IN), lambda b, i: (b * nc + i, 0))] + [_const_spec(a.shape) for a in consts],
        out_specs=pl.BlockSpec((tc, w), lambda b, i: (b * nc + i, 0)),
        out_shape=jax.ShapeDtypeStruct((t, w), BF16),
        scratch_shapes=[pltpu.VMEM((HEADS // 2, 2 * RWKV_HEAD, 2 * RWKV_HEAD), F32)]
        + [pltpu.VMEM((tc, w), BF16)] * 7
        + [pltpu.VMEM((tc, w), F32)] * 4,
        compiler_params=pltpu.CompilerParams(dimension_semantics=("parallel", "arbitrary"),
                                             vmem_limit_bytes=VMEM_LIMIT),
        name="rwkv",
    )(zs, *consts)


def _route(logits):
    lane = lax.broadcasted_iota(jnp.int32, logits.shape, 1)
    lane_f = lane.astype(F32)
    big = float(ROUTER_WIDTH)
    is_group = (lane >= N_EXPERTS) & (lane < N_EXPERTS + N_GROUPS)
    gl = jnp.where(is_group, logits, NEG_INF)
    g_max = jnp.max(gl, axis=-1, keepdims=True)
    g_sel = jnp.min(jnp.where(gl == g_max, lane_f, big), axis=-1, keepdims=True) - float(N_EXPERTS)
    p_sel = 1.0 / jnp.sum(jnp.where(is_group, jnp.exp(gl - g_max), 0.0), axis=-1, keepdims=True)
    in_group = (lane < N_EXPERTS) & ((lane // EXPERTS_PER_GROUP).astype(F32) == g_sel)
    el = jnp.where(in_group, logits, NEG_INF)
    t1 = jnp.max(el, axis=-1, keepdims=True)
    i1 = jnp.min(jnp.where(el == t1, lane_f, big), axis=-1, keepdims=True)
    el2 = jnp.where(lane_f == i1, NEG_INF, el)
    t2 = jnp.max(el2, axis=-1, keepdims=True)
    i2 = jnp.min(jnp.where(el2 == t2, lane_f, big), axis=-1, keepdims=True)
    e21 = jnp.exp(t2 - t1)
    w1 = p_sel / (1.0 + e21)
    w2 = w1 * e21
    first = g_sel * float(EXPERTS_PER_GROUP)
    cw = jnp.where(lane_f == i1 - first, w1, 0.0) + jnp.where(lane_f == i2 - first, w2, 0.0)
    return g_sel, cw


def _merge_kernel(attn_ref, rw_ref, gate_ref, x_ref, mod_ref, womla_ref, worw_ref, wout_ref,
                  gpost_ref, gffn_ref, wrh_ref, wrl_ref, br_ref, stril_ref,
                  x1_ref, hp_ref, dest_ref, cnt_ref, carry_ref, *, group_capacity):
    i = pl.program_id(0)
    tm, d = x_ref.shape

    @pl.when(i == 0)
    def _():
        carry_ref[...] = jnp.zeros_like(carry_ref)

    mod = mod_ref[0]
    gate1, shift2, scale2 = mod[2:3], mod[3:4], mod[4:5]

    ts = tm // MERGE_SUBTILES
    subs = [slice(n * ts, (n + 1) * ts) for n in range(MERGE_SUBTILES)]
    o_mla = [_dot(attn_ref[r, :], womla_ref[...]) for r in subs]
    o_rw = [_dot(rw_ref[r, :], worw_ref[...]) for r in subs]
    o = [(gate_ref[r, :d].astype(F32) * o_mla[n] + gate_ref[r, d:].astype(F32) * o_rw[n]).astype(BF16)
         for n, r in enumerate(subs)]
    y = [_dot(t, wout_ref[...]) for t in o]
    x1 = [x_ref[r, :] + gate1 * _rms(y[n], gpost_ref[...]) for n, r in enumerate(subs)]
    h2 = [_rms(t, gffn_ref[...]) * (1.0 + scale2) + shift2 for t in x1]
    logits = [_dot3(t, wrh_ref[...], wrl_ref[...]) + br_ref[...] for t in h2]
    for n, r in enumerate(subs):
        x1_ref[r, :] = x1[n]
        for c in range(d // LANES):
            hp_ref[pl.ds(n * ts * ROW_CHUNKS + c, ts, stride=ROW_CHUNKS), :] = h2[n][:, c * LANES:(c + 1) * LANES]

    lane_f = lax.broadcasted_iota(jnp.int32, (ts, ROUTER_WIDTH), 1).astype(F32)
    routes = [_route(t) for t in logits]
    g_sels = [g for g, _ in routes]
    for n, r in enumerate(subs):
        hp_ref[pl.ds(n * ts * ROW_CHUNKS + d // LANES, ts, stride=ROW_CHUNKS), :] = routes[n][1]
    onehots = [jnp.where(lane_f == g, 1.0, 0.0) for g in g_sels]

    onehot = jnp.concatenate(onehots, axis=0)
    earlier = _dot(stril_ref[...], onehot.astype(BF16))
    carry = carry_ref[0:1, :]
    for n, r in enumerate(subs):
        rank = jnp.sum(jnp.where(lane_f == g_sels[n], carry + earlier[r, :], 0.0), axis=-1, keepdims=True)
        dest_col = jnp.broadcast_to(g_sels[n] * float(group_capacity) + rank, (ts, LANES))
        dest_ref[:, r] = dest_col.T[0:1, :].astype(jnp.int32)
    total = jnp.broadcast_to(carry + jnp.sum(onehot, axis=0, keepdims=True), carry_ref.shape)
    carry_ref[...] = total
    cnt_ref[...] = total


def _merge(attn, rw, gates, x2, mod3, w_o_mla, w_o_rwkv, w_out, g_post, g_ffn, wr_hi, wr_lo, b_r, *, seq, tm):
    t, d = x2.shape
    ns = seq // tm
    row = lambda i: (i, 0)
    tid = jnp.arange(tm)
    stril = (tid[None, :] < tid[:, None]).astype(BF16)
    consts = [w_o_mla, w_o_rwkv, w_out, g_post, g_ffn, wr_hi, wr_lo, b_r, stril]
    return pl.pallas_call(
        functools.partial(_merge_kernel, group_capacity=t),
        grid=(t // tm,),
        in_specs=[pl.BlockSpec((tm, attn.shape[1]), row), pl.BlockSpec((tm, rw.shape[1]), row),
                  pl.BlockSpec((tm, 2 * d), row), pl.BlockSpec((tm, d), row),
                  pl.BlockSpec((1, 6, d), lambda i: (i // ns, 0, 0))] + [_const_spec(a.shape) for a in consts],
        out_specs=[pl.BlockSpec((tm, d), row), pl.BlockSpec((tm * ROW_CHUNKS, LANES), row),
                   pl.BlockSpec((None, 1, tm), lambda i: (i, 0, 0)), pl.BlockSpec((8, ROUTER_WIDTH), lambda i: (0, 0))],
        out_shape=[jax.ShapeDtypeStruct((t, d), F32), jax.ShapeDtypeStruct((t * ROW_CHUNKS, LANES), F32),
                   jax.ShapeDtypeStruct((t // tm, 1, tm), jnp.int32), jax.ShapeDtypeStruct((8, ROUTER_WIDTH), F32)],
        scratch_shapes=[pltpu.VMEM((8, ROUTER_WIDTH), F32)],
        compiler_params=pltpu.CompilerParams(dimension_semantics=("arbitrary",), vmem_limit_bytes=VMEM_LIMIT),
        name="merge",
    )(attn, rw, gates, x2, mod3, *consts)


def _token_rows(i):
    return pl.ds(i * ROW_CHUNKS, ROW_CHUNKS)


def _chunk(ref, c, n):
    return ref[pl.ds(c, n, stride=ROW_CHUNKS), :]


def _dispatch_kernel(dest_ref, hp_ref, xs_ref, sem):
    td = hp_ref.shape[0] // ROW_CHUNKS
    base = pl.program_id(0) * td

    def issue(r0, c):
        for j in range(DMA_UNROLL):
            r = r0 * DMA_UNROLL + j
            pltpu.make_async_copy(hp_ref.at[_token_rows(r)], xs_ref.at[_token_rows(dest_ref[base + r])],
                                  sem).start(priority=j % 2)
        return c

    lax.fori_loop(0, td // DMA_UNROLL, issue, 0)
    pltpu.make_async_copy(hp_ref, xs_ref.at[pl.ds(0, td * ROW_CHUNKS)], sem).wait()


def _dispatch(dest, hp, *, tokens_out, td):
    t = hp.shape[0] // ROW_CHUNKS
    return pl.pallas_call(
        _dispatch_kernel,
        grid_spec=pltpu.PrefetchScalarGridSpec(
            num_scalar_prefetch=1, grid=(t // td,),
            in_specs=[pl.BlockSpec((td * ROW_CHUNKS, LANES), lambda i, dest: (i, 0))],
            out_specs=pl.BlockSpec(memory_space=pl.ANY),
            scratch_shapes=[pltpu.SemaphoreType.DMA]),
        out_shape=jax.ShapeDtypeStruct((tokens_out * ROW_CHUNKS, LANES), F32),
        compiler_params=pltpu.CompilerParams(dimension_semantics=("arbitrary",), vmem_limit_bytes=VMEM_LIMIT),
        name="dispatch",
    )(dest, hp)


def _experts_kernel(tg_ref, tb_ref, tr_ref, xs_ref, wg_ref, wu_ref, wd_ref, ex_ref, ys_ref):
    del tg_ref, tb_ref
    nrows = tr_ref[pl.program_id(0)]

    @pl.when(nrows > 0)
    def _():
        tmx = xs_ref.shape[0] // ROW_CHUNKS
        nchunk = D_MODEL // LANES
        valid = lax.broadcasted_iota(jnp.int32, (tmx, 1), 0) < nrows
        x = jnp.concatenate([jnp.where(valid, _chunk(xs_ref, c, tmx), 0.0).astype(BF16) for c in range(nchunk)],
                            axis=1)
        cw_hi, cw_lo = _split(jnp.where(valid, _chunk(xs_ref, nchunk, tmx), 0.0))
        cwx = _dot(cw_hi, ex_ref[...]) + _dot(cw_lo, ex_ref[...])
        acc = jnp.zeros((tmx, D_MODEL), F32)
        for e in range(EXPERTS_PER_GROUP):
            a = _dot(x, wg_ref[e])
            act = a * _sigmoid(a) * _dot(x, wu_ref[e]) * cwx[:, e * EXPERT_FF:(e + 1) * EXPERT_FF]
            acc = acc + _dot(act.astype(BF16), wd_ref[e])
        for c in range(nchunk):
            ys_ref[pl.ds(c, tmx, stride=ROW_CHUNKS), :] = acc[:, c * LANES:(c + 1) * LANES]
        ys_ref[pl.ds(nchunk, tmx, stride=ROW_CHUNKS), :] = jnp.zeros((tmx, LANES), F32)


def _experts(tile_group, tile_blk, tile_rows, xs, w_gate, w_up, w_down, expand, *, tmx):
    nt = tile_group.shape[0]
    d = D_MODEL
    grp = lambda j, tg, tb, tr: (tg[j], 0, 0)
    tile = pl.BlockSpec((tmx * ROW_CHUNKS, LANES), lambda j, tg, tb, tr: (tb[j], 0))
    return pl.pallas_call(
        _experts_kernel,
        grid_spec=pltpu.PrefetchScalarGridSpec(
            num_scalar_prefetch=3, grid=(nt,),
            in_specs=[tile,
                      pl.BlockSpec((EXPERTS_PER_GROUP, d, EXPERT_FF), grp),
                      pl.BlockSpec((EXPERTS_PER_GROUP, d, EXPERT_FF), grp),
                      pl.BlockSpec((EXPERTS_PER_GROUP, EXPERT_FF, d), grp),
                      pl.BlockSpec(expand.shape, lambda j, tg, tb, tr: (0, 0))],
            out_specs=tile),
        out_shape=jax.ShapeDtypeStruct(xs.shape, F32),
        compiler_params=pltpu.CompilerParams(dimension_semantics=("arbitrary",), vmem_limit_bytes=VMEM_LIMIT),
        name="experts",
    )(tile_group, tile_blk, tile_rows, xs, w_gate, w_up, w_down, expand)


def _final_kernel(dest_ref, x1_ref, mod_ref, gpost_ref, ys_ref, o_ref, ybuf, sems):
    tmf, d = x1_ref.shape
    i = pl.program_id(0)
    slot = i % 2

    def gather(tile, into):
        base = tile * tmf

        def issue(r0, c):
            for j in range(DMA_UNROLL):
                r = r0 * DMA_UNROLL + j
                pltpu.make_async_copy(ys_ref.at[_token_rows(dest_ref[base + r])], ybuf.at[into, _token_rows(r)],
                                      sems.at[into]).start(priority=j % 2)
            return c

        lax.fori_loop(0, tmf // DMA_UNROLL, issue, 0)

    @pl.when(i == 0)
    def _():
        gather(i, slot)

    @pl.when(i + 1 < pl.num_programs(0))
    def _():
        gather(i + 1, 1 - slot)

    pltpu.make_async_copy(ys_ref.at[pl.ds(0, tmf * ROW_CHUNKS)], ybuf.at[slot], sems.at[slot]).wait()
    gate2 = mod_ref[0][5:6]
    y = jnp.concatenate([ybuf[slot, pl.ds(c, tmf, stride=ROW_CHUNKS), :] for c in range(d // LANES)], axis=1)
    o_ref[...] = x1_ref[...] + gate2 * _rms(y, gpost_ref[...])


def _final(dest, x1, mod3, g_post, ys, *, seq, tmf):
    t, d = x1.shape
    ns = seq // tmf
    return pl.pallas_call(
        _final_kernel,
        grid_spec=pltpu.PrefetchScalarGridSpec(
            num_scalar_prefetch=1, grid=(t // tmf,),
            in_specs=[pl.BlockSpec((tmf, d), lambda i, dest: (i, 0)),
                      pl.BlockSpec((1, 6, d), lambda i, dest: (i // ns, 0, 0)),
                      pl.BlockSpec((1, d), lambda i, dest: (0, 0)),
                      pl.BlockSpec(memory_space=pl.ANY)],
            out_specs=pl.BlockSpec((tmf, d), lambda i, dest: (i, 0)),
            scratch_shapes=[pltpu.VMEM((2, tmf * ROW_CHUNKS, LANES), F32), pltpu.SemaphoreType.DMA((2,))]),
        out_shape=jax.ShapeDtypeStruct((t, d), F32),
        compiler_params=pltpu.CompilerParams(dimension_semantics=("arbitrary",), vmem_limit_bytes=VMEM_LIMIT),
        name="final",
    )(dest, x1, mod3, g_post, ys)


def _tile_map(counts, *, capacity, tmx):
    ntile = (counts + tmx - 1) // tmx
    ends = jnp.cumsum(ntile)
    starts = ends - ntile
    j = jnp.arange(capacity // tmx + N_GROUPS, dtype=jnp.int32)
    g = jnp.minimum(jnp.sum((j[:, None] >= ends[None, :]).astype(jnp.int32), axis=1), N_GROUPS - 1)
    local = j - starts[g]
    valid = j < ends[-1]
    blk = g * (capacity // tmx) + local
    rows = jnp.clip(counts[g] - local * tmx, 0, tmx)
    last = ends[-1] - 1
    return (jnp.where(valid, g, g[last]).astype(jnp.int32), jnp.where(valid, blk, blk[last]).astype(jnp.int32),
            jnp.where(valid, rows, 0).astype(jnp.int32))


def _pick_tile(seq, want):
    return want if seq % want == 0 else seq


def _layer(x, c, positions, w_ada, b_ada, g_pre_mix, g_post_mix, g_pre_ffn, g_post_ffn, w_in, g_cq, w_uq,
           g_ckv, w_ukv, w_o_mla, mu_shift, w_decay_up, decay_bias, w_a_up, a_bias, w_g_up, k_k, k_a, r_k,
           lnx_w, lnx_b, w_o_rwkv, w_out, w_router_group, b_router_group, w_router_expert, b_router_expert,
           w_exp_gate, w_exp_up, w_exp_down):
    batch, seq, d = x.shape
    t = batch * seq
    x2 = x.reshape(t, d)
    row1 = lambda a: a.reshape(1, -1)

    mod3 = _ada(c, w_ada, b_ada).reshape(batch, 6, d)

    zeros = lambda n: jnp.zeros((d, n), F32)
    w_c = jnp.concatenate([w_in[:, :MLA_Q_RANK + MLA_KV_RANK], zeros(MLA_NOPE),
                           w_in[:, MLA_Q_RANK + MLA_KV_RANK:MLA_IN], zeros(HEAD_PAD - MLA_NOPE - MLA_ROPE)],
                          axis=1).astype(BF16)
    w_r = w_in[:, MLA_IN:MLA_IN + RWKV_IN].astype(BF16)
    w_g = w_in[:, MLA_IN + RWKV_IN:].astype(BF16)
    w_uq_p = jnp.pad(w_uq.reshape(MLA_Q_RANK, HEADS, MLA_NOPE + MLA_ROPE),
                     ((0, 0), (0, 0), (0, HEAD_PAD - MLA_NOPE - MLA_ROPE))).reshape(MLA_Q_RANK, QK_WIDTH).astype(BF16)
    w_ukv3 = w_ukv.reshape(MLA_KV_RANK, HEADS, MLA_NOPE + MLA_V)
    w_uk_p = jnp.pad(w_ukv3[..., :MLA_NOPE], ((0, 0), (0, 0), (0, HEAD_PAD - MLA_NOPE))
                     ).reshape(MLA_KV_RANK, QK_WIDTH).astype(BF16)
    w_uv_t = jnp.pad(w_ukv3[..., MLA_NOPE:], ((0, 0), (0, 0), (0, V_ROWS - MLA_V))
                     ).reshape(MLA_KV_RANK, HEADS * V_ROWS).T.astype(BF16)
    inv_freq = jnp.power(ROPE_THETA, -jnp.arange(0, MLA_ROPE, 2, dtype=F32) / MLA_ROPE)
    invf = jnp.concatenate([jnp.zeros((MLA_NOPE,), F32), inv_freq, inv_freq,
                            jnp.zeros((HEAD_PAD - MLA_NOPE - MLA_ROPE,), F32)]).reshape(1, LANES)

    tm = _pick_tile(seq, 512)
    q_all, k_all, vt_all, zs, gates = _inproj(
        x2, mod3, positions.reshape(batch, 1, seq), row1(g_pre_mix), w_c, w_r, w_g, row1(g_cq), row1(g_ckv),
        w_uq_p, w_uk_p, w_uv_t, row1(mu_shift), invf, batch=batch, seq=seq, tm=tm)

    attn = _attn(q_all, k_all, vt_all, batch=batch, seq=seq, tq=ATTN_BLOCK)

    tc = 2 * CHUNK * RWKV_CHUNKS_PER_STEP
    w_da = jnp.concatenate([
        jnp.concatenate([w_decay_up, jnp.zeros_like(w_decay_up)], axis=1),
        jnp.concatenate([jnp.zeros_like(w_a_up), w_a_up], axis=1)], axis=0).astype(BF16)
    hid = jnp.arange(2 * LANES) // RWKV_HEAD
    ones_bd = (hid[:, None] == hid[None, :]).astype(BF16)
    tid = jnp.arange(tc // 2)
    same_chunk = (tid[:, None] // CHUNK) == (tid[None, :] // CHUNK)
    tri = (same_chunk & (tid[None, :] <= tid[:, None])).astype(BF16)
    rw = _rwkv(zs, w_da, w_g_up.astype(BF16), row1(decay_bias), row1(a_bias), row1(k_k), row1(k_a), row1(r_k),
               row1(lnx_w), row1(lnx_b), ones_bd, tri, batch=batch, seq=seq, tc=tc)

    w_rt = jnp.concatenate([w_router_expert, w_router_group,
                            jnp.zeros((d, ROUTER_WIDTH - N_EXPERTS - N_GROUPS), F32)], axis=1)
    wr_hi = w_rt.astype(BF16)
    wr_lo = (w_rt - wr_hi.astype(F32)).astype(BF16)
    b_r = jnp.concatenate([b_router_expert, b_router_group,
                           jnp.zeros((ROUTER_WIDTH - N_EXPERTS - N_GROUPS,), F32)]).reshape(1, ROUTER_WIDTH)
    x1, hp, dest, cnt = _merge(attn, rw, gates, x2, mod3, w_o_mla.astype(BF16), w_o_rwkv.astype(BF16),
                               w_out.astype(BF16), row1(g_post_mix), row1(g_pre_ffn), wr_hi, wr_lo, b_r,
                               seq=seq, tm=_pick_tile(seq, 512))

    tmx = _pick_tile(seq, 512)
    dest = dest.reshape(t)
    xs = _dispatch(dest, hp, tokens_out=N_GROUPS * t, td=_pick_tile(seq, 2048))
    tile_group, tile_blk, tile_rows = _tile_map(cnt[0, :N_GROUPS].astype(jnp.int32), capacity=t, tmx=tmx)
    eid = jnp.arange(EXPERTS_PER_GROUP * EXPERT_FF) // EXPERT_FF
    expand = (jnp.arange(ROUTER_WIDTH)[:, None] == eid[None, :]).astype(BF16)
    ys = _experts(tile_group, tile_blk, tile_rows, xs, w_exp_gate.astype(BF16), w_exp_up.astype(BF16),
                  w_exp_down.astype(BF16), expand, tmx=tmx)
    out = _final(dest, x1, mod3, row1(g_post_ffn), ys, seq=seq, tmf=_pick_tile(seq, 1024))
    return out.reshape(batch, seq, d)


def kernel(x, c, positions, w_ada, b_ada, g_pre_mix, g_post_mix, g_pre_ffn, g_post_ffn, w_in, g_cq, w_uq, g_ckv, w_ukv, w_o_mla, mu_shift, w_decay_up, decay_bias, w_a_up, a_bias, w_g_up, k_k, k_a, r_k, lnx_w, lnx_b, w_o_rwkv, w_out, w_router_group, b_router_group, w_router_expert, b_router_expert, w_exp_gate, w_exp_up, w_exp_down):
    depth = w_ada.shape[0]
    for l in range(depth):
        x = _layer(x, c, positions, w_ada[l], b_ada[l], g_pre_mix[l], g_post_mix[l], g_pre_ffn[l], g_post_ffn[l],
                   w_in[l], g_cq[l], w_uq[l], g_ckv[l], w_ukv[l], w_o_mla[l], mu_shift[l], w_decay_up[l],
                   decay_bias[l], w_a_up[l], a_bias[l], w_g_up[l], k_k[l], k_a[l], r_k[l], lnx_w[l], lnx_b[l],
                   w_o_rwkv[l], w_out[l], w_router_group[l], b_router_group[l], w_router_expert[l],
                   b_router_expert[l], w_exp_gate[l], w_exp_up[l], w_exp_down[l])
    return x
```

```python
import functools
import math

import jax
import jax.numpy as jnp
from jax import lax
from jax.experimental import pallas as pl
from jax.experimental.pallas import tpu as pltpu

F32 = jnp.float32
BF16 = jnp.bfloat16

D_MODEL = 1024
CHUNK = 64
HEADS = 8
MLA_NOPE = 64
MLA_ROPE = 32
MLA_V = 64
MLA_Q_RANK = 384
MLA_KV_RANK = 256
ROPE_THETA = 10000.0
RWKV_HEAD = 64
RWKV_WIDTH = HEADS * RWKV_HEAD
DECAY_RANK = 64
AAA_RANK = 64
GATE_RANK = 128
GN_EPS = 64e-5
N_GROUPS = 4
EXPERTS_PER_GROUP = 8
N_EXPERTS = N_GROUPS * EXPERTS_PER_GROUP
EXPERT_FF = 256
RMS_EPS = 1e-6
NEG_INF = -1e30
MLA_IN = MLA_Q_RANK + MLA_KV_RANK + MLA_ROPE
RWKV_IN = 3 * RWKV_WIDTH + DECAY_RANK + AAA_RANK + GATE_RANK

LANES = 128
DMA_UNROLL = 8
ROW_CHUNKS = 9
HEAD_PAD = LANES
QK_WIDTH = HEADS * HEAD_PAD
MLA_C_WIDTH = 768
ROUTER_WIDTH = LANES
VMEM_LIMIT = 56 * 1024 * 1024
ATTN_HEADS_PER_STEP = 8
ATTN_BLOCK = 256
ATTN_BLOCKS_PER_ITER = 2
V_ROWS = 80
RWKV_CHUNKS_PER_STEP = 4
MERGE_SUBTILES = 2


def _dot(a, b):
    return jnp.dot(a, b, preferred_element_type=F32)


def _dot_nt(a, b):
    return lax.dot_general(a, b, (((1,), (1,)), ((), ())), preferred_element_type=F32)


def _dot_tn(a, b):
    return lax.dot_general(a, b, (((0,), (0,)), ((), ())), preferred_element_type=F32)


def _split(x):
    hi = x.astype(BF16)
    lo = (x - hi.astype(F32)).astype(BF16)
    return hi, lo


def _dot3(x, w_hi, w_lo):
    x_hi, x_lo = _split(x)
    return _dot(x_hi, w_hi) + (_dot(x_hi, w_lo) + _dot(x_lo, w_hi))


def _sigmoid(x):
    return 1.0 / (1.0 + jnp.exp(-x))


def _rms(x, g):
    return x * lax.rsqrt(jnp.mean(x * x, axis=-1, keepdims=True) + RMS_EPS) * g


def _ada_kernel(c_ref, w_ref, b_ref, o_ref):
    c = c_ref[...]
    s = c * _sigmoid(c)
    w_hi, w_lo = _split(w_ref[...])
    o_ref[...] = _dot3(s, w_hi, w_lo) + b_ref[...]


def _ada(c, w_ada, b_ada):
    b, d = c.shape
    n = w_ada.shape[1]
    tn = 512
    return pl.pallas_call(
        _ada_kernel,
        grid=(n // tn,),
        in_specs=[pl.BlockSpec((b, d), lambda j: (0, 0)),
                  pl.BlockSpec((d, tn), lambda j: (0, j)),
                  pl.BlockSpec((1, tn), lambda j: (0, j))],
        out_specs=pl.BlockSpec((b, tn), lambda j: (0, j)),
        out_shape=jax.ShapeDtypeStruct((b, n), F32),
        name="ada",
    )(c, w_ada, b_ada.reshape(1, n))


def _inproj_kernel(x_ref, mod_ref, pos_ref, gpre_ref, wc_ref, wr_ref, wg_ref, gcq_ref, gckv_ref,
                   wuq_ref, wuk_ref, wuv_ref, mu_ref, invf_ref,
                   q_ref, k_ref, vt_ref, zs_ref, gate_ref, carry_ref):
    i = pl.program_id(1)
    tm = x_ref.shape[0]

    @pl.when(i == 0)
    def _():
        carry_ref[...] = jnp.zeros_like(carry_ref)

    mod = mod_ref[0]
    shift, scale = mod[0:1], mod[1:2]
    h = (_rms(x_ref[...], gpre_ref[...]) * (1.0 + scale) + shift).astype(BF16)

    zc = _dot(h, wc_ref[...])
    zg = _dot(h, wg_ref[...])
    nq = _rms(zc[:, :MLA_Q_RANK], gcq_ref[...]).astype(BF16)
    nkv = _rms(zc[:, MLA_Q_RANK:MLA_Q_RANK + MLA_KV_RANK], gckv_ref[...]).astype(BF16)
    kr = zc[:, MLA_Q_RANK + MLA_KV_RANK:]
    qf = _dot(nq, wuq_ref[...])
    kn = _dot(nkv, wuk_ref[...])
    vt = _dot_nt(wuv_ref[...], nkv)
    zr = _dot(h, wr_ref[...])

    pos_col = jnp.broadcast_to(pos_ref[...].astype(F32), (LANES, tm)).T
    ang = pos_col * invf_ref[...]
    lane = lax.broadcasted_iota(jnp.int32, (1, LANES), 1)
    first_half = lane < MLA_NOPE + MLA_ROPE // 2
    in_rope = (lane >= MLA_NOPE) & (lane < MLA_NOPE + MLA_ROPE)
    cos_t = jnp.where(in_rope, jnp.cos(ang), 1.0)
    sin_a = jnp.sin(ang)
    sin_t = jnp.where(in_rope, jnp.where(first_half, -sin_a, sin_a), 0.0)

    def rope(t):
        rot = jnp.where(first_half, pltpu.roll(t, LANES - MLA_ROPE // 2, 1), pltpu.roll(t, MLA_ROPE // 2, 1))
        return t * cos_t + rot * sin_t

    kr_rot = rope(kr)
    q_scale = math.log2(math.e) / math.sqrt(MLA_NOPE + MLA_ROPE)
    for hh in range(HEADS):
        sl = slice(hh * HEAD_PAD, (hh + 1) * HEAD_PAD)
        q_ref[:, sl] = (rope(qf[:, sl]) * q_scale).astype(BF16)
        k_ref[:, sl] = (kn[:, sl] + kr_rot).astype(BF16)
    vrow = lax.broadcasted_iota(jnp.int32, vt.shape, 0) % V_ROWS
    vt = jnp.where(vrow == MLA_V, 1.0, vt).astype(BF16)
    for j in range(vt_ref.shape[0]):
        vt_ref[j] = vt[:, j * ATTN_BLOCK:(j + 1) * ATTN_BLOCK]

    gate_ref[...] = _sigmoid(zg).astype(BF16)

    row = lax.broadcasted_iota(jnp.int32, (tm, 1), 0)
    prev = jnp.where(row == 0, carry_ref[0:1, :], pltpu.roll(zr, 1, 0))
    carry_ref[0:1, :] = zr[tm - 1:tm, :]
    zs_ref[...] = (zr + (prev - zr) * mu_ref[...]).astype(BF16)


def _const_spec(shape):
    nd = len(shape)
    return pl.BlockSpec(shape, lambda *_: (0,) * nd, pipeline_mode=pl.Buffered(1))


def _inproj(x2, mod3, pos2, g_pre, w_c, w_r, w_g, g_cq, g_ckv, w_uq, w_uk, w_uv, mu, invf, *, batch, seq, tm):
    t, d = x2.shape
    ns = seq // tm
    row = lambda b, i: (b * ns + i, 0)
    widths = [QK_WIDTH, QK_WIDTH, RWKV_IN, 2 * d]
    specs = [pl.BlockSpec((tm, w), row) for w in widths]
    shapes = [jax.ShapeDtypeStruct((t, w), BF16) for w in widths]
    vw = HEADS * V_ROWS
    nb = tm // ATTN_BLOCK
    specs.insert(2, pl.BlockSpec((None, nb, vw, ATTN_BLOCK), lambda b, i: (b, i, 0, 0)))
    shapes.insert(2, jax.ShapeDtypeStruct((batch, ns * nb, vw, ATTN_BLOCK), BF16))
    return pl.pallas_call(
        _inproj_kernel,
        grid=(batch, ns),
        in_specs=[pl.BlockSpec((tm, d), row),
                  pl.BlockSpec((1, 6, d), lambda b, i: (b, 0, 0)),
                  pl.BlockSpec((None, 1, tm), lambda b, i: (b, 0, i)),
                  _const_spec(g_pre.shape), _const_spec(w_c.shape), _const_spec(w_r.shape),
                  _const_spec(w_g.shape), _const_spec(g_cq.shape), _const_spec(g_ckv.shape),
                  _const_spec(w_uq.shape), _const_spec(w_uk.shape), _const_spec(w_uv.shape),
                  _const_spec(mu.shape), _const_spec(invf.shape)],
        out_specs=specs,
        out_shape=shapes,
        scratch_shapes=[pltpu.VMEM((8, RWKV_IN), F32)],
        compiler_params=pltpu.CompilerParams(dimension_semantics=("parallel", "arbitrary"),
                                             vmem_limit_bytes=VMEM_LIMIT),
        name="inproj",
    )(x2, mod3, pos2, g_pre, w_c, w_r, w_g, g_cq, g_ckv, w_uq, w_uk, w_uv, mu, invf)


def _attn_kernel(q_ref, k_ref, vt_ref, o_ref):
    i = pl.program_id(2)
    tq = q_ref.shape[0]
    kc = lax.broadcasted_iota(jnp.int32, (tq, tq), 0) // CHUNK
    qc = lax.broadcasted_iota(jnp.int32, (tq, tq), 1) // CHUNK
    diag_mask = kc <= qc
    heads = range(ATTN_HEADS_PER_STEP)
    hs = [slice(h * HEAD_PAD, (h + 1) * HEAD_PAD) for h in heads]
    vr = [slice(h * V_ROWS, (h + 1) * V_ROWS) for h in heads]
    q = [q_ref[:, s] for s in hs]

    def block(kb, carry, nblk, masked):
        kbs = [kb + n for n in range(nblk)]
        rows = [pl.ds(pl.multiple_of(b * tq, tq), tq) for b in kbs]
        s = [[_dot_nt(k_ref[r, hs[h]], q[h]) for r in rows] for h in heads]
        if masked:
            s = [[jnp.where(diag_mask, t, NEG_INF) for t in sh] for sh in s]
        m_new = []
        for h in heads:
            m = carry[h][0]
            for t in s[h]:
                m = jnp.maximum(m, jnp.max(t, axis=0, keepdims=True))
            m_new.append(m)
        alpha = [jnp.exp2(carry[h][0] - m_new[h]) for h in heads]
        p = [jnp.concatenate([jnp.exp2(t - m_new[h]).astype(BF16) for t in s[h]], axis=0) for h in heads]
        acc = [alpha[h] * carry[h][1]
               + _dot(jnp.concatenate([vt_ref[b, vr[h], :] for b in kbs], axis=1), p[h]) for h in heads]
        return tuple((m_new[h], acc[h]) for h in heads)

    init = tuple((jnp.full((1, tq), NEG_INF, F32), jnp.zeros((V_ROWS, tq), F32)) for _ in heads)
    nb = ATTN_BLOCKS_PER_ITER
    carry = lax.fori_loop(0, i // nb, lambda n, c: block(n * nb, c, nb, False), init)
    carry = lax.fori_loop((i // nb) * nb, i, lambda kb, c: block(kb, c, 1, False), carry)
    final = block(i, carry, 1, True)
    outs = [final[h][1][:MLA_V] / final[h][1][MLA_V:MLA_V + 1] for h in heads]
    for pr in range(ATTN_HEADS_PER_STEP // 2):
        pair_t = jnp.concatenate([outs[2 * pr], outs[2 * pr + 1]], axis=0)
        o_ref[:, pr * LANES:(pr + 1) * LANES] = pair_t.T.astype(BF16)


def _attn(q_all, k_all, vt_all, *, batch, seq, tq):
    t = q_all.shape[0]
    nq = seq // tq
    g = ATTN_HEADS_PER_STEP
    return pl.pallas_call(
        _attn_kernel,
        grid=(batch, HEADS // g, nq),
        in_specs=[pl.BlockSpec((tq, g * HEAD_PAD), lambda b, p, i: (b * nq + i, p)),
                  pl.BlockSpec((seq, g * HEAD_PAD), lambda b, p, i: (b, p)),
                  pl.BlockSpec((None, nq, g * V_ROWS, tq), lambda b, p, i: (b, 0, p, 0))],
        out_specs=pl.BlockSpec((tq, g * MLA_V), lambda b, p, i: (b * nq + i, p)),
        out_shape=jax.ShapeDtypeStruct((t, HEADS * MLA_V), BF16),
        compiler_params=pltpu.CompilerParams(dimension_semantics=("parallel", "parallel", "arbitrary"),
                                             vmem_limit_bytes=VMEM_LIMIT),
        name="attn",
    )(q_all, k_all, vt_all)


def _rwkv_kernel(zs_ref, wda_ref, wgu_ref, dbias_ref, abias_ref, kk_ref, ka_ref, rk_ref, lnw_ref, lnb_ref,
                 ones_ref, tri_ref,
                 o_ref,
                 st_ref, a_s, r_s, b_s, k_s, bg_s, kg_s, v_s, gc_s, bonus_s, g_s, y_s):
    w = RWKV_WIDTH
    nsub = RWKV_CHUNKS_PER_STEP
    th = nsub * CHUNK
    n2 = 2 * CHUNK
    staged = (a_s, r_s, b_s, k_s, bg_s, kg_s, v_s, gc_s, bonus_s, g_s)

    @pl.when(pl.program_id(1) == 0)
    def _():
        st_ref[...] = jnp.zeros_like(st_ref)

    ones_bd = ones_ref[...]

    def head_sum(t):
        tb = t.astype(BF16)
        hw = ones_bd.shape[0]
        return jnp.concatenate([_dot(tb[:, n * hw:(n + 1) * hw], ones_bd) for n in range(w // hw)], axis=1)

    ri = lax.broadcasted_iota(jnp.int32, (4 * CHUNK, n2), 0)
    ci = lax.broadcasted_iota(jnp.int32, (4 * CHUNK, n2), 1) % CHUNK
    tri_mask = (ri % CHUNK + ri // n2) > ci
    lane2 = lax.broadcasted_iota(jnp.int32, (CHUNK, n2), 1)
    h1 = lane2 < RWKV_HEAD
    bi = lax.broadcasted_iota(jnp.int32, (n2, n2), 0) // RWKV_HEAD
    bj = lax.broadcasted_iota(jnp.int32, (n2, n2), 1) // RWKV_HEAD
    bd_mask = bi == bj
    zero_bf = jnp.zeros((CHUNK, n2), BF16)
    zero_f = jnp.zeros((CHUNK, n2), F32)
    lane = lax.broadcasted_iota(jnp.int32, (1, LANES), 1)

    def prepare(h):
        hr = slice(h * th, (h + 1) * th)
        zs = zs_ref[hr, :].astype(F32)
        r, k, v = zs[:, 0:w], zs[:, w:2 * w], zs[:, 2 * w:3 * w]
        da = zs[:, 3 * w:3 * w + LANES]
        gd = zs[:, 3 * w + LANES:]
        lora_in = jnp.where(lane < DECAY_RANK, jnp.tanh(da), da).astype(BF16)
        pre = _dot(lora_in, wda_ref[...])
        g_new = _dot(_sigmoid(gd).astype(BF16), wgu_ref[...])
        yield
        u = -(dbias_ref[...] + pre[:, :w])
        softplus = jnp.maximum(u, 0.0) + jnp.log(1.0 + jnp.exp(-jnp.abs(u)))
        logw = -jnp.exp(-softplus - 0.5)
        eta = _sigmoid(abias_ref[...] + pre[:, w:])
        kk = k * kk_ref[...]
        kk_ss = head_sum(kk * kk)
        lw_hi, lw_lo = _split(logw)
        cum = _dot(tri_ref[...], lw_hi) + _dot(tri_ref[...], lw_lo)
        yield
        kk = kk * jnp.minimum(lax.rsqrt(kk_ss), 1e12)
        kp = k * (1.0 + (eta - 1.0) * ka_ref[...])
        bonus_new = head_sum(r * kp * rk_ref[...]) * v
        yield
        cum_end = jnp.concatenate([jnp.broadcast_to(cum[(c + 1) * CHUNK - 1:(c + 1) * CHUNK, :], (CHUNK, w))
                                   for c in range(nsub)], axis=0)
        b_in = kk * eta
        g_inv = jnp.exp(-cum)
        g_rem = jnp.exp(cum_end - cum)
        staged_new = ((-kk * jnp.exp(cum - logw)).astype(BF16), (r * jnp.exp(cum)).astype(BF16),
                      (b_in * g_inv).astype(BF16), (kp * g_inv).astype(BF16),
                      (b_in * g_rem).astype(BF16), (kp * g_rem).astype(BF16), v.astype(BF16),
                      jnp.exp(cum_end), bonus_new, g_new)
        for ref, val in zip(staged, staged_new):
            ref[hr, :] = val
        yield

    def solve(h):
        blocks = [(sub, p) for sub in range(nsub) for p in range(HEADS // 2)]
        rows = [slice(h * th + sub * CHUNK, h * th + (sub + 1) * CHUNK) for sub in range(nsub)]
        lns = [slice(p * n2, (p + 1) * n2) for p in range(HEADS // 2)]
        a_t = [a_s[rows[sub], lns[p]] for sub, p in blocks]
        r_t = [r_s[rows[sub], lns[p]] for sub, p in blocks]
        v_t = [v_s[rows[sub], lns[p]] for sub, p in blocks]
        ml = []
        for n, (sub, p) in enumerate(blocks):
            lhs = jnp.concatenate([jnp.where(h1, a_t[n], zero_bf), jnp.where(h1, zero_bf, a_t[n]),
                                   jnp.where(h1, r_t[n], zero_bf), jnp.where(h1, zero_bf, r_t[n])], axis=0)
            rhs = jnp.concatenate([b_s[rows[sub], lns[p]], k_s[rows[sub], lns[p]]], axis=0)
            ml.append(jnp.where(tri_mask, _dot_nt(lhs, rhs), 0.0))
        ml_b = [m.astype(BF16) for m in ml]
        yield
        units = [(n, hh) for n in range(len(blocks)) for hh in range(2)]
        w_t = [_dot(ml_b[n][hh * CHUNK:(hh + 1) * CHUNK, :], jnp.concatenate([zero_bf, v_t[n]], axis=0))
               for n, hh in units]
        a_f = [t.astype(F32) for t in a_t]
        a_sw = [pltpu.roll(t, RWKV_HEAD, 1) for t in a_f]
        x = [jnp.where(h1, a_f[n], pltpu.roll(w_t[2 * n], RWKV_HEAD, 1)) if hh == 0
             else jnp.where(h1, a_sw[n], w_t[2 * n + 1]) for n, hh in units]
        lsq = [jnp.where(h1, ml[n][hh * CHUNK:(hh + 1) * CHUNK, :], 0.0) for n, hh in units]
        yield
        for _ in range(6):
            out = [_dot(lsq[u][:, :CHUNK].astype(BF16),
                        jnp.concatenate([x[u], lsq[u]], axis=1).astype(BF16)) for u in range(len(units))]
            x = [x[u] + out[u][:, :n2] for u in range(len(units))]
            lsq = [t[:, n2:] for t in out]
            yield
        z = []
        for n in range(len(blocks)):
            x0, x1 = x[2 * n], x[2 * n + 1]
            a_hat = jnp.where(h1, x0, pltpu.roll(x1, RWKV_HEAD, 1))
            u0 = jnp.where(h1, pltpu.roll(x0, RWKV_HEAD, 1), x1)
            z.append(jnp.concatenate([jnp.concatenate([a_hat, u0], axis=1),
                                      jnp.concatenate([zero_f, v_t[n].astype(F32)], axis=1)],
                                     axis=0).astype(BF16))
        o0 = [_dot(ml_b[n][2 * CHUNK:3 * CHUNK, :], z[n]) for n in range(len(blocks))]
        o1 = [_dot(ml_b[n][3 * CHUNK:4 * CHUNK, :], z[n]) for n in range(len(blocks))]
        pm = [_dot_tn(z[n], jnp.concatenate([bg_s[rows[sub], lns[p]], kg_s[rows[sub], lns[p]]], axis=0))
              for n, (sub, p) in enumerate(blocks)]
        r_hat = [(r_t[n].astype(F32) + jnp.where(h1, o0[n][:, :n2], o1[n][:, :n2])).astype(BF16)
                 for n in range(len(blocks))]
        yield
        state = [st_ref[p] for p in range(HEADS // 2)]
        for n, (sub, p) in enumerate(blocks):
            y0 = jnp.where(h1, o0[n][:, n2:], o1[n][:, n2:])
            g_t = jnp.where(bd_mask, pm[n][:n2], 0.0).astype(BF16)
            h_t = jnp.where(bd_mask, pm[n][n2:], 0.0)
            s_b = state[p].astype(BF16)
            y_s[rows[sub], lns[p]] = _dot_nt(r_hat[n], s_b) + y0
            gc = gc_s[rows[sub].start:rows[sub].start + 1, lns[p]]
            state[p] = gc * state[p] + _dot(s_b, g_t) + h_t
        for p in range(HEADS // 2):
            st_ref[p] = state[p]
        yield

    def finish(h):
        hr = slice(h * th, (h + 1) * th)
        y = y_s[hr, :]
        inv_n = 1.0 / RWKV_HEAD
        dev = y - head_sum(y) * inv_n
        yield
        var = head_sum(dev * dev) * inv_n
        yield
        out = (dev * lax.rsqrt(var + GN_EPS) * lnw_ref[...] + lnb_ref[...] + bonus_s[hr, :]) * g_s[hr, :]
        o_ref[hr, :] = out.astype(BF16)
        yield

    def emit(main, side=None, every=1):
        n = 0
        for _ in main:
            n += 1
            if side is not None and n % every == 0:
                next(side, None)
        if side is not None:
            for _ in side:
                pass

    emit(prepare(0))
    emit(solve(0), prepare(1), every=2)
    emit(solve(1), finish(0), every=3)
    emit(finish(1))


def _rwkv(zs, w_da, w_gu, dbias, abias, k_k, k_a, r_k, lnw, lnb, ones_bd, tri, *, batch, seq, tc):
    t = zs.shape[0]
    nc = seq // tc
    w = RWKV_WIDTH
    consts = [w_da, w_gu, dbias, abias, k_k, k_a, r_k, lnw, lnb, ones_bd, tri]
    return pl.pallas_call(
        _rwkv_kernel,
        grid=(batch, nc),
        in_specs=[pl.BlockSpec((tc, RWKV_IN), lambda b, i: (b * nc + i, 0))] + [_const_spec(a.shape) for a in consts],
        out_specs=pl.BlockSpec((tc, w), lambda b, i: (b * nc + i, 0)),
        out_shape=jax.ShapeDtypeStruct((t, w), BF16),
        scratch_shapes=[pltpu.VMEM((HEADS // 2, 2 * RWKV_HEAD, 2 * RWKV_HEAD), F32)]
        + [pltpu.VMEM((tc, w), BF16)] * 7
        + [pltpu.VMEM((tc, w), F32)] * 4,
        compiler_params=pltpu.CompilerParams(dimension_semantics=("parallel", "arbitrary"),
                                             vmem_limit_bytes=VMEM_LIMIT),
        name="rwkv",
    )(zs, *consts)


def _route(logits):
    lane = lax.broadcasted_iota(jnp.int32, logits.shape, 1)
    lane_f = lane.astype(F32)
    big = float(ROUTER_WIDTH)
    is_group = (lane >= N_EXPERTS) & (lane < N_EXPERTS + N_GROUPS)
    gl = jnp.where(is_group, logits, NEG_INF)
    g_max = jnp.max(gl, axis=-1, keepdims=True)
    g_sel = jnp.min(jnp.where(gl == g_max, lane_f, big), axis=-1, keepdims=True) - float(N_EXPERTS)
    p_sel = 1.0 / jnp.sum(jnp.where(is_group, jnp.exp(gl - g_max), 0.0), axis=-1, keepdims=True)
    in_group = (lane < N_EXPERTS) & ((lane // EXPERTS_PER_GROUP).astype(F32) == g_sel)
    el = jnp.where(in_group, logits, NEG_INF)
    t1 = jnp.max(el, axis=-1, keepdims=True)
    i1 = jnp.min(jnp.where(el == t1, lane_f, big), axis=-1, keepdims=True)
    el2 = jnp.where(lane_f == i1, NEG_INF, el)
    t2 = jnp.max(el2, axis=-1, keepdims=True)
    i2 = jnp.min(jnp.where(el2 == t2, lane_f, big), axis=-1, keepdims=True)
    e21 = jnp.exp(t2 - t1)
    w1 = p_sel / (1.0 + e21)
    w2 = w1 * e21
    first = g_sel * float(EXPERTS_PER_GROUP)
    cw = jnp.where(lane_f == i1 - first, w1, 0.0) + jnp.where(lane_f == i2 - first, w2, 0.0)
    return g_sel, cw


def _merge_kernel(attn_ref, rw_ref, gate_ref, x_ref, mod_ref, womla_ref, worw_ref, wout_ref,
                  gpost_ref, gffn_ref, wrh_ref, wrl_ref, br_ref, stril_ref,
                  x1_ref, hp_ref, dest_ref, cnt_ref, carry_ref, *, group_capacity):
    i = pl.program_id(0)
    tm, d = x_ref.shape

    @pl.when(i == 0)
    def _():
        carry_ref[...] = jnp.zeros_like(carry_ref)

    mod = mod_ref[0]
    gate1, shift2, scale2 = mod[2:3], mod[3:4], mod[4:5]

    ts = tm // MERGE_SUBTILES
    subs = [slice(n * ts, (n + 1) * ts) for n in range(MERGE_SUBTILES)]
    o_mla = [_dot(attn_ref[r, :], womla_ref[...]) for r in subs]
    o_rw = [_dot(rw_ref[r, :], worw_ref[...]) for r in subs]
    o = [(gate_ref[r, :d].astype(F32) * o_mla[n] + gate_ref[r, d:].astype(F32) * o_rw[n]).astype(BF16)
         for n, r in enumerate(subs)]
    y = [_dot(t, wout_ref[...]) for t in o]
    x1 = [x_ref[r, :] + gate1 * _rms(y[n], gpost_ref[...]) for n, r in enumerate(subs)]
    h2 = [_rms(t, gffn_ref[...]) * (1.0 + scale2) + shift2 for t in x1]
    logits = [_dot3(t, wrh_ref[...], wrl_ref[...]) + br_ref[...] for t in h2]
    for n, r in enumerate(subs):
        x1_ref[r, :] = x1[n]
        for c in range(d // LANES):
            hp_ref[pl.ds(n * ts * ROW_CHUNKS + c, ts, stride=ROW_CHUNKS), :] = h2[n][:, c * LANES:(c + 1) * LANES]

    lane_f = lax.broadcasted_iota(jnp.int32, (ts, ROUTER_WIDTH), 1).astype(F32)
    routes = [_route(t) for t in logits]
    g_sels = [g for g, _ in routes]
    for n, r in enumerate(subs):
        hp_ref[pl.ds(n * ts * ROW_CHUNKS + d // LANES, ts, stride=ROW_CHUNKS), :] = routes[n][1]
    onehots = [jnp.where(lane_f == g, 1.0, 0.0) for g in g_sels]

    onehot = jnp.concatenate(onehots, axis=0)
    earlier = _dot(stril_ref[...], onehot.astype(BF16))
    carry = carry_ref[0:1, :]
    for n, r in enumerate(subs):
        rank = jnp.sum(jnp.where(lane_f == g_sels[n], carry + earlier[r, :], 0.0), axis=-1, keepdims=True)
        dest_col = jnp.broadcast_to(g_sels[n] * float(group_capacity) + rank, (ts, LANES))
        dest_ref[:, r] = dest_col.T[0:1, :].astype(jnp.int32)
    total = jnp.broadcast_to(carry + jnp.sum(onehot, axis=0, keepdims=True), carry_ref.shape)
    carry_ref[...] = total
    cnt_ref[...] = total


def _merge(attn, rw, gates, x2, mod3, w_o_mla, w_o_rwkv, w_out, g_post, g_ffn, wr_hi, wr_lo, b_r, *, seq, tm):
    t, d = x2.shape
    ns = seq // tm
    row = lambda i: (i, 0)
    tid = jnp.arange(tm)
    stril = (tid[None, :] < tid[:, None]).astype(BF16)
    consts = [w_o_mla, w_o_rwkv, w_out, g_post, g_ffn, wr_hi, wr_lo, b_r, stril]
    return pl.pallas_call(
        functools.partial(_merge_kernel, group_capacity=t),
        grid=(t // tm,),
        in_specs=[pl.BlockSpec((tm, attn.shape[1]), row), pl.BlockSpec((tm, rw.shape[1]), row),
                  pl.BlockSpec((tm, 2 * d), row), pl.BlockSpec((tm, d), row),
                  pl.BlockSpec((1, 6, d), lambda i: (i // ns, 0, 0))] + [_const_spec(a.shape) for a in consts],
        out_specs=[pl.BlockSpec((tm, d), row), pl.BlockSpec((tm * ROW_CHUNKS, LANES), row),
                   pl.BlockSpec((None, 1, tm), lambda i: (i, 0, 0)), pl.BlockSpec((8, ROUTER_WIDTH), lambda i: (0, 0))],
        out_shape=[jax.ShapeDtypeStruct((t, d), F32), jax.ShapeDtypeStruct((t * ROW_CHUNKS, LANES), F32),
                   jax.ShapeDtypeStruct((t // tm, 1, tm), jnp.int32), jax.ShapeDtypeStruct((8, ROUTER_WIDTH), F32)],
        scratch_shapes=[pltpu.VMEM((8, ROUTER_WIDTH), F32)],
        compiler_params=pltpu.CompilerParams(dimension_semantics=("arbitrary",), vmem_limit_bytes=VMEM_LIMIT),
        name="merge",
    )(attn, rw, gates, x2, mod3, *consts)


def _token_rows(i):
    return pl.ds(i * ROW_CHUNKS, ROW_CHUNKS)


def _chunk(ref, c, n):
    return ref[pl.ds(c, n, stride=ROW_CHUNKS), :]


def _dispatch_kernel(dest_ref, hp_ref, xs_ref, sem):
    td = hp_ref.shape[0] // ROW_CHUNKS
    base = pl.program_id(0) * td

    def issue(r0, c):
        for j in range(DMA_UNROLL):
            r = r0 * DMA_UNROLL + j
            pltpu.make_async_copy(hp_ref.at[_token_rows(r)], xs_ref.at[_token_rows(dest_ref[base + r])],
                                  sem).start(priority=j % 2)
        return c

    lax.fori_loop(0, td // DMA_UNROLL, issue, 0)
    pltpu.make_async_copy(hp_ref, xs_ref.at[pl.ds(0, td * ROW_CHUNKS)], sem).wait()


def _dispatch(dest, hp, *, tokens_out, td):
    t = hp.shape[0] // ROW_CHUNKS
    return pl.pallas_call(
        _dispatch_kernel,
        grid_spec=pltpu.PrefetchScalarGridSpec(
            num_scalar_prefetch=1, grid=(t // td,),
            in_specs=[pl.BlockSpec((td * ROW_CHUNKS, LANES), lambda i, dest: (i, 0))],
            out_specs=pl.BlockSpec(memory_space=pl.ANY),
            scratch_shapes=[pltpu.SemaphoreType.DMA]),
        out_shape=jax.ShapeDtypeStruct((tokens_out * ROW_CHUNKS, LANES), F32),
        compiler_params=pltpu.CompilerParams(dimension_semantics=("arbitrary",), vmem_limit_bytes=VMEM_LIMIT),
        name="dispatch",
    )(dest, hp)


def _experts_kernel(tg_ref, tb_ref, tr_ref, xs_ref, wg_ref, wu_ref, wd_ref, ex_ref, ys_ref):
    del tg_ref, tb_ref
    nrows = tr_ref[pl.program_id(0)]

    @pl.when(nrows > 0)
    def _():
        tmx = xs_ref.shape[0] // ROW_CHUNKS
        nchunk = D_MODEL // LANES
        valid = lax.broadcasted_iota(jnp.int32, (tmx, 1), 0) < nrows
        x = jnp.concatenate([jnp.where(valid, _chunk(xs_ref, c, tmx), 0.0).astype(BF16) for c in range(nchunk)],
                            axis=1)
        cw_hi, cw_lo = _split(jnp.where(valid, _chunk(xs_ref, nchunk, tmx), 0.0))
        cwx = _dot(cw_hi, ex_ref[...]) + _dot(cw_lo, ex_ref[...])
        acc = jnp.zeros((tmx, D_MODEL), F32)
        for e in range(EXPERTS_PER_GROUP):
            a = _dot(x, wg_ref[e])
            act = a * _sigmoid(a) * _dot(x, wu_ref[e]) * cwx[:, e * EXPERT_FF:(e + 1) * EXPERT_FF]
            acc = acc + _dot(act.astype(BF16), wd_ref[e])
        for c in range(nchunk):
            ys_ref[pl.ds(c, tmx, stride=ROW_CHUNKS), :] = acc[:, c * LANES:(c + 1) * LANES]
        ys_ref[pl.ds(nchunk, tmx, stride=ROW_CHUNKS), :] = jnp.zeros((tmx, LANES), F32)


def _experts(tile_group, tile_blk, tile_rows, xs, w_gate, w_up, w_down, expand, *, tmx):
    nt = tile_group.shape[0]
    d = D_MODEL
    grp = lambda j, tg, tb, tr: (tg[j], 0, 0)
    tile = pl.BlockSpec((tmx * ROW_CHUNKS, LANES), lambda j, tg, tb, tr: (tb[j], 0))
    return pl.pallas_call(
        _experts_kernel,
        grid_spec=pltpu.PrefetchScalarGridSpec(
            num_scalar_prefetch=3, grid=(nt,),
            in_specs=[tile,
                      pl.BlockSpec((EXPERTS_PER_GROUP, d, EXPERT_FF), grp),
                      pl.BlockSpec((EXPERTS_PER_GROUP, d, EXPERT_FF), grp),
                      pl.BlockSpec((EXPERTS_PER_GROUP, EXPERT_FF, d), grp),
                      pl.BlockSpec(expand.shape, lambda j, tg, tb, tr: (0, 0))],
            out_specs=tile),
        out_shape=jax.ShapeDtypeStruct(xs.shape, F32),
        compiler_params=pltpu.CompilerParams(dimension_semantics=("arbitrary",), vmem_limit_bytes=VMEM_LIMIT),
        name="experts",
    )(tile_group, tile_blk, tile_rows, xs, w_gate, w_up, w_down, expand)


def _final_kernel(dest_ref, x1_ref, mod_ref, gpost_ref, ys_ref, o_ref, ybuf, sems):
    tmf, d = x1_ref.shape
    i = pl.program_id(0)
    slot = i % 2

    def gather(tile, into):
        base = tile * tmf

        def issue(r0, c):
            for j in range(DMA_UNROLL):
                r = r0 * DMA_UNROLL + j
                pltpu.make_async_copy(ys_ref.at[_token_rows(dest_ref[base + r])], ybuf.at[into, _token_rows(r)],
                                      sems.at[into]).start(priority=j % 2)
            return c

        lax.fori_loop(0, tmf // DMA_UNROLL, issue, 0)

    @pl.when(i == 0)
    def _():
        gather(i, slot)

    @pl.when(i + 1 < pl.num_programs(0))
    def _():
        gather(i + 1, 1 - slot)

    pltpu.make_async_copy(ys_ref.at[pl.ds(0, tmf * ROW_CHUNKS)], ybuf.at[slot], sems.at[slot]).wait()
    gate2 = mod_ref[0][5:6]
    y = jnp.concatenate([ybuf[slot, pl.ds(c, tmf, stride=ROW_CHUNKS), :] for c in range(d // LANES)], axis=1)
    o_ref[...] = x1_ref[...] + gate2 * _rms(y, gpost_ref[...])


def _final(dest, x1, mod3, g_post, ys, *, seq, tmf):
    t, d = x1.shape
    ns = seq // tmf
    return pl.pallas_call(
        _final_kernel,
        grid_spec=pltpu.PrefetchScalarGridSpec(
            num_scalar_prefetch=1, grid=(t // tmf,),
            in_specs=[pl.BlockSpec((tmf, d), lambda i, dest: (i, 0)),
                      pl.BlockSpec((1, 6, d), lambda i, dest: (i // ns, 0, 0)),
                      pl.BlockSpec((1, d), lambda i, dest: (0, 0)),
                      pl.BlockSpec(memory_space=pl.ANY)],
            out_specs=pl.BlockSpec((tmf, d), lambda i, dest: (i, 0)),
            scratch_shapes=[pltpu.VMEM((2, tmf * ROW_CHUNKS, LANES), F32), pltpu.SemaphoreType.DMA((2,))]),
        out_shape=jax.ShapeDtypeStruct((t, d), F32),
        compiler_params=pltpu.CompilerParams(dimension_semantics=("arbitrary",), vmem_limit_bytes=VMEM_LIMIT),
        name="final",
    )(dest, x1, mod3, g_post, ys)


def _tile_map(counts, *, capacity, tmx):
    ntile = (counts + tmx - 1) // tmx
    ends = jnp.cumsum(ntile)
    starts = ends - ntile
    j = jnp.arange(capacity // tmx + N_GROUPS, dtype=jnp.int32)
    g = jnp.minimum(jnp.sum((j[:, None] >= ends[None, :]).astype(jnp.int32), axis=1), N_GROUPS - 1)
    local = j - starts[g]
    valid = j < ends[-1]
    blk = g * (capacity // tmx) + local
    rows = jnp.clip(counts[g] - local * tmx, 0, tmx)
    last = ends[-1] - 1
    return (jnp.where(valid, g, g[last]).astype(jnp.int32), jnp.where(valid, blk, blk[last]).astype(jnp.int32),
            jnp.where(valid, rows, 0).astype(jnp.int32))


def _pick_tile(seq, want):
    return want if seq % want == 0 else seq


def _layer(x, c, positions, w_ada, b_ada, g_pre_mix, g_post_mix, g_pre_ffn, g_post_ffn, w_in, g_cq, w_uq,
           g_ckv, w_ukv, w_o_mla, mu_shift, w_decay_up, decay_bias, w_a_up, a_bias, w_g_up, k_k, k_a, r_k,
           lnx_w, lnx_b, w_o_rwkv, w_out, w_router_group, b_router_group, w_router_expert, b_router_expert,
           w_exp_gate, w_exp_up, w_exp_down):
    batch, seq, d = x.shape
    t = batch * seq
    x2 = x.reshape(t, d)
    row1 = lambda a: a.reshape(1, -1)

    mod3 = _ada(c, w_ada, b_ada).reshape(batch, 6, d)

    zeros = lambda n: jnp.zeros((d, n), F32)
    w_c = jnp.concatenate([w_in[:, :MLA_Q_RANK + MLA_KV_RANK], zeros(MLA_NOPE),
                           w_in[:, MLA_Q_RANK + MLA_KV_RANK:MLA_IN], zeros(HEAD_PAD - MLA_NOPE - MLA_ROPE)],
                          axis=1).astype(BF16)
    w_r = w_in[:, MLA_IN:MLA_IN + RWKV_IN].astype(BF16)
    w_g = w_in[:, MLA_IN + RWKV_IN:].astype(BF16)
    w_uq_p = jnp.pad(w_uq.reshape(MLA_Q_RANK, HEADS, MLA_NOPE + MLA_ROPE),
                     ((0, 0), (0, 0), (0, HEAD_PAD - MLA_NOPE - MLA_ROPE))).reshape(MLA_Q_RANK, QK_WIDTH).astype(BF16)
    w_ukv3 = w_ukv.reshape(MLA_KV_RANK, HEADS, MLA_NOPE + MLA_V)
    w_uk_p = jnp.pad(w_ukv3[..., :MLA_NOPE], ((0, 0), (0, 0), (0, HEAD_PAD - MLA_NOPE))
                     ).reshape(MLA_KV_RANK, QK_WIDTH).astype(BF16)
    w_uv_t = jnp.pad(w_ukv3[..., MLA_NOPE:], ((0, 0), (0, 0), (0, V_ROWS - MLA_V))
                     ).reshape(MLA_KV_RANK, HEADS * V_ROWS).T.astype(BF16)
    inv_freq = jnp.power(ROPE_THETA, -jnp.arange(0, MLA_ROPE, 2, dtype=F32) / MLA_ROPE)
    invf = jnp.concatenate([jnp.zeros((MLA_NOPE,), F32), inv_freq, inv_freq,
                            jnp.zeros((HEAD_PAD - MLA_NOPE - MLA_ROPE,), F32)]).reshape(1, LANES)

    tm = _pick_tile(seq, 512)
    q_all, k_all, vt_all, zs, gates = _inproj(
        x2, mod3, positions.reshape(batch, 1, seq), row1(g_pre_mix), w_c, w_r, w_g, row1(g_cq), row1(g_ckv),
        w_uq_p, w_uk_p, w_uv_t, row1(mu_shift), invf, batch=batch, seq=seq, tm=tm)

    attn = _attn(q_all, k_all, vt_all, batch=batch, seq=seq, tq=ATTN_BLOCK)

    tc = 2 * CHUNK * RWKV_CHUNKS_PER_STEP
    w_da = jnp.concatenate([
        jnp.concatenate([w_decay_up, jnp.zeros_like(w_decay_up)], axis=1),
        jnp.concatenate([jnp.zeros_like(w_a_up), w_a_up], axis=1)], axis=0).astype(BF16)
    hid = jnp.arange(2 * LANES) // RWKV_HEAD
    ones_bd = (hid[:, None] == hid[None, :]).astype(BF16)
    tid = jnp.arange(tc // 2)
    same_chunk = (tid[:, None] // CHUNK) == (tid[None, :] // CHUNK)
    tri = (same_chunk & (tid[None, :] <= tid[:, None])).astype(BF16)
    rw = _rwkv(zs, w_da, w_g_up.astype(BF16), row1(decay_bias), row1(a_bias), row1(k_k), row1(k_a), row1(r_k),
               row1(lnx_w), row1(lnx_b), ones_bd, tri, batch=batch, seq=seq, tc=tc)

    w_rt = jnp.concatenate([w_router_expert, w_router_group,
                            jnp.zeros((d, ROUTER_WIDTH - N_EXPERTS - N_GROUPS), F32)], axis=1)
    wr_hi = w_rt.astype(BF16)
    wr_lo = (w_rt - wr_hi.astype(F32)).astype(BF16)
    b_r = jnp.concatenate([b_router_expert, b_router_group,
                           jnp.zeros((ROUTER_WIDTH - N_EXPERTS - N_GROUPS,), F32)]).reshape(1, ROUTER_WIDTH)
    x1, hp, dest, cnt = _merge(attn, rw, gates, x2, mod3, w_o_mla.astype(BF16), w_o_rwkv.astype(BF16),
                               w_out.astype(BF16), row1(g_post_mix), row1(g_pre_ffn), wr_hi, wr_lo, b_r,
                               seq=seq, tm=_pick_tile(seq, 512))

    tmx = _pick_tile(seq, 512)
    dest = dest.reshape(t)
    xs = _dispatch(dest, hp, tokens_out=N_GROUPS * t, td=_pick_tile(seq, 2048))
    tile_group, tile_blk, tile_rows = _tile_map(cnt[0, :N_GROUPS].astype(jnp.int32), capacity=t, tmx=tmx)
    eid = jnp.arange(EXPERTS_PER_GROUP * EXPERT_FF) // EXPERT_FF
    expand = (jnp.arange(ROUTER_WIDTH)[:, None] == eid[None, :]).astype(BF16)
    ys = _experts(tile_group, tile_blk, tile_rows, xs, w_exp_gate.astype(BF16), w_exp_up.astype(BF16),
                  w_exp_down.astype(BF16), expand, tmx=tmx)
    out = _final(dest, x1, mod3, row1(g_post_ffn), ys, seq=seq, tmf=_pick_tile(seq, 512))
    return out.reshape(batch, seq, d)


def kernel(x, c, positions, w_ada, b_ada, g_pre_mix, g_post_mix, g_pre_ffn, g_post_ffn, w_in, g_cq, w_uq, g_ckv, w_ukv, w_o_mla, mu_shift, w_decay_up, decay_bias, w_a_up, a_bias, w_g_up, k_k, k_a, r_k, lnx_w, lnx_b, w_o_rwkv, w_out, w_router_group, b_router_group, w_router_expert, b_router_expert, w_exp_gate, w_exp_up, w_exp_down):
    depth = w_ada.shape[0]
    for l in range(depth):
        x = _layer(x, c, positions, w_ada[l], b_ada[l], g_pre_mix[l], g_post_mix[l], g_pre_ffn[l], g_post_ffn[l],
                   w_in[l], g_cq[l], w_uq[l], g_ckv[l], w_ukv[l], w_o_mla[l], mu_shift[l], w_decay_up[l],
                   decay_bias[l], w_a_up[l], a_bias[l], w_g_up[l], k_k[l], k_a[l], r_k[l], lnx_w[l], lnx_b[l],
                   w_o_rwkv[l], w_out[l], w_router_group[l], b_router_group[l], w_router_expert[l],
                   b_router_expert[l], w_exp_gate[l], w_exp_up[l], w_exp_down[l])
    return x
```

```python
import functools
import math

import jax
import jax.numpy as jnp
from jax import lax
from jax.experimental import pallas as pl
from jax.experimental.pallas import tpu as pltpu

F32 = jnp.float32
BF16 = jnp.bfloat16

D_MODEL = 1024
CHUNK = 64
HEADS = 8
MLA_NOPE = 64
MLA_ROPE = 32
MLA_V = 64
MLA_Q_RANK = 384
MLA_KV_RANK = 256
ROPE_THETA = 10000.0
RWKV_HEAD = 64
RWKV_WIDTH = HEADS * RWKV_HEAD
DECAY_RANK = 64
AAA_RANK = 64
GATE_RANK = 128
GN_EPS = 64e-5
N_GROUPS = 4
EXPERTS_PER_GROUP = 8
N_EXPERTS = N_GROUPS * EXPERTS_PER_GROUP
EXPERT_FF = 256
RMS_EPS = 1e-6
NEG_INF = -1e30
MLA_IN = MLA_Q_RANK + MLA_KV_RANK + MLA_ROPE
RWKV_IN = 3 * RWKV_WIDTH + DECAY_RANK + AAA_RANK + GATE_RANK

LANES = 128
DMA_UNROLL = 8
ROW_CHUNKS = 9
HEAD_PAD = LANES
QK_WIDTH = HEADS * HEAD_PAD
MLA_C_WIDTH = 768
ROUTER_WIDTH = LANES
VMEM_LIMIT = 56 * 1024 * 1024
ATTN_HEADS_PER_STEP = 8
ATTN_BLOCK = 256
ATTN_BLOCKS_PER_ITER = 2
V_ROWS = 80
RWKV_CHUNKS_PER_STEP = 4
MERGE_SUBTILES = 2


def _dot(a, b):
    return jnp.dot(a, b, preferred_element_type=F32)


def _dot_nt(a, b):
    return lax.dot_general(a, b, (((1,), (1,)), ((), ())), preferred_element_type=F32)


def _dot_tn(a, b):
    return lax.dot_general(a, b, (((0,), (0,)), ((), ())), preferred_element_type=F32)


def _split(x):
    hi = x.astype(BF16)
    lo = (x - hi.astype(F32)).astype(BF16)
    return hi, lo


def _dot3(x, w_hi, w_lo):
    x_hi, x_lo = _split(x)
    return _dot(x_hi, w_hi) + (_dot(x_hi, w_lo) + _dot(x_lo, w_hi))


def _sigmoid(x):
    return 1.0 / (1.0 + jnp.exp(-x))


def _rms(x, g):
    return x * lax.rsqrt(jnp.mean(x * x, axis=-1, keepdims=True) + RMS_EPS) * g


def _ada_kernel(c_ref, w_ref, b_ref, o_ref):
    c = c_ref[...]
    s = c * _sigmoid(c)
    w_hi, w_lo = _split(w_ref[...])
    o_ref[...] = _dot3(s, w_hi, w_lo) + b_ref[...]


def _ada(c, w_ada, b_ada):
    b, d = c.shape
    n = w_ada.shape[1]
    tn = 512
    return pl.pallas_call(
        _ada_kernel,
        grid=(n // tn,),
        in_specs=[pl.BlockSpec((b, d), lambda j: (0, 0)),
                  pl.BlockSpec((d, tn), lambda j: (0, j)),
                  pl.BlockSpec((1, tn), lambda j: (0, j))],
        out_specs=pl.BlockSpec((b, tn), lambda j: (0, j)),
        out_shape=jax.ShapeDtypeStruct((b, n), F32),
        name="ada",
    )(c, w_ada, b_ada.reshape(1, n))


def _inproj_kernel(x_ref, mod_ref, pos_ref, gpre_ref, wc_ref, wr_ref, wg_ref, gcq_ref, gckv_ref,
                   wuq_ref, wuk_ref, wuv_ref, mu_ref, invf_ref,
                   q_ref, k_ref, vt_ref, zs_ref, gate_ref, carry_ref):
    i = pl.program_id(1)
    tm = x_ref.shape[0]

    @pl.when(i == 0)
    def _():
        carry_ref[...] = jnp.zeros_like(carry_ref)

    mod = mod_ref[0]
    shift, scale = mod[0:1], mod[1:2]
    h = (_rms(x_ref[...], gpre_ref[...]) * (1.0 + scale) + shift).astype(BF16)

    zc = _dot(h, wc_ref[...])
    zg = _dot(h, wg_ref[...])
    nq = _rms(zc[:, :MLA_Q_RANK], gcq_ref[...]).astype(BF16)
    nkv = _rms(zc[:, MLA_Q_RANK:MLA_Q_RANK + MLA_KV_RANK], gckv_ref[...]).astype(BF16)
    kr = zc[:, MLA_Q_RANK + MLA_KV_RANK:]
    qf = _dot(nq, wuq_ref[...])
    kn = _dot(nkv, wuk_ref[...])
    vt = _dot_nt(wuv_ref[...], nkv)
    zr = _dot(h, wr_ref[...])

    pos_col = jnp.broadcast_to(pos_ref[...].astype(F32), (LANES, tm)).T
    ang = pos_col * invf_ref[...]
    lane = lax.broadcasted_iota(jnp.int32, (1, LANES), 1)
    first_half = lane < MLA_NOPE + MLA_ROPE // 2
    in_rope = (lane >= MLA_NOPE) & (lane < MLA_NOPE + MLA_ROPE)
    cos_t = jnp.where(in_rope, jnp.cos(ang), 1.0)
    sin_a = jnp.sin(ang)
    sin_t = jnp.where(in_rope, jnp.where(first_half, -sin_a, sin_a), 0.0)

    def rope(t):
        rot = jnp.where(first_half, pltpu.roll(t, LANES - MLA_ROPE // 2, 1), pltpu.roll(t, MLA_ROPE // 2, 1))
        return t * cos_t + rot * sin_t

    kr_rot = rope(kr)
    q_scale = math.log2(math.e) / math.sqrt(MLA_NOPE + MLA_ROPE)
    for hh in range(HEADS):
        sl = slice(hh * HEAD_PAD, (hh + 1) * HEAD_PAD)
        q_ref[:, sl] = (rope(qf[:, sl]) * q_scale).astype(BF16)
        k_ref[:, sl] = (kn[:, sl] + kr_rot).astype(BF16)
    vrow = lax.broadcasted_iota(jnp.int32, vt.shape, 0) % V_ROWS
    vt = jnp.where(vrow == MLA_V, 1.0, vt).astype(BF16)
    for j in range(vt_ref.shape[0]):
        vt_ref[j] = vt[:, j * ATTN_BLOCK:(j + 1) * ATTN_BLOCK]

    gate_ref[...] = _sigmoid(zg).astype(BF16)

    row = lax.broadcasted_iota(jnp.int32, (tm, 1), 0)
    prev = jnp.where(row == 0, carry_ref[0:1, :], pltpu.roll(zr, 1, 0))
    carry_ref[0:1, :] = zr[tm - 1:tm, :]
    zs_ref[...] = (zr + (prev - zr) * mu_ref[...]).astype(BF16)


def _const_spec(shape):
    nd = len(shape)
    return pl.BlockSpec(shape, lambda *_: (0,) * nd, pipeline_mode=pl.Buffered(1))


def _inproj(x2, mod3, pos2, g_pre, w_c, w_r, w_g, g_cq, g_ckv, w_uq, w_uk, w_uv, mu, invf, *, batch, seq, tm):
    t, d = x2.shape
    ns = seq // tm
    row = lambda b, i: (b * ns + i, 0)
    widths = [QK_WIDTH, QK_WIDTH, RWKV_IN, 2 * d]
    specs = [pl.BlockSpec((tm, w), row) for w in widths]
    shapes = [jax.ShapeDtypeStruct((t, w), BF16) for w in widths]
    vw = HEADS * V_ROWS
    nb = tm // ATTN_BLOCK
    specs.insert(2, pl.BlockSpec((None, nb, vw, ATTN_BLOCK), lambda b, i: (b, i, 0, 0)))
    shapes.insert(2, jax.ShapeDtypeStruct((batch, ns * nb, vw, ATTN_BLOCK), BF16))
    return pl.pallas_call(
        _inproj_kernel,
        grid=(batch, ns),
        in_specs=[pl.BlockSpec((tm, d), row),
                  pl.BlockSpec((1, 6, d), lambda b, i: (b, 0, 0)),
                  pl.BlockSpec((None, 1, tm), lambda b, i: (b, 0, i)),
                  _const_spec(g_pre.shape), _const_spec(w_c.shape), _const_spec(w_r.shape),
                  _const_spec(w_g.shape), _const_spec(g_cq.shape), _const_spec(g_ckv.shape),
                  _const_spec(w_uq.shape), _const_spec(w_uk.shape), _const_spec(w_uv.shape),
                  _const_spec(mu.shape), _const_spec(invf.shape)],
        out_specs=specs,
        out_shape=shapes,
        scratch_shapes=[pltpu.VMEM((8, RWKV_IN), F32)],
        compiler_params=pltpu.CompilerParams(dimension_semantics=("parallel", "arbitrary"),
                                             vmem_limit_bytes=VMEM_LIMIT),
        name="inproj",
    )(x2, mod3, pos2, g_pre, w_c, w_r, w_g, g_cq, g_ckv, w_uq, w_uk, w_uv, mu, invf)


def _attn_kernel(q_ref, k_ref, vt_ref, o_ref):
    i = pl.program_id(2)
    tq = q_ref.shape[0]
    kc = lax.broadcasted_iota(jnp.int32, (tq, tq), 0) // CHUNK
    qc = lax.broadcasted_iota(jnp.int32, (tq, tq), 1) // CHUNK
    diag_mask = kc <= qc
    heads = range(ATTN_HEADS_PER_STEP)
    hs = [slice(h * HEAD_PAD, (h + 1) * HEAD_PAD) for h in heads]
    vr = [slice(h * V_ROWS, (h + 1) * V_ROWS) for h in heads]
    q = [q_ref[:, s] for s in hs]

    def block(kb, carry, nblk, masked):
        kbs = [kb + n for n in range(nblk)]
        rows = [pl.ds(pl.multiple_of(b * tq, tq), tq) for b in kbs]
        s = [[_dot_nt(k_ref[r, hs[h]], q[h]) for r in rows] for h in heads]
        if masked:
            s = [[jnp.where(diag_mask, t, NEG_INF) for t in sh] for sh in s]
        m_new = []
        for h in heads:
            m = carry[h][0]
            for t in s[h]:
                m = jnp.maximum(m, jnp.max(t, axis=0, keepdims=True))
            m_new.append(m)
        alpha = [jnp.exp2(carry[h][0] - m_new[h]) for h in heads]
        p = [jnp.concatenate([jnp.exp2(t - m_new[h]).astype(BF16) for t in s[h]], axis=0) for h in heads]
        acc = [alpha[h] * carry[h][1]
               + _dot(jnp.concatenate([vt_ref[b, vr[h], :] for b in kbs], axis=1), p[h]) for h in heads]
        return tuple((m_new[h], acc[h]) for h in heads)

    init = tuple((jnp.full((1, tq), NEG_INF, F32), jnp.zeros((V_ROWS, tq), F32)) for _ in heads)
    nb = ATTN_BLOCKS_PER_ITER
    carry = lax.fori_loop(0, i // nb, lambda n, c: block(n * nb, c, nb, False), init)
    carry = lax.fori_loop((i // nb) * nb, i, lambda kb, c: block(kb, c, 1, False), carry)
    final = block(i, carry, 1, True)
    outs = [final[h][1][:MLA_V] / final[h][1][MLA_V:MLA_V + 1] for h in heads]
    for pr in range(ATTN_HEADS_PER_STEP // 2):
        pair_t = jnp.concatenate([outs[2 * pr], outs[2 * pr + 1]], axis=0)
        o_ref[:, pr * LANES:(pr + 1) * LANES] = pair_t.T.astype(BF16)


def _attn(q_all, k_all, vt_all, *, batch, seq, tq):
    t = q_all.shape[0]
    nq = seq // tq
    g = ATTN_HEADS_PER_STEP
    return pl.pallas_call(
        _attn_kernel,
        grid=(batch, HEADS // g, nq),
        in_specs=[pl.BlockSpec((tq, g * HEAD_PAD), lambda b, p, i: (b * nq + i, p)),
                  pl.BlockSpec((seq, g * HEAD_PAD), lambda b, p, i: (b, p)),
                  pl.BlockSpec((None, nq, g * V_ROWS, tq), lambda b, p, i: (b, 0, p, 0))],
        out_specs=pl.BlockSpec((tq, g * MLA_V), lambda b, p, i: (b * nq + i, p)),
        out_shape=jax.ShapeDtypeStruct((t, HEADS * MLA_V), BF16),
        compiler_params=pltpu.CompilerParams(dimension_semantics=("parallel", "parallel", "arbitrary"),
                                             vmem_limit_bytes=VMEM_LIMIT),
        name="attn",
    )(q_all, k_all, vt_all)


def _rwkv_kernel(zs_ref, wda_ref, wgu_ref, dbias_ref, abias_ref, kk_ref, ka_ref, rk_ref, lnw_ref, lnb_ref,
                 ones_ref, tri_ref,
                 o_ref,
                 st_ref, a_s, r_s, b_s, k_s, bg_s, kg_s, v_s, gc_s, bonus_s, g_s, y_s):
    w = RWKV_WIDTH
    nsub = RWKV_CHUNKS_PER_STEP
    th = nsub * CHUNK
    n2 = 2 * CHUNK
    staged = (a_s, r_s, b_s, k_s, bg_s, kg_s, v_s, gc_s, bonus_s, g_s)

    @pl.when(pl.program_id(1) == 0)
    def _():
        st_ref[...] = jnp.zeros_like(st_ref)

    ones_bd = ones_ref[...]

    def head_sum(t):
        tb = t.astype(BF16)
        hw = ones_bd.shape[0]
        return jnp.concatenate([_dot(tb[:, n * hw:(n + 1) * hw], ones_bd) for n in range(w // hw)], axis=1)

    ri = lax.broadcasted_iota(jnp.int32, (4 * CHUNK, n2), 0)
    ci = lax.broadcasted_iota(jnp.int32, (4 * CHUNK, n2), 1) % CHUNK
    tri_mask = (ri % CHUNK + ri // n2) > ci
    lane2 = lax.broadcasted_iota(jnp.int32, (CHUNK, n2), 1)
    h1 = lane2 < RWKV_HEAD
    bi = lax.broadcasted_iota(jnp.int32, (n2, n2), 0) // RWKV_HEAD
    bj = lax.broadcasted_iota(jnp.int32, (n2, n2), 1) // RWKV_HEAD
    bd_mask = bi == bj
    zero_bf = jnp.zeros((CHUNK, n2), BF16)
    zero_f = jnp.zeros((CHUNK, n2), F32)
    lane = lax.broadcasted_iota(jnp.int32, (1, LANES), 1)

    def prepare(h):
        hr = slice(h * th, (h + 1) * th)
        zs = zs_ref[hr, :].astype(F32)
        r, k, v = zs[:, 0:w], zs[:, w:2 * w], zs[:, 2 * w:3 * w]
        da = zs[:, 3 * w:3 * w + LANES]
        gd = zs[:, 3 * w + LANES:]
        lora_in = jnp.where(lane < DECAY_RANK, jnp.tanh(da), da).astype(BF16)
        pre = _dot(lora_in, wda_ref[...])
        g_new = _dot(_sigmoid(gd).astype(BF16), wgu_ref[...])
        yield
        u = -(dbias_ref[...] + pre[:, :w])
        softplus = jnp.maximum(u, 0.0) + jnp.log(1.0 + jnp.exp(-jnp.abs(u)))
        logw = -jnp.exp(-softplus - 0.5)
        eta = _sigmoid(abias_ref[...] + pre[:, w:])
        kk = k * kk_ref[...]
        kk_ss = head_sum(kk * kk)
        lw_hi, lw_lo = _split(logw)
        cum = _dot(tri_ref[...], lw_hi) + _dot(tri_ref[...], lw_lo)
        yield
        kk = kk * jnp.minimum(lax.rsqrt(kk_ss), 1e12)
        kp = k * (1.0 + (eta - 1.0) * ka_ref[...])
        bonus_new = head_sum(r * kp * rk_ref[...]) * v
        yield
        cum_end = jnp.concatenate([jnp.broadcast_to(cum[(c + 1) * CHUNK - 1:(c + 1) * CHUNK, :], (CHUNK, w))
                                   for c in range(nsub)], axis=0)
        b_in = kk * eta
        g_inv = jnp.exp(-cum)
        g_rem = jnp.exp(cum_end - cum)
        staged_new = ((-kk * jnp.exp(cum - logw)).astype(BF16), (r * jnp.exp(cum)).astype(BF16),
                      (b_in * g_inv).astype(BF16), (kp * g_inv).astype(BF16),
                      (b_in * g_rem).astype(BF16), (kp * g_rem).astype(BF16), v.astype(BF16),
                      jnp.exp(cum_end), bonus_new, g_new)
        for ref, val in zip(staged, staged_new):
            ref[hr, :] = val
        yield

    def solve(h):
        blocks = [(sub, p) for sub in range(nsub) for p in range(HEADS // 2)]
        rows = [slice(h * th + sub * CHUNK, h * th + (sub + 1) * CHUNK) for sub in range(nsub)]
        lns = [slice(p * n2, (p + 1) * n2) for p in range(HEADS // 2)]
        a_t = [a_s[rows[sub], lns[p]] for sub, p in blocks]
        r_t = [r_s[rows[sub], lns[p]] for sub, p in blocks]
        v_t = [v_s[rows[sub], lns[p]] for sub, p in blocks]
        ml = []
        for n, (sub, p) in enumerate(blocks):
            lhs = jnp.concatenate([jnp.where(h1, a_t[n], zero_bf), jnp.where(h1, zero_bf, a_t[n]),
                                   jnp.where(h1, r_t[n], zero_bf), jnp.where(h1, zero_bf, r_t[n])], axis=0)
            rhs = jnp.concatenate([b_s[rows[sub], lns[p]], k_s[rows[sub], lns[p]]], axis=0)
            ml.append(jnp.where(tri_mask, _dot_nt(lhs, rhs), 0.0))
        ml_b = [m.astype(BF16) for m in ml]
        yield
        units = [(n, hh) for n in range(len(blocks)) for hh in range(2)]
        w_t = [_dot(ml_b[n][hh * CHUNK:(hh + 1) * CHUNK, :], jnp.concatenate([zero_bf, v_t[n]], axis=0))
               for n, hh in units]
        a_f = [t.astype(F32) for t in a_t]
        a_sw = [pltpu.roll(t, RWKV_HEAD, 1) for t in a_f]
        x = [jnp.where(h1, a_f[n], pltpu.roll(w_t[2 * n], RWKV_HEAD, 1)) if hh == 0
             else jnp.where(h1, a_sw[n], w_t[2 * n + 1]) for n, hh in units]
        lsq = [jnp.where(h1, ml[n][hh * CHUNK:(hh + 1) * CHUNK, :], 0.0) for n, hh in units]
        yield
        for _ in range(6):
            out = [_dot(lsq[u][:, :CHUNK].astype(BF16),
                        jnp.concatenate([x[u], lsq[u]], axis=1).astype(BF16)) for u in range(len(units))]
            x = [x[u] + out[u][:, :n2] for u in range(len(units))]
            lsq = [t[:, n2:] for t in out]
            yield
        z = []
        for n in range(len(blocks)):
            x0, x1 = x[2 * n], x[2 * n + 1]
            a_hat = jnp.where(h1, x0, pltpu.roll(x1, RWKV_HEAD, 1))
            u0 = jnp.where(h1, pltpu.roll(x0, RWKV_HEAD, 1), x1)
            z.append(jnp.concatenate([jnp.concatenate([a_hat, u0], axis=1),
                                      jnp.concatenate([zero_f, v_t[n].astype(F32)], axis=1)],
                                     axis=0).astype(BF16))
        o0 = [_dot(ml_b[n][2 * CHUNK:3 * CHUNK, :], z[n]) for n in range(len(blocks))]
        o1 = [_dot(ml_b[n][3 * CHUNK:4 * CHUNK, :], z[n]) for n in range(len(blocks))]
        pm = [_dot_tn(z[n], jnp.concatenate([bg_s[rows[sub], lns[p]], kg_s[rows[sub], lns[p]]], axis=0))
              for n, (sub, p) in enumerate(blocks)]
        r_hat = [(r_t[n].astype(F32) + jnp.where(h1, o0[n][:, :n2], o1[n][:, :n2])).astype(BF16)
                 for n in range(len(blocks))]
        yield
        state = [st_ref[p] for p in range(HEADS // 2)]
        for n, (sub, p) in enumerate(blocks):
            y0 = jnp.where(h1, o0[n][:, n2:], o1[n][:, n2:])
            g_t = jnp.where(bd_mask, pm[n][:n2], 0.0).astype(BF16)
            h_t = jnp.where(bd_mask, pm[n][n2:], 0.0)
            s_b = state[p].astype(BF16)
            y_s[rows[sub], lns[p]] = _dot_nt(r_hat[n], s_b) + y0
            gc = gc_s[rows[sub].start:rows[sub].start + 1, lns[p]]
            state[p] = gc * state[p] + _dot(s_b, g_t) + h_t
        for p in range(HEADS // 2):
            st_ref[p] = state[p]
        yield

    def finish(h):
        hr = slice(h * th, (h + 1) * th)
        y = y_s[hr, :]
        inv_n = 1.0 / RWKV_HEAD
        dev = y - head_sum(y) * inv_n
        yield
        var = head_sum(dev * dev) * inv_n
        yield
        out = (dev * lax.rsqrt(var + GN_EPS) * lnw_ref[...] + lnb_ref[...] + bonus_s[hr, :]) * g_s[hr, :]
        o_ref[hr, :] = out.astype(BF16)
        yield

    def emit(main, side=None, every=1):
        n = 0
        for _ in main:
            n += 1
            if side is not None and n % every == 0:
                next(side, None)
        if side is not None:
            for _ in side:
                pass

    emit(prepare(0))
    emit(solve(0), prepare(1), every=2)
    emit(solve(1), finish(0), every=3)
    emit(finish(1))


def _rwkv(zs, w_da, w_gu, dbias, abias, k_k, k_a, r_k, lnw, lnb, ones_bd, tri, *, batch, seq, tc):
    t = zs.shape[0]
    nc = seq // tc
    w = RWKV_WIDTH
    consts = [w_da, w_gu, dbias, abias, k_k, k_a, r_k, lnw, lnb, ones_bd, tri]
    return pl.pallas_call(
        _rwkv_kernel,
        grid=(batch, nc),
        in_specs=[pl.BlockSpec((tc, RWKV_IN), lambda b, i: (b * nc + i, 0))] + [_const_spec(a.shape) for a in consts],
        out_specs=pl.BlockSpec((tc, w), lambda b, i: (b * nc + i, 0)),
        out_shape=jax.ShapeDtypeStruct((t, w), BF16),
        scratch_shapes=[pltpu.VMEM((HEADS // 2, 2 * RWKV_HEAD, 2 * RWKV_HEAD), F32)]
        + [pltpu.VMEM((tc, w), BF16)] * 7
        + [pltpu.VMEM((tc, w), F32)] * 4,
        compiler_params=pltpu.CompilerParams(dimension_semantics=("parallel", "arbitrary"),
                                             vmem_limit_bytes=VMEM_LIMIT),
        name="rwkv",
    )(zs, *consts)


def _route(logits):
    lane = lax.broadcasted_iota(jnp.int32, logits.shape, 1)
    lane_f = lane.astype(F32)
    big = float(ROUTER_WIDTH)
    is_group = (lane >= N_EXPERTS) & (lane < N_EXPERTS + N_GROUPS)
    gl = jnp.where(is_group, logits, NEG_INF)
    g_max = jnp.max(gl, axis=-1, keepdims=True)
    g_sel = jnp.min(jnp.where(gl == g_max, lane_f, big), axis=-1, keepdims=True) - float(N_EXPERTS)
    p_sel = 1.0 / jnp.sum(jnp.where(is_group, jnp.exp(gl - g_max), 0.0), axis=-1, keepdims=True)
    in_group = (lane < N_EXPERTS) & ((lane // EXPERTS_PER_GROUP).astype(F32) == g_sel)
    el = jnp.where(in_group, logits, NEG_INF)
    t1 = jnp.max(el, axis=-1, keepdims=True)
    i1 = jnp.min(jnp.where(el == t1, lane_f, big), axis=-1, keepdims=True)
    el2 = jnp.where(lane_f == i1, NEG_INF, el)
    t2 = jnp.max(el2, axis=-1, keepdims=True)
    i2 = jnp.min(jnp.where(el2 == t2, lane_f, big), axis=-1, keepdims=True)
    e21 = jnp.exp(t2 - t1)
    w1 = p_sel / (1.0 + e21)
    w2 = w1 * e21
    first = g_sel * float(EXPERTS_PER_GROUP)
    cw = jnp.where(lane_f == i1 - first, w1, 0.0) + jnp.where(lane_f == i2 - first, w2, 0.0)
    return g_sel, cw


def _merge_kernel(attn_ref, rw_ref, gate_ref, x_ref, mod_ref, womla_ref, worw_ref, wout_ref,
                  gpost_ref, gffn_ref, wrh_ref, wrl_ref, br_ref, stril_ref,
                  x1_ref, hp_ref, dest_ref, cnt_ref, carry_ref, *, group_capacity):
    i = pl.program_id(0)
    tm, d = x_ref.shape

    @pl.when(i == 0)
    def _():
        carry_ref[...] = jnp.zeros_like(carry_ref)

    mod = mod_ref[0]
    gate1, shift2, scale2 = mod[2:3], mod[3:4], mod[4:5]

    ts = tm // MERGE_SUBTILES
    subs = [slice(n * ts, (n + 1) * ts) for n in range(MERGE_SUBTILES)]
    o_mla = [_dot(attn_ref[r, :], womla_ref[...]) for r in subs]
    o_rw = [_dot(rw_ref[r, :], worw_ref[...]) for r in subs]
    o = [(gate_ref[r, :d].astype(F32) * o_mla[n] + gate_ref[r, d:].astype(F32) * o_rw[n]).astype(BF16)
         for n, r in enumerate(subs)]
    y = [_dot(t, wout_ref[...]) for t in o]
    x1 = [x_ref[r, :] + gate1 * _rms(y[n], gpost_ref[...]) for n, r in enumerate(subs)]
    h2 = [_rms(t, gffn_ref[...]) * (1.0 + scale2) + shift2 for t in x1]
    logits = [_dot3(t, wrh_ref[...], wrl_ref[...]) + br_ref[...] for t in h2]
    for n, r in enumerate(subs):
        x1_ref[r, :] = x1[n]
        for c in range(d // LANES):
            hp_ref[pl.ds(n * ts * ROW_CHUNKS + c, ts, stride=ROW_CHUNKS), :] = h2[n][:, c * LANES:(c + 1) * LANES]

    lane_f = lax.broadcasted_iota(jnp.int32, (ts, ROUTER_WIDTH), 1).astype(F32)
    routes = [_route(t) for t in logits]
    g_sels = [g for g, _ in routes]
    for n, r in enumerate(subs):
        hp_ref[pl.ds(n * ts * ROW_CHUNKS + d // LANES, ts, stride=ROW_CHUNKS), :] = routes[n][1]
    onehots = [jnp.where(lane_f == g, 1.0, 0.0) for g in g_sels]

    onehot = jnp.concatenate(onehots, axis=0)
    earlier = _dot(stril_ref[...], onehot.astype(BF16))
    carry = carry_ref[0:1, :]
    for n, r in enumerate(subs):
        rank = jnp.sum(jnp.where(lane_f == g_sels[n], carry + earlier[r, :], 0.0), axis=-1, keepdims=True)
        dest_col = jnp.broadcast_to(g_sels[n] * float(group_capacity) + rank, (ts, LANES))
        dest_ref[:, r] = dest_col.T[0:1, :].astype(jnp.int32)
    total = jnp.broadcast_to(carry + jnp.sum(onehot, axis=0, keepdims=True), carry_ref.shape)
    carry_ref[...] = total
    cnt_ref[...] = total


def _merge(attn, rw, gates, x2, mod3, w_o_mla, w_o_rwkv, w_out, g_post, g_ffn, wr_hi, wr_lo, b_r, *, seq, tm):
    t, d = x2.shape
    ns = seq // tm
    row = lambda i: (i, 0)
    tid = jnp.arange(tm)
    stril = (tid[None, :] < tid[:, None]).astype(BF16)
    consts = [w_o_mla, w_o_rwkv, w_out, g_post, g_ffn, wr_hi, wr_lo, b_r, stril]
    return pl.pallas_call(
        functools.partial(_merge_kernel, group_capacity=t),
        grid=(t // tm,),
        in_specs=[pl.BlockSpec((tm, attn.shape[1]), row), pl.BlockSpec((tm, rw.shape[1]), row),
                  pl.BlockSpec((tm, 2 * d), row), pl.BlockSpec((tm, d), row),
                  pl.BlockSpec((1, 6, d), lambda i: (i // ns, 0, 0))] + [_const_spec(a.shape) for a in consts],
        out_specs=[pl.BlockSpec((tm, d), row), pl.BlockSpec((tm * ROW_CHUNKS, LANES), row),
                   pl.BlockSpec((None, 1, tm), lambda i: (i, 0, 0)), pl.BlockSpec((8, ROUTER_WIDTH), lambda i: (0, 0))],
        out_shape=[jax.ShapeDtypeStruct((t, d), F32), jax.ShapeDtypeStruct((t * ROW_CHUNKS, LANES), F32),
                   jax.ShapeDtypeStruct((t // tm, 1, tm), jnp.int32), jax.ShapeDtypeStruct((8, ROUTER_WIDTH), F32)],
        scratch_shapes=[pltpu.VMEM((8, ROUTER_WIDTH), F32)],
        compiler_params=pltpu.CompilerParams(dimension_semantics=("arbitrary",), vmem_limit_bytes=VMEM_LIMIT),
        name="merge",
    )(attn, rw, gates, x2, mod3, *consts)


def _token_rows(i):
    return pl.ds(i * ROW_CHUNKS, ROW_CHUNKS)


def _chunk(ref, c, n):
    return ref[pl.ds(c, n, stride=ROW_CHUNKS), :]


def _dispatch_kernel(dest_ref, hp_ref, xs_ref, sem):
    td = hp_ref.shape[0] // ROW_CHUNKS
    base = pl.program_id(0) * td

    def issue(r0, c):
        for j in range(DMA_UNROLL):
            r = r0 * DMA_UNROLL + j
            pltpu.make_async_copy(hp_ref.at[_token_rows(r)], xs_ref.at[_token_rows(dest_ref[base + r])],
                                  sem).start(priority=j % 2)
        return c

    lax.fori_loop(0, td // DMA_UNROLL, issue, 0)
    pltpu.make_async_copy(hp_ref, xs_ref.at[pl.ds(0, td * ROW_CHUNKS)], sem).wait()


def _dispatch(dest, hp, *, tokens_out, td):
    t = hp.shape[0] // ROW_CHUNKS
    return pl.pallas_call(
        _dispatch_kernel,
        grid_spec=pltpu.PrefetchScalarGridSpec(
            num_scalar_prefetch=1, grid=(t // td,),
            in_specs=[pl.BlockSpec((td * ROW_CHUNKS, LANES), lambda i, dest: (i, 0))],
            out_specs=pl.BlockSpec(memory_space=pl.ANY),
            scratch_shapes=[pltpu.SemaphoreType.DMA]),
        out_shape=jax.ShapeDtypeStruct((tokens_out * ROW_CHUNKS, LANES), F32),
        compiler_params=pltpu.CompilerParams(dimension_semantics=("arbitrary",), vmem_limit_bytes=VMEM_LIMIT),
        name="dispatch",
    )(dest, hp)


def _experts_kernel(tg_ref, tb_ref, tr_ref, xs_ref, wg_ref, wu_ref, wd_ref, ex_ref, ys_ref):
    del tg_ref, tb_ref
    nrows = tr_ref[pl.program_id(0)]

    @pl.when(nrows > 0)
    def _():
        tmx = xs_ref.shape[0] // ROW_CHUNKS
        nchunk = D_MODEL // LANES
        valid = lax.broadcasted_iota(jnp.int32, (tmx, 1), 0) < nrows
        x = jnp.concatenate([jnp.where(valid, _chunk(xs_ref, c, tmx), 0.0).astype(BF16) for c in range(nchunk)],
                            axis=1)
        cw_hi, cw_lo = _split(jnp.where(valid, _chunk(xs_ref, nchunk, tmx), 0.0))
        cwx = _dot(cw_hi, ex_ref[...]) + _dot(cw_lo, ex_ref[...])
        acc = jnp.zeros((tmx, D_MODEL), F32)
        for e in range(EXPERTS_PER_GROUP):
            a = _dot(x, wg_ref[e])
            act = a * _sigmoid(a) * _dot(x, wu_ref[e]) * cwx[:, e * EXPERT_FF:(e + 1) * EXPERT_FF]
            acc = acc + _dot(act.astype(BF16), wd_ref[e])
        for c in range(nchunk):
            ys_ref[pl.ds(c, tmx, stride=ROW_CHUNKS), :] = acc[:, c * LANES:(c + 1) * LANES]
        ys_ref[pl.ds(nchunk, tmx, stride=ROW_CHUNKS), :] = jnp.zeros((tmx, LANES), F32)


def _experts(tile_group, tile_blk, tile_rows, xs, w_gate, w_up, w_down, expand, *, tmx):
    nt = tile_group.shape[0]
    d = D_MODEL
    grp = lambda j, tg, tb, tr: (tg[j], 0, 0)
    tile = pl.BlockSpec((tmx * ROW_CHUNKS, LANES), lambda j, tg, tb, tr: (tb[j], 0))
    return pl.pallas_call(
        _experts_kernel,
        grid_spec=pltpu.PrefetchScalarGridSpec(
            num_scalar_prefetch=3, grid=(nt,),
            in_specs=[tile,
                      pl.BlockSpec((EXPERTS_PER_GROUP, d, EXPERT_FF), grp),
                      pl.BlockSpec((EXPERTS_PER_GROUP, d, EXPERT_FF), grp),
                      pl.BlockSpec((EXPERTS_PER_GROUP, EXPERT_FF, d), grp),
                      pl.BlockSpec(expand.shape, lambda j, tg, tb, tr: (0, 0))],
            out_specs=tile),
        out_shape=jax.ShapeDtypeStruct(xs.shape, F32),
        compiler_params=pltpu.CompilerParams(dimension_semantics=("arbitrary",), vmem_limit_bytes=VMEM_LIMIT),
        name="experts",
    )(tile_group, tile_blk, tile_rows, xs, w_gate, w_up, w_down, expand)


def _final_kernel(dest_ref, x1_ref, mod_ref, gpost_ref, ys_ref, o_ref, ybuf, sems):
    tmf, d = x1_ref.shape
    i = pl.program_id(0)
    slot = i % 2

    def gather(tile, into):
        base = tile * tmf

        def issue(r0, c):
            for j in range(DMA_UNROLL):
                r = r0 * DMA_UNROLL + j
                pltpu.make_async_copy(ys_ref.at[_token_rows(dest_ref[base + r])], ybuf.at[into, _token_rows(r)],
                                      sems.at[into]).start(priority=j % 2)
            return c

        lax.fori_loop(0, tmf // DMA_UNROLL, issue, 0)

    @pl.when(i == 0)
    def _():
        gather(i, slot)

    @pl.when(i + 1 < pl.num_programs(0))
    def _():
        gather(i + 1, 1 - slot)

    pltpu.make_async_copy(ys_ref.at[pl.ds(0, tmf * ROW_CHUNKS)], ybuf.at[slot], sems.at[slot]).wait()
    gate2 = mod_ref[0][5:6]
    y = jnp.concatenate([ybuf[slot, pl.ds(c, tmf, stride=ROW_CHUNKS), :] for c in range(d // LANES)], axis=1)
    o_ref[...] = x1_ref[...] + gate2 * _rms(y, gpost_ref[...])


def _final(dest, x1, mod3, g_post, ys, *, seq, tmf):
    t, d = x1.shape
    ns = seq // tmf
    return pl.pallas_call(
        _final_kernel,
        grid_spec=pltpu.PrefetchScalarGridSpec(
            num_scalar_prefetch=1, grid=(t // tmf,),
            in_specs=[pl.BlockSpec((tmf, d), lambda i, dest: (i, 0)),
                      pl.BlockSpec((1, 6, d), lambda i, dest: (i // ns, 0, 0)),
                      pl.BlockSpec((1, d), lambda i, dest: (0, 0)),
                      pl.BlockSpec(memory_space=pl.ANY)],
            out_specs=pl.BlockSpec((tmf, d), lambda i, dest: (i, 0)),
            scratch_shapes=[pltpu.VMEM((2, tmf * ROW_CHUNKS, LANES), F32), pltpu.SemaphoreType.DMA((2,))]),
        out_shape=jax.ShapeDtypeStruct((t, d), F32),
        compiler_params=pltpu.CompilerParams(dimension_semantics=("arbitrary",), vmem_limit_bytes=VMEM_LIMIT),
        name="final",
    )(dest, x1, mod3, g_post, ys)


def _tile_map(counts, *, capacity, tmx):
    ntile = (counts + tmx - 1) // tmx
    ends = jnp.cumsum(ntile)
    starts = ends - ntile
    j = jnp.arange(capacity // tmx + N_GROUPS, dtype=jnp.int32)
    g = jnp.minimum(jnp.sum((j[:, None] >= ends[None, :]).astype(jnp.int32), axis=1), N_GROUPS - 1)
    local = j - starts[g]
    valid = j < ends[-1]
    blk = g * (capacity // tmx) + local
    rows = jnp.clip(counts[g] - local * tmx, 0, tmx)
    last = ends[-1] - 1
    return (jnp.where(valid, g, g[last]).astype(jnp.int32), jnp.where(valid, blk, blk[last]).astype(jnp.int32),
            jnp.where(valid, rows, 0).astype(jnp.int32))


def _pick_tile(seq, want):
    return want if seq % want == 0 else seq


def _layer(x, c, positions, w_ada, b_ada, g_pre_mix, g_post_mix, g_pre_ffn, g_post_ffn, w_in, g_cq, w_uq,
           g_ckv, w_ukv, w_o_mla, mu_shift, w_decay_up, decay_bias, w_a_up, a_bias, w_g_up, k_k, k_a, r_k,
           lnx_w, lnx_b, w_o_rwkv, w_out, w_router_group, b_router_group, w_router_expert, b_router_expert,
           w_exp_gate, w_exp_up, w_exp_down):
    batch, seq, d = x.shape
    t = batch * seq
    x2 = x.reshape(t, d)
    row1 = lambda a: a.reshape(1, -1)

    mod3 = _ada(c, w_ada, b_ada).reshape(batch, 6, d)

    zeros = lambda n: jnp.zeros((d, n), F32)
    w_c = jnp.concatenate([w_in[:, :MLA_Q_RANK + MLA_KV_RANK], zeros(MLA_NOPE),
                           w_in[:, MLA_Q_RANK + MLA_KV_RANK:MLA_IN], zeros(HEAD_PAD - MLA_NOPE - MLA_ROPE)],
                          axis=1).astype(BF16)
    w_r = w_in[:, MLA_IN:MLA_IN + RWKV_IN].astype(BF16)
    w_g = w_in[:, MLA_IN + RWKV_IN:].astype(BF16)
    w_uq_p = jnp.pad(w_uq.reshape(MLA_Q_RANK, HEADS, MLA_NOPE + MLA_ROPE),
                     ((0, 0), (0, 0), (0, HEAD_PAD - MLA_NOPE - MLA_ROPE))).reshape(MLA_Q_RANK, QK_WIDTH).astype(BF16)
    w_ukv3 = w_ukv.reshape(MLA_KV_RANK, HEADS, MLA_NOPE + MLA_V)
    w_uk_p = jnp.pad(w_ukv3[..., :MLA_NOPE], ((0, 0), (0, 0), (0, HEAD_PAD - MLA_NOPE))
                     ).reshape(MLA_KV_RANK, QK_WIDTH).astype(BF16)
    w_uv_t = jnp.pad(w_ukv3[..., MLA_NOPE:], ((0, 0), (0, 0), (0, V_ROWS - MLA_V))
                     ).reshape(MLA_KV_RANK, HEADS * V_ROWS).T.astype(BF16)
    inv_freq = jnp.power(ROPE_THETA, -jnp.arange(0, MLA_ROPE, 2, dtype=F32) / MLA_ROPE)
    invf = jnp.concatenate([jnp.zeros((MLA_NOPE,), F32), inv_freq, inv_freq,
                            jnp.zeros((HEAD_PAD - MLA_NOPE - MLA_ROPE,), F32)]).reshape(1, LANES)

    tm = _pick_tile(seq, 512)
    q_all, k_all, vt_all, zs, gates = _inproj(
        x2, mod3, positions.reshape(batch, 1, seq), row1(g_pre_mix), w_c, w_r, w_g, row1(g_cq), row1(g_ckv),
        w_uq_p, w_uk_p, w_uv_t, row1(mu_shift), invf, batch=batch, seq=seq, tm=tm)

    attn = _attn(q_all, k_all, vt_all, batch=batch, seq=seq, tq=ATTN_BLOCK)

    tc = 2 * CHUNK * RWKV_CHUNKS_PER_STEP
    w_da = jnp.concatenate([
        jnp.concatenate([w_decay_up, jnp.zeros_like(w_decay_up)], axis=1),
        jnp.concatenate([jnp.zeros_like(w_a_up), w_a_up], axis=1)], axis=0).astype(BF16)
    hid = jnp.arange(2 * LANES) // RWKV_HEAD
    ones_bd = (hid[:, None] == hid[None, :]).astype(BF16)
    tid = jnp.arange(tc // 2)
    same_chunk = (tid[:, None] // CHUNK) == (tid[None, :] // CHUNK)
    tri = (same_chunk & (tid[None, :] <= tid[:, None])).astype(BF16)
    rw = _rwkv(zs, w_da, w_g_up.astype(BF16), row1(decay_bias), row1(a_bias), row1(k_k), row1(k_a), row1(r_k),
               row1(lnx_w), row1(lnx_b), ones_bd, tri, batch=batch, seq=seq, tc=tc)

    w_rt = jnp.concatenate([w_router_expert, w_router_group,
                            jnp.zeros((d, ROUTER_WIDTH - N_EXPERTS - N_GROUPS), F32)], axis=1)
    wr_hi = w_rt.astype(BF16)
    wr_lo = (w_rt - wr_hi.astype(F32)).astype(BF16)
    b_r = jnp.concatenate([b_router_expert, b_router_group,
                           jnp.zeros((ROUTER_WIDTH - N_EXPERTS - N_GROUPS,), F32)]).reshape(1, ROUTER_WIDTH)
    x1, hp, dest, cnt = _merge(attn, rw, gates, x2, mod3, w_o_mla.astype(BF16), w_o_rwkv.astype(BF16),
                               w_out.astype(BF16), row1(g_post_mix), row1(g_pre_ffn), wr_hi, wr_lo, b_r,
                               seq=seq, tm=_pick_tile(seq, 512))

    tmx = _pick_tile(seq, 512)
    dest = dest.reshape(t)
    xs = _dispatch(dest, hp, tokens_out=N_GROUPS * t, td=_pick_tile(seq, 2048))
    tile_group, tile_blk, tile_rows = _tile_map(cnt[0, :N_GROUPS].astype(jnp.int32), capacity=t, tmx=tmx)
    eid = jnp.arange(EXPERTS_PER_GROUP * EXPERT_FF) // EXPERT_FF
    expand = (jnp.arange(ROUTER_WIDTH)[:, None] == eid[None, :]).astype(BF16)
    ys = _experts(tile_group, tile_blk, tile_rows, xs, w_exp_gate.astype(BF16), w_exp_up.astype(BF16),
                  w_exp_down.astype(BF16), expand, tmx=tmx)
    out = _final(dest, x1, mod3, row1(g_post_ffn), ys, seq=seq, tmf=_pick_tile(seq, 256))
    return out.reshape(batch, seq, d)


def kernel(x, c, positions, w_ada, b_ada, g_pre_mix, g_post_mix, g_pre_ffn, g_post_ffn, w_in, g_cq, w_uq, g_ckv, w_ukv, w_o_mla, mu_shift, w_decay_up, decay_bias, w_a_up, a_bias, w_g_up, k_k, k_a, r_k, lnx_w, lnx_b, w_o_rwkv, w_out, w_router_group, b_router_group, w_router_expert, b_router_expert, w_exp_gate, w_exp_up, w_exp_down):
    depth = w_ada.shape[0]
    for l in range(depth):
        x = _layer(x, c, positions, w_ada[l], b_ada[l], g_pre_mix[l], g_post_mix[l], g_pre_ffn[l], g_post_ffn[l],
                   w_in[l], g_cq[l], w_uq[l], g_ckv[l], w_ukv[l], w_o_mla[l], mu_shift[l], w_decay_up[l],
                   decay_bias[l], w_a_up[l], a_bias[l], w_g_up[l], k_k[l], k_a[l], r_k[l], lnx_w[l], lnx_b[l],
                   w_o_rwkv[l], w_out[l], w_router_group[l], b_router_group[l], w_router_expert[l],
                   b_router_expert[l], w_exp_gate[l], w_exp_up[l], w_exp_down[l])
    return x
```

```python
import functools
import math

import jax
import jax.numpy as jnp
from jax import lax
from jax.experimental import pallas as pl
from jax.experimental.pallas import tpu as pltpu

F32 = jnp.float32
BF16 = jnp.bfloat16

D_MODEL = 1024
CHUNK = 64
HEADS = 8
MLA_NOPE = 64
MLA_ROPE = 32
MLA_V = 64
MLA_Q_RANK = 384
MLA_KV_RANK = 256
ROPE_THETA = 10000.0
RWKV_HEAD = 64
RWKV_WIDTH = HEADS * RWKV_HEAD
DECAY_RANK = 64
AAA_RANK = 64
GATE_RANK = 128
GN_EPS = 64e-5
N_GROUPS = 4
EXPERTS_PER_GROUP = 8
N_EXPERTS = N_GROUPS * EXPERTS_PER_GROUP
EXPERT_FF = 256
RMS_EPS = 1e-6
NEG_INF = -1e30
MLA_IN = MLA_Q_RANK + MLA_KV_RANK + MLA_ROPE
RWKV_IN = 3 * RWKV_WIDTH + DECAY_RANK + AAA_RANK + GATE_RANK

LANES = 128
DMA_UNROLL = 8
ROW_CHUNKS = 9
HEAD_PAD = LANES
QK_WIDTH = HEADS * HEAD_PAD
MLA_C_WIDTH = 768
ROUTER_WIDTH = LANES
VMEM_LIMIT = 56 * 1024 * 1024
ATTN_HEADS_PER_STEP = 8
ATTN_BLOCK = 256
ATTN_BLOCKS_PER_ITER = 2
V_ROWS = 80
RWKV_CHUNKS_PER_STEP = 4
MERGE_SUBTILES = 2


def _dot(a, b):
    return jnp.dot(a, b, preferred_element_type=F32)


def _dot_nt(a, b):
    return lax.dot_general(a, b, (((1,), (1,)), ((), ())), preferred_element_type=F32)


def _dot_tn(a, b):
    return lax.dot_general(a, b, (((0,), (0,)), ((), ())), preferred_element_type=F32)


def _split(x):
    hi = x.astype(BF16)
    lo = (x - hi.astype(F32)).astype(BF16)
    return hi, lo


def _dot3(x, w_hi, w_lo):
    x_hi, x_lo = _split(x)
    return _dot(x_hi, w_hi) + (_dot(x_hi, w_lo) + _dot(x_lo, w_hi))


def _sigmoid(x):
    return 1.0 / (1.0 + jnp.exp(-x))


def _rms(x, g):
    return x * lax.rsqrt(jnp.mean(x * x, axis=-1, keepdims=True) + RMS_EPS) * g


def _ada_kernel(c_ref, w_ref, b_ref, o_ref):
    c = c_ref[...]
    s = c * _sigmoid(c)
    w_hi, w_lo = _split(w_ref[...])
    o_ref[...] = _dot3(s, w_hi, w_lo) + b_ref[...]


def _ada(c, w_ada, b_ada):
    b, d = c.shape
    n = w_ada.shape[1]
    tn = 512
    return pl.pallas_call(
        _ada_kernel,
        grid=(n // tn,),
        in_specs=[pl.BlockSpec((b, d), lambda j: (0, 0)),
                  pl.BlockSpec((d, tn), lambda j: (0, j)),
                  pl.BlockSpec((1, tn), lambda j: (0, j))],
        out_specs=pl.BlockSpec((b, tn), lambda j: (0, j)),
        out_shape=jax.ShapeDtypeStruct((b, n), F32),
        name="ada",
    )(c, w_ada, b_ada.reshape(1, n))


def _inproj_kernel(x_ref, mod_ref, pos_ref, gpre_ref, wc_ref, wr_ref, wg_ref, gcq_ref, gckv_ref,
                   wuq_ref, wuk_ref, wuv_ref, mu_ref, invf_ref,
                   q_ref, k_ref, vt_ref, zs_ref, gate_ref, carry_ref):
    i = pl.program_id(1)
    tm = x_ref.shape[0]

    @pl.when(i == 0)
    def _():
        carry_ref[...] = jnp.zeros_like(carry_ref)

    mod = mod_ref[0]
    shift, scale = mod[0:1], mod[1:2]
    h = (_rms(x_ref[...], gpre_ref[...]) * (1.0 + scale) + shift).astype(BF16)

    zc = _dot(h, wc_ref[...])
    zg = _dot(h, wg_ref[...])
    nq = _rms(zc[:, :MLA_Q_RANK], gcq_ref[...]).astype(BF16)
    nkv = _rms(zc[:, MLA_Q_RANK:MLA_Q_RANK + MLA_KV_RANK], gckv_ref[...]).astype(BF16)
    kr = zc[:, MLA_Q_RANK + MLA_KV_RANK:]
    qf = _dot(nq, wuq_ref[...])
    kn = _dot(nkv, wuk_ref[...])
    vt = _dot_nt(wuv_ref[...], nkv)
    zr = _dot(h, wr_ref[...])

    pos_col = jnp.broadcast_to(pos_ref[...].astype(F32), (LANES, tm)).T
    ang = pos_col * invf_ref[...]
    lane = lax.broadcasted_iota(jnp.int32, (1, LANES), 1)
    first_half = lane < MLA_NOPE + MLA_ROPE // 2
    in_rope = (lane >= MLA_NOPE) & (lane < MLA_NOPE + MLA_ROPE)
    cos_t = jnp.where(in_rope, jnp.cos(ang), 1.0)
    sin_a = jnp.sin(ang)
    sin_t = jnp.where(in_rope, jnp.where(first_half, -sin_a, sin_a), 0.0)

    def rope(t):
        rot = jnp.where(first_half, pltpu.roll(t, LANES - MLA_ROPE // 2, 1), pltpu.roll(t, MLA_ROPE // 2, 1))
        return t * cos_t + rot * sin_t

    kr_rot = rope(kr)
    q_scale = math.log2(math.e) / math.sqrt(MLA_NOPE + MLA_ROPE)
    for hh in range(HEADS):
        sl = slice(hh * HEAD_PAD, (hh + 1) * HEAD_PAD)
        q_ref[:, sl] = (rope(qf[:, sl]) * q_scale).astype(BF16)
        k_ref[:, sl] = (kn[:, sl] + kr_rot).astype(BF16)
    vrow = lax.broadcasted_iota(jnp.int32, vt.shape, 0) % V_ROWS
    vt = jnp.where(vrow == MLA_V, 1.0, vt).astype(BF16)
    for j in range(vt_ref.shape[0]):
        vt_ref[j] = vt[:, j * ATTN_BLOCK:(j + 1) * ATTN_BLOCK]

    gate_ref[...] = _sigmoid(zg).astype(BF16)

    row = lax.broadcasted_iota(jnp.int32, (tm, 1), 0)
    prev = jnp.where(row == 0, carry_ref[0:1, :], pltpu.roll(zr, 1, 0))
    carry_ref[0:1, :] = zr[tm - 1:tm, :]
    zs_ref[...] = (zr + (prev - zr) * mu_ref[...]).astype(BF16)


def _const_spec(shape):
    nd = len(shape)
    return pl.BlockSpec(shape, lambda *_: (0,) * nd, pipeline_mode=pl.Buffered(1))


def _inproj(x2, mod3, pos2, g_pre, w_c, w_r, w_g, g_cq, g_ckv, w_uq, w_uk, w_uv, mu, invf, *, batch, seq, tm):
    t, d = x2.shape
    ns = seq // tm
    row = lambda b, i: (b * ns + i, 0)
    widths = [QK_WIDTH, QK_WIDTH, RWKV_IN, 2 * d]
    specs = [pl.BlockSpec((tm, w), row) for w in widths]
    shapes = [jax.ShapeDtypeStruct((t, w), BF16) for w in widths]
    vw = HEADS * V_ROWS
    nb = tm // ATTN_BLOCK
    specs.insert(2, pl.BlockSpec((None, nb, vw, ATTN_BLOCK), lambda b, i: (b, i, 0, 0)))
    shapes.insert(2, jax.ShapeDtypeStruct((batch, ns * nb, vw, ATTN_BLOCK), BF16))
    return pl.pallas_call(
        _inproj_kernel,
        grid=(batch, ns),
        in_specs=[pl.BlockSpec((tm, d), row),
                  pl.BlockSpec((1, 6, d), lambda b, i: (b, 0, 0)),
                  pl.BlockSpec((None, 1, tm), lambda b, i: (b, 0, i)),
                  _const_spec(g_pre.shape), _const_spec(w_c.shape), _const_spec(w_r.shape),
                  _const_spec(w_g.shape), _const_spec(g_cq.shape), _const_spec(g_ckv.shape),
                  _const_spec(w_uq.shape), _const_spec(w_uk.shape), _const_spec(w_uv.shape),
                  _const_spec(mu.shape), _const_spec(invf.shape)],
        out_specs=specs,
        out_shape=shapes,
        scratch_shapes=[pltpu.VMEM((8, RWKV_IN), F32)],
        compiler_params=pltpu.CompilerParams(dimension_semantics=("parallel", "arbitrary"),
                                             vmem_limit_bytes=VMEM_LIMIT),
        name="inproj",
    )(x2, mod3, pos2, g_pre, w_c, w_r, w_g, g_cq, g_ckv, w_uq, w_uk, w_uv, mu, invf)


def _attn_kernel(q_ref, k_ref, vt_ref, o_ref):
    i = pl.program_id(2)
    tq = q_ref.shape[0]
    kc = lax.broadcasted_iota(jnp.int32, (tq, tq), 0) // CHUNK
    qc = lax.broadcasted_iota(jnp.int32, (tq, tq), 1) // CHUNK
    diag_mask = kc <= qc
    heads = range(ATTN_HEADS_PER_STEP)
    hs = [slice(h * HEAD_PAD, (h + 1) * HEAD_PAD) for h in heads]
    vr = [slice(h * V_ROWS, (h + 1) * V_ROWS) for h in heads]
    q = [q_ref[:, s] for s in hs]

    def block(kb, carry, nblk, masked):
        kbs = [kb + n for n in range(nblk)]
        rows = [pl.ds(pl.multiple_of(b * tq, tq), tq) for b in kbs]
        s = [[_dot_nt(k_ref[r, hs[h]], q[h]) for r in rows] for h in heads]
        if masked:
            s = [[jnp.where(diag_mask, t, NEG_INF) for t in sh] for sh in s]
        m_new = []
        for h in heads:
            m = carry[h][0]
            for t in s[h]:
                m = jnp.maximum(m, jnp.max(t, axis=0, keepdims=True))
            m_new.append(m)
        alpha = [jnp.exp2(carry[h][0] - m_new[h]) for h in heads]
        p = [jnp.concatenate([jnp.exp2(t - m_new[h]).astype(BF16) for t in s[h]], axis=0) for h in heads]
        acc = [alpha[h] * carry[h][1]
               + _dot(jnp.concatenate([vt_ref[b, vr[h], :] for b in kbs], axis=1), p[h]) for h in heads]
        return tuple((m_new[h], acc[h]) for h in heads)

    init = tuple((jnp.full((1, tq), NEG_INF, F32), jnp.zeros((V_ROWS, tq), F32)) for _ in heads)
    nb = ATTN_BLOCKS_PER_ITER
    carry = lax.fori_loop(0, i // nb, lambda n, c: block(n * nb, c, nb, False), init)
    carry = lax.fori_loop((i // nb) * nb, i, lambda kb, c: block(kb, c, 1, False), carry)
    final = block(i, carry, 1, True)
    outs = [final[h][1][:MLA_V] / final[h][1][MLA_V:MLA_V + 1] for h in heads]
    for pr in range(ATTN_HEADS_PER_STEP // 2):
        pair_t = jnp.concatenate([outs[2 * pr], outs[2 * pr + 1]], axis=0)
        o_ref[:, pr * LANES:(pr + 1) * LANES] = pair_t.T.astype(BF16)


def _attn(q_all, k_all, vt_all, *, batch, seq, tq):
    t = q_all.shape[0]
    nq = seq // tq
    g = ATTN_HEADS_PER_STEP
    return pl.pallas_call(
        _attn_kernel,
        grid=(batch, HEADS // g, nq),
        in_specs=[pl.BlockSpec((tq, g * HEAD_PAD), lambda b, p, i: (b * nq + i, p)),
                  pl.BlockSpec((seq, g * HEAD_PAD), lambda b, p, i: (b, p)),
                  pl.BlockSpec((None, nq, g * V_ROWS, tq), lambda b, p, i: (b, 0, p, 0))],
        out_specs=pl.BlockSpec((tq, g * MLA_V), lambda b, p, i: (b * nq + i, p)),
        out_shape=jax.ShapeDtypeStruct((t, HEADS * MLA_V), BF16),
        compiler_params=pltpu.CompilerParams(dimension_semantics=("parallel", "parallel", "arbitrary"),
                                             vmem_limit_bytes=VMEM_LIMIT),
        name="attn",
    )(q_all, k_all, vt_all)


def _rwkv_kernel(zs_ref, wda_ref, wgu_ref, dbias_ref, abias_ref, kk_ref, ka_ref, rk_ref, lnw_ref, lnb_ref,
                 ones_ref, tri_ref,
                 o_ref,
                 st_ref, a_s, r_s, b_s, k_s, bg_s, kg_s, v_s, gc_s, bonus_s, g_s, y_s):
    w = RWKV_WIDTH
    nsub = RWKV_CHUNKS_PER_STEP
    th = nsub * CHUNK
    n2 = 2 * CHUNK
    staged = (a_s, r_s, b_s, k_s, bg_s, kg_s, v_s, gc_s, bonus_s, g_s)

    @pl.when(pl.program_id(1) == 0)
    def _():
        st_ref[...] = jnp.zeros_like(st_ref)

    ones_bd = ones_ref[...]

    def head_sum(t):
        tb = t.astype(BF16)
        hw = ones_bd.shape[0]
        return jnp.concatenate([_dot(tb[:, n * hw:(n + 1) * hw], ones_bd) for n in range(w // hw)], axis=1)

    ri = lax.broadcasted_iota(jnp.int32, (4 * CHUNK, n2), 0)
    ci = lax.broadcasted_iota(jnp.int32, (4 * CHUNK, n2), 1) % CHUNK
    tri_mask = (ri % CHUNK + ri // n2) > ci
    lane2 = lax.broadcasted_iota(jnp.int32, (CHUNK, n2), 1)
    h1 = lane2 < RWKV_HEAD
    bi = lax.broadcasted_iota(jnp.int32, (n2, n2), 0) // RWKV_HEAD
    bj = lax.broadcasted_iota(jnp.int32, (n2, n2), 1) // RWKV_HEAD
    bd_mask = bi == bj
    zero_bf = jnp.zeros((CHUNK, n2), BF16)
    zero_f = jnp.zeros((CHUNK, n2), F32)
    lane = lax.broadcasted_iota(jnp.int32, (1, LANES), 1)

    def prepare(h):
        hr = slice(h * th, (h + 1) * th)
        zs = zs_ref[hr, :].astype(F32)
        r, k, v = zs[:, 0:w], zs[:, w:2 * w], zs[:, 2 * w:3 * w]
        da = zs[:, 3 * w:3 * w + LANES]
        gd = zs[:, 3 * w + LANES:]
        lora_in = jnp.where(lane < DECAY_RANK, jnp.tanh(da), da).astype(BF16)
        pre = _dot(lora_in, wda_ref[...])
        g_new = _dot(_sigmoid(gd).astype(BF16), wgu_ref[...])
        yield
        u = -(dbias_ref[...] + pre[:, :w])
        softplus = jnp.maximum(u, 0.0) + jnp.log(1.0 + jnp.exp(-jnp.abs(u)))
        logw = -jnp.exp(-softplus - 0.5)
        eta = _sigmoid(abias_ref[...] + pre[:, w:])
        kk = k * kk_ref[...]
        kk_ss = head_sum(kk * kk)
        lw_hi, lw_lo = _split(logw)
        cum = _dot(tri_ref[...], lw_hi) + _dot(tri_ref[...], lw_lo)
        yield
        kk = kk * jnp.minimum(lax.rsqrt(kk_ss), 1e12)
        kp = k * (1.0 + (eta - 1.0) * ka_ref[...])
        bonus_new = head_sum(r * kp * rk_ref[...]) * v
        yield
        cum_end = jnp.concatenate([jnp.broadcast_to(cum[(c + 1) * CHUNK - 1:(c + 1) * CHUNK, :], (CHUNK, w))
                                   for c in range(nsub)], axis=0)
        b_in = kk * eta
        g_inv = jnp.exp(-cum)
        g_rem = jnp.exp(cum_end - cum)
        staged_new = ((-kk * jnp.exp(cum - logw)).astype(BF16), (r * jnp.exp(cum)).astype(BF16),
                      (b_in * g_inv).astype(BF16), (kp * g_inv).astype(BF16),
                      (b_in * g_rem).astype(BF16), (kp * g_rem).astype(BF16), v.astype(BF16),
                      jnp.exp(cum_end), bonus_new, g_new)
        for ref, val in zip(staged, staged_new):
            ref[hr, :] = val
        yield

    def solve(h):
        blocks = [(sub, p) for sub in range(nsub) for p in range(HEADS // 2)]
        rows = [slice(h * th + sub * CHUNK, h * th + (sub + 1) * CHUNK) for sub in range(nsub)]
        lns = [slice(p * n2, (p + 1) * n2) for p in range(HEADS // 2)]
        a_t = [a_s[rows[sub], lns[p]] for sub, p in blocks]
        r_t = [r_s[rows[sub], lns[p]] for sub, p in blocks]
        v_t = [v_s[rows[sub], lns[p]] for sub, p in blocks]
        ml = []
        for n, (sub, p) in enumerate(blocks):
            lhs = jnp.concatenate([jnp.where(h1, a_t[n], zero_bf), jnp.where(h1, zero_bf, a_t[n]),
                                   jnp.where(h1, r_t[n], zero_bf), jnp.where(h1, zero_bf, r_t[n])], axis=0)
            rhs = jnp.concatenate([b_s[rows[sub], lns[p]], k_s[rows[sub], lns[p]]], axis=0)
            ml.append(jnp.where(tri_mask, _dot_nt(lhs, rhs), 0.0))
        ml_b = [m.astype(BF16) for m in ml]
        yield
        units = [(n, hh) for n in range(len(blocks)) for hh in range(2)]
        w_t = [_dot(ml_b[n][hh * CHUNK:(hh + 1) * CHUNK, :], jnp.concatenate([zero_bf, v_t[n]], axis=0))
               for n, hh in units]
        a_f = [t.astype(F32) for t in a_t]
        a_sw = [pltpu.roll(t, RWKV_HEAD, 1) for t in a_f]
        x = [jnp.where(h1, a_f[n], pltpu.roll(w_t[2 * n], RWKV_HEAD, 1)) if hh == 0
             else jnp.where(h1, a_sw[n], w_t[2 * n + 1]) for n, hh in units]
        lsq = [jnp.where(h1, ml[n][hh * CHUNK:(hh + 1) * CHUNK, :], 0.0) for n, hh in units]
        yield
        for _ in range(6):
            out = [_dot(lsq[u][:, :CHUNK].astype(BF16),
                        jnp.concatenate([x[u], lsq[u]], axis=1).astype(BF16)) for u in range(len(units))]
            x = [x[u] + out[u][:, :n2] for u in range(len(units))]
            lsq = [t[:, n2:] for t in out]
            yield
        z = []
        for n in range(len(blocks)):
            x0, x1 = x[2 * n], x[2 * n + 1]
            a_hat = jnp.where(h1, x0, pltpu.roll(x1, RWKV_HEAD, 1))
            u0 = jnp.where(h1, pltpu.roll(x0, RWKV_HEAD, 1), x1)
            z.append(jnp.concatenate([jnp.concatenate([a_hat, u0], axis=1),
                                      jnp.concatenate([zero_f, v_t[n].astype(F32)], axis=1)],
                                     axis=0).astype(BF16))
        o0 = [_dot(ml_b[n][2 * CHUNK:3 * CHUNK, :], z[n]) for n in range(len(blocks))]
        o1 = [_dot(ml_b[n][3 * CHUNK:4 * CHUNK, :], z[n]) for n in range(len(blocks))]
        pm = [_dot_tn(z[n], jnp.concatenate([bg_s[rows[sub], lns[p]], kg_s[rows[sub], lns[p]]], axis=0))
              for n, (sub, p) in enumerate(blocks)]
        r_hat = [(r_t[n].astype(F32) + jnp.where(h1, o0[n][:, :n2], o1[n][:, :n2])).astype(BF16)
                 for n in range(len(blocks))]
        yield
        state = [st_ref[p] for p in range(HEADS // 2)]
        for n, (sub, p) in enumerate(blocks):
            y0 = jnp.where(h1, o0[n][:, n2:], o1[n][:, n2:])
            g_t = jnp.where(bd_mask, pm[n][:n2], 0.0).astype(BF16)
            h_t = jnp.where(bd_mask, pm[n][n2:], 0.0)
            s_b = state[p].astype(BF16)
            y_s[rows[sub], lns[p]] = _dot_nt(r_hat[n], s_b) + y0
            gc = gc_s[rows[sub].start:rows[sub].start + 1, lns[p]]
            state[p] = gc * state[p] + _dot(s_b, g_t) + h_t
        for p in range(HEADS // 2):
            st_ref[p] = state[p]
        yield

    def finish(h):
        hr = slice(h * th, (h + 1) * th)
        y = y_s[hr, :]
        inv_n = 1.0 / RWKV_HEAD
        dev = y - head_sum(y) * inv_n
        yield
        var = head_sum(dev * dev) * inv_n
        yield
        out = (dev * lax.rsqrt(var + GN_EPS) * lnw_ref[...] + lnb_ref[...] + bonus_s[hr, :]) * g_s[hr, :]
        o_ref[hr, :] = out.astype(BF16)
        yield

    def emit(main, side=None, every=1):
        n = 0
        for _ in main:
            n += 1
            if side is not None and n % every == 0:
                next(side, None)
        if side is not None:
            for _ in side:
                pass

    emit(prepare(0))
    emit(solve(0), prepare(1), every=2)
    emit(solve(1), finish(0), every=3)
    emit(finish(1))


def _rwkv(zs, w_da, w_gu, dbias, abias, k_k, k_a, r_k, lnw, lnb, ones_bd, tri, *, batch, seq, tc):
    t = zs.shape[0]
    nc = seq // tc
    w = RWKV_WIDTH
    consts = [w_da, w_gu, dbias, abias, k_k, k_a, r_k, lnw, lnb, ones_bd, tri]
    return pl.pallas_call(
        _rwkv_kernel,
        grid=(batch, nc),
        in_specs=[pl.BlockSpec((tc, RWKV_IN), lambda b, i: (b * nc + i, 0))] + [_const_spec(a.shape) for a in consts],
        out_specs=pl.BlockSpec((tc, w), lambda b, i: (b * nc + i, 0)),
        out_shape=jax.ShapeDtypeStruct((t, w), BF16),
        scratch_shapes=[pltpu.VMEM((HEADS // 2, 2 * RWKV_HEAD, 2 * RWKV_HEAD), F32)]
        + [pltpu.VMEM((tc, w), BF16)] * 7
        + [pltpu.VMEM((tc, w), F32)] * 4,
        compiler_params=pltpu.CompilerParams(dimension_semantics=("parallel", "arbitrary"),
                                             vmem_limit_bytes=VMEM_LIMIT),
        name="rwkv",
    )(zs, *consts)


def _route(logits):
    lane = lax.broadcasted_iota(jnp.int32, logits.shape, 1)
    lane_f = lane.astype(F32)
    big = float(ROUTER_WIDTH)
    is_group = (lane >= N_EXPERTS) & (lane < N_EXPERTS + N_GROUPS)
    gl = jnp.where(is_group, logits, NEG_INF)
    g_max = jnp.max(gl, axis=-1, keepdims=True)
    g_sel = jnp.min(jnp.where(gl == g_max, lane_f, big), axis=-1, keepdims=True) - float(N_EXPERTS)
    p_sel = 1.0 / jnp.sum(jnp.where(is_group, jnp.exp(gl - g_max), 0.0), axis=-1, keepdims=True)
    in_group = (lane < N_EXPERTS) & ((lane // EXPERTS_PER_GROUP).astype(F32) == g_sel)
    el = jnp.where(in_group, logits, NEG_INF)
    t1 = jnp.max(el, axis=-1, keepdims=True)
    i1 = jnp.min(jnp.where(el == t1, lane_f, big), axis=-1, keepdims=True)
    el2 = jnp.where(lane_f == i1, NEG_INF, el)
    t2 = jnp.max(el2, axis=-1, keepdims=True)
    i2 = jnp.min(jnp.where(el2 == t2, lane_f, big), axis=-1, keepdims=True)
    e21 = jnp.exp(t2 - t1)
    w1 = p_sel / (1.0 + e21)
    w2 = w1 * e21
    first = g_sel * float(EXPERTS_PER_GROUP)
    cw = jnp.where(lane_f == i1 - first, w1, 0.0) + jnp.where(lane_f == i2 - first, w2, 0.0)
    return g_sel, cw


def _merge_kernel(attn_ref, rw_ref, gate_ref, x_ref, mod_ref, womla_ref, worw_ref, wout_ref,
                  gpost_ref, gffn_ref, wrh_ref, wrl_ref, br_ref, stril_ref,
                  x1_ref, hp_ref, dest_ref, cnt_ref, carry_ref, *, group_capacity):
    i = pl.program_id(0)
    tm, d = x_ref.shape

    @pl.when(i == 0)
    def _():
        carry_ref[...] = jnp.zeros_like(carry_ref)

    mod = mod_ref[0]
    gate1, shift2, scale2 = mod[2:3], mod[3:4], mod[4:5]

    ts = tm // MERGE_SUBTILES
    subs = [slice(n * ts, (n + 1) * ts) for n in range(MERGE_SUBTILES)]
    o_mla = [_dot(attn_ref[r, :], womla_ref[...]) for r in subs]
    o_rw = [_dot(rw_ref[r, :], worw_ref[...]) for r in subs]
    o = [(gate_ref[r, :d].astype(F32) * o_mla[n] + gate_ref[r, d:].astype(F32) * o_rw[n]).astype(BF16)
         for n, r in enumerate(subs)]
    y = [_dot(t, wout_ref[...]) for t in o]
    x1 = [x_ref[r, :] + gate1 * _rms(y[n], gpost_ref[...]) for n, r in enumerate(subs)]
    h2 = [_rms(t, gffn_ref[...]) * (1.0 + scale2) + shift2 for t in x1]
    logits = [_dot3(t, wrh_ref[...], wrl_ref[...]) + br_ref[...] for t in h2]
    for n, r in enumerate(subs):
        x1_ref[r, :] = x1[n]
        for c in range(d // LANES):
            hp_ref[pl.ds(n * ts * ROW_CHUNKS + c, ts, stride=ROW_CHUNKS), :] = h2[n][:, c * LANES:(c + 1) * LANES]

    lane_f = lax.broadcasted_iota(jnp.int32, (ts, ROUTER_WIDTH), 1).astype(F32)
    routes = [_route(t) for t in logits]
    g_sels = [g for g, _ in routes]
    for n, r in enumerate(subs):
        hp_ref[pl.ds(n * ts * ROW_CHUNKS + d // LANES, ts, stride=ROW_CHUNKS), :] = routes[n][1]
    onehots = [jnp.where(lane_f == g, 1.0, 0.0) for g in g_sels]

    onehot = jnp.concatenate(onehots, axis=0)
    earlier = _dot(stril_ref[...], onehot.astype(BF16))
    carry = carry_ref[0:1, :]
    for n, r in enumerate(subs):
        rank = jnp.sum(jnp.where(lane_f == g_sels[n], carry + earlier[r, :], 0.0), axis=-1, keepdims=True)
        dest_col = jnp.broadcast_to(g_sels[n] * float(group_capacity) + rank, (ts, LANES))
        dest_ref[:, r] = dest_col.T[0:1, :].astype(jnp.int32)
    total = jnp.broadcast_to(carry + jnp.sum(onehot, axis=0, keepdims=True), carry_ref.shape)
    carry_ref[...] = total
    cnt_ref[...] = total


def _merge(attn, rw, gates, x2, mod3, w_o_mla, w_o_rwkv, w_out, g_post, g_ffn, wr_hi, wr_lo, b_r, *, seq, tm):
    t, d = x2.shape
    ns = seq // tm
    row = lambda i: (i, 0)
    tid = jnp.arange(tm)
    stril = (tid[None, :] < tid[:, None]).astype(BF16)
    consts = [w_o_mla, w_o_rwkv, w_out, g_post, g_ffn, wr_hi, wr_lo, b_r, stril]
    return pl.pallas_call(
        functools.partial(_merge_kernel, group_capacity=t),
        grid=(t // tm,),
        in_specs=[pl.BlockSpec((tm, attn.shape[1]), row), pl.BlockSpec((tm, rw.shape[1]), row),
                  pl.BlockSpec((tm, 2 * d), row), pl.BlockSpec((tm, d), row),
                  pl.BlockSpec((1, 6, d), lambda i: (i // ns, 0, 0))] + [_const_spec(a.shape) for a in consts],
        out_specs=[pl.BlockSpec((tm, d), row), pl.BlockSpec((tm * ROW_CHUNKS, LANES), row),
                   pl.BlockSpec((None, 1, tm), lambda i: (i, 0, 0)), pl.BlockSpec((8, ROUTER_WIDTH), lambda i: (0, 0))],
        out_shape=[jax.ShapeDtypeStruct((t, d), F32), jax.ShapeDtypeStruct((t * ROW_CHUNKS, LANES), F32),
                   jax.ShapeDtypeStruct((t // tm, 1, tm), jnp.int32), jax.ShapeDtypeStruct((8, ROUTER_WIDTH), F32)],
        scratch_shapes=[pltpu.VMEM((8, ROUTER_WIDTH), F32)],
        compiler_params=pltpu.CompilerParams(dimension_semantics=("arbitrary",), vmem_limit_bytes=VMEM_LIMIT),
        name="merge",
    )(attn, rw, gates, x2, mod3, *consts)


def _token_rows(i):
    return pl.ds(i * ROW_CHUNKS, ROW_CHUNKS)


def _chunk(ref, c, n):
    return ref[pl.ds(c, n, stride=ROW_CHUNKS), :]


def _dispatch_kernel(dest_ref, hp_ref, xs_ref, sem):
    td = hp_ref.shape[0] // ROW_CHUNKS
    base = pl.program_id(0) * td

    def issue(r0, c):
        for j in range(DMA_UNROLL):
            r = r0 * DMA_UNROLL + j
            pltpu.make_async_copy(hp_ref.at[_token_rows(r)], xs_ref.at[_token_rows(dest_ref[base + r])],
                                  sem).start(priority=j % 2)
        return c

    lax.fori_loop(0, td // DMA_UNROLL, issue, 0)
    pltpu.make_async_copy(hp_ref, xs_ref.at[pl.ds(0, td * ROW_CHUNKS)], sem).wait()


def _dispatch(dest, hp, *, tokens_out, td):
    t = hp.shape[0] // ROW_CHUNKS
    return pl.pallas_call(
        _dispatch_kernel,
        grid_spec=pltpu.PrefetchScalarGridSpec(
            num_scalar_prefetch=1, grid=(t // td,),
            in_specs=[pl.BlockSpec((td * ROW_CHUNKS, LANES), lambda i, dest: (i, 0))],
            out_specs=pl.BlockSpec(memory_space=pl.ANY),
            scratch_shapes=[pltpu.SemaphoreType.DMA]),
        out_shape=jax.ShapeDtypeStruct((tokens_out * ROW_CHUNKS, LANES), F32),
        compiler_params=pltpu.CompilerParams(dimension_semantics=("arbitrary",), vmem_limit_bytes=VMEM_LIMIT),
        name="dispatch",
    )(dest, hp)


def _experts_kernel(tg_ref, tb_ref, tr_ref, xs_ref, wg_ref, wu_ref, wd_ref, ex_ref, ys_ref):
    del tg_ref, tb_ref
    nrows = tr_ref[pl.program_id(0)]

    @pl.when(nrows > 0)
    def _():
        tmx = xs_ref.shape[0] // ROW_CHUNKS
        nchunk = D_MODEL // LANES
        valid = lax.broadcasted_iota(jnp.int32, (tmx, 1), 0) < nrows
        x = jnp.concatenate([jnp.where(valid, _chunk(xs_ref, c, tmx), 0.0).astype(BF16) for c in range(nchunk)],
                            axis=1)
        cw_hi, cw_lo = _split(jnp.where(valid, _chunk(xs_ref, nchunk, tmx), 0.0))
        cwx = _dot(cw_hi, ex_ref[...]) + _dot(cw_lo, ex_ref[...])
        acc = jnp.zeros((tmx, D_MODEL), F32)
        for e in range(EXPERTS_PER_GROUP):
            a = _dot(x, wg_ref[e])
            act = a * _sigmoid(a) * _dot(x, wu_ref[e]) * cwx[:, e * EXPERT_FF:(e + 1) * EXPERT_FF]
            acc = acc + _dot(act.astype(BF16), wd_ref[e])
        for c in range(nchunk):
            ys_ref[pl.ds(c, tmx, stride=ROW_CHUNKS), :] = acc[:, c * LANES:(c + 1) * LANES]
        ys_ref[pl.ds(nchunk, tmx, stride=ROW_CHUNKS), :] = jnp.zeros((tmx, LANES), F32)


def _experts(tile_group, tile_blk, tile_rows, xs, w_gate, w_up, w_down, expand, *, tmx):
    nt = tile_group.shape[0]
    d = D_MODEL
    grp = lambda j, tg, tb, tr: (tg[j], 0, 0)
    tile = pl.BlockSpec((tmx * ROW_CHUNKS, LANES), lambda j, tg, tb, tr: (tb[j], 0))
    return pl.pallas_call(
        _experts_kernel,
        grid_spec=pltpu.PrefetchScalarGridSpec(
            num_scalar_prefetch=3, grid=(nt,),
            in_specs=[tile,
                      pl.BlockSpec((EXPERTS_PER_GROUP, d, EXPERT_FF), grp),
                      pl.BlockSpec((EXPERTS_PER_GROUP, d, EXPERT_FF), grp),
                      pl.BlockSpec((EXPERTS_PER_GROUP, EXPERT_FF, d), grp),
                      pl.BlockSpec(expand.shape, lambda j, tg, tb, tr: (0, 0))],
            out_specs=tile),
        out_shape=jax.ShapeDtypeStruct(xs.shape, F32),
        compiler_params=pltpu.CompilerParams(dimension_semantics=("arbitrary",), vmem_limit_bytes=VMEM_LIMIT),
        name="experts",
    )(tile_group, tile_blk, tile_rows, xs, w_gate, w_up, w_down, expand)


def _final_kernel(dest_ref, x1_ref, mod_ref, gpost_ref, ys_ref, o_ref, ybuf, sems):
    tmf, d = x1_ref.shape
    i = pl.program_id(0)
    slot = i % 2

    def gather(tile, into):
        base = tile * tmf

        def issue(r0, c):
            for j in range(DMA_UNROLL):
                r = r0 * DMA_UNROLL + j
                pltpu.make_async_copy(ys_ref.at[_token_rows(dest_ref[base + r])], ybuf.at[into, _token_rows(r)],
                                      sems.at[into]).start(priority=j % 2)
            return c

        lax.fori_loop(0, tmf // DMA_UNROLL, issue, 0)

    @pl.when(i == 0)
    def _():
        gather(i, slot)

    @pl.when(i + 1 < pl.num_programs(0))
    def _():
        gather(i + 1, 1 - slot)

    pltpu.make_async_copy(ys_ref.at[pl.ds(0, tmf * ROW_CHUNKS)], ybuf.at[slot], sems.at[slot]).wait()
    gate2 = mod_ref[0][5:6]
    y = jnp.concatenate([ybuf[slot, pl.ds(c, tmf, stride=ROW_CHUNKS), :] for c in range(d // LANES)], axis=1)
    o_ref[...] = x1_ref[...] + gate2 * _rms(y, gpost_ref[...])


def _final(dest, x1, mod3, g_post, ys, *, seq, tmf):
    t, d = x1.shape
    ns = seq // tmf
    return pl.pallas_call(
        _final_kernel,
        grid_spec=pltpu.PrefetchScalarGridSpec(
            num_scalar_prefetch=1, grid=(t // tmf,),
            in_specs=[pl.BlockSpec((tmf, d), lambda i, dest: (i, 0)),
                      pl.BlockSpec((1, 6, d), lambda i, dest: (i // ns, 0, 0)),
                      pl.BlockSpec((1, d), lambda i, dest: (0, 0)),
                      pl.BlockSpec(memory_space=pl.ANY)],
            out_specs=pl.BlockSpec((tmf, d), lambda i, dest: (i, 0)),
            scratch_shapes=[pltpu.VMEM((2, tmf * ROW_CHUNKS, LANES), F32), pltpu.SemaphoreType.DMA((2,))]),
        out_shape=jax.ShapeDtypeStruct((t, d), F32),
        compiler_params=pltpu.CompilerParams(dimension_semantics=("arbitrary",), vmem_limit_bytes=VMEM_LIMIT),
        name="final",
    )(dest, x1, mod3, g_post, ys)


def _tile_map(counts, *, capacity, tmx):
    ntile = (counts + tmx - 1) // tmx
    ends = jnp.cumsum(ntile)
    starts = ends - ntile
    j = jnp.arange(capacity // tmx + N_GROUPS, dtype=jnp.int32)
    g = jnp.minimum(jnp.sum((j[:, None] >= ends[None, :]).astype(jnp.int32), axis=1), N_GROUPS - 1)
    local = j - starts[g]
    valid = j < ends[-1]
    blk = g * (capacity // tmx) + local
    rows = jnp.clip(counts[g] - local * tmx, 0, tmx)
    last = ends[-1] - 1
    return (jnp.where(valid, g, g[last]).astype(jnp.int32), jnp.where(valid, blk, blk[last]).astype(jnp.int32),
            jnp.where(valid, rows, 0).astype(jnp.int32))


def _pick_tile(seq, want):
    return want if seq % want == 0 else seq


def _layer(x, c, positions, w_ada, b_ada, g_pre_mix, g_post_mix, g_pre_ffn, g_post_ffn, w_in, g_cq, w_uq,
           g_ckv, w_ukv, w_o_mla, mu_shift, w_decay_up, decay_bias, w_a_up, a_bias, w_g_up, k_k, k_a, r_k,
           lnx_w, lnx_b, w_o_rwkv, w_out, w_router_group, b_router_group, w_router_expert, b_router_expert,
           w_exp_gate, w_exp_up, w_exp_down):
    batch, seq, d = x.shape
    t = batch * seq
    x2 = x.reshape(t, d)
    row1 = lambda a: a.reshape(1, -1)

    mod3 = _ada(c, w_ada, b_ada).reshape(batch, 6, d)

    zeros = lambda n: jnp.zeros((d, n), F32)
    w_c = jnp.concatenate([w_in[:, :MLA_Q_RANK + MLA_KV_RANK], zeros(MLA_NOPE),
                           w_in[:, MLA_Q_RANK + MLA_KV_RANK:MLA_IN], zeros(HEAD_PAD - MLA_NOPE - MLA_ROPE)],
                          axis=1).astype(BF16)
    w_r = w_in[:, MLA_IN:MLA_IN + RWKV_IN].astype(BF16)
    w_g = w_in[:, MLA_IN + RWKV_IN:].astype(BF16)
    w_uq_p = jnp.pad(w_uq.reshape(MLA_Q_RANK, HEADS, MLA_NOPE + MLA_ROPE),
                     ((0, 0), (0, 0), (0, HEAD_PAD - MLA_NOPE - MLA_ROPE))).reshape(MLA_Q_RANK, QK_WIDTH).astype(BF16)
    w_ukv3 = w_ukv.reshape(MLA_KV_RANK, HEADS, MLA_NOPE + MLA_V)
    w_uk_p = jnp.pad(w_ukv3[..., :MLA_NOPE], ((0, 0), (0, 0), (0, HEAD_PAD - MLA_NOPE))
                     ).reshape(MLA_KV_RANK, QK_WIDTH).astype(BF16)
    w_uv_t = jnp.pad(w_ukv3[..., MLA_NOPE:], ((0, 0), (0, 0), (0, V_ROWS - MLA_V))
                     ).reshape(MLA_KV_RANK, HEADS * V_ROWS).T.astype(BF16)
    inv_freq = jnp.power(ROPE_THETA, -jnp.arange(0, MLA_ROPE, 2, dtype=F32) / MLA_ROPE)
    invf = jnp.concatenate([jnp.zeros((MLA_NOPE,), F32), inv_freq, inv_freq,
                            jnp.zeros((HEAD_PAD - MLA_NOPE - MLA_ROPE,), F32)]).reshape(1, LANES)

    tm = _pick_tile(seq, 512)
    q_all, k_all, vt_all, zs, gates = _inproj(
        x2, mod3, positions.reshape(batch, 1, seq), row1(g_pre_mix), w_c, w_r, w_g, row1(g_cq), row1(g_ckv),
        w_uq_p, w_uk_p, w_uv_t, row1(mu_shift), invf, batch=batch, seq=seq, tm=tm)

    attn = _attn(q_all, k_all, vt_all, batch=batch, seq=seq, tq=ATTN_BLOCK)

    tc = 2 * CHUNK * RWKV_CHUNKS_PER_STEP
    w_da = jnp.concatenate([
        jnp.concatenate([w_decay_up, jnp.zeros_like(w_decay_up)], axis=1),
        jnp.concatenate([jnp.zeros_like(w_a_up), w_a_up], axis=1)], axis=0).astype(BF16)
    hid = jnp.arange(2 * LANES) // RWKV_HEAD
    ones_bd = (hid[:, None] == hid[None, :]).astype(BF16)
    tid = jnp.arange(tc // 2)
    same_chunk = (tid[:, None] // CHUNK) == (tid[None, :] // CHUNK)
    tri = (same_chunk & (tid[None, :] <= tid[:, None])).astype(BF16)
    rw = _rwkv(zs, w_da, w_g_up.astype(BF16), row1(decay_bias), row1(a_bias), row1(k_k), row1(k_a), row1(r_k),
               row1(lnx_w), row1(lnx_b), ones_bd, tri, batch=batch, seq=seq, tc=tc)

    w_rt = jnp.concatenate([w_router_expert, w_router_group,
                            jnp.zeros((d, ROUTER_WIDTH - N_EXPERTS - N_GROUPS), F32)], axis=1)
    wr_hi = w_rt.astype(BF16)
    wr_lo = (w_rt - wr_hi.astype(F32)).astype(BF16)
    b_r = jnp.concatenate([b_router_expert, b_router_group,
                           jnp.zeros((ROUTER_WIDTH - N_EXPERTS - N_GROUPS,), F32)]).reshape(1, ROUTER_WIDTH)
    x1, hp, dest, cnt = _merge(attn, rw, gates, x2, mod3, w_o_mla.astype(BF16), w_o_rwkv.astype(BF16),
                               w_out.astype(BF16), row1(g_post_mix), row1(g_pre_ffn), wr_hi, wr_lo, b_r,
                               seq=seq, tm=_pick_tile(seq, 512))

    tmx = _pick_tile(seq, 512)
    dest = dest.reshape(t)
    xs = _dispatch(dest, hp, tokens_out=N_GROUPS * t, td=_pick_tile(seq, 2048))
    tile_group, tile_blk, tile_rows = _tile_map(cnt[0, :N_GROUPS].astype(jnp.int32), capacity=t, tmx=tmx)
    eid = jnp.arange(EXPERTS_PER_GROUP * EXPERT_FF) // EXPERT_FF
    expand = (jnp.arange(ROUTER_WIDTH)[:, None] == eid[None, :]).astype(BF16)
    ys = _experts(tile_group, tile_blk, tile_rows, xs, w_exp_gate.astype(BF16), w_exp_up.astype(BF16),
                  w_exp_down.astype(BF16), expand, tmx=tmx)
    out = _final(dest, x1, mod3, row1(g_post_ffn), ys, seq=seq, tmf=_pick_tile(seq, 512))
    return out.reshape(batch, seq, d)


def kernel(x, c, positions, w_ada, b_ada, g_pre_mix, g_post_mix, g_pre_ffn, g_post_ffn, w_in, g_cq, w_uq, g_ckv, w_ukv, w_o_mla, mu_shift, w_decay_up, decay_bias, w_a_up, a_bias, w_g_up, k_k, k_a, r_k, lnx_w, lnx_b, w_o_rwkv, w_out, w_router_group, b_router_group, w_router_expert, b_router_expert, w_exp_gate, w_exp_up, w_exp_down):
    depth = w_ada.shape[0]
    for l in range(depth):
        x = _layer(x, c, positions, w_ada[l], b_ada[l], g_pre_mix[l], g_post_mix[l], g_pre_ffn[l], g_post_ffn[l],
                   w_in[l], g_cq[l], w_uq[l], g_ckv[l], w_ukv[l], w_o_mla[l], mu_shift[l], w_decay_up[l],
                   decay_bias[l], w_a_up[l], a_bias[l], w_g_up[l], k_k[l], k_a[l], r_k[l], lnx_w[l], lnx_b[l],
                   w_o_rwkv[l], w_out[l], w_router_group[l], b_router_group[l], w_router_expert[l],
                   b_router_expert[l], w_exp_gate[l], w_exp_up[l], w_exp_down[l])
    return x
```
